```python
import jax, jax.numpy as jnp
from jax import lax
import numpy as np

D_MODEL = 1024
BATCH = 2
SEQ = 8192
DEPTH = 2

GRID_W = 64
CTX_LEN = 256
CONV_DIM = 256
CONV_GROUPS = 4
CONV_K = 3
POOL_DIM = 256
POOL_WINDOWS = (2, 4, 8, 16)
N_POOL = 4
POOL_GROUP_DIM = POOL_DIM // N_POOL
HEAD_DIM = 64
GQA_HEADS = 4
GQA_KV_HEADS = 2
GQA_DIM = GQA_HEADS * HEAD_DIM
GQA_SCALE = HEAD_DIM ** -0.5
MLA_HEADS = 4
MLA_NOPE_DIM = 64
MLA_ROPE_DIM = 32
MLA_QK_DIM = MLA_NOPE_DIM + MLA_ROPE_DIM
MLA_V_DIM = 64
MLA_Q_RANK = 256
MLA_KV_RANK = 128
MLA_DIM = MLA_HEADS * MLA_V_DIM
MLA_SCALE = MLA_QK_DIM ** -0.5
IN_LAYOUT = (("conv", 3 * CONV_DIM), ("pool", POOL_DIM), ("gqa_q", GQA_HEADS * HEAD_DIM),
             ("gqa_kv", 2 * GQA_KV_HEADS * HEAD_DIM), ("mla_q", MLA_Q_RANK), ("mla_kv", MLA_KV_RANK),
             ("mla_kr", MLA_ROPE_DIM))
D_IN = 3 * CONV_DIM + POOL_DIM + GQA_HEADS * HEAD_DIM + 2 * GQA_KV_HEADS * HEAD_DIM + MLA_Q_RANK + MLA_KV_RANK + MLA_ROPE_DIM
D_MIX = CONV_DIM + POOL_DIM + GQA_DIM + MLA_DIM
N_EXPERTS = 16
N_EXPERT_GROUPS = 4
EXPERTS_PER_GROUP = N_EXPERTS // N_EXPERT_GROUPS
TOP_K = 2
D_EXPERT = 1024
MOE_BLOCK = 256
ROPE_THETA = 10000.0
NORM_EPS = 1e-6
Q_BLOCK = 128

kernel_name = "hybrid_parallel_heads_dit_moe"


def rms_norm(t, gain):
    tf = t.astype(jnp.float32)
    y = tf * lax.rsqrt(jnp.mean(tf * tf, axis=-1, keepdims=True) + NORM_EPS)
    return (y * gain.astype(jnp.float32)).astype(t.dtype)


def modulate(t, shift, scale):
    return t * (1.0 + scale) + shift


def adaln(cvec, w_mod, b_mod, n_chunks):
    n = n_chunks * w_mod.shape[0]
    m = jax.nn.silu(cvec) @ w_mod[:, :n] + b_mod[:n]
    return jnp.split(m, n_chunks, axis=-1)


def in_offsets():
    out, off = {}, 0
    for name, n in IN_LAYOUT:
        out[name] = (off, n)
        off += n
    return out


def split_in(z):
    return {name: z[..., off:off + n] for name, (off, n) in in_offsets().items()}


def col_block(w, name):
    off, n = in_offsets()[name]
    return w[:, off:off + n]


def axial_rope(row_id, col_id, rot_dim):
    n_freq = rot_dim // 4
    inv_freq = jnp.power(ROPE_THETA, -jnp.arange(n_freq, dtype=jnp.float32) / n_freq)
    ang = jnp.concatenate([row_id[:, None] * inv_freq, col_id[:, None] * inv_freq], axis=-1)
    return jnp.cos(ang), jnp.sin(ang)


def apply_rope(t, cos, sin):
    half = t.shape[-1] // 2
    t1, t2 = t[..., :half], t[..., half:]
    cs = cos[None, :, None, :].astype(t.dtype)
    sn = sin[None, :, None, :].astype(t.dtype)
    return jnp.concatenate([t1 * cs - t2 * sn, t1 * sn + t2 * cs], axis=-1)


def rope_tail(t, cos, sin):
    return jnp.concatenate([t[..., :MLA_NOPE_DIM], apply_rope(t[..., MLA_NOPE_DIM:], cos, sin)], axis=-1)


def attend(q, k, v, scale):
    b, nq, hk, g, dq = q.shape
    nb = nq // Q_BLOCK
    qb = jnp.moveaxis(q.reshape(b, nb, Q_BLOCK, hk, g, dq), 1, 0)

    def one_block(qblk):
        s = jnp.einsum("bqhgd,bkhd->bhgqk", qblk, k).astype(jnp.float32) * scale
        p = jax.nn.softmax(s, axis=-1).astype(v.dtype)
        return jnp.einsum("bhgqk,bkhd->bqhgd", p, v)

    o = lax.map(one_block, qb)
    return jnp.moveaxis(o, 0, 1).reshape(b, nq, hk, g, v.shape[-1])


def short_conv_mixer(z, conv_w):
    b_gate, c_gate, u = jnp.split(z, 3, axis=-1)
    vp = jnp.pad(c_gate * u, ((0, 0), (1, 1), (0, 0)))
    conv = vp[:, :-2] * conv_w[0] + vp[:, 1:-1] * conv_w[1] + vp[:, 2:] * conv_w[2]
    return b_gate * conv


def pool_mixer(z, w_pool, pool_scale):
    b, n, _ = z.shape
    zf = z.astype(jnp.float32)
    cs = jnp.concatenate([jnp.zeros((b, 1, POOL_DIM), jnp.float32), jnp.cumsum(zf, axis=1)], axis=1)
    t = jnp.arange(n)
    outs = []
    for g, w in enumerate(POOL_WINDOWS):
        lo_c, hi_c = g * POOL_GROUP_DIM, (g + 1) * POOL_GROUP_DIM
        lo = jnp.clip(t - w // 2, 0, n)
        hi = jnp.clip(t + w // 2, 0, n)
        csg = cs[..., lo_c:hi_c]
        mean = (csg[:, hi] - csg[:, lo]) / (hi - lo).astype(jnp.float32)[None, :, None]
        d = (mean - zf[..., lo_c:hi_c]).astype(z.dtype)
        outs.append(d @ w_pool[g])
    return jnp.concatenate(outs, axis=-1) * pool_scale


def gqa_queries(zq, gain):
    b, n, _ = zq.shape
    return rms_norm(zq.reshape(b, n, GQA_HEADS, HEAD_DIM), gain)


def gqa_keys_values(zkv, gain):
    b, n, _ = zkv.shape
    k, v = jnp.split(zkv, 2, axis=-1)
    k = rms_norm(k.reshape(b, n, GQA_KV_HEADS, HEAD_DIM), gain)
    return k, v.reshape(b, n, GQA_KV_HEADS, HEAD_DIM)


def group_heads(q):
    b, n, h, d = q.shape
    return q.reshape(b, n, GQA_KV_HEADS, h // GQA_KV_HEADS, d)


def mla_queries(zq, q_norm, w_uq, qk_gain):
    b, n, _ = zq.shape
    q = (rms_norm(zq, q_norm) @ w_uq).reshape(b, n, MLA_HEADS, MLA_QK_DIM)
    return rms_norm(q, qk_gain)


def mla_keys_values(zkv, zkr, kv_norm, w_uk, w_uv, qk_gain):
    b, n, _ = zkv.shape
    ckv = rms_norm(zkv, kv_norm)
    k_nope = (ckv @ w_uk).reshape(b, n, MLA_HEADS, MLA_NOPE_DIM)
    v = (ckv @ w_uv).reshape(b, n, MLA_HEADS, MLA_V_DIM)
    k_rope = jnp.broadcast_to(zkr[:, :, None, :], (b, n, MLA_HEADS, MLA_ROPE_DIM))
    k = rms_norm(jnp.concatenate([k_nope, k_rope], axis=-1), qk_gain)
    return k, v


def moe_ffn(h, w_router, b_router, w_gate, w_up, w_down):
    t_tok, d = h.shape
    scores = jax.nn.sigmoid(h.astype(jnp.float32) @ w_router.astype(jnp.float32))
    sel = (scores + b_router.astype(jnp.float32)).reshape(t_tok, N_EXPERT_GROUPS, EXPERTS_PER_GROUP)
    group_score = lax.top_k(sel, 2)[0].sum(-1)
    g_best = jnp.argmax(group_score, axis=-1)
    in_group = jnp.take_along_axis(sel, g_best[:, None, None], axis=1)[:, 0]
    _, local = lax.top_k(in_group, TOP_K)
    idx = g_best[:, None] * EXPERTS_PER_GROUP + local
    gw = jnp.take_along_axis(scores, idx, axis=-1)
    gw = gw / jnp.sum(gw, axis=-1, keepdims=True)
    tk = t_tok * TOP_K
    flat_e = idx.reshape(tk)
    flat_t = jnp.arange(tk) // TOP_K
    flat_w = gw.reshape(tk)
    order = jnp.argsort(flat_e)
    se, st, sw = flat_e[order], flat_t[order], flat_w[order]
    counts = jnp.bincount(flat_e, length=N_EXPERTS)
    padded = ((counts + MOE_BLOCK - 1) // MOE_BLOCK) * MOE_BLOCK
    pad_end = jnp.cumsum(padded)
    pad_start = pad_end - padded
    start = jnp.cumsum(counts) - counts
    dest = pad_start[se] + jnp.arange(tk) - start[se]
    n_blocks = -(-tk // MOE_BLOCK) + N_EXPERTS
    slot_tok = jnp.zeros((n_blocks * MOE_BLOCK,), jnp.int32).at[dest].set(st.astype(jnp.int32))
    block_exp = jnp.minimum(jnp.searchsorted(pad_end, jnp.arange(n_blocks) * MOE_BLOCK, side="right"), N_EXPERTS - 1)
    xb = h[slot_tok].reshape(n_blocks, MOE_BLOCK, d)

    def expert_block(args):
        xblk, e = args
        a = xblk @ w_gate[e]
        u = xblk @ w_up[e]
        return (jax.nn.silu(a) * u) @ w_down[e]

    yb = lax.map(expert_block, (xb, block_exp)).reshape(n_blocks * MOE_BLOCK, d)
    y_assign = yb[dest] * sw[:, None].astype(h.dtype)
    return jax.ops.segment_sum(y_assign, st, num_segments=t_tok)


def setup_inputs(seed: int = 0) -> dict:
    key = jax.random.key(seed)
    ks = jax.random.split(key, 32)
    f32 = jnp.float32
    D = D_MODEL

    def nrm(k, shape, scale):
        return jax.random.normal(k, shape, f32) * scale

    def gain(k, shape, s=0.02):
        return 1.0 + s * jax.random.normal(k, shape, f32)

    return {
        "x": nrm(ks[0], (BATCH, SEQ, D), 1.0),
        "c": nrm(ks[1], (BATCH, D), 1.0),
        "ctx": nrm(ks[2], (BATCH, CTX_LEN, D), 1.0),
        "c_ctx": nrm(ks[3], (D,), 1.0),
        "w_mod": nrm(ks[4], (DEPTH, D, 6 * D), 0.5 * D ** -0.5),
        "b_mod": nrm(ks[5], (DEPTH, 6 * D), 0.02),
        "norm1": gain(ks[6], (DEPTH, D)),
        "norm2": gain(ks[7], (DEPTH, D)),
        "w_in": nrm(ks[8], (DEPTH, D, D_IN), D ** -0.5),
        "conv_w": nrm(ks[9], (DEPTH, CONV_K, CONV_DIM), CONV_K ** -0.5),
        "w_pool": nrm(ks[10], (DEPTH, N_POOL, POOL_GROUP_DIM, POOL_GROUP_DIM), POOL_GROUP_DIM ** -0.5),
        "pool_scale": gain(ks[11], (DEPTH, POOL_DIM), 0.1),
        "gqa_q_norm": gain(ks[12], (DEPTH, HEAD_DIM)),
        "gqa_k_norm": gain(ks[13], (DEPTH, HEAD_DIM)),
        "mla_q_norm": gain(ks[14], (DEPTH, MLA_Q_RANK)),
        "mla_kv_norm": gain(ks[15], (DEPTH, MLA_KV_RANK)),
        "mla_w_uq": nrm(ks[16], (DEPTH, MLA_Q_RANK, MLA_HEADS * MLA_QK_DIM), MLA_Q_RANK ** -0.5),
        "mla_w_uk": nrm(ks[17], (DEPTH, MLA_KV_RANK, MLA_HEADS * MLA_NOPE_DIM), MLA_KV_RANK ** -0.5),
        "mla_w_uv": nrm(ks[18], (DEPTH, MLA_KV_RANK, MLA_HEADS * MLA_V_DIM), MLA_KV_RANK ** -0.5),
        "mla_qk_q_norm": gain(ks[19], (DEPTH, MLA_QK_DIM)),
        "mla_qk_k_norm": gain(ks[20], (DEPTH, MLA_QK_DIM)),
        "w_out": nrm(ks[21], (DEPTH, D_MIX, D), D_MIX ** -0.5),
        "w_router": nrm(ks[22], (D, N_EXPERTS), D ** -0.5),
        "b_router": nrm(ks[23], (N_EXPERTS,), 0.01),
        "w_gate": nrm(ks[24], (DEPTH, N_EXPERTS, D, D_EXPERT), D ** -0.5),
        "w_up": nrm(ks[25], (DEPTH, N_EXPERTS, D, D_EXPERT), D ** -0.5),
        "w_down": nrm(ks[26], (DEPTH, N_EXPERTS, D_EXPERT, D), D_EXPERT ** -0.5),
    }


def reference(x, c, ctx, c_ctx, w_mod, b_mod, norm1, norm2, w_in, conv_w, w_pool, pool_scale,
              gqa_q_norm, gqa_k_norm, mla_q_norm, mla_kv_norm, mla_w_uq, mla_w_uk, mla_w_uv,
              mla_qk_q_norm, mla_qk_k_norm, w_out, w_router, b_router, w_gate, w_up, w_down):
    b, n_lat, d = x.shape
    n_ctx = ctx.shape[1]
    rows = n_lat // GRID_W
    row_id = jnp.repeat(jnp.arange(rows, dtype=jnp.float32), GRID_W)
    col_id = jnp.tile(jnp.arange(GRID_W, dtype=jnp.float32), rows)
    g_cos, g_sin = axial_rope(row_id, col_id, HEAD_DIM)
    m_cos, m_sin = axial_rope(row_id, col_id, MLA_ROPE_DIM)
    xc = ctx
    for i in range(DEPTH):
        last = i == DEPTH - 1
        sh1, sc1, g1, sh2, sc2, g2 = adaln(c, w_mod[i], b_mod[i], 6)
        cm = adaln(c_ctx, w_mod[i], b_mod[i], 2 if last else 6)
        h = modulate(rms_norm(x, norm1[i]), sh1[:, None], sc1[:, None])
        hc = modulate(rms_norm(xc, norm1[i]), cm[0], cm[1])
        zl = split_in(h @ w_in[i])
        if last:
            zc = {nm: hc @ col_block(w_in[i], nm) for nm in ("gqa_kv", "mla_kv", "mla_kr")}
        else:
            zc = split_in(hc @ w_in[i])
        kc_g, vc_g = gqa_keys_values(zc["gqa_kv"], gqa_k_norm[i])
        kc_m, vc_m = mla_keys_values(zc["mla_kv"], zc["mla_kr"], mla_kv_norm[i], mla_w_uk[i], mla_w_uv[i], mla_qk_k_norm[i])
        y_conv = short_conv_mixer(zl["conv"], conv_w[i])
        y_pool = pool_mixer(zl["pool"], w_pool[i], pool_scale[i])
        q_g = apply_rope(gqa_queries(zl["gqa_q"], gqa_q_norm[i]), g_cos, g_sin)
        k_g, v_g = gqa_keys_values(zl["gqa_kv"], gqa_k_norm[i])
        k_g = apply_rope(k_g, g_cos, g_sin)
        y_gqa = attend(group_heads(q_g), jnp.concatenate([kc_g, k_g], axis=1),
                       jnp.concatenate([vc_g, v_g], axis=1), GQA_SCALE).reshape(b, n_lat, GQA_DIM)
        q_m = rope_tail(mla_queries(zl["mla_q"], mla_q_norm[i], mla_w_uq[i], mla_qk_q_norm[i]), m_cos, m_sin)
        k_m, v_m = mla_keys_values(zl["mla_kv"], zl["mla_kr"], mla_kv_norm[i], mla_w_uk[i], mla_w_uv[i], mla_qk_k_norm[i])
        k_m = rope_tail(k_m, m_cos, m_sin)
        y_mla = attend(q_m[:, :, :, None], jnp.concatenate([kc_m, k_m], axis=1),
                       jnp.concatenate([vc_m, v_m], axis=1), MLA_SCALE).reshape(b, n_lat, MLA_DIM)
        y = jnp.concatenate([y_conv, y_pool, y_gqa, y_mla], axis=-1) @ w_out[i]
        x = x + g1[:, None] * y
        if not last:
            yc_conv = short_conv_mixer(zc["conv"], conv_w[i])
            yc_pool = pool_mixer(zc["pool"], w_pool[i], pool_scale[i])
            qc_g = gqa_queries(zc["gqa_q"], gqa_q_norm[i])
            yc_gqa = attend(group_heads(qc_g), kc_g, vc_g, GQA_SCALE).reshape(b, n_ctx, GQA_DIM)
            qc_m = mla_queries(zc["mla_q"], mla_q_norm[i], mla_w_uq[i], mla_qk_q_norm[i])
            yc_mla = attend(qc_m[:, :, :, None], kc_m, vc_m, MLA_SCALE).reshape(b, n_ctx, MLA_DIM)
            yc = jnp.concatenate([yc_conv, yc_pool, yc_gqa, yc_mla], axis=-1) @ w_out[i]
            xc = xc + cm[2] * yc
        h2 = modulate(rms_norm(x, norm2[i]), sh2[:, None], sc2[:, None])
        tok = h2.reshape(b * n_lat, d)
        if not last:
            h2c = modulate(rms_norm(xc, norm2[i]), cm[3], cm[4])
            tok = jnp.concatenate([tok, h2c.reshape(b * n_ctx, d)], axis=0)
        y2 = moe_ffn(tok, w_router, b_router, w_gate[i], w_up[i], w_down[i])
        x = x + g2[:, None] * y2[:b * n_lat].reshape(b, n_lat, d)
        if not last:
            xc = xc + cm[5] * y2[b * n_lat:].reshape(b, n_ctx, d)
    return x
```

```python
import functools
import math

import numpy as np
import jax
import jax.numpy as jnp
from jax import lax
from jax.experimental import pallas as pl
from jax.experimental.pallas import tpu as pltpu

F32 = jnp.float32
BF16 = jnp.bfloat16
I32 = jnp.int32

GRID_W = 64
CONV_DIM = 256
POOL_DIM = 256
POOL_WINDOWS = (2, 4, 8, 16)
HEAD_DIM = 64
GQA_HEADS = 4
GQA_KV_HEADS = 2
MLA_HEADS = 4
MLA_NOPE_DIM = 64
MLA_ROPE_DIM = 32
MLA_QK_DIM = MLA_NOPE_DIM + MLA_ROPE_DIM
MLA_V_DIM = 64
MLA_Q_RANK = 256
MLA_KV_RANK = 128
N_EXPERTS = 16
EXPERTS_PER_GROUP = 4
ROPE_THETA = 10000.0
NORM_EPS = 1e-6
LOG2E = 1.4426950408889634
GQA_SCALE = HEAD_DIM ** -0.5
MLA_SCALE = MLA_QK_DIM ** -0.5

LANES = 128
SUBLANES = 8
BF16_ROWS = 16
VMEM_LIMIT = 56 * 1024 * 1024

TM = 256
TK = 256
BM = 512
POOL_EXT = 512
POOL_HALO = 8

ZC_B, ZC_C, ZC_U, ZC_P = 0, 256, 512, 768
ZC_GQ = 1024
ZC_GK = ZC_GQ + GQA_HEADS * LANES
ZC_GV = ZC_GK + GQA_KV_HEADS * LANES
ZC_MQ = ZC_GV + GQA_KV_HEADS * LANES
ZC_MKV = ZC_MQ + MLA_Q_RANK
ZC_MKR = ZC_MKV + MLA_KV_RANK
ZC_END = ZC_MKR + LANES

HIGHEST = lax.Precision.HIGHEST


def _cparams(sem, vmem=VMEM_LIMIT):
    return pltpu.CompilerParams(dimension_semantics=sem, vmem_limit_bytes=vmem)


def _sigmoid(v):
    return 1.0 / (1.0 + jnp.exp(-v))


def _adaln_kernel(c_ref, w_ref, b_ref, o_ref):
    c = c_ref[...]
    s = c * _sigmoid(c)
    o_ref[0] = jnp.dot(s, w_ref[0], preferred_element_type=F32, precision=HIGHEST) + b_ref[0]


def _adaln(cvec, w_mod, b_mod):
    depth, d, n6 = w_mod.shape
    tn = 1536 if n6 % 1536 == 0 else n6
    rows = cvec.shape[0]
    return pl.pallas_call(
        _adaln_kernel,
        grid=(depth, n6 // tn),
        in_specs=[
            pl.BlockSpec((rows, d), lambda i, n: (0, 0)),
            pl.BlockSpec((1, d, tn), lambda i, n: (i, 0, n)),
            pl.BlockSpec((1, 1, tn), lambda i, n: (i, 0, n)),
        ],
        out_specs=pl.BlockSpec((1, rows, tn), lambda i, n: (i, 0, n)),
        out_shape=jax.ShapeDtypeStruct((depth, rows, n6), F32),
        compiler_params=_cparams(("arbitrary", "arbitrary")),
        name="adaln",
    )(cvec, w_mod, b_mod.reshape(depth, 1, n6))


def _norm_rope(slab, gain, cos, sin, n_valid, first_half, half):
    ms = jnp.sum(slab * slab, axis=-1, keepdims=True) * (1.0 / n_valid)
    y = slab * lax.rsqrt(ms + NORM_EPS) * gain
    partner = jnp.where(first_half, pltpu.roll(y, LANES - half, 1), pltpu.roll(y, half, 1))
    return y * cos + partner * sin


def _inproj_kernel(x_ref, mods_ref, n1_ref, cg_ref, sg_ref, cm_ref, sm_ref, win_ref,
                   gq_ref, gk_ref, mqn_ref, mkvn_ref, qkq_ref, qkk_ref, wuq_ref, wukv_ref,
                   cb_ref, cv_ref, zp_ref, qg_ref, kg_ref, vg_ref, qm_ref, km_ref, vm_ref,
                   *, d_model, nct, nbatch):
    b = pl.program_id(0)
    jt = pl.program_id(1)
    d = d_model
    row = jnp.where(jt < nct, nbatch, b)
    shift = mods_ref[0, pl.ds(row, 1), 0:d]
    scale = mods_ref[0, pl.ds(row, 1), d:2 * d]
    x = x_ref[...]
    ms = jnp.mean(x * x, axis=-1, keepdims=True)
    h = (x * lax.rsqrt(ms + NORM_EPS)) * (n1_ref[0] * (1.0 + scale)) + shift
    z = jnp.dot(h.astype(BF16), win_ref[0], preferred_element_type=F32)

    cb_ref[...] = z[:, ZC_B:ZC_B + CONV_DIM].astype(BF16)
    cv_ref[...] = (z[:, ZC_C:ZC_C + CONV_DIM] * z[:, ZC_U:ZC_U + CONV_DIM]).astype(BF16)
    zp_ref[...] = z[:, ZC_P:ZC_P + POOL_DIM]

    tm = x.shape[0]
    lane = lax.broadcasted_iota(I32, (tm, LANES), 1)
    ones_col = lane == HEAD_DIM
    cg, sg = cg_ref[...], sg_ref[...]
    g_first = lane < HEAD_DIM // 2
    for hd in range(GQA_HEADS):
        slab = z[:, ZC_GQ + hd * LANES:ZC_GQ + (hd + 1) * LANES]
        qg_ref[:, hd * LANES:(hd + 1) * LANES] = _norm_rope(
            slab, gq_ref[0], cg, sg, HEAD_DIM, g_first, HEAD_DIM // 2).astype(BF16)
    for hd in range(GQA_KV_HEADS):
        slab = z[:, ZC_GK + hd * LANES:ZC_GK + (hd + 1) * LANES]
        kg_ref[:, hd * LANES:(hd + 1) * LANES] = _norm_rope(
            slab, gk_ref[0], cg, sg, HEAD_DIM, g_first, HEAD_DIM // 2).astype(BF16)
        vs = z[:, ZC_GV + hd * LANES:ZC_GV + (hd + 1) * LANES]
        vg_ref[:, hd * LANES:(hd + 1) * LANES] = jnp.where(ones_col, 1.0, vs).astype(BF16)

    cm, sm = cm_ref[...], sm_ref[...]
    m_first = lane < MLA_NOPE_DIM + MLA_ROPE_DIM // 2
    zq = z[:, ZC_MQ:ZC_MQ + MLA_Q_RANK]
    cq = zq * lax.rsqrt(jnp.mean(zq * zq, axis=-1, keepdims=True) + NORM_EPS) * mqn_ref[0]
    qpre = jnp.dot(cq.astype(BF16), wuq_ref[0], preferred_element_type=F32)
    zkv = z[:, ZC_MKV:ZC_MKV + MLA_KV_RANK]
    ckv = zkv * lax.rsqrt(jnp.mean(zkv * zkv, axis=-1, keepdims=True) + NORM_EPS) * mkvn_ref[0]
    kvp = jnp.dot(ckv.astype(BF16), wukv_ref[0], preferred_element_type=F32)
    zkr = z[:, ZC_MKR:ZC_MKR + LANES]
    for hd in range(MLA_HEADS):
        sl = slice(hd * LANES, (hd + 1) * LANES)
        qm_ref[:, sl] = _norm_rope(qpre[:, sl], qkq_ref[0], cm, sm, MLA_QK_DIM,
                                   m_first, MLA_ROPE_DIM // 2).astype(BF16)
        km_ref[:, sl] = _norm_rope(kvp[:, sl] + zkr, qkk_ref[0], cm, sm, MLA_QK_DIM,
                                   m_first, MLA_ROPE_DIM // 2).astype(BF16)
        vs = kvp[:, MLA_HEADS * LANES + hd * LANES:MLA_HEADS * LANES + (hd + 1) * LANES]
        vm_ref[:, sl] = jnp.where(ones_col, 1.0, vs).astype(BF16)


def _inproj(xa, mods, layer, p, geo):
    r, d = xa.shape
    tpb, nct, nb = geo["tpb"], geo["nct"], geo["nbatch"]
    rowmap = lambda b, t: (b * tpb + t, 0)
    posmap = lambda b, t: (t, 0)
    lay3 = lambda b, t: (layer, 0, 0)

    def full3(a):
        return pl.BlockSpec((1,) + a.shape[1:], lay3)

    outs = [("cb", CONV_DIM, BF16), ("cv", CONV_DIM, BF16), ("zp", POOL_DIM, F32),
            ("qg", GQA_HEADS * LANES, BF16), ("kg", GQA_KV_HEADS * LANES, BF16),
            ("vg", GQA_KV_HEADS * LANES, BF16), ("qm", MLA_HEADS * LANES, BF16),
            ("km", MLA_HEADS * LANES, BF16), ("vm", MLA_HEADS * LANES, BF16)]
    res = pl.pallas_call(
        functools.partial(_inproj_kernel, d_model=d, nct=nct, nbatch=nb),
        grid=(nb, tpb),
        in_specs=[
            pl.BlockSpec((TM, d), rowmap),
            full3(mods), full3(p["norm1"]),
            pl.BlockSpec((TM, LANES), posmap), pl.BlockSpec((TM, LANES), posmap),
            pl.BlockSpec((TM, LANES), posmap), pl.BlockSpec((TM, LANES), posmap),
            full3(p["w_in"]), full3(p["gq"]), full3(p["gk"]), full3(p["mqn"]), full3(p["mkvn"]),
            full3(p["qkq"]), full3(p["qkk"]), full3(p["wuq"]), full3(p["wukv"]),
        ],
        out_specs=[pl.BlockSpec((TM, w), rowmap) for _, w, _ in outs],
        out_shape=[jax.ShapeDtypeStruct((r, w), dt) for _, w, dt in outs],
        compiler_params=_cparams(("arbitrary", "arbitrary")),
        name="inproj",
    )(xa, mods, p["norm1"], p["cos_g"], p["sin_g"], p["cos_m"], p["sin_m"], p["w_in"],
      p["gq"], p["gk"], p["mqn"], p["mkvn"], p["qkq"], p["qkk"], p["wuq"], p["wukv"])
    return {name: a for (name, _, _), a in zip(outs, res)}


def _attn_kernel(q_ref, k_ref, v_ref, o_ref, *, shared_kv, nct, nkv_ctx, nkv_all):
    qi = pl.program_id(2)
    tq = q_ref.shape[0]
    nkv = jnp.where(qi < nct, nkv_ctx, nkv_all)

    def flash(q, col):
        rows = q.shape[0]

        def step(i, carry):
            m, acc = carry
            start = pl.multiple_of(i * TK, TK)
            k = k_ref[pl.ds(start, TK), col:col + LANES]
            v = v_ref[pl.ds(start, TK), col:col + LANES]
            s = lax.dot_general(q, k, (((1,), (1,)), ((), ())), preferred_element_type=F32)
            m_new = jnp.maximum(m, jnp.max(s, axis=-1, keepdims=True))
            alpha = jnp.exp2(m - m_new)
            pmat = jnp.exp2(s - m_new)
            acc = alpha * acc + jnp.dot(pmat.astype(BF16), v, preferred_element_type=F32)
            return m_new, acc

        m0 = jnp.full((rows, 1), -1e30, F32)
        a0 = jnp.zeros((rows, LANES), F32)
        return lax.fori_loop(0, nkv, step, (m0, a0))[1]

    if shared_kv:
        q2 = jnp.concatenate([q_ref[:, 0:LANES], q_ref[:, LANES:2 * LANES]], axis=0)
        acc = flash(q2, 0)
        a0, a1 = acc[:tq], acc[tq:]
    else:
        a0 = flash(q_ref[:, 0:LANES], 0)
        a1 = flash(q_ref[:, LANES:2 * LANES], LANES)
    o0 = a0 * (1.0 / a0[:, HEAD_DIM:HEAD_DIM + 1])
    o1 = a1 * (1.0 / a1[:, HEAD_DIM:HEAD_DIM + 1])
    lane = lax.broadcasted_iota(I32, (tq, LANES), 1)
    o_ref[...] = jnp.where(lane < HEAD_DIM, o0, pltpu.roll(o1, HEAD_DIM, 1)).astype(BF16)


def _attention(q, k, v, shared_kv, geo):
    r = q.shape[0]
    tpb, nct, nb, na = geo["tpb"], geo["nct"], geo["nbatch"], geo["na"]
    kw = LANES if shared_kv else 2 * LANES
    return pl.pallas_call(
        functools.partial(_attn_kernel, shared_kv=shared_kv, nct=nct,
                          nkv_ctx=geo["nc"] // TK, nkv_all=na // TK),
        grid=(nb, 2, tpb),
        in_specs=[
            pl.BlockSpec((TM, 2 * LANES), lambda b, g, t: (b * tpb + t, g)),
            pl.BlockSpec((na, kw), lambda b, g, t: (b, g)),
            pl.BlockSpec((na, kw), lambda b, g, t: (b, g)),
        ],
        out_specs=pl.BlockSpec((TM, LANES), lambda b, g, t: (b * tpb + t, g)),
        out_shape=jax.ShapeDtypeStruct((r, 2 * LANES), BF16),
        compiler_params=_cparams(("arbitrary", "arbitrary", "arbitrary")),
        name="attn_gqa" if shared_kv else "attn_mla",
    )(q, k, v)


def _top2_sum(a, b, c, d):
    hi_ab, lo_ab = jnp.maximum(a, b), jnp.minimum(a, b)
    hi_cd, lo_cd = jnp.maximum(c, d), jnp.minimum(c, d)
    first = jnp.maximum(hi_ab, hi_cd)
    second = jnp.maximum(jnp.minimum(hi_ab, hi_cd), jnp.maximum(lo_ab, lo_cd))
    return first + second


def _outproj_kernel(x_ref, cb_ref, cv_ref, cvp_ref, cvn_ref, zp_ref, zpp_ref, zpn_ref,
                    yg_ref, ym_ref, mods_ref, convw_ref, band_ref, wpool_ref, pscale_ref,
                    wout_ref, n2_ref, wrt_ref, br_ref, tri_ref,
                    xo_ref, h2_ref, info_ref, cnt_ref,
                    *, d_model, tpb, nct, nbatch, nc, nl):
    b = pl.program_id(0)
    jt = pl.program_id(1)
    d = d_model
    tm = x_ref.shape[0]
    is_ctx = jt < nct
    seg_first = jnp.logical_or(jt == 0, jt == nct)
    seg_last = jnp.logical_or(jt == nct - 1, jt == tpb - 1)
    keep_prev = jnp.where(seg_first, 0.0, 1.0)
    keep_next = jnp.where(seg_last, 0.0, 1.0)
    row = jnp.where(is_ctx, nbatch, b)
    gate1 = mods_ref[0, pl.ds(row, 1), 2 * d:3 * d]
    shift2 = mods_ref[0, pl.ds(row, 1), 3 * d:4 * d]
    scale2 = mods_ref[0, pl.ds(row, 1), 4 * d:5 * d]

    v = cv_ref[...].astype(F32)
    prev_row = cvp_ref[...].astype(F32)[BF16_ROWS - 1:BF16_ROWS] * keep_prev
    next_row = cvn_ref[...].astype(F32)[0:1] * keep_next
    rid = lax.broadcasted_iota(I32, (tm, CONV_DIM), 0)
    vm1 = jnp.where(rid == 0, prev_row, pltpu.roll(v, 1, 0))
    vp1 = jnp.where(rid == tm - 1, next_row, pltpu.roll(v, tm - 1, 0))
    cw = convw_ref[0]
    y_conv = cb_ref[...].astype(F32) * (vm1 * cw[0:1] + v * cw[1:2] + vp1 * cw[2:3])

    zp = zp_ref[...]
    ext = jnp.concatenate(
        [zpp_ref[...] * keep_prev, zp, zpn_ref[...] * keep_next,
         jnp.zeros((POOL_EXT - tm - 2 * POOL_HALO, POOL_DIM), F32)], axis=0).astype(BF16)
    ext_a, ext_b = ext[:, 0:LANES], ext[:, LANES:2 * LANES]
    lane = lax.broadcasted_iota(I32, (tm, LANES), 1)
    low = lane < POOL_DIM // 4
    sum_a = jnp.where(low, jnp.dot(band_ref[0], ext_a, preferred_element_type=F32),
                      jnp.dot(band_ref[1], ext_a, preferred_element_type=F32))
    sum_b = jnp.where(low, jnp.dot(band_ref[2], ext_b, preferred_element_type=F32),
                      jnp.dot(band_ref[3], ext_b, preferred_element_type=F32))
    sums = jnp.concatenate([sum_a, sum_b], axis=1)
    lane_p = lax.broadcasted_iota(I32, (tm, POOL_DIM), 1)
    half_w = jnp.left_shift(1, jnp.right_shift(lane_p, int(math.log2(POOL_DIM // 4))))
    pos = jnp.where(is_ctx, jt, jt - nct) * tm + rid
    seg_len = jnp.where(is_ctx, nc, nl)
    cnt = (jnp.minimum(pos + half_w, seg_len) - jnp.maximum(pos - half_w, 0)).astype(F32)
    dlt = sums / cnt - zp
    y_pool = jnp.dot(dlt.astype(BF16), wpool_ref[0], preferred_element_type=F32) * pscale_ref[0]

    ycat = jnp.concatenate([y_conv.astype(BF16), y_pool.astype(BF16), yg_ref[...], ym_ref[...]], axis=1)
    y = jnp.dot(ycat, wout_ref[0], preferred_element_type=F32)
    xn = x_ref[...] + gate1 * y
    xo_ref[...] = xn
    ms = jnp.mean(xn * xn, axis=-1, keepdims=True)
    h2 = (xn * lax.rsqrt(ms + NORM_EPS)) * (n2_ref[0] * (1.0 + scale2)) + shift2
    h2_ref[...] = h2

    logits = lax.dot_general(wrt_ref[...], h2, (((1,), (1,)), ((), ())),
                             preferred_element_type=F32, precision=HIGHEST)
    scores = _sigmoid(logits)
    sel = scores + br_ref[...]
    epg = EXPERTS_PER_GROUP
    n_groups = N_EXPERTS // epg
    srow = [sel[e:e + 1] for e in range(N_EXPERTS)]
    crow = [scores[e:e + 1] for e in range(N_EXPERTS)]
    gscore = [_top2_sum(*srow[g * epg:(g + 1) * epg]) for g in range(n_groups)]
    gbest = jnp.zeros_like(gscore[0]).astype(I32)
    best = gscore[0]
    for g in range(1, n_groups):
        upd = gscore[g] > best
        gbest = jnp.where(upd, g, gbest)
        best = jnp.where(upd, gscore[g], best)

    def pick(rows_, j):
        out = rows_[(n_groups - 1) * epg + j]
        for g in range(n_groups - 2, -1, -1):
            out = jnp.where(gbest == g, rows_[g * epg + j], out)
        return out

    sv = [pick(srow, j) for j in range(epg)]
    cv_ = [pick(crow, j) for j in range(epg)]
    i1 = jnp.zeros_like(gbest)
    b1 = sv[0]
    for j in range(1, epg):
        upd = sv[j] > b1
        i1 = jnp.where(upd, j, i1)
        b1 = jnp.where(upd, sv[j], b1)
    i2 = jnp.zeros_like(gbest)
    b2 = jnp.full_like(b1, -jnp.inf)
    for j in range(epg):
        upd = jnp.logical_and(i1 != j, sv[j] > b2)
        i2 = jnp.where(upd, j, i2)
        b2 = jnp.where(upd, sv[j], b2)
    s1 = cv_[epg - 1]
    s2 = cv_[epg - 1]
    for j in range(epg - 2, -1, -1):
        s1 = jnp.where(i1 == j, cv_[j], s1)
        s2 = jnp.where(i2 == j, cv_[j], s2)
    inv = 1.0 / (s1 + s2)
    e1 = gbest * epg + i1
    e2 = gbest * epg + i2

    @pl.when(jnp.logical_and(b == 0, jt == 0))
    def _():
        cnt_ref[...] = jnp.zeros_like(cnt_ref)

    erow = lax.broadcasted_iota(I32, (N_EXPERTS, tm), 0)
    hit1 = erow == e1
    hit2 = erow == e2
    onehot = jnp.where(hit1, 1.0, 0.0) + jnp.where(hit2, 1.0, 0.0)
    before = jnp.dot(onehot.astype(BF16), tri_ref[...], preferred_element_type=F32)
    tot = cnt_ref[:, 0:1] + before
    rank1 = jnp.sum(jnp.where(hit1, tot, 0.0), axis=0, keepdims=True).astype(I32)
    rank2 = jnp.sum(jnp.where(hit2, tot, 0.0), axis=0, keepdims=True).astype(I32)
    cnt_ref[...] = cnt_ref[...] + jnp.sum(onehot, axis=1, keepdims=True)

    w1 = lax.bitcast_convert_type(s1 * inv, I32)
    w2 = lax.bitcast_convert_type(s2 * inv, I32)
    irow = lax.broadcasted_iota(I32, (SUBLANES, tm), 0)
    info = jnp.where(irow == 0, e1, jnp.where(irow == 1, e2, jnp.where(
        irow == 2, rank1, jnp.where(irow == 3, rank2, jnp.where(
            irow == 4, w1, jnp.where(irow == 5, w2, 0))))))
    info_ref[0] = info


def _outproj(xa, mix, yg, ym, mods, layer, p, geo):
    r, d = xa.shape
    tpb, nct, nb = geo["tpb"], geo["nct"], geo["nbatch"]
    nt = nb * tpb
    rowmap = lambda b, t: (b * tpb + t, 0)
    lay3 = lambda b, t: (layer, 0, 0)
    c0 = lambda b, t: (0, 0)
    c3 = lambda b, t: (0, 0, 0)
    bf_blocks = TM // BF16_ROWS
    f_blocks = TM // SUBLANES

    def full3(a):
        return pl.BlockSpec((1,) + a.shape[1:], lay3)

    return pl.pallas_call(
        functools.partial(_outproj_kernel, d_model=d, tpb=tpb, nct=nct, nbatch=nb,
                          nc=geo["nc"], nl=geo["nl"]),
        grid=(nb, tpb),
        in_specs=[
            pl.BlockSpec((TM, d), rowmap),
            pl.BlockSpec((TM, CONV_DIM), rowmap),
            pl.BlockSpec((TM, CONV_DIM), rowmap),
            pl.BlockSpec((BF16_ROWS, CONV_DIM),
                         lambda b, t: (jnp.maximum((b * tpb + t) * bf_blocks - 1, 0), 0)),
            pl.BlockSpec((BF16_ROWS, CONV_DIM),
                         lambda b, t: (jnp.minimum((b * tpb + t + 1) * bf_blocks, nt * bf_blocks - 1), 0)),
            pl.BlockSpec((TM, POOL_DIM), rowmap),
            pl.BlockSpec((SUBLANES, POOL_DIM),
                         lambda b, t: (jnp.maximum((b * tpb + t) * f_blocks - 1, 0), 0)),
            pl.BlockSpec((SUBLANES, POOL_DIM),
                         lambda b, t: (jnp.minimum((b * tpb + t + 1) * f_blocks, nt * f_blocks - 1), 0)),
            pl.BlockSpec((TM, 2 * LANES), rowmap),
            pl.BlockSpec((TM, 2 * LANES), rowmap),
            full3(mods), full3(p["conv_w"]),
            pl.BlockSpec(p["band"].shape, c3),
            full3(p["wpool"]), full3(p["pscale"]), full3(p["w_out"]), full3(p["norm2"]),
            pl.BlockSpec(p["wrt"].shape, c0), pl.BlockSpec(p["br"].shape, c0),
            pl.BlockSpec(p["tri"].shape, c0),
        ],
        out_specs=[
            pl.BlockSpec((TM, d), rowmap),
            pl.BlockSpec((TM, d), rowmap),
            pl.BlockSpec((1, SUBLANES, TM), lambda b, t: (b * tpb + t, 0, 0)),
            pl.BlockSpec((N_EXPERTS, LANES), c0),
        ],
        out_shape=[
            jax.ShapeDtypeStruct((r, d), F32),
            jax.ShapeDtypeStruct((r, d), F32),
            jax.ShapeDtypeStruct((nt, SUBLANES, TM), I32),
            jax.ShapeDtypeStruct((N_EXPERTS, LANES), F32),
        ],
        compiler_params=_cparams(("arbitrary", "arbitrary")),
        name="outproj",
    )(xa, mix["cb"], mix["cv"], mix["cv"], mix["cv"], mix["zp"], mix["zp"], mix["zp"],
      yg, ym, mods, p["conv_w"], p["band"], p["wpool"], p["pscale"], p["w_out"], p["norm2"],
      p["wrt"], p["br"], p["tri"])


def _row_copies(idx_ref, n_rows, make):
    def body(r, carry):
        make(0, r, idx_ref[0, r]).start()
        make(1, r, idx_ref[1, r]).start()
        return carry
    lax.fori_loop(0, n_rows, body, 0)


def _scatter_kernel(ps_ref, zs_ref, has_ref, nu_ref, info_ref, h2_ref, dest_ref, xs_ref,
                    dsm_ref, zbuf_ref, sem_ref, *, first_spare, n_blocks):
    j = pl.program_id(0)
    tm = h2_ref.shape[0]

    @pl.when(j == 0)
    def _():
        zbuf_ref[...] = jnp.zeros_like(zbuf_ref)

        def zero_copy(start):
            return pltpu.make_async_copy(
                zbuf_ref, xs_ref.at[pl.ds(pl.multiple_of(start, BM), BM)], sem_ref.at[1])

        for act in ("start", "wait"):
            for e in range(N_EXPERTS):
                @pl.when(has_ref[e] > 0)
                def _():
                    getattr(zero_copy(zs_ref[e]), act)()
            for jb in range(first_spare, n_blocks):
                @pl.when(jb >= nu_ref[0])
                def _():
                    getattr(zero_copy(jb * BM), act)()

    info = info_ref[0]
    e1, e2 = info[0:1], info[1:2]
    d1, d2 = info[2:3], info[3:4]
    for e in range(N_EXPERTS):
        d1 = d1 + jnp.where(e1 == e, ps_ref[e], 0)
        d2 = d2 + jnp.where(e2 == e, ps_ref[e], 0)
    irow = lax.broadcasted_iota(I32, (SUBLANES, tm), 0)
    dest_ref[0] = jnp.where(irow == 0, d1, jnp.where(irow == 1, d2, 0))
    to_smem = pltpu.make_async_copy(dest_ref.at[0], dsm_ref, sem_ref.at[2])
    to_smem.start()
    to_smem.wait()

    def row_copy(_, r, slot):
        return pltpu.make_async_copy(h2_ref.at[pl.ds(r, 1)], xs_ref.at[pl.ds(slot, 1)], sem_ref.at[0])

    _row_copies(dsm_ref, tm, row_copy)
    for _ in range(2):
        pltpu.make_async_copy(h2_ref, xs_ref.at[pl.ds(0, tm)], sem_ref.at[0]).wait()


def _scatter(info, h2, pad_start, zero_start, has_rows, n_used, n_slots):
    r, d = h2.shape
    nt = r // TM
    return pl.pallas_call(
        functools.partial(_scatter_kernel, first_spare=-(-2 * r // BM), n_blocks=n_slots // BM),
        grid_spec=pltpu.PrefetchScalarGridSpec(
            num_scalar_prefetch=4,
            grid=(nt,),
            in_specs=[
                pl.BlockSpec((1, SUBLANES, TM), lambda j, *_: (j, 0, 0)),
                pl.BlockSpec((TM, d), lambda j, *_: (j, 0)),
            ],
            out_specs=[
                pl.BlockSpec((1, SUBLANES, TM), lambda j, *_: (j, 0, 0)),
                pl.BlockSpec(memory_space=pl.ANY),
            ],
            scratch_shapes=[
                pltpu.SMEM((SUBLANES, TM), I32),
                pltpu.VMEM((BM, d), F32),
                pltpu.SemaphoreType.DMA((3,)),
            ],
        ),
        out_shape=[
            jax.ShapeDtypeStruct((nt, SUBLANES, TM), I32),
            jax.ShapeDtypeStruct((n_slots, d), F32),
        ],
        compiler_params=_cparams(("arbitrary",)),
        name="moe_scatter",
    )(pad_start, zero_start, has_rows, n_used, info, h2)


def _expert_kernel(be_ref, nu_ref, xs_ref, wg_ref, wu_ref, wd_ref, y_ref, wgb, wub, wdb):
    j = pl.program_id(0)

    @pl.when(j < nu_ref[0])
    def _():
        e = be_ref[j]
        prev = be_ref[jnp.maximum(j - 1, 0)]

        @pl.when(jnp.logical_or(j == 0, e != prev))
        def _():
            wgb[...] = wg_ref[0].astype(BF16)
            wub[...] = wu_ref[0].astype(BF16)
            wdb[...] = wd_ref[0].astype(BF16)

        x = xs_ref[...].astype(BF16)
        a = jnp.dot(x, wgb[...], preferred_element_type=F32)
        u = jnp.dot(x, wub[...], preferred_element_type=F32)
        hmid = (a * _sigmoid(a) * u).astype(BF16)
        y_ref[...] = jnp.dot(hmid, wdb[...], preferred_element_type=F32)

    @pl.when(j >= nu_ref[0])
    def _():
        y_ref[...] = jnp.zeros_like(y_ref)


def _experts(xs, block_exp, n_used, w_gate, w_up, w_down, layer):
    n_slots, d = xs.shape
    de = w_gate.shape[-1]
    nbm = n_slots // BM
    n_exp = w_gate.shape[1]

    def blk(j, be, nu):
        return (jnp.minimum(j, nu[0] - 1), 0)

    def wmap(j, be, nu):
        return (layer * n_exp + be[jnp.minimum(j, nu[0] - 1)], 0, 0)

    wg = w_gate.reshape((-1,) + w_gate.shape[2:])
    wu = w_up.reshape((-1,) + w_up.shape[2:])
    wd = w_down.reshape((-1,) + w_down.shape[2:])
    return pl.pallas_call(
        _expert_kernel,
        grid_spec=pltpu.PrefetchScalarGridSpec(
            num_scalar_prefetch=2,
            grid=(nbm,),
            in_specs=[
                pl.BlockSpec((BM, d), blk),
                pl.BlockSpec((1, d, de), wmap),
                pl.BlockSpec((1, d, de), wmap),
                pl.BlockSpec((1, de, d), wmap),
            ],
            out_specs=pl.BlockSpec((BM, d), lambda j, be, nu: (j, 0)),
            scratch_shapes=[
                pltpu.VMEM((d, de), BF16),
                pltpu.VMEM((d, de), BF16),
                pltpu.VMEM((de, d), BF16),
            ],
        ),
        out_shape=jax.ShapeDtypeStruct((n_slots, d), F32),
        compiler_params=_cparams(("arbitrary",)),
        name="moe_experts",
    )(block_exp, n_used, xs, wg, wu, wd)


def _combine_kernel(dest_ref, info_ref, x_ref, mods_ref, y_ref, o_ref,
                    dsm_ref, ybuf_ref, sem_ref, *, d_model, tpb, nct, nbatch, latent_only):
    j = pl.program_id(0)
    d = d_model
    tm = x_ref.shape[0]
    if latent_only:
        b = j // (tpb - nct)
        row = b
    else:
        b = j // tpb
        row = jnp.where(j % tpb < nct, nbatch, b)
    gate2 = mods_ref[0, pl.ds(row, 1), 5 * d:6 * d]

    to_smem = pltpu.make_async_copy(dest_ref.at[0], dsm_ref, sem_ref.at[1])
    to_smem.start()
    to_smem.wait()

    def row_copy(k, r, slot):
        return pltpu.make_async_copy(y_ref.at[pl.ds(slot, 1)], ybuf_ref.at[k, pl.ds(r, 1)], sem_ref.at[0])

    _row_copies(dsm_ref, tm, row_copy)
    for k in range(2):
        pltpu.make_async_copy(y_ref.at[pl.ds(0, tm)], ybuf_ref.at[k], sem_ref.at[0]).wait()

    info = info_ref[0]
    wrow = lax.broadcasted_iota(I32, (LANES, tm), 0)
    w_lanes = jnp.where(wrow == 0, lax.bitcast_convert_type(info[4:5], F32),
                        jnp.where(wrow == 1, lax.bitcast_convert_type(info[5:6], F32), 0.0))
    w_rows = w_lanes.T
    mixed = w_rows[:, 0:1] * ybuf_ref[0] + w_rows[:, 1:2] * ybuf_ref[1]
    o_ref[...] = x_ref[...] + gate2 * mixed


def _combine(dest, info, xn, mods, y, layer, geo, latent_only):
    r, d = xn.shape
    tpb, nct, nb = geo["tpb"], geo["nct"], geo["nbatch"]
    if latent_only:
        lpb = tpb - nct
        n_steps = nb * lpb
        tile = lambda j: (j // lpb) * tpb + nct + j % lpb
    else:
        n_steps = nb * tpb
        tile = lambda j: j
    return pl.pallas_call(
        functools.partial(_combine_kernel, d_model=d, tpb=tpb, nct=nct, nbatch=nb,
                          latent_only=latent_only),
        grid=(n_steps,),
        in_specs=[
            pl.BlockSpec((1, SUBLANES, TM), lambda j: (tile(j), 0, 0)),
            pl.BlockSpec((1, SUBLANES, TM), lambda j: (tile(j), 0, 0)),
            pl.BlockSpec((TM, d), lambda j: (tile(j), 0)),
            pl.BlockSpec((1,) + mods.shape[1:], lambda j: (layer, 0, 0)),
            pl.BlockSpec(memory_space=pl.ANY),
        ],
        out_specs=pl.BlockSpec((TM, d), lambda j: (j, 0)),
        out_shape=jax.ShapeDtypeStruct((n_steps * TM, d), F32),
        scratch_shapes=[
            pltpu.SMEM((SUBLANES, TM), I32),
            pltpu.VMEM((2, TM, d), F32),
            pltpu.SemaphoreType.DMA((2,)),
        ],
        compiler_params=_cparams(("arbitrary",)),
        name="moe_combine",
    )(dest, info, xn, mods, y)


def _pad_heads(w, n_heads, width):
    lead = w.shape[:-1]
    w = w.reshape(lead + (n_heads, width))
    w = jnp.pad(w, [(0, 0)] * len(lead) + [(0, 0), (0, LANES - width)])
    return w.reshape(lead + (n_heads * LANES,))


def _rope_tables(nc, nl):
    t = jnp.arange(nl)
    row_id = (t // GRID_W).astype(F32)
    col_id = (t % GRID_W).astype(F32)

    def angles(rot_dim):
        n_freq = rot_dim // 4
        inv_freq = jnp.power(ROPE_THETA, -jnp.arange(n_freq, dtype=F32) / n_freq)
        return jnp.concatenate([row_id[:, None] * inv_freq, col_id[:, None] * inv_freq], axis=-1)

    def with_ctx(tab, fill):
        return jnp.concatenate([jnp.full((nc, LANES), fill, F32), tab], axis=0)

    ag = angles(HEAD_DIM)
    one_g = jnp.ones((nl, LANES - HEAD_DIM), F32)
    cos_g = jnp.concatenate([jnp.cos(ag), jnp.cos(ag), one_g], axis=-1)
    sin_g = jnp.concatenate([-jnp.sin(ag), jnp.sin(ag), 0.0 * one_g], axis=-1)
    am = angles(MLA_ROPE_DIM)
    one_n = jnp.ones((nl, MLA_NOPE_DIM), F32)
    one_t = jnp.ones((nl, LANES - MLA_QK_DIM), F32)
    cos_m = jnp.concatenate([one_n, jnp.cos(am), jnp.cos(am), one_t], axis=-1)
    sin_m = jnp.concatenate([0.0 * one_n, -jnp.sin(am), jnp.sin(am), 0.0 * one_t], axis=-1)
    return with_ctx(cos_g, 1.0), with_ctx(sin_g, 0.0), with_ctx(cos_m, 1.0), with_ctx(sin_m, 0.0)


def _pool_band():
    t = np.arange(TM)[:, None]
    src = np.arange(POOL_EXT)[None, :] - POOL_HALO
    live = np.arange(POOL_EXT)[None, :] < TM + 2 * POOL_HALO
    mats = [((src >= t - w // 2) & (src < t + w // 2) & live) for w in POOL_WINDOWS]
    return jnp.asarray(np.stack(mats).astype(np.float32), dtype=BF16)


def _prep_params(w_in, norm1, norm2, conv_w, w_pool, pool_scale, gqa_q_norm, gqa_k_norm,
                 mla_q_norm, mla_kv_norm, mla_w_uq, mla_w_uk, mla_w_uv, mla_qk_q_norm,
                 mla_qk_k_norm, w_out, w_router, b_router, nc, nl):
    dep = w_in.shape[0]
    o = 0
    pieces = {}
    for name, n in (("conv", 3 * CONV_DIM), ("pool", POOL_DIM), ("gq", GQA_HEADS * HEAD_DIM),
                    ("gk", GQA_KV_HEADS * HEAD_DIM), ("gv", GQA_KV_HEADS * HEAD_DIM),
                    ("mq", MLA_Q_RANK), ("mkv", MLA_KV_RANK), ("mkr", MLA_ROPE_DIM)):
        pieces[name] = w_in[..., o:o + n]
        o += n
    mkr = jnp.pad(pieces["mkr"], ((0, 0), (0, 0), (MLA_NOPE_DIM, LANES - MLA_QK_DIM)))
    w_in_p = jnp.concatenate([
        pieces["conv"], pieces["pool"], _pad_heads(pieces["gq"], GQA_HEADS, HEAD_DIM),
        _pad_heads(pieces["gk"], GQA_KV_HEADS, HEAD_DIM), _pad_heads(pieces["gv"], GQA_KV_HEADS, HEAD_DIM),
        pieces["mq"], pieces["mkv"], mkr], axis=-1).astype(BF16)
    wukv = jnp.concatenate([_pad_heads(mla_w_uk, MLA_HEADS, MLA_NOPE_DIM),
                            _pad_heads(mla_w_uv, MLA_HEADS, MLA_V_DIM)], axis=-1).astype(BF16)
    eye = jnp.eye(len(POOL_WINDOWS), dtype=F32)
    wpool = jnp.einsum("gh,dgij->dgihj", eye, w_pool).reshape(dep, POOL_DIM, POOL_DIM).astype(BF16)
    cos_g, sin_g, cos_m, sin_m = _rope_tables(nc, nl)
    tri = np.triu(np.ones((TM, TM), np.float32), 1)

    def row3(a):
        return a.reshape(dep, 1, a.shape[-1])

    return {
        "w_in": w_in_p,
        "norm1": row3(norm1), "norm2": row3(norm2),
        "gq": row3(_pad_heads(gqa_q_norm * (GQA_SCALE * LOG2E), 1, HEAD_DIM)),
        "gk": row3(_pad_heads(gqa_k_norm, 1, HEAD_DIM)),
        "mqn": row3(mla_q_norm), "mkvn": row3(mla_kv_norm),
        "qkq": row3(_pad_heads(mla_qk_q_norm * (MLA_SCALE * LOG2E), 1, MLA_QK_DIM)),
        "qkk": row3(_pad_heads(mla_qk_k_norm, 1, MLA_QK_DIM)),
        "wuq": _pad_heads(mla_w_uq, MLA_HEADS, MLA_QK_DIM).astype(BF16),
        "wukv": wukv,
        "cos_g": cos_g, "sin_g": sin_g, "cos_m": cos_m, "sin_m": sin_m,
        "conv_w": jnp.pad(conv_w, ((0, 0), (0, SUBLANES - conv_w.shape[1]), (0, 0))),
        "band": _pool_band(),
        "wpool": wpool, "pscale": row3(pool_scale),
        "w_out": w_out.astype(BF16),
        "wrt": w_router.T, "br": b_router.reshape(-1, 1),
        "tri": jnp.asarray(tri, dtype=BF16),
    }


def _moe_plan(counts):
    counts = counts.astype(I32)
    padded = ((counts + BM - 1) // BM) * BM
    pad_end = jnp.cumsum(padded)
    pad_start = pad_end - padded
    return pad_start, pad_end, padded


def kernel(x, c, ctx, c_ctx, w_mod, b_mod, norm1, norm2, w_in, conv_w, w_pool, pool_scale,
           gqa_q_norm, gqa_k_norm, mla_q_norm, mla_kv_norm, mla_w_uq, mla_w_uk, mla_w_uv,
           mla_qk_q_norm, mla_qk_k_norm, w_out, w_router, b_router, w_gate, w_up, w_down):
    nb, nl, d = x.shape
    nc = ctx.shape[1]
    depth = w_mod.shape[0]
    assert nc % TM == 0 and nl % TM == 0 and nl % GRID_W == 0 and nb < SUBLANES
    na = nc + nl
    geo = {"nbatch": nb, "nc": nc, "nl": nl, "na": na, "tpb": na // TM, "nct": nc // TM}
    r = nb * na
    n_assign = 2 * r
    n_blocks = -(-n_assign // BM) + N_EXPERTS
    n_slots = n_blocks * BM

    p = _prep_params(w_in, norm1, norm2, conv_w, w_pool, pool_scale, gqa_q_norm, gqa_k_norm,
                     mla_q_norm, mla_kv_norm, mla_w_uq, mla_w_uk, mla_w_uv, mla_qk_q_norm,
                     mla_qk_k_norm, w_out, w_router, b_router, nc, nl)
    cvec = jnp.concatenate([c, c_ctx[None, :], jnp.zeros((SUBLANES - nb - 1, d), F32)], axis=0)
    mods = _adaln(cvec, w_mod, b_mod)

    xa = jnp.concatenate([ctx, x], axis=1).reshape(r, d)
    for i in range(depth):
        mix = _inproj(xa, mods, i, p, geo)
        yg = _attention(mix["qg"], mix["kg"], mix["vg"], True, geo)
        ym = _attention(mix["qm"], mix["km"], mix["vm"], False, geo)
        xn, h2, info, cnt = _outproj(xa, mix, yg, ym, mods, i, p, geo)
        pad_start, pad_end, padded = _moe_plan(cnt[:, 0])
        n_used = (pad_end[-1:] // BM).astype(I32)
        block_exp = jnp.minimum(
            jnp.searchsorted(pad_end, jnp.arange(n_blocks, dtype=I32) * BM, side="right"),
            N_EXPERTS - 1).astype(I32)
        dest, xs = _scatter(info, h2, pad_start, jnp.maximum(pad_end - BM, 0),
                            (padded > 0).astype(I32), n_used, n_slots)
        y = _experts(xs, block_exp, n_used, w_gate, w_up, w_down, i)
        xa = _combine(dest, info, xn, mods, y, i, geo, latent_only=(i == depth - 1))
    return xa.reshape(nb, nl, d)
```

```python
import functools
import math

import numpy as np
import jax
import jax.numpy as jnp
from jax import lax
from jax.experimental import pallas as pl
from jax.experimental.pallas import tpu as pltpu

F32 = jnp.float32
BF16 = jnp.bfloat16
I32 = jnp.int32

GRID_W = 64
CONV_DIM = 256
POOL_DIM = 256
POOL_WINDOWS = (2, 4, 8, 16)
HEAD_DIM = 64
GQA_HEADS = 4
GQA_KV_HEADS = 2
MLA_HEADS = 4
MLA_NOPE_DIM = 64
MLA_ROPE_DIM = 32
MLA_QK_DIM = MLA_NOPE_DIM + MLA_ROPE_DIM
MLA_V_DIM = 64
MLA_Q_RANK = 256
MLA_KV_RANK = 128
N_EXPERTS = 16
EXPERTS_PER_GROUP = 4
ROPE_THETA = 10000.0
NORM_EPS = 1e-6
LOG2E = 1.4426950408889634
GQA_SCALE = HEAD_DIM ** -0.5
MLA_SCALE = MLA_QK_DIM ** -0.5

LANES = 128
SUBLANES = 8
BF16_ROWS = 16
VMEM_LIMIT = 56 * 1024 * 1024

TM = 256
ATT_ROWS = 1024
ATT_TK = 512
BM = 512
POOL_EXT = 512
POOL_HALO = 8

ZC_B, ZC_C, ZC_U, ZC_P = 0, 256, 512, 768
ZC_GQ = 1024
ZC_GK = ZC_GQ + GQA_HEADS * LANES
ZC_GV = ZC_GK + GQA_KV_HEADS * LANES
ZC_MQ = ZC_GV + GQA_KV_HEADS * LANES
ZC_MKV = ZC_MQ + MLA_Q_RANK
ZC_MKR = ZC_MKV + MLA_KV_RANK
ZC_END = ZC_MKR + LANES

HIGHEST = lax.Precision.HIGHEST


def _cparams(sem, vmem=VMEM_LIMIT):
    return pltpu.CompilerParams(dimension_semantics=sem, vmem_limit_bytes=vmem)


def _sigmoid(v):
    return 1.0 / (1.0 + jnp.exp(-v))


def _adaln_kernel(c_ref, w_ref, b_ref, o_ref):
    c = c_ref[...]
    s = c * _sigmoid(c)
    o_ref[0] = jnp.dot(s, w_ref[0], preferred_element_type=F32, precision=HIGHEST) + b_ref[0]


def _adaln(cvec, w_mod, b_mod):
    depth, d, n6 = w_mod.shape
    tn = 1536 if n6 % 1536 == 0 else n6
    rows = cvec.shape[0]
    return pl.pallas_call(
        _adaln_kernel,
        grid=(depth, n6 // tn),
        in_specs=[
            pl.BlockSpec((rows, d), lambda i, n: (0, 0)),
            pl.BlockSpec((1, d, tn), lambda i, n: (i, 0, n)),
            pl.BlockSpec((1, 1, tn), lambda i, n: (i, 0, n)),
        ],
        out_specs=pl.BlockSpec((1, rows, tn), lambda i, n: (i, 0, n)),
        out_shape=jax.ShapeDtypeStruct((depth, rows, n6), F32),
        compiler_params=_cparams(("arbitrary", "arbitrary")),
        name="adaln",
    )(cvec, w_mod, b_mod.reshape(depth, 1, n6))


def _tile_geometry(j, geo):
    lt, nct, n_lat = geo["lt"], geo["nct"], geo["n_lat_tiles"]
    is_ctx = j >= n_lat
    jc = j - n_lat
    sample = jnp.where(is_ctx, jc // nct, j // lt)
    jt = jnp.where(is_ctx, jc % nct, j % lt)
    return {
        "is_ctx": is_ctx,
        "mod_row": jnp.where(is_ctx, geo["nbatch"], sample),
        "first": jt == 0,
        "last": jt == jnp.where(is_ctx, nct, lt) - 1,
        "pos0": jt * TM,
        "seg_len": jnp.where(is_ctx, geo["nc"], geo["nl"]),
    }


def _norm_rope(slab, gain, cos, sin, n_valid, first_half, half):
    ms = jnp.sum(slab * slab, axis=-1, keepdims=True) * (1.0 / n_valid)
    y = slab * lax.rsqrt(ms + NORM_EPS) * gain
    partner = jnp.where(first_half, pltpu.roll(y, LANES - half, 1), pltpu.roll(y, half, 1))
    return y * cos + partner * sin


def _inproj_kernel(x_ref, mods_ref, n1_ref, cg_ref, sg_ref, cm_ref, sm_ref, win_ref,
                   gq_ref, gk_ref, mqn_ref, mkvn_ref, qkq_ref, qkk_ref, wuq_ref, wukv_ref,
                   cb_ref, cv_ref, zp_ref, qg_ref, kg_ref, vg_ref, qm_ref, km_ref, vm_ref,
                   *, d_model, geo):
    d = d_model
    row = _tile_geometry(pl.program_id(0), geo)["mod_row"]
    shift = mods_ref[0, pl.ds(row, 1), 0:d]
    scale = mods_ref[0, pl.ds(row, 1), d:2 * d]
    x = x_ref[...]
    ms = jnp.mean(x * x, axis=-1, keepdims=True)
    h = (x * lax.rsqrt(ms + NORM_EPS)) * (n1_ref[0] * (1.0 + scale)) + shift
    z = jnp.dot(h.astype(BF16), win_ref[0], preferred_element_type=F32)

    cb_ref[...] = z[:, ZC_B:ZC_B + CONV_DIM].astype(BF16)
    cv_ref[...] = (z[:, ZC_C:ZC_C + CONV_DIM] * z[:, ZC_U:ZC_U + CONV_DIM]).astype(BF16)
    zp_ref[...] = z[:, ZC_P:ZC_P + POOL_DIM]

    tm = x.shape[0]
    lane = lax.broadcasted_iota(I32, (tm, LANES), 1)
    ones_col = lane == HEAD_DIM
    cg, sg = cg_ref[...], sg_ref[...]
    g_first = lane < HEAD_DIM // 2
    for hd in range(GQA_HEADS):
        slab = z[:, ZC_GQ + hd * LANES:ZC_GQ + (hd + 1) * LANES]
        qg_ref[:, hd * LANES:(hd + 1) * LANES] = _norm_rope(
            slab, gq_ref[0], cg, sg, HEAD_DIM, g_first, HEAD_DIM // 2).astype(BF16)
    for hd in range(GQA_KV_HEADS):
        slab = z[:, ZC_GK + hd * LANES:ZC_GK + (hd + 1) * LANES]
        kg_ref[:, hd * LANES:(hd + 1) * LANES] = _norm_rope(
            slab, gk_ref[0], cg, sg, HEAD_DIM, g_first, HEAD_DIM // 2).astype(BF16)
        vs = z[:, ZC_GV + hd * LANES:ZC_GV + (hd + 1) * LANES]
        vg_ref[:, hd * LANES:(hd + 1) * LANES] = jnp.where(ones_col, 1.0, vs).astype(BF16)

    cm, sm = cm_ref[...], sm_ref[...]
    m_first = lane < MLA_NOPE_DIM + MLA_ROPE_DIM // 2
    zq = z[:, ZC_MQ:ZC_MQ + MLA_Q_RANK]
    cq = zq * lax.rsqrt(jnp.mean(zq * zq, axis=-1, keepdims=True) + NORM_EPS) * mqn_ref[0]
    qpre = jnp.dot(cq.astype(BF16), wuq_ref[0], preferred_element_type=F32)
    zkv = z[:, ZC_MKV:ZC_MKV + MLA_KV_RANK]
    ckv = zkv * lax.rsqrt(jnp.mean(zkv * zkv, axis=-1, keepdims=True) + NORM_EPS) * mkvn_ref[0]
    kvp = jnp.dot(ckv.astype(BF16), wukv_ref[0], preferred_element_type=F32)
    zkr = z[:, ZC_MKR:ZC_MKR + LANES]
    for hd in range(MLA_HEADS):
        sl = slice(hd * LANES, (hd + 1) * LANES)
        qm_ref[:, sl] = _norm_rope(qpre[:, sl], qkq_ref[0], cm, sm, MLA_QK_DIM,
                                   m_first, MLA_ROPE_DIM // 2).astype(BF16)
        km_ref[:, sl] = _norm_rope(kvp[:, sl] + zkr, qkk_ref[0], cm, sm, MLA_QK_DIM,
                                   m_first, MLA_ROPE_DIM // 2).astype(BF16)
        vs = kvp[:, MLA_HEADS * LANES + hd * LANES:MLA_HEADS * LANES + (hd + 1) * LANES]
        vm_ref[:, sl] = jnp.where(ones_col, 1.0, vs).astype(BF16)


def _inproj(xa, mods, layer, p, geo):
    r, d = xa.shape
    lt, n_lat = geo["lt"], geo["n_lat_tiles"]
    rowmap = lambda j: (j, 0)
    posmap = lambda j: (jnp.where(j >= n_lat, lt, j % lt), 0)
    lay3 = lambda j: (layer, 0, 0)

    def full3(a):
        return pl.BlockSpec((1,) + a.shape[1:], lay3)

    outs = [("cb", CONV_DIM, BF16), ("cv", CONV_DIM, BF16), ("zp", POOL_DIM, F32),
            ("qg", GQA_HEADS * LANES, BF16), ("kg", GQA_KV_HEADS * LANES, BF16),
            ("vg", GQA_KV_HEADS * LANES, BF16), ("qm", MLA_HEADS * LANES, BF16),
            ("km", MLA_HEADS * LANES, BF16), ("vm", MLA_HEADS * LANES, BF16)]
    res = pl.pallas_call(
        functools.partial(_inproj_kernel, d_model=d, geo=geo),
        grid=(geo["n_tiles"],),
        in_specs=[
            pl.BlockSpec((TM, d), rowmap),
            full3(mods), full3(p["norm1"]),
            pl.BlockSpec((TM, LANES), posmap), pl.BlockSpec((TM, LANES), posmap),
            pl.BlockSpec((TM, LANES), posmap), pl.BlockSpec((TM, LANES), posmap),
            full3(p["w_in"]), full3(p["gq"]), full3(p["gk"]), full3(p["mqn"]), full3(p["mkvn"]),
            full3(p["qkq"]), full3(p["qkk"]), full3(p["wuq"]), full3(p["wukv"]),
        ],
        out_specs=[pl.BlockSpec((TM, w), rowmap) for _, w, _ in outs],
        out_shape=[jax.ShapeDtypeStruct((r, w), dt) for _, w, dt in outs],
        compiler_params=_cparams(("arbitrary",)),
        name="inproj",
    )(xa, mods, p["norm1"], p["cos_g"], p["sin_g"], p["cos_m"], p["sin_m"], p["w_in"],
      p["gq"], p["gk"], p["mqn"], p["mkvn"], p["qkq"], p["qkk"], p["wuq"], p["wukv"])
    return {name: a for (name, _, _), a in zip(outs, res)}


def _flash_step(q, k, v, m_ref, acc_ref):
    s = lax.dot_general(q, k, (((1,), (1,)), ((), ())), preferred_element_type=F32)
    m_old = m_ref[...]
    m_new = jnp.maximum(m_old, jnp.max(s, axis=-1, keepdims=True))
    pmat = jnp.exp2(s - m_new).astype(BF16)
    acc_ref[...] = jnp.exp2(m_old - m_new) * acc_ref[...] + jnp.dot(pmat, v, preferred_element_type=F32)
    m_ref[...] = m_new


def _attn_kernel(q_ref, kc_ref, vc_ref, kl_ref, vl_ref, o_ref, m_ref, acc_ref,
                 *, shared_kv, n_lat_steps, tk):
    tq = q_ref.shape[0]
    if shared_kv:
        streams = [(jnp.concatenate([q_ref[:, 0:LANES], q_ref[:, LANES:2 * LANES]], axis=0), 0)]
    else:
        streams = [(q_ref[:, 0:LANES], 0), (q_ref[:, LANES:2 * LANES], LANES)]
    for si, (q, col) in enumerate(streams):
        m_s, acc_s = m_ref.at[si], acc_ref.at[si]
        m_s[...] = jnp.full(m_s.shape, -1e30, F32)
        acc_s[...] = jnp.zeros(acc_s.shape, F32)
        _flash_step(q, kc_ref[:, col:col + LANES], vc_ref[:, col:col + LANES], m_s, acc_s)
        if n_lat_steps:
            def step(i, carry, q=q, col=col, m_s=m_s, acc_s=acc_s):
                start = pl.multiple_of(i * tk, tk)
                _flash_step(q, kl_ref[pl.ds(start, tk), col:col + LANES],
                            vl_ref[pl.ds(start, tk), col:col + LANES], m_s, acc_s)
                return carry
            lax.fori_loop(0, n_lat_steps, step, 0)
    if shared_kv:
        a0, a1 = acc_ref[0, 0:tq], acc_ref[0, tq:2 * tq]
    else:
        a0, a1 = acc_ref[0], acc_ref[1]
    o0 = a0 * (1.0 / a0[:, HEAD_DIM:HEAD_DIM + 1])
    o1 = a1 * (1.0 / a1[:, HEAD_DIM:HEAD_DIM + 1])
    lane = lax.broadcasted_iota(I32, (tq, LANES), 1)
    o_ref[...] = jnp.where(lane < HEAD_DIM, o0, pltpu.roll(o1, HEAD_DIM, 1)).astype(BF16)


def _attention(q, k, v, shared_kv, geo, ctx_queries):
    nb, nc, nl = geo["nbatch"], geo["nc"], geo["nl"]
    kw = LANES if shared_kv else 2 * LANES
    ctx_blk0 = nb * nl // nc
    ctx_spec = pl.BlockSpec((nc, kw), lambda b, g, t: (ctx_blk0 + b, g))
    tk = min(ATT_TK, nl)
    if ctx_queries:
        tq, q_per, q_blk0, n_lat_steps = nc, 1, ctx_blk0, 0
        lat_spec = ctx_spec
    else:
        tq = min(ATT_ROWS, nl) // (2 if shared_kv else 1)
        q_per, q_blk0, n_lat_steps = nl // tq, 0, nl // tk
        lat_spec = pl.BlockSpec((nl, kw), lambda b, g, t: (b, g))
    n_streams, rows = (1, 2 * tq) if shared_kv else (2, tq)
    return pl.pallas_call(
        functools.partial(_attn_kernel, shared_kv=shared_kv, n_lat_steps=n_lat_steps, tk=tk),
        grid=(nb, 2, q_per),
        in_specs=[
            pl.BlockSpec((tq, 2 * LANES), lambda b, g, t: (q_blk0 + b * q_per + t, g)),
            ctx_spec, ctx_spec, lat_spec, lat_spec,
        ],
        out_specs=pl.BlockSpec((tq, LANES), lambda b, g, t: (b * q_per + t, g)),
        out_shape=jax.ShapeDtypeStruct((nb * q_per * tq, 2 * LANES), BF16),
        scratch_shapes=[pltpu.VMEM((n_streams, rows, 1), F32),
                        pltpu.VMEM((n_streams, rows, LANES), F32)],
        compiler_params=_cparams(("arbitrary", "arbitrary", "arbitrary")),
        name=("attn_gqa" if shared_kv else "attn_mla") + ("_ctx" if ctx_queries else ""),
    )(q, k, v, k, v)


def _top2_sum(a, b, c, d):
    hi_ab, lo_ab = jnp.maximum(a, b), jnp.minimum(a, b)
    hi_cd, lo_cd = jnp.maximum(c, d), jnp.minimum(c, d)
    first = jnp.maximum(hi_ab, hi_cd)
    second = jnp.maximum(jnp.minimum(hi_ab, hi_cd), jnp.maximum(lo_ab, lo_cd))
    return first + second


def _outproj_kernel(x_ref, cb_ref, cv_ref, cvp_ref, cvn_ref, zp_ref, zpp_ref, zpn_ref,
                    ygl_ref, ygc_ref, yml_ref, ymc_ref,
                    mods_ref, convw_ref, band_ref, wpool_ref, pscale_ref,
                    wout_ref, n2_ref, wrt_ref, br_ref, tri_ref,
                    xo_ref, h2_ref, info_ref, cnt_ref,
                    *, d_model, geo):
    tile = pl.program_id(0)
    d = d_model
    tm = x_ref.shape[0]
    tg = _tile_geometry(tile, geo)
    is_ctx = tg["is_ctx"]
    keep_prev = jnp.where(tg["first"], 0.0, 1.0)
    keep_next = jnp.where(tg["last"], 0.0, 1.0)
    row = tg["mod_row"]
    gate1 = mods_ref[0, pl.ds(row, 1), 2 * d:3 * d]
    shift2 = mods_ref[0, pl.ds(row, 1), 3 * d:4 * d]
    scale2 = mods_ref[0, pl.ds(row, 1), 4 * d:5 * d]

    v = cv_ref[...].astype(F32)
    prev_row = cvp_ref[...].astype(F32)[BF16_ROWS - 1:BF16_ROWS] * keep_prev
    next_row = cvn_ref[...].astype(F32)[0:1] * keep_next
    rid = lax.broadcasted_iota(I32, (tm, CONV_DIM), 0)
    vm1 = jnp.where(rid == 0, prev_row, pltpu.roll(v, 1, 0))
    vp1 = jnp.where(rid == tm - 1, next_row, pltpu.roll(v, tm - 1, 0))
    cw = convw_ref[0]
    y_conv = cb_ref[...].astype(F32) * (vm1 * cw[0:1] + v * cw[1:2] + vp1 * cw[2:3])

    zp = zp_ref[...]
    ext = jnp.concatenate(
        [zpp_ref[...] * keep_prev, zp, zpn_ref[...] * keep_next,
         jnp.zeros((POOL_EXT - tm - 2 * POOL_HALO, POOL_DIM), F32)], axis=0).astype(BF16)
    ext_a, ext_b = ext[:, 0:LANES], ext[:, LANES:2 * LANES]
    lane = lax.broadcasted_iota(I32, (tm, LANES), 1)
    low = lane < POOL_DIM // 4
    sum_a = jnp.where(low, jnp.dot(band_ref[0], ext_a, preferred_element_type=F32),
                      jnp.dot(band_ref[1], ext_a, preferred_element_type=F32))
    sum_b = jnp.where(low, jnp.dot(band_ref[2], ext_b, preferred_element_type=F32),
                      jnp.dot(band_ref[3], ext_b, preferred_element_type=F32))
    sums = jnp.concatenate([sum_a, sum_b], axis=1)
    lane_p = lax.broadcasted_iota(I32, (tm, POOL_DIM), 1)
    half_w = jnp.left_shift(1, jnp.right_shift(lane_p, int(math.log2(POOL_DIM // 4))))
    pos = tg["pos0"] + rid
    cnt = (jnp.minimum(pos + half_w, tg["seg_len"]) - jnp.maximum(pos - half_w, 0)).astype(F32)
    dlt = sums / cnt - zp
    y_pool = jnp.dot(dlt.astype(BF16), wpool_ref[0], preferred_element_type=F32) * pscale_ref[0]

    y_gqa = jnp.where(is_ctx, ygc_ref[...], ygl_ref[...])
    y_mla = jnp.where(is_ctx, ymc_ref[...], yml_ref[...])
    ycat = jnp.concatenate([y_conv.astype(BF16), y_pool.astype(BF16), y_gqa, y_mla], axis=1)
    y = jnp.dot(ycat, wout_ref[0], preferred_element_type=F32)
    xn = x_ref[...] + gate1 * y
    xo_ref[...] = xn
    ms = jnp.mean(xn * xn, axis=-1, keepdims=True)
    h2 = (xn * lax.rsqrt(ms + NORM_EPS)) * (n2_ref[0] * (1.0 + scale2)) + shift2
    h2_ref[...] = h2

    logits = lax.dot_general(wrt_ref[...], h2, (((1,), (1,)), ((), ())),
                             preferred_element_type=F32, precision=HIGHEST)
    scores = _sigmoid(logits)
    sel = scores + br_ref[...]
    epg = EXPERTS_PER_GROUP
    n_groups = N_EXPERTS // epg
    srow = [sel[e:e + 1] for e in range(N_EXPERTS)]
    crow = [scores[e:e + 1] for e in range(N_EXPERTS)]
    gscore = [_top2_sum(*srow[g * epg:(g + 1) * epg]) for g in range(n_groups)]
    gbest = jnp.zeros_like(gscore[0]).astype(I32)
    best = gscore[0]
    for g in range(1, n_groups):
        upd = gscore[g] > best
        gbest = jnp.where(upd, g, gbest)
        best = jnp.where(upd, gscore[g], best)

    def pick(rows_, j):
        out = rows_[(n_groups - 1) * epg + j]
        for g in range(n_groups - 2, -1, -1):
            out = jnp.where(gbest == g, rows_[g * epg + j], out)
        return out

    sv = [pick(srow, j) for j in range(epg)]
    cv_ = [pick(crow, j) for j in range(epg)]
    i1 = jnp.zeros_like(gbest)
    b1 = sv[0]
    for j in range(1, epg):
        upd = sv[j] > b1
        i1 = jnp.where(upd, j, i1)
        b1 = jnp.where(upd, sv[j], b1)
    i2 = jnp.zeros_like(gbest)
    b2 = jnp.full_like(b1, -jnp.inf)
    for j in range(epg):
        upd = jnp.logical_and(i1 != j, sv[j] > b2)
        i2 = jnp.where(upd, j, i2)
        b2 = jnp.where(upd, sv[j], b2)
    s1 = cv_[epg - 1]
    s2 = cv_[epg - 1]
    for j in range(epg - 2, -1, -1):
        s1 = jnp.where(i1 == j, cv_[j], s1)
        s2 = jnp.where(i2 == j, cv_[j], s2)
    inv = 1.0 / (s1 + s2)
    e1 = gbest * epg + i1
    e2 = gbest * epg + i2

    @pl.when(tile == 0)
    def _():
        cnt_ref[...] = jnp.zeros_like(cnt_ref)

    erow = lax.broadcasted_iota(I32, (N_EXPERTS, tm), 0)
    hit1 = erow == e1
    hit2 = erow == e2
    onehot = jnp.where(hit1, 1.0, 0.0) + jnp.where(hit2, 1.0, 0.0)
    before = jnp.dot(onehot.astype(BF16), tri_ref[...], preferred_element_type=F32)
    tot = cnt_ref[:, 0:1] + before
    rank1 = jnp.sum(jnp.where(hit1, tot, 0.0), axis=0, keepdims=True).astype(I32)
    rank2 = jnp.sum(jnp.where(hit2, tot, 0.0), axis=0, keepdims=True).astype(I32)
    cnt_ref[...] = cnt_ref[...] + jnp.sum(onehot, axis=1, keepdims=True)

    w1 = lax.bitcast_convert_type(s1 * inv, I32)
    w2 = lax.bitcast_convert_type(s2 * inv, I32)
    irow = lax.broadcasted_iota(I32, (SUBLANES, tm), 0)
    info = jnp.where(irow == 0, e1, jnp.where(irow == 1, e2, jnp.where(
        irow == 2, rank1, jnp.where(irow == 3, rank2, jnp.where(
            irow == 4, w1, jnp.where(irow == 5, w2, 0))))))
    info_ref[0] = info


def _outproj(xa, mix, yg, ygc, ym, ymc, mods, layer, p, geo, n_tiles):
    d = xa.shape[1]
    nt_all, n_lat = geo["n_tiles"], geo["n_lat_tiles"]
    r = n_tiles * TM
    nt = n_tiles
    rowmap = lambda j: (j, 0)
    latmap = lambda j: (jnp.minimum(j, n_lat - 1), 0)
    ctxmap = lambda j: (jnp.clip(j - n_lat, 0, ygc.shape[0] // TM - 1), 0)
    lay3 = lambda j: (layer, 0, 0)
    c0 = lambda j: (0, 0)
    c3 = lambda j: (0, 0, 0)
    bf_blocks = TM // BF16_ROWS
    f_blocks = TM // SUBLANES

    def full3(a):
        return pl.BlockSpec((1,) + a.shape[1:], lay3)

    return pl.pallas_call(
        functools.partial(_outproj_kernel, d_model=d, geo=geo),
        grid=(n_tiles,),
        in_specs=[
            pl.BlockSpec((TM, d), rowmap),
            pl.BlockSpec((TM, CONV_DIM), rowmap),
            pl.BlockSpec((TM, CONV_DIM), rowmap),
            pl.BlockSpec((BF16_ROWS, CONV_DIM), lambda j: (jnp.maximum(j * bf_blocks - 1, 0), 0)),
            pl.BlockSpec((BF16_ROWS, CONV_DIM),
                         lambda j: (jnp.minimum((j + 1) * bf_blocks, nt_all * bf_blocks - 1), 0)),
            pl.BlockSpec((TM, POOL_DIM), rowmap),
            pl.BlockSpec((SUBLANES, POOL_DIM), lambda j: (jnp.maximum(j * f_blocks - 1, 0), 0)),
            pl.BlockSpec((SUBLANES, POOL_DIM),
                         lambda j: (jnp.minimum((j + 1) * f_blocks, nt_all * f_blocks - 1), 0)),
            pl.BlockSpec((TM, 2 * LANES), latmap),
            pl.BlockSpec((TM, 2 * LANES), ctxmap),
            pl.BlockSpec((TM, 2 * LANES), latmap),
            pl.BlockSpec((TM, 2 * LANES), ctxmap),
            full3(mods), full3(p["conv_w"]),
            pl.BlockSpec(p["band"].shape, c3),
            full3(p["wpool"]), full3(p["pscale"]), full3(p["w_out"]), full3(p["norm2"]),
            pl.BlockSpec(p["wrt"].shape, c0), pl.BlockSpec(p["br"].shape, c0),
            pl.BlockSpec(p["tri"].shape, c0),
        ],
        out_specs=[
            pl.BlockSpec((TM, d), rowmap),
            pl.BlockSpec((TM, d), rowmap),
            pl.BlockSpec((1, SUBLANES, TM), lambda j: (j, 0, 0)),
            pl.BlockSpec((N_EXPERTS, LANES), c0),
        ],
        out_shape=[
            jax.ShapeDtypeStruct((r, d), F32),
            jax.ShapeDtypeStruct((r, d), F32),
            jax.ShapeDtypeStruct((nt, SUBLANES, TM), I32),
            jax.ShapeDtypeStruct((N_EXPERTS, LANES), F32),
        ],
        compiler_params=_cparams(("arbitrary",)),
        name="outproj",
    )(xa, mix["cb"], mix["cv"], mix["cv"], mix["cv"], mix["zp"], mix["zp"], mix["zp"],
      yg, ygc, ym, ymc, mods, p["conv_w"], p["band"], p["wpool"], p["pscale"], p["w_out"],
      p["norm2"], p["wrt"], p["br"], p["tri"])


def _row_copies(idx_ref, n_rows, make):
    def body(r, carry):
        make(0, r, idx_ref[0, r]).start()
        make(1, r, idx_ref[1, r]).start()
        return carry
    lax.fori_loop(0, n_rows, body, 0)


def _scatter_kernel(ps_ref, zs_ref, has_ref, nu_ref, info_ref, h2_ref, dest_ref, xs_ref,
                    dsm_ref, zbuf_ref, sem_ref, *, first_spare, n_blocks):
    j = pl.program_id(0)
    tm = h2_ref.shape[0]

    @pl.when(j == 0)
    def _():
        zbuf_ref[...] = jnp.zeros_like(zbuf_ref)

        def zero_copy(start):
            return pltpu.make_async_copy(
                zbuf_ref, xs_ref.at[pl.ds(pl.multiple_of(start, BM), BM)], sem_ref.at[1])

        for act in ("start", "wait"):
            for e in range(N_EXPERTS):
                @pl.when(has_ref[e] > 0)
                def _():
                    getattr(zero_copy(zs_ref[e]), act)()
            for jb in range(first_spare, n_blocks):
                @pl.when(jb >= nu_ref[0])
                def _():
                    getattr(zero_copy(jb * BM), act)()

    info = info_ref[0]
    e1, e2 = info[0:1], info[1:2]
    d1, d2 = info[2:3], info[3:4]
    for e in range(N_EXPERTS):
        d1 = d1 + jnp.where(e1 == e, ps_ref[e], 0)
        d2 = d2 + jnp.where(e2 == e, ps_ref[e], 0)
    irow = lax.broadcasted_iota(I32, (SUBLANES, tm), 0)
    dest_ref[0] = jnp.where(irow == 0, d1, jnp.where(irow == 1, d2, 0))
    to_smem = pltpu.make_async_copy(dest_ref.at[0], dsm_ref, sem_ref.at[2])
    to_smem.start()
    to_smem.wait()

    def row_copy(_, r, slot):
        return pltpu.make_async_copy(h2_ref.at[pl.ds(r, 1)], xs_ref.at[pl.ds(slot, 1)], sem_ref.at[0])

    _row_copies(dsm_ref, tm, row_copy)
    for _ in range(2):
        pltpu.make_async_copy(h2_ref, xs_ref.at[pl.ds(0, tm)], sem_ref.at[0]).wait()


def _scatter(info, h2, pad_start, zero_start, has_rows, n_used, n_slots):
    r, d = h2.shape
    nt = r // TM
    return pl.pallas_call(
        functools.partial(_scatter_kernel, first_spare=-(-2 * r // BM), n_blocks=n_slots // BM),
        grid_spec=pltpu.PrefetchScalarGridSpec(
            num_scalar_prefetch=4,
            grid=(nt,),
            in_specs=[
                pl.BlockSpec((1, SUBLANES, TM), lambda j, *_: (j, 0, 0)),
                pl.BlockSpec((TM, d), lambda j, *_: (j, 0)),
            ],
            out_specs=[
                pl.BlockSpec((1, SUBLANES, TM), lambda j, *_: (j, 0, 0)),
                pl.BlockSpec(memory_space=pl.ANY),
            ],
            scratch_shapes=[
                pltpu.SMEM((SUBLANES, TM), I32),
                pltpu.VMEM((BM, d), F32),
                pltpu.SemaphoreType.DMA((3,)),
            ],
        ),
        out_shape=[
            jax.ShapeDtypeStruct((nt, SUBLANES, TM), I32),
            jax.ShapeDtypeStruct((n_slots, d), F32),
        ],
        compiler_params=_cparams(("arbitrary",)),
        name="moe_scatter",
    )(pad_start, zero_start, has_rows, n_used, info, h2)


def _expert_kernel(be_ref, nu_ref, xs_ref, wg_ref, wu_ref, wd_ref, y_ref, wgb, wub, wdb):
    j = pl.program_id(0)

    @pl.when(j < nu_ref[0])
    def _():
        e = be_ref[j]
        prev = be_ref[jnp.maximum(j - 1, 0)]

        @pl.when(jnp.logical_or(j == 0, e != prev))
        def _():
            wgb[...] = wg_ref[0].astype(BF16)
            wub[...] = wu_ref[0].astype(BF16)
            wdb[...] = wd_ref[0].astype(BF16)

        x = xs_ref[...].astype(BF16)
        a = jnp.dot(x, wgb[...], preferred_element_type=F32)
        u = jnp.dot(x, wub[...], preferred_element_type=F32)
        hmid = (a * _sigmoid(a) * u).astype(BF16)
        y_ref[...] = jnp.dot(hmid, wdb[...], preferred_element_type=F32)

    @pl.when(j >= nu_ref[0])
    def _():
        y_ref[...] = jnp.zeros_like(y_ref)


def _experts(xs, block_exp, n_used, w_gate, w_up, w_down, layer):
    n_slots, d = xs.shape
    de = w_gate.shape[-1]
    nbm = n_slots // BM
    n_exp = w_gate.shape[1]

    def blk(j, be, nu):
        return (jnp.minimum(j, nu[0] - 1), 0)

    def wmap(j, be, nu):
        return (layer * n_exp + be[jnp.minimum(j, nu[0] - 1)], 0, 0)

    wg = w_gate.reshape((-1,) + w_gate.shape[2:])
    wu = w_up.reshape((-1,) + w_up.shape[2:])
    wd = w_down.reshape((-1,) + w_down.shape[2:])
    return pl.pallas_call(
        _expert_kernel,
        grid_spec=pltpu.PrefetchScalarGridSpec(
            num_scalar_prefetch=2,
            grid=(nbm,),
            in_specs=[
                pl.BlockSpec((BM, d), blk),
                pl.BlockSpec((1, d, de), wmap),
                pl.BlockSpec((1, d, de), wmap),
                pl.BlockSpec((1, de, d), wmap),
            ],
            out_specs=pl.BlockSpec((BM, d), lambda j, be, nu: (j, 0)),
            scratch_shapes=[
                pltpu.VMEM((d, de), BF16),
                pltpu.VMEM((d, de), BF16),
                pltpu.VMEM((de, d), BF16),
            ],
        ),
        out_shape=jax.ShapeDtypeStruct((n_slots, d), F32),
        compiler_params=_cparams(("arbitrary",)),
        name="moe_experts",
    )(block_exp, n_used, xs, wg, wu, wd)


def _combine_kernel(dest_ref, info_ref, x_ref, mods_ref, y_ref, o_ref,
                    dsm_ref, ybuf_ref, sem_ref, *, d_model, geo):
    d = d_model
    tm = x_ref.shape[0]
    row = _tile_geometry(pl.program_id(0), geo)["mod_row"]
    gate2 = mods_ref[0, pl.ds(row, 1), 5 * d:6 * d]

    to_smem = pltpu.make_async_copy(dest_ref.at[0], dsm_ref, sem_ref.at[1])
    to_smem.start()
    to_smem.wait()

    def row_copy(k, r, slot):
        return pltpu.make_async_copy(y_ref.at[pl.ds(slot, 1)], ybuf_ref.at[k, pl.ds(r, 1)], sem_ref.at[0])

    _row_copies(dsm_ref, tm, row_copy)
    for k in range(2):
        pltpu.make_async_copy(y_ref.at[pl.ds(0, tm)], ybuf_ref.at[k], sem_ref.at[0]).wait()

    info = info_ref[0]
    wrow = lax.broadcasted_iota(I32, (LANES, tm), 0)
    w_lanes = jnp.where(wrow == 0, lax.bitcast_convert_type(info[4:5], F32),
                        jnp.where(wrow == 1, lax.bitcast_convert_type(info[5:6], F32), 0.0))
    w_rows = w_lanes.T
    mixed = w_rows[:, 0:1] * ybuf_ref[0] + w_rows[:, 1:2] * ybuf_ref[1]
    o_ref[...] = x_ref[...] + gate2 * mixed


def _combine(dest, info, xn, mods, y, layer, geo):
    r, d = xn.shape
    n_steps = r // TM
    return pl.pallas_call(
        functools.partial(_combine_kernel, d_model=d, geo=geo),
        grid=(n_steps,),
        in_specs=[
            pl.BlockSpec((1, SUBLANES, TM), lambda j: (j, 0, 0)),
            pl.BlockSpec((1, SUBLANES, TM), lambda j: (j, 0, 0)),
            pl.BlockSpec((TM, d), lambda j: (j, 0)),
            pl.BlockSpec((1,) + mods.shape[1:], lambda j: (layer, 0, 0)),
            pl.BlockSpec(memory_space=pl.ANY),
        ],
        out_specs=pl.BlockSpec((TM, d), lambda j: (j, 0)),
        out_shape=jax.ShapeDtypeStruct((n_steps * TM, d), F32),
        scratch_shapes=[
            pltpu.SMEM((SUBLANES, TM), I32),
            pltpu.VMEM((2, TM, d), F32),
            pltpu.SemaphoreType.DMA((2,)),
        ],
        compiler_params=_cparams(("arbitrary",)),
        name="moe_combine",
    )(dest, info, xn, mods, y)


def _pad_heads(w, n_heads, width):
    lead = w.shape[:-1]
    w = w.reshape(lead + (n_heads, width))
    w = jnp.pad(w, [(0, 0)] * len(lead) + [(0, 0), (0, LANES - width)])
    return w.reshape(lead + (n_heads * LANES,))


def _rope_tables(nc, nl):
    t = jnp.arange(nl)
    row_id = (t // GRID_W).astype(F32)
    col_id = (t % GRID_W).astype(F32)

    def angles(rot_dim):
        n_freq = rot_dim // 4
        inv_freq = jnp.power(ROPE_THETA, -jnp.arange(n_freq, dtype=F32) / n_freq)
        return jnp.concatenate([row_id[:, None] * inv_freq, col_id[:, None] * inv_freq], axis=-1)

    def with_ctx(tab, fill):
        return jnp.concatenate([tab, jnp.full((TM, LANES), fill, F32)], axis=0)

    ag = angles(HEAD_DIM)
    one_g = jnp.ones((nl, LANES - HEAD_DIM), F32)
    cos_g = jnp.concatenate([jnp.cos(ag), jnp.cos(ag), one_g], axis=-1)
    sin_g = jnp.concatenate([-jnp.sin(ag), jnp.sin(ag), 0.0 * one_g], axis=-1)
    am = angles(MLA_ROPE_DIM)
    one_n = jnp.ones((nl, MLA_NOPE_DIM), F32)
    one_t = jnp.ones((nl, LANES - MLA_QK_DIM), F32)
    cos_m = jnp.concatenate([one_n, jnp.cos(am), jnp.cos(am), one_t], axis=-1)
    sin_m = jnp.concatenate([0.0 * one_n, -jnp.sin(am), jnp.sin(am), 0.0 * one_t], axis=-1)
    return with_ctx(cos_g, 1.0), with_ctx(sin_g, 0.0), with_ctx(cos_m, 1.0), with_ctx(sin_m, 0.0)


def _pool_band():
    t = np.arange(TM)[:, None]
    src = np.arange(POOL_EXT)[None, :] - POOL_HALO
    live = np.arange(POOL_EXT)[None, :] < TM + 2 * POOL_HALO
    mats = [((src >= t - w // 2) & (src < t + w // 2) & live) for w in POOL_WINDOWS]
    return jnp.asarray(np.stack(mats).astype(np.float32), dtype=BF16)


def _prep_params(w_in, norm1, norm2, conv_w, w_pool, pool_scale, gqa_q_norm, gqa_k_norm,
                 mla_q_norm, mla_kv_norm, mla_w_uq, mla_w_uk, mla_w_uv, mla_qk_q_norm,
                 mla_qk_k_norm, w_out, w_router, b_router, nc, nl):
    dep = w_in.shape[0]
    o = 0
    pieces = {}
    for name, n in (("conv", 3 * CONV_DIM), ("pool", POOL_DIM), ("gq", GQA_HEADS * HEAD_DIM),
                    ("gk", GQA_KV_HEADS * HEAD_DIM), ("gv", GQA_KV_HEADS * HEAD_DIM),
                    ("mq", MLA_Q_RANK), ("mkv", MLA_KV_RANK), ("mkr", MLA_ROPE_DIM)):
        pieces[name] = w_in[..., o:o + n]
        o += n
    mkr = jnp.pad(pieces["mkr"], ((0, 0), (0, 0), (MLA_NOPE_DIM, LANES - MLA_QK_DIM)))
    w_in_p = jnp.concatenate([
        pieces["conv"], pieces["pool"], _pad_heads(pieces["gq"], GQA_HEADS, HEAD_DIM),
        _pad_heads(pieces["gk"], GQA_KV_HEADS, HEAD_DIM), _pad_heads(pieces["gv"], GQA_KV_HEADS, HEAD_DIM),
        pieces["mq"], pieces["mkv"], mkr], axis=-1).astype(BF16)
    wukv = jnp.concatenate([_pad_heads(mla_w_uk, MLA_HEADS, MLA_NOPE_DIM),
                            _pad_heads(mla_w_uv, MLA_HEADS, MLA_V_DIM)], axis=-1).astype(BF16)
    eye = jnp.eye(len(POOL_WINDOWS), dtype=F32)
    wpool = jnp.einsum("gh,dgij->dgihj", eye, w_pool).reshape(dep, POOL_DIM, POOL_DIM).astype(BF16)
    cos_g, sin_g, cos_m, sin_m = _rope_tables(nc, nl)
    tri = np.triu(np.ones((TM, TM), np.float32), 1)

    def row3(a):
        return a.reshape(dep, 1, a.shape[-1])

    return {
        "w_in": w_in_p,
        "norm1": row3(norm1), "norm2": row3(norm2),
        "gq": row3(_pad_heads(gqa_q_norm * (GQA_SCALE * LOG2E), 1, HEAD_DIM)),
        "gk": row3(_pad_heads(gqa_k_norm, 1, HEAD_DIM)),
        "mqn": row3(mla_q_norm), "mkvn": row3(mla_kv_norm),
        "qkq": row3(_pad_heads(mla_qk_q_norm * (MLA_SCALE * LOG2E), 1, MLA_QK_DIM)),
        "qkk": row3(_pad_heads(mla_qk_k_norm, 1, MLA_QK_DIM)),
        "wuq": _pad_heads(mla_w_uq, MLA_HEADS, MLA_QK_DIM).astype(BF16),
        "wukv": wukv,
        "cos_g": cos_g, "sin_g": sin_g, "cos_m": cos_m, "sin_m": sin_m,
        "conv_w": jnp.pad(conv_w, ((0, 0), (0, SUBLANES - conv_w.shape[1]), (0, 0))),
        "band": _pool_band(),
        "wpool": wpool, "pscale": row3(pool_scale),
        "w_out": w_out.astype(BF16),
        "wrt": w_router.T, "br": b_router.reshape(-1, 1),
        "tri": jnp.asarray(tri, dtype=BF16),
    }


def _moe_plan(counts):
    counts = counts.astype(I32)
    padded = ((counts + BM - 1) // BM) * BM
    pad_end = jnp.cumsum(padded)
    pad_start = pad_end - padded
    return pad_start, pad_end, padded


def kernel(x, c, ctx, c_ctx, w_mod, b_mod, norm1, norm2, w_in, conv_w, w_pool, pool_scale,
           gqa_q_norm, gqa_k_norm, mla_q_norm, mla_kv_norm, mla_w_uq, mla_w_uk, mla_w_uv,
           mla_qk_q_norm, mla_qk_k_norm, w_out, w_router, b_router, w_gate, w_up, w_down):
    nb, nl, d = x.shape
    nc = ctx.shape[1]
    depth = w_mod.shape[0]
    assert nc % TM == 0 and nl % TM == 0 and nl % GRID_W == 0 and nb < SUBLANES
    lt, nct = nl // TM, nc // TM
    geo = {"nbatch": nb, "nc": nc, "nl": nl, "lt": lt, "nct": nct,
           "n_lat_tiles": nb * lt, "n_tiles": nb * (lt + nct)}

    p = _prep_params(w_in, norm1, norm2, conv_w, w_pool, pool_scale, gqa_q_norm, gqa_k_norm,
                     mla_q_norm, mla_kv_norm, mla_w_uq, mla_w_uk, mla_w_uv, mla_qk_q_norm,
                     mla_qk_k_norm, w_out, w_router, b_router, nc, nl)
    cvec = jnp.concatenate([c, c_ctx[None, :], jnp.zeros((SUBLANES - nb - 1, d), F32)], axis=0)
    mods = _adaln(cvec, w_mod, b_mod)

    xa = jnp.concatenate([x.reshape(nb * nl, d), ctx.reshape(nb * nc, d)], axis=0)
    for i in range(depth):
        last = i == depth - 1
        mix = _inproj(xa, mods, i, p, geo)
        yg = _attention(mix["qg"], mix["kg"], mix["vg"], True, geo, False)
        ym = _attention(mix["qm"], mix["km"], mix["vm"], False, geo, False)
        if last:
            n_tiles, ygc, ymc = geo["n_lat_tiles"], yg, ym
        else:
            n_tiles = geo["n_tiles"]
            ygc = _attention(mix["qg"], mix["kg"], mix["vg"], True, geo, True)
            ymc = _attention(mix["qm"], mix["km"], mix["vm"], False, geo, True)
        xn, h2, info, cnt = _outproj(xa, mix, yg, ygc, ym, ymc, mods, i, p, geo, n_tiles)
        n_blocks = -(-2 * n_tiles * TM // BM) + N_EXPERTS
        n_slots = n_blocks * BM
        pad_start, pad_end, padded = _moe_plan(cnt[:, 0])
        n_used = (pad_end[-1:] // BM).astype(I32)
        block_exp = jnp.minimum(
            jnp.searchsorted(pad_end, jnp.arange(n_blocks, dtype=I32) * BM, side="right"),
            N_EXPERTS - 1).astype(I32)
        dest, xs = _scatter(info, h2, pad_start, jnp.maximum(pad_end - BM, 0),
                            (padded > 0).astype(I32), n_used, n_slots)
        y = _experts(xs, block_exp, n_used, w_gate, w_up, w_down, i)
        xa = _combine(dest, info, xn, mods, y, i, geo)
    return xa.reshape(nb, nl, d)
```

```python
import functools
import math

import numpy as np
import jax
import jax.numpy as jnp
from jax import lax
from jax.experimental import pallas as pl
from jax.experimental.pallas import tpu as pltpu

F32 = jnp.float32
BF16 = jnp.bfloat16
I32 = jnp.int32

GRID_W = 64
CONV_DIM = 256
POOL_DIM = 256
POOL_WINDOWS = (2, 4, 8, 16)
HEAD_DIM = 64
GQA_HEADS = 4
GQA_KV_HEADS = 2
MLA_HEADS = 4
MLA_NOPE_DIM = 64
MLA_ROPE_DIM = 32
MLA_QK_DIM = MLA_NOPE_DIM + MLA_ROPE_DIM
MLA_V_DIM = 64
MLA_Q_RANK = 256
MLA_KV_RANK = 128
N_EXPERTS = 16
EXPERTS_PER_GROUP = 4
ROPE_THETA = 10000.0
NORM_EPS = 1e-6
LOG2E = 1.4426950408889634
GQA_SCALE = HEAD_DIM ** -0.5
MLA_SCALE = MLA_QK_DIM ** -0.5

LANES = 128
SUBLANES = 8
BF16_ROWS = 16
VMEM_LIMIT = 56 * 1024 * 1024

TM = 256
ATT_ROWS = 1024
ATT_TK = 256
BM = 512
POOL_EXT = 512
POOL_HALO = 8

ZC_B, ZC_C, ZC_U, ZC_P = 0, 256, 512, 768
ZC_GQ = 1024
ZC_GK = ZC_GQ + GQA_HEADS * LANES
ZC_GV = ZC_GK + GQA_KV_HEADS * LANES
ZC_MQ = ZC_GV + GQA_KV_HEADS * LANES
ZC_MKV = ZC_MQ + MLA_Q_RANK
ZC_MKR = ZC_MKV + MLA_KV_RANK
ZC_END = ZC_MKR + LANES

HIGHEST = lax.Precision.HIGHEST


def _cparams(sem, vmem=VMEM_LIMIT):
    return pltpu.CompilerParams(dimension_semantics=sem, vmem_limit_bytes=vmem)


def _sigmoid(v):
    return 1.0 / (1.0 + jnp.exp(-v))


def _adaln_kernel(c_ref, w_ref, b_ref, o_ref):
    c = c_ref[...]
    s = c * _sigmoid(c)
    o_ref[0] = jnp.dot(s, w_ref[0], preferred_element_type=F32, precision=HIGHEST) + b_ref[0]


def _adaln(cvec, w_mod, b_mod):
    depth, d, n6 = w_mod.shape
    tn = 1536 if n6 % 1536 == 0 else n6
    rows = cvec.shape[0]
    return pl.pallas_call(
        _adaln_kernel,
        grid=(depth, n6 // tn),
        in_specs=[
            pl.BlockSpec((rows, d), lambda i, n: (0, 0)),
            pl.BlockSpec((1, d, tn), lambda i, n: (i, 0, n)),
            pl.BlockSpec((1, 1, tn), lambda i, n: (i, 0, n)),
        ],
        out_specs=pl.BlockSpec((1, rows, tn), lambda i, n: (i, 0, n)),
        out_shape=jax.ShapeDtypeStruct((depth, rows, n6), F32),
        compiler_params=_cparams(("arbitrary", "arbitrary")),
        name="adaln",
    )(cvec, w_mod, b_mod.reshape(depth, 1, n6))


def _tile_geometry(j, geo):
    lt, nct, n_lat = geo["lt"], geo["nct"], geo["n_lat_tiles"]
    is_ctx = j >= n_lat
    jc = j - n_lat
    sample = jnp.where(is_ctx, jc // nct, j // lt)
    jt = jnp.where(is_ctx, jc % nct, j % lt)
    return {
        "is_ctx": is_ctx,
        "mod_row": jnp.where(is_ctx, geo["nbatch"], sample),
        "first": jt == 0,
        "last": jt == jnp.where(is_ctx, nct, lt) - 1,
        "pos0": jt * TM,
        "seg_len": jnp.where(is_ctx, geo["nc"], geo["nl"]),
    }


def _norm_rope(slab, gain, cos, sin, n_valid, first_half, half):
    ms = jnp.sum(slab * slab, axis=-1, keepdims=True) * (1.0 / n_valid)
    y = slab * lax.rsqrt(ms + NORM_EPS) * gain
    partner = jnp.where(first_half, pltpu.roll(y, LANES - half, 1), pltpu.roll(y, half, 1))
    return y * cos + partner * sin


def _inproj_kernel(x_ref, mods_ref, n1_ref, cg_ref, sg_ref, cm_ref, sm_ref, win_ref,
                   gq_ref, gk_ref, mqn_ref, mkvn_ref, qkq_ref, qkk_ref, wuq_ref, wukv_ref,
                   cb_ref, cv_ref, zp_ref, qg_ref, kg_ref, vg_ref, qm_ref, km_ref, vm_ref,
                   *, d_model, geo):
    d = d_model
    row = _tile_geometry(pl.program_id(0), geo)["mod_row"]
    shift = mods_ref[0, pl.ds(row, 1), 0:d]
    scale = mods_ref[0, pl.ds(row, 1), d:2 * d]
    x = x_ref[...]
    ms = jnp.mean(x * x, axis=-1, keepdims=True)
    h = (x * lax.rsqrt(ms + NORM_EPS)) * (n1_ref[0] * (1.0 + scale)) + shift
    z = jnp.dot(h.astype(BF16), win_ref[0], preferred_element_type=F32)

    cb_ref[...] = z[:, ZC_B:ZC_B + CONV_DIM].astype(BF16)
    cv_ref[...] = (z[:, ZC_C:ZC_C + CONV_DIM] * z[:, ZC_U:ZC_U + CONV_DIM]).astype(BF16)
    zp_ref[...] = z[:, ZC_P:ZC_P + POOL_DIM]

    tm = x.shape[0]
    lane = lax.broadcasted_iota(I32, (tm, LANES), 1)
    ones_col = lane == HEAD_DIM
    cg, sg = cg_ref[...], sg_ref[...]
    g_first = lane < HEAD_DIM // 2
    for hd in range(GQA_HEADS):
        slab = z[:, ZC_GQ + hd * LANES:ZC_GQ + (hd + 1) * LANES]
        qg_ref[:, hd * LANES:(hd + 1) * LANES] = _norm_rope(
            slab, gq_ref[0], cg, sg, HEAD_DIM, g_first, HEAD_DIM // 2).astype(BF16)
    for hd in range(GQA_KV_HEADS):
        slab = z[:, ZC_GK + hd * LANES:ZC_GK + (hd + 1) * LANES]
        kg_ref[:, hd * LANES:(hd + 1) * LANES] = _norm_rope(
            slab, gk_ref[0], cg, sg, HEAD_DIM, g_first, HEAD_DIM // 2).astype(BF16)
        vs = z[:, ZC_GV + hd * LANES:ZC_GV + (hd + 1) * LANES]
        vg_ref[:, hd * LANES:(hd + 1) * LANES] = jnp.where(ones_col, 1.0, vs).astype(BF16)

    cm, sm = cm_ref[...], sm_ref[...]
    m_first = lane < MLA_NOPE_DIM + MLA_ROPE_DIM // 2
    zq = z[:, ZC_MQ:ZC_MQ + MLA_Q_RANK]
    cq = zq * lax.rsqrt(jnp.mean(zq * zq, axis=-1, keepdims=True) + NORM_EPS) * mqn_ref[0]
    qpre = jnp.dot(cq.astype(BF16), wuq_ref[0], preferred_element_type=F32)
    zkv = z[:, ZC_MKV:ZC_MKV + MLA_KV_RANK]
    ckv = zkv * lax.rsqrt(jnp.mean(zkv * zkv, axis=-1, keepdims=True) + NORM_EPS) * mkvn_ref[0]
    kvp = jnp.dot(ckv.astype(BF16), wukv_ref[0], preferred_element_type=F32)
    zkr = z[:, ZC_MKR:ZC_MKR + LANES]
    for hd in range(MLA_HEADS):
        sl = slice(hd * LANES, (hd + 1) * LANES)
        qm_ref[:, sl] = _norm_rope(qpre[:, sl], qkq_ref[0], cm, sm, MLA_QK_DIM,
                                   m_first, MLA_ROPE_DIM // 2).astype(BF16)
        km_ref[:, sl] = _norm_rope(kvp[:, sl] + zkr, qkk_ref[0], cm, sm, MLA_QK_DIM,
                                   m_first, MLA_ROPE_DIM // 2).astype(BF16)
        vs = kvp[:, MLA_HEADS * LANES + hd * LANES:MLA_HEADS * LANES + (hd + 1) * LANES]
        vm_ref[:, sl] = jnp.where(ones_col, 1.0, vs).astype(BF16)


def _inproj(xa, mods, layer, p, geo):
    r, d = xa.shape
    lt, n_lat = geo["lt"], geo["n_lat_tiles"]
    rowmap = lambda j: (j, 0)
    posmap = lambda j: (jnp.where(j >= n_lat, lt, j % lt), 0)
    lay3 = lambda j: (layer, 0, 0)
    nct, tps = geo["nct"], geo["lt"] + geo["nct"]
    kv_names = ("kg", "vg", "km", "vm")

    def kvmap(j):
        jc = j - n_lat
        return (jnp.where(j >= n_lat, (jc // nct) * tps + lt + jc % nct, (j // lt) * tps + j % lt), 0)

    def full3(a):
        return pl.BlockSpec((1,) + a.shape[1:], lay3)

    outs = [("cb", CONV_DIM, BF16), ("cv", CONV_DIM, BF16), ("zp", POOL_DIM, F32),
            ("qg", GQA_HEADS * LANES, BF16), ("kg", GQA_KV_HEADS * LANES, BF16),
            ("vg", GQA_KV_HEADS * LANES, BF16), ("qm", MLA_HEADS * LANES, BF16),
            ("km", MLA_HEADS * LANES, BF16), ("vm", MLA_HEADS * LANES, BF16)]
    res = pl.pallas_call(
        functools.partial(_inproj_kernel, d_model=d, geo=geo),
        grid=(geo["n_tiles"],),
        in_specs=[
            pl.BlockSpec((TM, d), rowmap),
            full3(mods), full3(p["norm1"]),
            pl.BlockSpec((TM, LANES), posmap), pl.BlockSpec((TM, LANES), posmap),
            pl.BlockSpec((TM, LANES), posmap), pl.BlockSpec((TM, LANES), posmap),
            full3(p["w_in"]), full3(p["gq"]), full3(p["gk"]), full3(p["mqn"]), full3(p["mkvn"]),
            full3(p["qkq"]), full3(p["qkk"]), full3(p["wuq"]), full3(p["wukv"]),
        ],
        out_specs=[pl.BlockSpec((TM, w), kvmap if name in kv_names else rowmap) for name, w, _ in outs],
        out_shape=[jax.ShapeDtypeStruct((r, w), dt) for _, w, dt in outs],
        compiler_params=_cparams(("arbitrary",)),
        name="inproj",
    )(xa, mods, p["norm1"], p["cos_g"], p["sin_g"], p["cos_m"], p["sin_m"], p["w_in"],
      p["gq"], p["gk"], p["mqn"], p["mkvn"], p["qkq"], p["qkk"], p["wuq"], p["wukv"])
    return {name: a for (name, _, _), a in zip(outs, res)}


def _attn_kernel(q_ref, k_ref, v_ref, o_ref, q_st, s_buf, p_buf, a_buf, m_ref, acc_ref,
                 *, shared_kv, n_steps, tk):
    tq = q_ref.shape[0]
    if shared_kv:
        q_st[0] = jnp.concatenate([q_ref[:, 0:LANES], q_ref[:, LANES:2 * LANES]], axis=0)
        cols = [0]
    else:
        q_st[0] = q_ref[:, 0:LANES]
        q_st[1] = q_ref[:, LANES:2 * LANES]
        cols = [0, LANES]
    for si, col in enumerate(cols):
        q_s, m_s, acc_s = q_st.at[si], m_ref.at[si], acc_ref.at[si]
        m_s[...] = jnp.full(m_s.shape, -1e30, F32)
        acc_s[...] = jnp.zeros(acc_s.shape, F32)
        p_buf[1] = jnp.zeros(p_buf.shape[1:], BF16)
        a_buf[1] = jnp.ones(a_buf.shape[1:], F32)

        def key_rows(i):
            return pl.ds(pl.multiple_of(jnp.clip(i, 0, n_steps - 1) * tk, tk), tk)

        def scores(i, slot, q_s=q_s, col=col):
            k = k_ref[key_rows(i), col:col + LANES]
            s_buf[slot] = lax.dot_general(q_s[...], k, (((1,), (1,)), ((), ())),
                                          preferred_element_type=F32)

        def softmax(i, slot, m_s=m_s):
            s = s_buf[slot]
            m_old = m_s[...]
            row_max = jnp.broadcast_to(jnp.max(s, axis=-1, keepdims=True), m_old.shape)
            m_new = jnp.maximum(m_old, row_max)
            a_buf[slot] = jnp.exp2(m_old - m_new)
            m_eff = m_new + jnp.where(i < n_steps, 0.0, 1e9)
            m_wide = jnp.concatenate([m_eff] * (tk // LANES), axis=1)
            p_buf[slot] = jnp.exp2(s - m_wide).astype(BF16)
            m_s[...] = m_new

        def accumulate(i, slot, acc_s=acc_s, col=col):
            v = v_ref[key_rows(i), col:col + LANES]
            acc_s[...] = a_buf[slot] * acc_s[...] + jnp.dot(p_buf[slot], v, preferred_element_type=F32)

        scores(0, 0)

        def pair(t, carry):
            i = 2 * t
            scores(i + 1, 1)
            softmax(i, 0)
            accumulate(i - 1, 1)
            scores(i + 2, 0)
            softmax(i + 1, 1)
            accumulate(i, 0)
            return carry

        lax.fori_loop(0, (n_steps + 2) // 2, pair, 0)
    if shared_kv:
        a0, a1 = acc_ref[0, 0:tq], acc_ref[0, tq:2 * tq]
    else:
        a0, a1 = acc_ref[0], acc_ref[1]
    o0 = a0 * (1.0 / a0[:, HEAD_DIM:HEAD_DIM + 1])
    o1 = a1 * (1.0 / a1[:, HEAD_DIM:HEAD_DIM + 1])
    lane = lax.broadcasted_iota(I32, (tq, LANES), 1)
    o_ref[...] = jnp.where(lane < HEAD_DIM, o0, pltpu.roll(o1, HEAD_DIM, 1)).astype(BF16)


def _attention(q, k, v, shared_kv, geo, ctx_queries):
    nb, nc, nl = geo["nbatch"], geo["nc"], geo["nl"]
    na = nl + nc
    kw = LANES if shared_kv else 2 * LANES
    if ctx_queries:
        tq, q_per, q_blk0 = nc, 1, nb * nl // nc
        kv_rows = nc
        kv_map = lambda b, g, t: (b * (na // nc) + nl // nc, g)
    else:
        tq = min(ATT_ROWS, nl) // (2 if shared_kv else 1)
        q_per, q_blk0 = nl // tq, 0
        kv_rows = na
        kv_map = lambda b, g, t: (b, g)
    n_streams, rows = (1, 2 * tq) if shared_kv else (2, tq)
    return pl.pallas_call(
        functools.partial(_attn_kernel, shared_kv=shared_kv, n_steps=kv_rows // ATT_TK, tk=ATT_TK),
        grid=(nb, 2, q_per),
        in_specs=[
            pl.BlockSpec((tq, 2 * LANES), lambda b, g, t: (q_blk0 + b * q_per + t, g)),
            pl.BlockSpec((kv_rows, kw), kv_map),
            pl.BlockSpec((kv_rows, kw), kv_map),
        ],
        out_specs=pl.BlockSpec((tq, LANES), lambda b, g, t: (b * q_per + t, g)),
        out_shape=jax.ShapeDtypeStruct((nb * q_per * tq, 2 * LANES), BF16),
        scratch_shapes=[pltpu.VMEM((n_streams, rows, LANES), BF16),
                        pltpu.VMEM((2, rows, ATT_TK), F32),
                        pltpu.VMEM((2, rows, ATT_TK), BF16),
                        pltpu.VMEM((2, rows, LANES), F32),
                        pltpu.VMEM((n_streams, rows, LANES), F32),
                        pltpu.VMEM((n_streams, rows, LANES), F32)],
        compiler_params=_cparams(("arbitrary", "arbitrary", "arbitrary")),
        name=("attn_gqa" if shared_kv else "attn_mla") + ("_ctx" if ctx_queries else ""),
    )(q, k, v)


def _top2_sum(a, b, c, d):
    hi_ab, lo_ab = jnp.maximum(a, b), jnp.minimum(a, b)
    hi_cd, lo_cd = jnp.maximum(c, d), jnp.minimum(c, d)
    first = jnp.maximum(hi_ab, hi_cd)
    second = jnp.maximum(jnp.minimum(hi_ab, hi_cd), jnp.maximum(lo_ab, lo_cd))
    return first + second


def _outproj_kernel(x_ref, cb_ref, cv_ref, cvp_ref, cvn_ref, zp_ref, zpp_ref, zpn_ref,
                    ygl_ref, ygc_ref, yml_ref, ymc_ref,
                    mods_ref, convw_ref, band_ref, wpool_ref, pscale_ref,
                    wout_ref, n2_ref, wrt_ref, br_ref, tri_ref,
                    xo_ref, h2_ref, info_ref, cnt_ref,
                    *, d_model, geo):
    tile = pl.program_id(0)
    d = d_model
    tm = x_ref.shape[0]
    tg = _tile_geometry(tile, geo)
    is_ctx = tg["is_ctx"]
    keep_prev = jnp.where(tg["first"], 0.0, 1.0)
    keep_next = jnp.where(tg["last"], 0.0, 1.0)
    row = tg["mod_row"]
    gate1 = mods_ref[0, pl.ds(row, 1), 2 * d:3 * d]
    shift2 = mods_ref[0, pl.ds(row, 1), 3 * d:4 * d]
    scale2 = mods_ref[0, pl.ds(row, 1), 4 * d:5 * d]

    v = cv_ref[...].astype(F32)
    prev_row = cvp_ref[...].astype(F32)[BF16_ROWS - 1:BF16_ROWS] * keep_prev
    next_row = cvn_ref[...].astype(F32)[0:1] * keep_next
    rid = lax.broadcasted_iota(I32, (tm, CONV_DIM), 0)
    vm1 = jnp.where(rid == 0, prev_row, pltpu.roll(v, 1, 0))
    vp1 = jnp.where(rid == tm - 1, next_row, pltpu.roll(v, tm - 1, 0))
    cw = convw_ref[0]
    y_conv = cb_ref[...].astype(F32) * (vm1 * cw[0:1] + v * cw[1:2] + vp1 * cw[2:3])

    zp = zp_ref[...]
    ext = jnp.concatenate(
        [zpp_ref[...] * keep_prev, zp, zpn_ref[...] * keep_next,
         jnp.zeros((POOL_EXT - tm - 2 * POOL_HALO, POOL_DIM), F32)], axis=0).astype(BF16)
    ext_a, ext_b = ext[:, 0:LANES], ext[:, LANES:2 * LANES]
    lane = lax.broadcasted_iota(I32, (tm, LANES), 1)
    low = lane < POOL_DIM // 4
    sum_a = jnp.where(low, jnp.dot(band_ref[0], ext_a, preferred_element_type=F32),
                      jnp.dot(band_ref[1], ext_a, preferred_element_type=F32))
    sum_b = jnp.where(low, jnp.dot(band_ref[2], ext_b, preferred_element_type=F32),
                      jnp.dot(band_ref[3], ext_b, preferred_element_type=F32))
    sums = jnp.concatenate([sum_a, sum_b], axis=1)
    lane_p = lax.broadcasted_iota(I32, (tm, POOL_DIM), 1)
    half_w = jnp.left_shift(1, jnp.right_shift(lane_p, int(math.log2(POOL_DIM // 4))))
    pos = tg["pos0"] + rid
    cnt = (jnp.minimum(pos + half_w, tg["seg_len"]) - jnp.maximum(pos - half_w, 0)).astype(F32)
    dlt = sums / cnt - zp
    y_pool = jnp.dot(dlt.astype(BF16), wpool_ref[0], preferred_element_type=F32) * pscale_ref[0]

    y_gqa = jnp.where(is_ctx, ygc_ref[...], ygl_ref[...])
    y_mla = jnp.where(is_ctx, ymc_ref[...], yml_ref[...])
    ycat = jnp.concatenate([y_conv.astype(BF16), y_pool.astype(BF16), y_gqa, y_mla], axis=1)
    y = jnp.dot(ycat, wout_ref[0], preferred_element_type=F32)
    xn = x_ref[...] + gate1 * y
    xo_ref[...] = xn
    ms = jnp.mean(xn * xn, axis=-1, keepdims=True)
    h2 = (xn * lax.rsqrt(ms + NORM_EPS)) * (n2_ref[0] * (1.0 + scale2)) + shift2
    h2_ref[...] = h2

    logits = lax.dot_general(wrt_ref[...], h2, (((1,), (1,)), ((), ())),
                             preferred_element_type=F32, precision=HIGHEST)
    scores = _sigmoid(logits)
    sel = scores + br_ref[...]
    epg = EXPERTS_PER_GROUP
    n_groups = N_EXPERTS // epg
    srow = [sel[e:e + 1] for e in range(N_EXPERTS)]
    crow = [scores[e:e + 1] for e in range(N_EXPERTS)]
    gscore = [_top2_sum(*srow[g * epg:(g + 1) * epg]) for g in range(n_groups)]
    gbest = jnp.zeros_like(gscore[0]).astype(I32)
    best = gscore[0]
    for g in range(1, n_groups):
        upd = gscore[g] > best
        gbest = jnp.where(upd, g, gbest)
        best = jnp.where(upd, gscore[g], best)

    def pick(rows_, j):
        out = rows_[(n_groups - 1) * epg + j]
        for g in range(n_groups - 2, -1, -1):
            out = jnp.where(gbest == g, rows_[g * epg + j], out)
        return out

    sv = [pick(srow, j) for j in range(epg)]
    cv_ = [pick(crow, j) for j in range(epg)]
    i1 = jnp.zeros_like(gbest)
    b1 = sv[0]
    for j in range(1, epg):
        upd = sv[j] > b1
        i1 = jnp.where(upd, j, i1)
        b1 = jnp.where(upd, sv[j], b1)
    i2 = jnp.zeros_like(gbest)
    b2 = jnp.full_like(b1, -jnp.inf)
    for j in range(epg):
        upd = jnp.logical_and(i1 != j, sv[j] > b2)
        i2 = jnp.where(upd, j, i2)
        b2 = jnp.where(upd, sv[j], b2)
    s1 = cv_[epg - 1]
    s2 = cv_[epg - 1]
    for j in range(epg - 2, -1, -1):
        s1 = jnp.where(i1 == j, cv_[j], s1)
        s2 = jnp.where(i2 == j, cv_[j], s2)
    inv = 1.0 / (s1 + s2)
    e1 = gbest * epg + i1
    e2 = gbest * epg + i2

    @pl.when(tile == 0)
    def _():
        cnt_ref[...] = jnp.zeros_like(cnt_ref)

    erow = lax.broadcasted_iota(I32, (N_EXPERTS, tm), 0)
    hit1 = erow == e1
    hit2 = erow == e2
    onehot = jnp.where(hit1, 1.0, 0.0) + jnp.where(hit2, 1.0, 0.0)
    before = jnp.dot(onehot.astype(BF16), tri_ref[...], preferred_element_type=F32)
    tot = cnt_ref[:, 0:1] + before
    rank1 = jnp.sum(jnp.where(hit1, tot, 0.0), axis=0, keepdims=True).astype(I32)
    rank2 = jnp.sum(jnp.where(hit2, tot, 0.0), axis=0, keepdims=True).astype(I32)
    cnt_ref[...] = cnt_ref[...] + jnp.sum(onehot, axis=1, keepdims=True)

    w1 = lax.bitcast_convert_type(s1 * inv, I32)
    w2 = lax.bitcast_convert_type(s2 * inv, I32)
    irow = lax.broadcasted_iota(I32, (SUBLANES, tm), 0)
    info = jnp.where(irow == 0, e1, jnp.where(irow == 1, e2, jnp.where(
        irow == 2, rank1, jnp.where(irow == 3, rank2, jnp.where(
            irow == 4, w1, jnp.where(irow == 5, w2, 0))))))
    info_ref[0] = info


def _outproj(xa, mix, yg, ygc, ym, ymc, mods, layer, p, geo, n_tiles):
    d = xa.shape[1]
    nt_all, n_lat = geo["n_tiles"], geo["n_lat_tiles"]
    r = n_tiles * TM
    nt = n_tiles
    rowmap = lambda j: (j, 0)
    latmap = lambda j: (jnp.minimum(j, n_lat - 1), 0)
    ctxmap = lambda j: (jnp.clip(j - n_lat, 0, ygc.shape[0] // TM - 1), 0)
    lay3 = lambda j: (layer, 0, 0)
    c0 = lambda j: (0, 0)
    c3 = lambda j: (0, 0, 0)
    bf_blocks = TM // BF16_ROWS
    f_blocks = TM // SUBLANES

    def full3(a):
        return pl.BlockSpec((1,) + a.shape[1:], lay3)

    return pl.pallas_call(
        functools.partial(_outproj_kernel, d_model=d, geo=geo),
        grid=(n_tiles,),
        in_specs=[
            pl.BlockSpec((TM, d), rowmap),
            pl.BlockSpec((TM, CONV_DIM), rowmap),
            pl.BlockSpec((TM, CONV_DIM), rowmap),
            pl.BlockSpec((BF16_ROWS, CONV_DIM), lambda j: (jnp.maximum(j * bf_blocks - 1, 0), 0)),
            pl.BlockSpec((BF16_ROWS, CONV_DIM),
                         lambda j: (jnp.minimum((j + 1) * bf_blocks, nt_all * bf_blocks - 1), 0)),
            pl.BlockSpec((TM, POOL_DIM), rowmap),
            pl.BlockSpec((SUBLANES, POOL_DIM), lambda j: (jnp.maximum(j * f_blocks - 1, 0), 0)),
            pl.BlockSpec((SUBLANES, POOL_DIM),
                         lambda j: (jnp.minimum((j + 1) * f_blocks, nt_all * f_blocks - 1), 0)),
            pl.BlockSpec((TM, 2 * LANES), latmap),
            pl.BlockSpec((TM, 2 * LANES), ctxmap),
            pl.BlockSpec((TM, 2 * LANES), latmap),
            pl.BlockSpec((TM, 2 * LANES), ctxmap),
            full3(mods), full3(p["conv_w"]),
            pl.BlockSpec(p["band"].shape, c3),
            full3(p["wpool"]), full3(p["pscale"]), full3(p["w_out"]), full3(p["norm2"]),
            pl.BlockSpec(p["wrt"].shape, c0), pl.BlockSpec(p["br"].shape, c0),
            pl.BlockSpec(p["tri"].shape, c0),
        ],
        out_specs=[
            pl.BlockSpec((TM, d), rowmap),
            pl.BlockSpec((TM, d), rowmap),
            pl.BlockSpec((1, SUBLANES, TM), lambda j: (j, 0, 0)),
            pl.BlockSpec((N_EXPERTS, LANES), c0),
        ],
        out_shape=[
            jax.ShapeDtypeStruct((r, d), F32),
            jax.ShapeDtypeStruct((r, d), F32),
            jax.ShapeDtypeStruct((nt, SUBLANES, TM), I32),
            jax.ShapeDtypeStruct((N_EXPERTS, LANES), F32),
        ],
        compiler_params=_cparams(("arbitrary",)),
        name="outproj",
    )(xa, mix["cb"], mix["cv"], mix["cv"], mix["cv"], mix["zp"], mix["zp"], mix["zp"],
      yg, ygc, ym, ymc, mods, p["conv_w"], p["band"], p["wpool"], p["pscale"], p["w_out"],
      p["norm2"], p["wrt"], p["br"], p["tri"])


def _row_copies(idx_ref, n_rows, make):
    def body(r, carry):
        make(0, r, idx_ref[0, r]).start()
        make(1, r, idx_ref[1, r]).start()
        return carry
    lax.fori_loop(0, n_rows, body, 0)


def _scatter_kernel(ps_ref, zs_ref, has_ref, nu_ref, info_ref, h2_ref, dest_ref, xs_ref,
                    dsm_ref, zbuf_ref, sem_ref, *, first_spare, n_blocks):
    j = pl.program_id(0)
    tm = h2_ref.shape[0]

    @pl.when(j == 0)
    def _():
        zbuf_ref[...] = jnp.zeros_like(zbuf_ref)

        def zero_copy(start):
            return pltpu.make_async_copy(
                zbuf_ref, xs_ref.at[pl.ds(pl.multiple_of(start, BM), BM)], sem_ref.at[1])

        for act in ("start", "wait"):
            for e in range(N_EXPERTS):
                @pl.when(has_ref[e] > 0)
                def _():
                    getattr(zero_copy(zs_ref[e]), act)()
            for jb in range(first_spare, n_blocks):
                @pl.when(jb >= nu_ref[0])
                def _():
                    getattr(zero_copy(jb * BM), act)()

    info = info_ref[0]
    e1, e2 = info[0:1], info[1:2]
    d1, d2 = info[2:3], info[3:4]
    for e in range(N_EXPERTS):
        d1 = d1 + jnp.where(e1 == e, ps_ref[e], 0)
        d2 = d2 + jnp.where(e2 == e, ps_ref[e], 0)
    irow = lax.broadcasted_iota(I32, (SUBLANES, tm), 0)
    dest_ref[0] = jnp.where(irow == 0, d1, jnp.where(irow == 1, d2, 0))
    to_smem = pltpu.make_async_copy(dest_ref.at[0], dsm_ref, sem_ref.at[2])
    to_smem.start()
    to_smem.wait()

    def row_copy(_, r, slot):
        return pltpu.make_async_copy(h2_ref.at[pl.ds(r, 1)], xs_ref.at[pl.ds(slot, 1)], sem_ref.at[0])

    _row_copies(dsm_ref, tm, row_copy)
    for _ in range(2):
        pltpu.make_async_copy(h2_ref, xs_ref.at[pl.ds(0, tm)], sem_ref.at[0]).wait()


def _scatter(info, h2, pad_start, zero_start, has_rows, n_used, n_slots):
    r, d = h2.shape
    nt = r // TM
    return pl.pallas_call(
        functools.partial(_scatter_kernel, first_spare=-(-2 * r // BM), n_blocks=n_slots // BM),
        grid_spec=pltpu.PrefetchScalarGridSpec(
            num_scalar_prefetch=4,
            grid=(nt,),
            in_specs=[
                pl.BlockSpec((1, SUBLANES, TM), lambda j, *_: (j, 0, 0)),
                pl.BlockSpec((TM, d), lambda j, *_: (j, 0)),
            ],
            out_specs=[
                pl.BlockSpec((1, SUBLANES, TM), lambda j, *_: (j, 0, 0)),
                pl.BlockSpec(memory_space=pl.ANY),
            ],
            scratch_shapes=[
                pltpu.SMEM((SUBLANES, TM), I32),
                pltpu.VMEM((BM, d), F32),
                pltpu.SemaphoreType.DMA((3,)),
            ],
        ),
        out_shape=[
            jax.ShapeDtypeStruct((nt, SUBLANES, TM), I32),
            jax.ShapeDtypeStruct((n_slots, d), F32),
        ],
        compiler_params=_cparams(("arbitrary",)),
        name="moe_scatter",
    )(pad_start, zero_start, has_rows, n_used, info, h2)


def _expert_kernel(be_ref, nu_ref, xs_ref, wg_ref, wu_ref, wd_ref, y_ref, wgb, wub, wdb):
    j = pl.program_id(0)

    @pl.when(j < nu_ref[0])
    def _():
        e = be_ref[j]
        prev = be_ref[jnp.maximum(j - 1, 0)]

        @pl.when(jnp.logical_or(j == 0, e != prev))
        def _():
            wgb[...] = wg_ref[0].astype(BF16)
            wub[...] = wu_ref[0].astype(BF16)
            wdb[...] = wd_ref[0].astype(BF16)

        x = xs_ref[...].astype(BF16)
        a = jnp.dot(x, wgb[...], preferred_element_type=F32)
        u = jnp.dot(x, wub[...], preferred_element_type=F32)
        hmid = (a * _sigmoid(a) * u).astype(BF16)
        y_ref[...] = jnp.dot(hmid, wdb[...], preferred_element_type=F32)

    @pl.when(j >= nu_ref[0])
    def _():
        y_ref[...] = jnp.zeros_like(y_ref)


def _experts(xs, block_exp, n_used, w_gate, w_up, w_down, layer):
    n_slots, d = xs.shape
    de = w_gate.shape[-1]
    nbm = n_slots // BM
    n_exp = w_gate.shape[1]

    def blk(j, be, nu):
        return (jnp.minimum(j, nu[0] - 1), 0)

    def wmap(j, be, nu):
        return (layer * n_exp + be[jnp.minimum(j, nu[0] - 1)], 0, 0)

    wg = w_gate.reshape((-1,) + w_gate.shape[2:])
    wu = w_up.reshape((-1,) + w_up.shape[2:])
    wd = w_down.reshape((-1,) + w_down.shape[2:])
    return pl.pallas_call(
        _expert_kernel,
        grid_spec=pltpu.PrefetchScalarGridSpec(
            num_scalar_prefetch=2,
            grid=(nbm,),
            in_specs=[
                pl.BlockSpec((BM, d), blk),
                pl.BlockSpec((1, d, de), wmap),
                pl.BlockSpec((1, d, de), wmap),
                pl.BlockSpec((1, de, d), wmap),
            ],
            out_specs=pl.BlockSpec((BM, d), lambda j, be, nu: (j, 0)),
            scratch_shapes=[
                pltpu.VMEM((d, de), BF16),
                pltpu.VMEM((d, de), BF16),
                pltpu.VMEM((de, d), BF16),
            ],
        ),
        out_shape=jax.ShapeDtypeStruct((n_slots, d), F32),
        compiler_params=_cparams(("arbitrary",)),
        name="moe_experts",
    )(block_exp, n_used, xs, wg, wu, wd)


def _combine_kernel(dest_ref, info_ref, x_ref, mods_ref, y_ref, o_ref,
                    dsm_ref, ybuf_ref, sem_ref, *, d_model, geo):
    d = d_model
    tm = x_ref.shape[0]
    row = _tile_geometry(pl.program_id(0), geo)["mod_row"]
    gate2 = mods_ref[0, pl.ds(row, 1), 5 * d:6 * d]

    to_smem = pltpu.make_async_copy(dest_ref.at[0], dsm_ref, sem_ref.at[1])
    to_smem.start()
    to_smem.wait()

    def row_copy(k, r, slot):
        return pltpu.make_async_copy(y_ref.at[pl.ds(slot, 1)], ybuf_ref.at[k, pl.ds(r, 1)], sem_ref.at[0])

    _row_copies(dsm_ref, tm, row_copy)
    for k in range(2):
        pltpu.make_async_copy(y_ref.at[pl.ds(0, tm)], ybuf_ref.at[k], sem_ref.at[0]).wait()

    info = info_ref[0]
    wrow = lax.broadcasted_iota(I32, (LANES, tm), 0)
    w_lanes = jnp.where(wrow == 0, lax.bitcast_convert_type(info[4:5], F32),
                        jnp.where(wrow == 1, lax.bitcast_convert_type(info[5:6], F32), 0.0))
    w_rows = w_lanes.T
    mixed = w_rows[:, 0:1] * ybuf_ref[0] + w_rows[:, 1:2] * ybuf_ref[1]
    o_ref[...] = x_ref[...] + gate2 * mixed


def _combine(dest, info, xn, mods, y, layer, geo):
    r, d = xn.shape
    n_steps = r // TM
    return pl.pallas_call(
        functools.partial(_combine_kernel, d_model=d, geo=geo),
        grid=(n_steps,),
        in_specs=[
            pl.BlockSpec((1, SUBLANES, TM), lambda j: (j, 0, 0)),
            pl.BlockSpec((1, SUBLANES, TM), lambda j: (j, 0, 0)),
            pl.BlockSpec((TM, d), lambda j: (j, 0)),
            pl.BlockSpec((1,) + mods.shape[1:], lambda j: (layer, 0, 0)),
            pl.BlockSpec(memory_space=pl.ANY),
        ],
        out_specs=pl.BlockSpec((TM, d), lambda j: (j, 0)),
        out_shape=jax.ShapeDtypeStruct((n_steps * TM, d), F32),
        scratch_shapes=[
            pltpu.SMEM((SUBLANES, TM), I32),
            pltpu.VMEM((2, TM, d), F32),
            pltpu.SemaphoreType.DMA((2,)),
        ],
        compiler_params=_cparams(("arbitrary",)),
        name="moe_combine",
    )(dest, info, xn, mods, y)


def _pad_heads(w, n_heads, width):
    lead = w.shape[:-1]
    w = w.reshape(lead + (n_heads, width))
    w = jnp.pad(w, [(0, 0)] * len(lead) + [(0, 0), (0, LANES - width)])
    return w.reshape(lead + (n_heads * LANES,))


def _rope_tables(nc, nl):
    t = jnp.arange(nl)
    row_id = (t // GRID_W).astype(F32)
    col_id = (t % GRID_W).astype(F32)

    def angles(rot_dim):
        n_freq = rot_dim // 4
        inv_freq = jnp.power(ROPE_THETA, -jnp.arange(n_freq, dtype=F32) / n_freq)
        return jnp.concatenate([row_id[:, None] * inv_freq, col_id[:, None] * inv_freq], axis=-1)

    def with_ctx(tab, fill):
        return jnp.concatenate([tab, jnp.full((TM, LANES), fill, F32)], axis=0)

    ag = angles(HEAD_DIM)
    one_g = jnp.ones((nl, LANES - HEAD_DIM), F32)
    cos_g = jnp.concatenate([jnp.cos(ag), jnp.cos(ag), one_g], axis=-1)
    sin_g = jnp.concatenate([-jnp.sin(ag), jnp.sin(ag), 0.0 * one_g], axis=-1)
    am = angles(MLA_ROPE_DIM)
    one_n = jnp.ones((nl, MLA_NOPE_DIM), F32)
    one_t = jnp.ones((nl, LANES - MLA_QK_DIM), F32)
    cos_m = jnp.concatenate([one_n, jnp.cos(am), jnp.cos(am), one_t], axis=-1)
    sin_m = jnp.concatenate([0.0 * one_n, -jnp.sin(am), jnp.sin(am), 0.0 * one_t], axis=-1)
    return with_ctx(cos_g, 1.0), with_ctx(sin_g, 0.0), with_ctx(cos_m, 1.0), with_ctx(sin_m, 0.0)


def _pool_band():
    t = np.arange(TM)[:, None]
    src = np.arange(POOL_EXT)[None, :] - POOL_HALO
    live = np.arange(POOL_EXT)[None, :] < TM + 2 * POOL_HALO
    mats = [((src >= t - w // 2) & (src < t + w // 2) & live) for w in POOL_WINDOWS]
    return jnp.asarray(np.stack(mats).astype(np.float32), dtype=BF16)


def _prep_params(w_in, norm1, norm2, conv_w, w_pool, pool_scale, gqa_q_norm, gqa_k_norm,
                 mla_q_norm, mla_kv_norm, mla_w_uq, mla_w_uk, mla_w_uv, mla_qk_q_norm,
                 mla_qk_k_norm, w_out, w_router, b_router, nc, nl):
    dep = w_in.shape[0]
    o = 0
    pieces = {}
    for name, n in (("conv", 3 * CONV_DIM), ("pool", POOL_DIM), ("gq", GQA_HEADS * HEAD_DIM),
                    ("gk", GQA_KV_HEADS * HEAD_DIM), ("gv", GQA_KV_HEADS * HEAD_DIM),
                    ("mq", MLA_Q_RANK), ("mkv", MLA_KV_RANK), ("mkr", MLA_ROPE_DIM)):
        pieces[name] = w_in[..., o:o + n]
        o += n
    mkr = jnp.pad(pieces["mkr"], ((0, 0), (0, 0), (MLA_NOPE_DIM, LANES - MLA_QK_DIM)))
    w_in_p = jnp.concatenate([
        pieces["conv"], pieces["pool"], _pad_heads(pieces["gq"], GQA_HEADS, HEAD_DIM),
        _pad_heads(pieces["gk"], GQA_KV_HEADS, HEAD_DIM), _pad_heads(pieces["gv"], GQA_KV_HEADS, HEAD_DIM),
        pieces["mq"], pieces["mkv"], mkr], axis=-1).astype(BF16)
    wukv = jnp.concatenate([_pad_heads(mla_w_uk, MLA_HEADS, MLA_NOPE_DIM),
                            _pad_heads(mla_w_uv, MLA_HEADS, MLA_V_DIM)], axis=-1).astype(BF16)
    eye = jnp.eye(len(POOL_WINDOWS), dtype=F32)
    wpool = jnp.einsum("gh,dgij->dgihj", eye, w_pool).reshape(dep, POOL_DIM, POOL_DIM).astype(BF16)
    cos_g, sin_g, cos_m, sin_m = _rope_tables(nc, nl)
    tri = np.triu(np.ones((TM, TM), np.float32), 1)

    def row3(a):
        return a.reshape(dep, 1, a.shape[-1])

    return {
        "w_in": w_in_p,
        "norm1": row3(norm1), "norm2": row3(norm2),
        "gq": row3(_pad_heads(gqa_q_norm * (GQA_SCALE * LOG2E), 1, HEAD_DIM)),
        "gk": row3(_pad_heads(gqa_k_norm, 1, HEAD_DIM)),
        "mqn": row3(mla_q_norm), "mkvn": row3(mla_kv_norm),
        "qkq": row3(_pad_heads(mla_qk_q_norm * (MLA_SCALE * LOG2E), 1, MLA_QK_DIM)),
        "qkk": row3(_pad_heads(mla_qk_k_norm, 1, MLA_QK_DIM)),
        "wuq": _pad_heads(mla_w_uq, MLA_HEADS, MLA_QK_DIM).astype(BF16),
        "wukv": wukv,
        "cos_g": cos_g, "sin_g": sin_g, "cos_m": cos_m, "sin_m": sin_m,
        "conv_w": jnp.pad(conv_w, ((0, 0), (0, SUBLANES - conv_w.shape[1]), (0, 0))),
        "band": _pool_band(),
        "wpool": wpool, "pscale": row3(pool_scale),
        "w_out": w_out.astype(BF16),
        "wrt": w_router.T, "br": b_router.reshape(-1, 1),
        "tri": jnp.asarray(tri, dtype=BF16),
    }


def _moe_plan(counts):
    counts = counts.astype(I32)
    padded = ((counts + BM - 1) // BM) * BM
    pad_end = jnp.cumsum(padded)
    pad_start = pad_end - padded
    return pad_start, pad_end, padded


def kernel(x, c, ctx, c_ctx, w_mod, b_mod, norm1, norm2, w_in, conv_w, w_pool, pool_scale,
           gqa_q_norm, gqa_k_norm, mla_q_norm, mla_kv_norm, mla_w_uq, mla_w_uk, mla_w_uv,
           mla_qk_q_norm, mla_qk_k_norm, w_out, w_router, b_router, w_gate, w_up, w_down):
    nb, nl, d = x.shape
    nc = ctx.shape[1]
    depth = w_mod.shape[0]
    assert nc % TM == 0 and nl % TM == 0 and nl % GRID_W == 0 and nb < SUBLANES
    lt, nct = nl // TM, nc // TM
    geo = {"nbatch": nb, "nc": nc, "nl": nl, "lt": lt, "nct": nct,
           "n_lat_tiles": nb * lt, "n_tiles": nb * (lt + nct)}

    p = _prep_params(w_in, norm1, norm2, conv_w, w_pool, pool_scale, gqa_q_norm, gqa_k_norm,
                     mla_q_norm, mla_kv_norm, mla_w_uq, mla_w_uk, mla_w_uv, mla_qk_q_norm,
                     mla_qk_k_norm, w_out, w_router, b_router, nc, nl)
    cvec = jnp.concatenate([c, c_ctx[None, :], jnp.zeros((SUBLANES - nb - 1, d), F32)], axis=0)
    mods = _adaln(cvec, w_mod, b_mod)

    xa = jnp.concatenate([x.reshape(nb * nl, d), ctx.reshape(nb * nc, d)], axis=0)
    for i in range(depth):
        last = i == depth - 1
        mix = _inproj(xa, mods, i, p, geo)
        yg = _attention(mix["qg"], mix["kg"], mix["vg"], True, geo, False)
        ym = _attention(mix["qm"], mix["km"], mix["vm"], False, geo, False)
        if last:
            n_tiles, ygc, ymc = geo["n_lat_tiles"], yg, ym
        else:
            n_tiles = geo["n_tiles"]
            ygc = _attention(mix["qg"], mix["kg"], mix["vg"], True, geo, True)
            ymc = _attention(mix["qm"], mix["km"], mix["vm"], False, geo, True)
        xn, h2, info, cnt = _outproj(xa, mix, yg, ygc, ym, ymc, mods, i, p, geo, n_tiles)
        n_blocks = -(-2 * n_tiles * TM // BM) + N_EXPERTS
        n_slots = n_blocks * BM
        pad_start, pad_end, padded = _moe_plan(cnt[:, 0])
        n_used = (pad_end[-1:] // BM).astype(I32)
        block_exp = jnp.minimum(
            jnp.searchsorted(pad_end, jnp.arange(n_blocks, dtype=I32) * BM, side="right"),
            N_EXPERTS - 1).astype(I32)
        dest, xs = _scatter(info, h2, pad_start, jnp.maximum(pad_end - BM, 0),
                            (padded > 0).astype(I32), n_used, n_slots)
        y = _experts(xs, block_exp, n_used, w_gate, w_up, w_down, i)
        xa = _combine(dest, info, xn, mods, y, i, geo)
    return xa.reshape(nb, nl, d)
```

```python
import functools
import math

import numpy as np
import jax
import jax.numpy as jnp
from jax import lax
from jax.experimental import pallas as pl
from jax.experimental.pallas import tpu as pltpu

F32 = jnp.float32
BF16 = jnp.bfloat16
I32 = jnp.int32

GRID_W = 64
CONV_DIM = 256
POOL_DIM = 256
POOL_WINDOWS = (2, 4, 8, 16)
HEAD_DIM = 64
GQA_HEADS = 4
GQA_KV_HEADS = 2
MLA_HEADS = 4
MLA_NOPE_DIM = 64
MLA_ROPE_DIM = 32
MLA_QK_DIM = MLA_NOPE_DIM + MLA_ROPE_DIM
MLA_V_DIM = 64
MLA_Q_RANK = 256
MLA_KV_RANK = 128
N_EXPERTS = 16
EXPERTS_PER_GROUP = 4
ROPE_THETA = 10000.0
NORM_EPS = 1e-6
LOG2E = 1.4426950408889634
GQA_SCALE = HEAD_DIM ** -0.5
MLA_SCALE = MLA_QK_DIM ** -0.5

LANES = 128
SUBLANES = 8
BF16_ROWS = 16
VMEM_LIMIT = 56 * 1024 * 1024

TM = 256
ATT_ROWS = 1024
ATT_TK = 256
BM = 512
ROW_COPY_UNROLL = 8
POOL_EXT = 512
POOL_HALO = 8

ZC_B, ZC_C, ZC_U, ZC_P = 0, 256, 512, 768
ZC_GQ = 1024
ZC_GK = ZC_GQ + GQA_HEADS * LANES
ZC_GV = ZC_GK + GQA_KV_HEADS * LANES
ZC_MQ = ZC_GV + GQA_KV_HEADS * LANES
ZC_MKV = ZC_MQ + MLA_Q_RANK
ZC_MKR = ZC_MKV + MLA_KV_RANK
ZC_END = ZC_MKR + LANES

HIGHEST = lax.Precision.HIGHEST


def _cparams(sem, vmem=VMEM_LIMIT):
    return pltpu.CompilerParams(dimension_semantics=sem, vmem_limit_bytes=vmem)


def _sigmoid(v):
    return 1.0 / (1.0 + jnp.exp(-v))


def _adaln_kernel(c_ref, w_ref, b_ref, o_ref):
    c = c_ref[...]
    s = c * _sigmoid(c)
    o_ref[0] = jnp.dot(s, w_ref[0], preferred_element_type=F32, precision=HIGHEST) + b_ref[0]


def _adaln(cvec, w_mod, b_mod):
    depth, d, n6 = w_mod.shape
    tn = 1536 if n6 % 1536 == 0 else n6
    rows = cvec.shape[0]
    return pl.pallas_call(
        _adaln_kernel,
        grid=(depth, n6 // tn),
        in_specs=[
            pl.BlockSpec((rows, d), lambda i, n: (0, 0)),
            pl.BlockSpec((1, d, tn), lambda i, n: (i, 0, n)),
            pl.BlockSpec((1, 1, tn), lambda i, n: (i, 0, n)),
        ],
        out_specs=pl.BlockSpec((1, rows, tn), lambda i, n: (i, 0, n)),
        out_shape=jax.ShapeDtypeStruct((depth, rows, n6), F32),
        compiler_params=_cparams(("arbitrary", "arbitrary")),
        name="adaln",
    )(cvec, w_mod, b_mod.reshape(depth, 1, n6))


def _tile_geometry(j, geo):
    lt, nct, n_lat = geo["lt"], geo["nct"], geo["n_lat_tiles"]
    is_ctx = j >= n_lat
    jc = j - n_lat
    sample = jnp.where(is_ctx, jc // nct, j // lt)
    jt = jnp.where(is_ctx, jc % nct, j % lt)
    return {
        "is_ctx": is_ctx,
        "mod_row": jnp.where(is_ctx, geo["nbatch"], sample),
        "first": jt == 0,
        "last": jt == jnp.where(is_ctx, nct, lt) - 1,
        "pos0": jt * TM,
        "seg_len": jnp.where(is_ctx, geo["nc"], geo["nl"]),
    }


def _norm_rope(slab, gain, cos, sin, n_valid, first_half, half):
    ms = jnp.sum(slab * slab, axis=-1, keepdims=True) * (1.0 / n_valid)
    y = slab * lax.rsqrt(ms + NORM_EPS) * gain
    partner = jnp.where(first_half, pltpu.roll(y, LANES - half, 1), pltpu.roll(y, half, 1))
    return y * cos + partner * sin


def _inproj_kernel(x_ref, mods_ref, n1_ref, cg_ref, sg_ref, cm_ref, sm_ref, win_ref,
                   gq_ref, gk_ref, mqn_ref, mkvn_ref, qkq_ref, qkk_ref, wuq_ref, wukv_ref,
                   cb_ref, cv_ref, zp_ref, qg_ref, kg_ref, vg_ref, qm_ref, km_ref, vm_ref,
                   *, d_model, geo):
    d = d_model
    row = _tile_geometry(pl.program_id(0), geo)["mod_row"]
    shift = mods_ref[0, pl.ds(row, 1), 0:d]
    scale = mods_ref[0, pl.ds(row, 1), d:2 * d]
    x = x_ref[...]
    ms = jnp.mean(x * x, axis=-1, keepdims=True)
    h = (x * lax.rsqrt(ms + NORM_EPS)) * (n1_ref[0] * (1.0 + scale)) + shift
    z = jnp.dot(h.astype(BF16), win_ref[0], preferred_element_type=F32)

    cb_ref[...] = z[:, ZC_B:ZC_B + CONV_DIM].astype(BF16)
    cv_ref[...] = (z[:, ZC_C:ZC_C + CONV_DIM] * z[:, ZC_U:ZC_U + CONV_DIM]).astype(BF16)
    zp_ref[...] = z[:, ZC_P:ZC_P + POOL_DIM]

    tm = x.shape[0]
    lane = lax.broadcasted_iota(I32, (tm, LANES), 1)
    ones_col = lane == HEAD_DIM
    cg, sg = cg_ref[...], sg_ref[...]
    g_first = lane < HEAD_DIM // 2
    for hd in range(GQA_HEADS):
        slab = z[:, ZC_GQ + hd * LANES:ZC_GQ + (hd + 1) * LANES]
        qg_ref[:, hd * LANES:(hd + 1) * LANES] = _norm_rope(
            slab, gq_ref[0], cg, sg, HEAD_DIM, g_first, HEAD_DIM // 2).astype(BF16)
    for hd in range(GQA_KV_HEADS):
        slab = z[:, ZC_GK + hd * LANES:ZC_GK + (hd + 1) * LANES]
        kg_ref[:, hd * LANES:(hd + 1) * LANES] = _norm_rope(
            slab, gk_ref[0], cg, sg, HEAD_DIM, g_first, HEAD_DIM // 2).astype(BF16)
        vs = z[:, ZC_GV + hd * LANES:ZC_GV + (hd + 1) * LANES]
        vg_ref[:, hd * LANES:(hd + 1) * LANES] = jnp.where(ones_col, 1.0, vs).astype(BF16)

    cm, sm = cm_ref[...], sm_ref[...]
    m_first = lane < MLA_NOPE_DIM + MLA_ROPE_DIM // 2
    zq = z[:, ZC_MQ:ZC_MQ + MLA_Q_RANK]
    cq = zq * lax.rsqrt(jnp.mean(zq * zq, axis=-1, keepdims=True) + NORM_EPS) * mqn_ref[0]
    qpre = jnp.dot(cq.astype(BF16), wuq_ref[0], preferred_element_type=F32)
    zkv = z[:, ZC_MKV:ZC_MKV + MLA_KV_RANK]
    ckv = zkv * lax.rsqrt(jnp.mean(zkv * zkv, axis=-1, keepdims=True) + NORM_EPS) * mkvn_ref[0]
    kvp = jnp.dot(ckv.astype(BF16), wukv_ref[0], preferred_element_type=F32)
    zkr = z[:, ZC_MKR:ZC_MKR + LANES]
    for hd in range(MLA_HEADS):
        sl = slice(hd * LANES, (hd + 1) * LANES)
        qm_ref[:, sl] = _norm_rope(qpre[:, sl], qkq_ref[0], cm, sm, MLA_QK_DIM,
                                   m_first, MLA_ROPE_DIM // 2).astype(BF16)
        km_ref[:, sl] = _norm_rope(kvp[:, sl] + zkr, qkk_ref[0], cm, sm, MLA_QK_DIM,
                                   m_first, MLA_ROPE_DIM // 2).astype(BF16)
        vs = kvp[:, MLA_HEADS * LANES + hd * LANES:MLA_HEADS * LANES + (hd + 1) * LANES]
        vm_ref[:, sl] = jnp.where(ones_col, 1.0, vs).astype(BF16)


def _inproj(xa, mods, layer, p, geo):
    r, d = xa.shape
    lt, n_lat = geo["lt"], geo["n_lat_tiles"]
    rowmap = lambda j: (j, 0)
    posmap = lambda j: (jnp.where(j >= n_lat, lt, j % lt), 0)
    lay3 = lambda j: (layer, 0, 0)
    nct, tps = geo["nct"], geo["lt"] + geo["nct"]
    kv_names = ("kg", "vg", "km", "vm")

    def kvmap(j):
        jc = j - n_lat
        return (jnp.where(j >= n_lat, (jc // nct) * tps + lt + jc % nct, (j // lt) * tps + j % lt), 0)

    def full3(a):
        return pl.BlockSpec((1,) + a.shape[1:], lay3)

    outs = [("cb", CONV_DIM, BF16), ("cv", CONV_DIM, BF16), ("zp", POOL_DIM, F32),
            ("qg", GQA_HEADS * LANES, BF16), ("kg", GQA_KV_HEADS * LANES, BF16),
            ("vg", GQA_KV_HEADS * LANES, BF16), ("qm", MLA_HEADS * LANES, BF16),
            ("km", MLA_HEADS * LANES, BF16), ("vm", MLA_HEADS * LANES, BF16)]
    res = pl.pallas_call(
        functools.partial(_inproj_kernel, d_model=d, geo=geo),
        grid=(geo["n_tiles"],),
        in_specs=[
            pl.BlockSpec((TM, d), rowmap),
            full3(mods), full3(p["norm1"]),
            pl.BlockSpec((TM, LANES), posmap), pl.BlockSpec((TM, LANES), posmap),
            pl.BlockSpec((TM, LANES), posmap), pl.BlockSpec((TM, LANES), posmap),
            full3(p["w_in"]), full3(p["gq"]), full3(p["gk"]), full3(p["mqn"]), full3(p["mkvn"]),
            full3(p["qkq"]), full3(p["qkk"]), full3(p["wuq"]), full3(p["wukv"]),
        ],
        out_specs=[pl.BlockSpec((TM, w), kvmap if name in kv_names else rowmap) for name, w, _ in outs],
        out_shape=[jax.ShapeDtypeStruct((r, w), dt) for _, w, dt in outs],
        compiler_params=_cparams(("arbitrary",)),
        name="inproj",
    )(xa, mods, p["norm1"], p["cos_g"], p["sin_g"], p["cos_m"], p["sin_m"], p["w_in"],
      p["gq"], p["gk"], p["mqn"], p["mkvn"], p["qkq"], p["qkk"], p["wuq"], p["wukv"])
    return {name: a for (name, _, _), a in zip(outs, res)}


def _attn_kernel(q_ref, k_ref, v_ref, o_ref, q_st, s_buf, p_buf, a_buf, m_ref, acc_ref,
                 *, shared_kv, n_steps, tk):
    tq = q_ref.shape[0]
    if shared_kv:
        q_st[0] = jnp.concatenate([q_ref[:, 0:LANES], q_ref[:, LANES:2 * LANES]], axis=0)
        cols = [0]
    else:
        q_st[0] = q_ref[:, 0:LANES]
        q_st[1] = q_ref[:, LANES:2 * LANES]
        cols = [0, LANES]
    for si, col in enumerate(cols):
        q_s, m_s, acc_s = q_st.at[si], m_ref.at[si], acc_ref.at[si]
        m_s[...] = jnp.full(m_s.shape, -1e30, F32)
        acc_s[...] = jnp.zeros(acc_s.shape, F32)
        p_buf[1] = jnp.zeros(p_buf.shape[1:], BF16)
        a_buf[1] = jnp.ones(a_buf.shape[1:], F32)

        def key_rows(i):
            return pl.ds(pl.multiple_of(jnp.clip(i, 0, n_steps - 1) * tk, tk), tk)

        def scores(i, slot, q_s=q_s, col=col):
            k = k_ref[key_rows(i), col:col + LANES]
            s_buf[slot] = lax.dot_general(q_s[...], k, (((1,), (1,)), ((), ())),
                                          preferred_element_type=F32)

        def softmax(i, slot, m_s=m_s):
            s = s_buf[slot]
            m_old = m_s[...]
            row_max = jnp.broadcast_to(jnp.max(s, axis=-1, keepdims=True), m_old.shape)
            m_new = jnp.maximum(m_old, row_max)
            a_buf[slot] = jnp.exp2(m_old - m_new)
            m_eff = m_new + jnp.where(i < n_steps, 0.0, 1e9)
            m_wide = jnp.concatenate([m_eff] * (tk // LANES), axis=1)
            p_buf[slot] = jnp.exp2(s - m_wide).astype(BF16)
            m_s[...] = m_new

        def accumulate(i, slot, acc_s=acc_s, col=col):
            v = v_ref[key_rows(i), col:col + LANES]
            acc_s[...] = a_buf[slot] * acc_s[...] + jnp.dot(p_buf[slot], v, preferred_element_type=F32)

        scores(0, 0)

        def pair(t, carry):
            i = 2 * t
            scores(i + 1, 1)
            softmax(i, 0)
            accumulate(i - 1, 1)
            scores(i + 2, 0)
            softmax(i + 1, 1)
            accumulate(i, 0)
            return carry

        lax.fori_loop(0, (n_steps + 2) // 2, pair, 0)
    if shared_kv:
        a0, a1 = acc_ref[0, 0:tq], acc_ref[0, tq:2 * tq]
    else:
        a0, a1 = acc_ref[0], acc_ref[1]
    o0 = a0 * (1.0 / a0[:, HEAD_DIM:HEAD_DIM + 1])
    o1 = a1 * (1.0 / a1[:, HEAD_DIM:HEAD_DIM + 1])
    lane = lax.broadcasted_iota(I32, (tq, LANES), 1)
    o_ref[...] = jnp.where(lane < HEAD_DIM, o0, pltpu.roll(o1, HEAD_DIM, 1)).astype(BF16)


def _attention(q, k, v, shared_kv, geo, ctx_queries):
    nb, nc, nl = geo["nbatch"], geo["nc"], geo["nl"]
    na = nl + nc
    kw = LANES if shared_kv else 2 * LANES
    if ctx_queries:
        tq, q_per, q_blk0 = nc, 1, nb * nl // nc
        kv_rows = nc
        kv_map = lambda b, g, t: (b * (na // nc) + nl // nc, g)
    else:
        tq = min(ATT_ROWS, nl) // (2 if shared_kv else 1)
        q_per, q_blk0 = nl // tq, 0
        kv_rows = na
        kv_map = lambda b, g, t: (b, g)
    n_streams, rows = (1, 2 * tq) if shared_kv else (2, tq)
    return pl.pallas_call(
        functools.partial(_attn_kernel, shared_kv=shared_kv, n_steps=kv_rows // ATT_TK, tk=ATT_TK),
        grid=(nb, 2, q_per),
        in_specs=[
            pl.BlockSpec((tq, 2 * LANES), lambda b, g, t: (q_blk0 + b * q_per + t, g)),
            pl.BlockSpec((kv_rows, kw), kv_map),
            pl.BlockSpec((kv_rows, kw), kv_map),
        ],
        out_specs=pl.BlockSpec((tq, LANES), lambda b, g, t: (b * q_per + t, g)),
        out_shape=jax.ShapeDtypeStruct((nb * q_per * tq, 2 * LANES), BF16),
        scratch_shapes=[pltpu.VMEM((n_streams, rows, LANES), BF16),
                        pltpu.VMEM((2, rows, ATT_TK), F32),
                        pltpu.VMEM((2, rows, ATT_TK), BF16),
                        pltpu.VMEM((2, rows, LANES), F32),
                        pltpu.VMEM((n_streams, rows, LANES), F32),
                        pltpu.VMEM((n_streams, rows, LANES), F32)],
        compiler_params=_cparams(("arbitrary", "arbitrary", "arbitrary")),
        name=("attn_gqa" if shared_kv else "attn_mla") + ("_ctx" if ctx_queries else ""),
    )(q, k, v)


def _top2_sum(a, b, c, d):
    hi_ab, lo_ab = jnp.maximum(a, b), jnp.minimum(a, b)
    hi_cd, lo_cd = jnp.maximum(c, d), jnp.minimum(c, d)
    first = jnp.maximum(hi_ab, hi_cd)
    second = jnp.maximum(jnp.minimum(hi_ab, hi_cd), jnp.maximum(lo_ab, lo_cd))
    return first + second


def _outproj_kernel(x_ref, cb_ref, cv_ref, cvp_ref, cvn_ref, zp_ref, zpp_ref, zpn_ref,
                    ygl_ref, ygc_ref, yml_ref, ymc_ref,
                    mods_ref, convw_ref, band_ref, wpool_ref, pscale_ref,
                    wout_ref, n2_ref, wrt_ref, br_ref, tri_ref,
                    xo_ref, h2_ref, info_ref, cnt_ref,
                    *, d_model, geo):
    tile = pl.program_id(0)
    d = d_model
    tm = x_ref.shape[0]
    tg = _tile_geometry(tile, geo)
    is_ctx = tg["is_ctx"]
    keep_prev = jnp.where(tg["first"], 0.0, 1.0)
    keep_next = jnp.where(tg["last"], 0.0, 1.0)
    row = tg["mod_row"]
    gate1 = mods_ref[0, pl.ds(row, 1), 2 * d:3 * d]
    shift2 = mods_ref[0, pl.ds(row, 1), 3 * d:4 * d]
    scale2 = mods_ref[0, pl.ds(row, 1), 4 * d:5 * d]

    v = cv_ref[...].astype(F32)
    prev_row = cvp_ref[...].astype(F32)[BF16_ROWS - 1:BF16_ROWS] * keep_prev
    next_row = cvn_ref[...].astype(F32)[0:1] * keep_next
    rid = lax.broadcasted_iota(I32, (tm, CONV_DIM), 0)
    vm1 = jnp.where(rid == 0, prev_row, pltpu.roll(v, 1, 0))
    vp1 = jnp.where(rid == tm - 1, next_row, pltpu.roll(v, tm - 1, 0))
    cw = convw_ref[0]
    y_conv = cb_ref[...].astype(F32) * (vm1 * cw[0:1] + v * cw[1:2] + vp1 * cw[2:3])

    zp = zp_ref[...]
    ext = jnp.concatenate(
        [zpp_ref[...] * keep_prev, zp, zpn_ref[...] * keep_next,
         jnp.zeros((POOL_EXT - tm - 2 * POOL_HALO, POOL_DIM), F32)], axis=0).astype(BF16)
    ext_a, ext_b = ext[:, 0:LANES], ext[:, LANES:2 * LANES]
    lane = lax.broadcasted_iota(I32, (tm, LANES), 1)
    low = lane < POOL_DIM // 4
    sum_a = jnp.where(low, jnp.dot(band_ref[0], ext_a, preferred_element_type=F32),
                      jnp.dot(band_ref[1], ext_a, preferred_element_type=F32))
    sum_b = jnp.where(low, jnp.dot(band_ref[2], ext_b, preferred_element_type=F32),
                      jnp.dot(band_ref[3], ext_b, preferred_element_type=F32))
    sums = jnp.concatenate([sum_a, sum_b], axis=1)
    lane_p = lax.broadcasted_iota(I32, (tm, POOL_DIM), 1)
    half_w = jnp.left_shift(1, jnp.right_shift(lane_p, int(math.log2(POOL_DIM // 4))))
    pos = tg["pos0"] + rid
    cnt = (jnp.minimum(pos + half_w, tg["seg_len"]) - jnp.maximum(pos - half_w, 0)).astype(F32)
    dlt = sums / cnt - zp
    y_pool = jnp.dot(dlt.astype(BF16), wpool_ref[0], preferred_element_type=F32) * pscale_ref[0]

    y_gqa = jnp.where(is_ctx, ygc_ref[...], ygl_ref[...])
    y_mla = jnp.where(is_ctx, ymc_ref[...], yml_ref[...])
    ycat = jnp.concatenate([y_conv.astype(BF16), y_pool.astype(BF16), y_gqa, y_mla], axis=1)
    y = jnp.dot(ycat, wout_ref[0], preferred_element_type=F32)
    xn = x_ref[...] + gate1 * y
    xo_ref[...] = xn
    ms = jnp.mean(xn * xn, axis=-1, keepdims=True)
    h2 = (xn * lax.rsqrt(ms + NORM_EPS)) * (n2_ref[0] * (1.0 + scale2)) + shift2
    h2_ref[...] = h2

    logits = lax.dot_general(wrt_ref[...], h2, (((1,), (1,)), ((), ())),
                             preferred_element_type=F32, precision=HIGHEST)
    scores = _sigmoid(logits)
    sel = scores + br_ref[...]
    epg = EXPERTS_PER_GROUP
    n_groups = N_EXPERTS // epg
    srow = [sel[e:e + 1] for e in range(N_EXPERTS)]
    crow = [scores[e:e + 1] for e in range(N_EXPERTS)]
    gscore = [_top2_sum(*srow[g * epg:(g + 1) * epg]) for g in range(n_groups)]
    gbest = jnp.zeros_like(gscore[0]).astype(I32)
    best = gscore[0]
    for g in range(1, n_groups):
        upd = gscore[g] > best
        gbest = jnp.where(upd, g, gbest)
        best = jnp.where(upd, gscore[g], best)

    def pick(rows_, j):
        out = rows_[(n_groups - 1) * epg + j]
        for g in range(n_groups - 2, -1, -1):
            out = jnp.where(gbest == g, rows_[g * epg + j], out)
        return out

    sv = [pick(srow, j) for j in range(epg)]
    cv_ = [pick(crow, j) for j in range(epg)]
    i1 = jnp.zeros_like(gbest)
    b1 = sv[0]
    for j in range(1, epg):
        upd = sv[j] > b1
        i1 = jnp.where(upd, j, i1)
        b1 = jnp.where(upd, sv[j], b1)
    i2 = jnp.zeros_like(gbest)
    b2 = jnp.full_like(b1, -jnp.inf)
    for j in range(epg):
        upd = jnp.logical_and(i1 != j, sv[j] > b2)
        i2 = jnp.where(upd, j, i2)
        b2 = jnp.where(upd, sv[j], b2)
    s1 = cv_[epg - 1]
    s2 = cv_[epg - 1]
    for j in range(epg - 2, -1, -1):
        s1 = jnp.where(i1 == j, cv_[j], s1)
        s2 = jnp.where(i2 == j, cv_[j], s2)
    inv = 1.0 / (s1 + s2)
    e1 = gbest * epg + i1
    e2 = gbest * epg + i2

    @pl.when(tile == 0)
    def _():
        cnt_ref[...] = jnp.zeros_like(cnt_ref)

    erow = lax.broadcasted_iota(I32, (N_EXPERTS, tm), 0)
    hit1 = erow == e1
    hit2 = erow == e2
    onehot = jnp.where(hit1, 1.0, 0.0) + jnp.where(hit2, 1.0, 0.0)
    before = jnp.dot(onehot.astype(BF16), tri_ref[...], preferred_element_type=F32)
    tot = cnt_ref[:, 0:1] + before
    rank1 = jnp.sum(jnp.where(hit1, tot, 0.0), axis=0, keepdims=True).astype(I32)
    rank2 = jnp.sum(jnp.where(hit2, tot, 0.0), axis=0, keepdims=True).astype(I32)
    cnt_ref[...] = cnt_ref[...] + jnp.sum(onehot, axis=1, keepdims=True)

    w1 = lax.bitcast_convert_type(s1 * inv, I32)
    w2 = lax.bitcast_convert_type(s2 * inv, I32)
    irow = lax.broadcasted_iota(I32, (SUBLANES, tm), 0)
    info = jnp.where(irow == 0, e1, jnp.where(irow == 1, e2, jnp.where(
        irow == 2, rank1, jnp.where(irow == 3, rank2, jnp.where(
            irow == 4, w1, jnp.where(irow == 5, w2, 0))))))
    info_ref[0] = info


def _outproj(xa, mix, yg, ygc, ym, ymc, mods, layer, p, geo, n_tiles):
    d = xa.shape[1]
    nt_all, n_lat = geo["n_tiles"], geo["n_lat_tiles"]
    r = n_tiles * TM
    nt = n_tiles
    rowmap = lambda j: (j, 0)
    latmap = lambda j: (jnp.minimum(j, n_lat - 1), 0)
    ctxmap = lambda j: (jnp.clip(j - n_lat, 0, ygc.shape[0] // TM - 1), 0)
    lay3 = lambda j: (layer, 0, 0)
    c0 = lambda j: (0, 0)
    c3 = lambda j: (0, 0, 0)
    bf_blocks = TM // BF16_ROWS
    f_blocks = TM // SUBLANES

    def full3(a):
        return pl.BlockSpec((1,) + a.shape[1:], lay3)

    return pl.pallas_call(
        functools.partial(_outproj_kernel, d_model=d, geo=geo),
        grid=(n_tiles,),
        in_specs=[
            pl.BlockSpec((TM, d), rowmap),
            pl.BlockSpec((TM, CONV_DIM), rowmap),
            pl.BlockSpec((TM, CONV_DIM), rowmap),
            pl.BlockSpec((BF16_ROWS, CONV_DIM), lambda j: (jnp.maximum(j * bf_blocks - 1, 0), 0)),
            pl.BlockSpec((BF16_ROWS, CONV_DIM),
                         lambda j: (jnp.minimum((j + 1) * bf_blocks, nt_all * bf_blocks - 1), 0)),
            pl.BlockSpec((TM, POOL_DIM), rowmap),
            pl.BlockSpec((SUBLANES, POOL_DIM), lambda j: (jnp.maximum(j * f_blocks - 1, 0), 0)),
            pl.BlockSpec((SUBLANES, POOL_DIM),
                         lambda j: (jnp.minimum((j + 1) * f_blocks, nt_all * f_blocks - 1), 0)),
            pl.BlockSpec((TM, 2 * LANES), latmap),
            pl.BlockSpec((TM, 2 * LANES), ctxmap),
            pl.BlockSpec((TM, 2 * LANES), latmap),
            pl.BlockSpec((TM, 2 * LANES), ctxmap),
            full3(mods), full3(p["conv_w"]),
            pl.BlockSpec(p["band"].shape, c3),
            full3(p["wpool"]), full3(p["pscale"]), full3(p["w_out"]), full3(p["norm2"]),
            pl.BlockSpec(p["wrt"].shape, c0), pl.BlockSpec(p["br"].shape, c0),
            pl.BlockSpec(p["tri"].shape, c0),
        ],
        out_specs=[
            pl.BlockSpec((TM, d), rowmap),
            pl.BlockSpec((TM, d), rowmap),
            pl.BlockSpec((1, SUBLANES, TM), lambda j: (j, 0, 0)),
            pl.BlockSpec((N_EXPERTS, LANES), c0),
        ],
        out_shape=[
            jax.ShapeDtypeStruct((r, d), F32),
            jax.ShapeDtypeStruct((r, d), F32),
            jax.ShapeDtypeStruct((nt, SUBLANES, TM), I32),
            jax.ShapeDtypeStruct((N_EXPERTS, LANES), F32),
        ],
        compiler_params=_cparams(("arbitrary",)),
        name="outproj",
    )(xa, mix["cb"], mix["cv"], mix["cv"], mix["cv"], mix["zp"], mix["zp"], mix["zp"],
      yg, ygc, ym, ymc, mods, p["conv_w"], p["band"], p["wpool"], p["pscale"], p["w_out"],
      p["norm2"], p["wrt"], p["br"], p["tri"])


def _row_copies(idx_ref, n_rows, make):
    def body(r, carry):
        make(0, r, idx_ref[0, r]).start()
        make(1, r, idx_ref[1, r]).start()
        return carry
    lax.fori_loop(0, n_rows, body, 0, unroll=ROW_COPY_UNROLL)


def _scatter_kernel(ps_ref, zs_ref, has_ref, nu_ref, info_ref, h2_ref, dest_ref, xs_ref,
                    dsm_ref, zbuf_ref, sem_ref, *, first_spare, n_blocks, n_tiles):
    j = pl.program_id(0)
    tm = TM

    @pl.when(j == 0)
    def _():
        zbuf_ref[...] = jnp.zeros_like(zbuf_ref)

        def zero_copy(start):
            return pltpu.make_async_copy(
                zbuf_ref, xs_ref.at[pl.ds(pl.multiple_of(start, BM), BM)], sem_ref.at[2])

        for act in ("start", "wait"):
            for e in range(N_EXPERTS):
                @pl.when(has_ref[e] > 0)
                def _():
                    getattr(zero_copy(zs_ref[e]), act)()
            for jb in range(first_spare, n_blocks):
                @pl.when(jb >= nu_ref[0])
                def _():
                    getattr(zero_copy(jb * BM), act)()

    info = info_ref[0]
    e1, e2 = info[0:1], info[1:2]
    d1, d2 = info[2:3], info[3:4]
    for e in range(N_EXPERTS):
        d1 = d1 + jnp.where(e1 == e, ps_ref[e], 0)
        d2 = d2 + jnp.where(e2 == e, ps_ref[e], 0)
    irow = lax.broadcasted_iota(I32, (SUBLANES, tm), 0)
    dest_ref[0] = jnp.where(irow == 0, d1, jnp.where(irow == 1, d2, 0))
    to_smem = pltpu.make_async_copy(dest_ref.at[0], dsm_ref, sem_ref.at[3])
    to_smem.start()
    to_smem.wait()

    par = j % 2
    row0 = j * tm

    def row_copy(_, r, slot):
        return pltpu.make_async_copy(h2_ref.at[pl.ds(row0 + r, 1)], xs_ref.at[pl.ds(slot, 1)],
                                     sem_ref.at[par])

    _row_copies(dsm_ref, tm, row_copy)

    def retire(which):
        for _ in range(2):
            pltpu.make_async_copy(h2_ref.at[pl.ds(0, tm)], xs_ref.at[pl.ds(0, tm)],
                                  sem_ref.at[which]).wait()

    @pl.when(j > 0)
    def _():
        retire(1 - par)

    @pl.when(j == n_tiles - 1)
    def _():
        retire(par)


def _scatter(info, h2, pad_start, zero_start, has_rows, n_used, n_slots):
    r, d = h2.shape
    nt = r // TM
    return pl.pallas_call(
        functools.partial(_scatter_kernel, first_spare=-(-2 * r // BM), n_blocks=n_slots // BM,
                          n_tiles=nt),
        grid_spec=pltpu.PrefetchScalarGridSpec(
            num_scalar_prefetch=4,
            grid=(nt,),
            in_specs=[
                pl.BlockSpec((1, SUBLANES, TM), lambda j, *_: (j, 0, 0)),
                pl.BlockSpec(memory_space=pl.ANY),
            ],
            out_specs=[
                pl.BlockSpec((1, SUBLANES, TM), lambda j, *_: (j, 0, 0)),
                pl.BlockSpec(memory_space=pl.ANY),
            ],
            scratch_shapes=[
                pltpu.SMEM((SUBLANES, TM), I32),
                pltpu.VMEM((BM, d), F32),
                pltpu.SemaphoreType.DMA((4,)),
            ],
        ),
        out_shape=[
            jax.ShapeDtypeStruct((nt, SUBLANES, TM), I32),
            jax.ShapeDtypeStruct((n_slots, d), F32),
        ],
        compiler_params=_cparams(("arbitrary",)),
        name="moe_scatter",
    )(pad_start, zero_start, has_rows, n_used, info, h2)


def _expert_kernel(be_ref, nu_ref, xs_ref, wg_ref, wu_ref, wd_ref, y_ref, wgb, wub, wdb):
    j = pl.program_id(0)

    @pl.when(j < nu_ref[0])
    def _():
        e = be_ref[j]
        prev = be_ref[jnp.maximum(j - 1, 0)]

        @pl.when(jnp.logical_or(j == 0, e != prev))
        def _():
            wgb[...] = wg_ref[0].astype(BF16)
            wub[...] = wu_ref[0].astype(BF16)
            wdb[...] = wd_ref[0].astype(BF16)

        x = xs_ref[...].astype(BF16)
        a = jnp.dot(x, wgb[...], preferred_element_type=F32)
        u = jnp.dot(x, wub[...], preferred_element_type=F32)
        hmid = (a * _sigmoid(a) * u).astype(BF16)
        y_ref[...] = jnp.dot(hmid, wdb[...], preferred_element_type=F32)

    @pl.when(j >= nu_ref[0])
    def _():
        y_ref[...] = jnp.zeros_like(y_ref)


def _experts(xs, block_exp, n_used, w_gate, w_up, w_down, layer):
    n_slots, d = xs.shape
    de = w_gate.shape[-1]
    nbm = n_slots // BM
    n_exp = w_gate.shape[1]

    def blk(j, be, nu):
        return (jnp.minimum(j, nu[0] - 1), 0)

    def wmap(j, be, nu):
        return (layer * n_exp + be[jnp.minimum(j, nu[0] - 1)], 0, 0)

    wg = w_gate.reshape((-1,) + w_gate.shape[2:])
    wu = w_up.reshape((-1,) + w_up.shape[2:])
    wd = w_down.reshape((-1,) + w_down.shape[2:])
    return pl.pallas_call(
        _expert_kernel,
        grid_spec=pltpu.PrefetchScalarGridSpec(
            num_scalar_prefetch=2,
            grid=(nbm,),
            in_specs=[
                pl.BlockSpec((BM, d), blk),
                pl.BlockSpec((1, d, de), wmap),
                pl.BlockSpec((1, d, de), wmap),
                pl.BlockSpec((1, de, d), wmap),
            ],
            out_specs=pl.BlockSpec((BM, d), lambda j, be, nu: (j, 0)),
            scratch_shapes=[
                pltpu.VMEM((d, de), BF16),
                pltpu.VMEM((d, de), BF16),
                pltpu.VMEM((de, d), BF16),
            ],
        ),
        out_shape=jax.ShapeDtypeStruct((n_slots, d), F32),
        compiler_params=_cparams(("arbitrary",)),
        name="moe_experts",
    )(block_exp, n_used, xs, wg, wu, wd)


def _combine_kernel(dest_ref, dest_next_ref, info_ref, x_ref, mods_ref, y_ref, o_ref,
                    dsm_ref, ybuf_ref, sem_ref, *, d_model, geo, n_steps):
    j = pl.program_id(0)
    d = d_model
    tm = x_ref.shape[0]
    row = _tile_geometry(j, geo)["mod_row"]
    gate2 = mods_ref[0, pl.ds(row, 1), 5 * d:6 * d]
    cur = j % 2

    def gather(idx_block_ref, buf):
        to_smem = pltpu.make_async_copy(idx_block_ref.at[0], dsm_ref, sem_ref.at[2])
        to_smem.start()
        to_smem.wait()

        def row_copy(k, r, slot):
            return pltpu.make_async_copy(y_ref.at[pl.ds(slot, 1)],
                                         ybuf_ref.at[buf, k, pl.ds(r, 1)], sem_ref.at[buf])

        _row_copies(dsm_ref, tm, row_copy)

    @pl.when(j == 0)
    def _():
        gather(dest_ref, 0)

    @pl.when(j + 1 < n_steps)
    def _():
        gather(dest_next_ref, 1 - cur)

    for k in range(2):
        pltpu.make_async_copy(y_ref.at[pl.ds(0, tm)], ybuf_ref.at[cur, k], sem_ref.at[cur]).wait()

    info = info_ref[0]
    wrow = lax.broadcasted_iota(I32, (LANES, tm), 0)
    w_lanes = jnp.where(wrow == 0, lax.bitcast_convert_type(info[4:5], F32),
                        jnp.where(wrow == 1, lax.bitcast_convert_type(info[5:6], F32), 0.0))
    w_rows = w_lanes.T
    mixed = w_rows[:, 0:1] * ybuf_ref[cur, 0] + w_rows[:, 1:2] * ybuf_ref[cur, 1]
    o_ref[...] = x_ref[...] + gate2 * mixed


def _combine(dest, info, xn, mods, y, layer, geo):
    r, d = xn.shape
    n_steps = r // TM
    return pl.pallas_call(
        functools.partial(_combine_kernel, d_model=d, geo=geo, n_steps=n_steps),
        grid=(n_steps,),
        in_specs=[
            pl.BlockSpec((1, SUBLANES, TM), lambda j: (j, 0, 0)),
            pl.BlockSpec((1, SUBLANES, TM), lambda j: (jnp.minimum(j + 1, n_steps - 1), 0, 0)),
            pl.BlockSpec((1, SUBLANES, TM), lambda j: (j, 0, 0)),
            pl.BlockSpec((TM, d), lambda j: (j, 0)),
            pl.BlockSpec((1,) + mods.shape[1:], lambda j: (layer, 0, 0)),
            pl.BlockSpec(memory_space=pl.ANY),
        ],
        out_specs=pl.BlockSpec((TM, d), lambda j: (j, 0)),
        out_shape=jax.ShapeDtypeStruct((n_steps * TM, d), F32),
        scratch_shapes=[
            pltpu.SMEM((SUBLANES, TM), I32),
            pltpu.VMEM((2, 2, TM, d), F32),
            pltpu.SemaphoreType.DMA((3,)),
        ],
        compiler_params=_cparams(("arbitrary",)),
        name="moe_combine",
    )(dest, dest, info, xn, mods, y)


def _pad_heads(w, n_heads, width):
    lead = w.shape[:-1]
    w = w.reshape(lead + (n_heads, width))
    w = jnp.pad(w, [(0, 0)] * len(lead) + [(0, 0), (0, LANES - width)])
    return w.reshape(lead + (n_heads * LANES,))


def _rope_tables(nc, nl):
    t = jnp.arange(nl)
    row_id = (t // GRID_W).astype(F32)
    col_id = (t % GRID_W).astype(F32)

    def angles(rot_dim):
        n_freq = rot_dim // 4
        inv_freq = jnp.power(ROPE_THETA, -jnp.arange(n_freq, dtype=F32) / n_freq)
        return jnp.concatenate([row_id[:, None] * inv_freq, col_id[:, None] * inv_freq], axis=-1)

    def with_ctx(tab, fill):
        return jnp.concatenate([tab, jnp.full((TM, LANES), fill, F32)], axis=0)

    ag = angles(HEAD_DIM)
    one_g = jnp.ones((nl, LANES - HEAD_DIM), F32)
    cos_g = jnp.concatenate([jnp.cos(ag), jnp.cos(ag), one_g], axis=-1)
    sin_g = jnp.concatenate([-jnp.sin(ag), jnp.sin(ag), 0.0 * one_g], axis=-1)
    am = angles(MLA_ROPE_DIM)
    one_n = jnp.ones((nl, MLA_NOPE_DIM), F32)
    one_t = jnp.ones((nl, LANES - MLA_QK_DIM), F32)
    cos_m = jnp.concatenate([one_n, jnp.cos(am), jnp.cos(am), one_t], axis=-1)
    sin_m = jnp.concatenate([0.0 * one_n, -jnp.sin(am), jnp.sin(am), 0.0 * one_t], axis=-1)
    return with_ctx(cos_g, 1.0), with_ctx(sin_g, 0.0), with_ctx(cos_m, 1.0), with_ctx(sin_m, 0.0)


def _pool_band():
    t = np.arange(TM)[:, None]
    src = np.arange(POOL_EXT)[None, :] - POOL_HALO
    live = np.arange(POOL_EXT)[None, :] < TM + 2 * POOL_HALO
    mats = [((src >= t - w // 2) & (src < t + w // 2) & live) for w in POOL_WINDOWS]
    return jnp.asarray(np.stack(mats).astype(np.float32), dtype=BF16)


def _prep_params(w_in, norm1, norm2, conv_w, w_pool, pool_scale, gqa_q_norm, gqa_k_norm,
                 mla_q_norm, mla_kv_norm, mla_w_uq, mla_w_uk, mla_w_uv, mla_qk_q_norm,
                 mla_qk_k_norm, w_out, w_router, b_router, nc, nl):
    dep = w_in.shape[0]
    o = 0
    pieces = {}
    for name, n in (("conv", 3 * CONV_DIM), ("pool", POOL_DIM), ("gq", GQA_HEADS * HEAD_DIM),
                    ("gk", GQA_KV_HEADS * HEAD_DIM), ("gv", GQA_KV_HEADS * HEAD_DIM),
                    ("mq", MLA_Q_RANK), ("mkv", MLA_KV_RANK), ("mkr", MLA_ROPE_DIM)):
        pieces[name] = w_in[..., o:o + n]
        o += n
    mkr = jnp.pad(pieces["mkr"], ((0, 0), (0, 0), (MLA_NOPE_DIM, LANES - MLA_QK_DIM)))
    w_in_p = jnp.concatenate([
        pieces["conv"], pieces["pool"], _pad_heads(pieces["gq"], GQA_HEADS, HEAD_DIM),
        _pad_heads(pieces["gk"], GQA_KV_HEADS, HEAD_DIM), _pad_heads(pieces["gv"], GQA_KV_HEADS, HEAD_DIM),
        pieces["mq"], pieces["mkv"], mkr], axis=-1).astype(BF16)
    wukv = jnp.concatenate([_pad_heads(mla_w_uk, MLA_HEADS, MLA_NOPE_DIM),
                            _pad_heads(mla_w_uv, MLA_HEADS, MLA_V_DIM)], axis=-1).astype(BF16)
    eye = jnp.eye(len(POOL_WINDOWS), dtype=F32)
    wpool = jnp.einsum("gh,dgij->dgihj", eye, w_pool).reshape(dep, POOL_DIM, POOL_DIM).astype(BF16)
    cos_g, sin_g, cos_m, sin_m = _rope_tables(nc, nl)
    tri = np.triu(np.ones((TM, TM), np.float32), 1)

    def row3(a):
        return a.reshape(dep, 1, a.shape[-1])

    return {
        "w_in": w_in_p,
        "norm1": row3(norm1), "norm2": row3(norm2),
        "gq": row3(_pad_heads(gqa_q_norm * (GQA_SCALE * LOG2E), 1, HEAD_DIM)),
        "gk": row3(_pad_heads(gqa_k_norm, 1, HEAD_DIM)),
        "mqn": row3(mla_q_norm), "mkvn": row3(mla_kv_norm),
        "qkq": row3(_pad_heads(mla_qk_q_norm * (MLA_SCALE * LOG2E), 1, MLA_QK_DIM)),
        "qkk": row3(_pad_heads(mla_qk_k_norm, 1, MLA_QK_DIM)),
        "wuq": _pad_heads(mla_w_uq, MLA_HEADS, MLA_QK_DIM).astype(BF16),
        "wukv": wukv,
        "cos_g": cos_g, "sin_g": sin_g, "cos_m": cos_m, "sin_m": sin_m,
        "conv_w": jnp.pad(conv_w, ((0, 0), (0, SUBLANES - conv_w.shape[1]), (0, 0))),
        "band": _pool_band(),
        "wpool": wpool, "pscale": row3(pool_scale),
        "w_out": w_out.astype(BF16),
        "wrt": w_router.T, "br": b_router.reshape(-1, 1),
        "tri": jnp.asarray(tri, dtype=BF16),
    }


def _moe_plan(counts):
    counts = counts.astype(I32)
    padded = ((counts + BM - 1) // BM) * BM
    pad_end = jnp.cumsum(padded)
    pad_start = pad_end - padded
    return pad_start, pad_end, padded


def kernel(x, c, ctx, c_ctx, w_mod, b_mod, norm1, norm2, w_in, conv_w, w_pool, pool_scale,
           gqa_q_norm, gqa_k_norm, mla_q_norm, mla_kv_norm, mla_w_uq, mla_w_uk, mla_w_uv,
           mla_qk_q_norm, mla_qk_k_norm, w_out, w_router, b_router, w_gate, w_up, w_down):
    nb, nl, d = x.shape
    nc = ctx.shape[1]
    depth = w_mod.shape[0]
    assert nc % TM == 0 and nl % TM == 0 and nl % GRID_W == 0 and nb < SUBLANES
    lt, nct = nl // TM, nc // TM
    geo = {"nbatch": nb, "nc": nc, "nl": nl, "lt": lt, "nct": nct,
           "n_lat_tiles": nb * lt, "n_tiles": nb * (lt + nct)}

    p = _prep_params(w_in, norm1, norm2, conv_w, w_pool, pool_scale, gqa_q_norm, gqa_k_norm,
                     mla_q_norm, mla_kv_norm, mla_w_uq, mla_w_uk, mla_w_uv, mla_qk_q_norm,
                     mla_qk_k_norm, w_out, w_router, b_router, nc, nl)
    cvec = jnp.concatenate([c, c_ctx[None, :], jnp.zeros((SUBLANES - nb - 1, d), F32)], axis=0)
    mods = _adaln(cvec, w_mod, b_mod)

    xa = jnp.concatenate([x.reshape(nb * nl, d), ctx.reshape(nb * nc, d)], axis=0)
    for i in range(depth):
        last = i == depth - 1
        mix = _inproj(xa, mods, i, p, geo)
        yg = _attention(mix["qg"], mix["kg"], mix["vg"], True, geo, False)
        ym = _attention(mix["qm"], mix["km"], mix["vm"], False, geo, False)
        if last:
            n_tiles, ygc, ymc = geo["n_lat_tiles"], yg, ym
        else:
            n_tiles = geo["n_tiles"]
            ygc = _attention(mix["qg"], mix["kg"], mix["vg"], True, geo, True)
            ymc = _attention(mix["qm"], mix["km"], mix["vm"], False, geo, True)
        xn, h2, info, cnt = _outproj(xa, mix, yg, ygc, ym, ymc, mods, i, p, geo, n_tiles)
        n_blocks = -(-2 * n_tiles * TM // BM) + N_EXPERTS
        n_slots = n_blocks * BM
        pad_start, pad_end, padded = _moe_plan(cnt[:, 0])
        n_used = (pad_end[-1:] // BM).astype(I32)
        block_row0 = jnp.arange(n_blocks, dtype=I32) * BM
        block_exp = jnp.minimum(jnp.sum((pad_end[None, :] <= block_row0[:, None]).astype(I32), axis=1),
                                N_EXPERTS - 1)
        dest, xs = _scatter(info, h2, pad_start, jnp.maximum(pad_end - BM, 0),
                            (padded > 0).astype(I32), n_used, n_slots)
        y = _experts(xs, block_exp, n_used, w_gate, w_up, w_down, i)
        xa = _combine(dest, info, xn, mods, y, i, geo)
    return xa.reshape(nb, nl, d)
```

```python
import functools
import math

import numpy as np
import jax
import jax.numpy as jnp
from jax import lax
from jax.experimental import pallas as pl
from jax.experimental.pallas import tpu as pltpu

F32 = jnp.float32
BF16 = jnp.bfloat16
I32 = jnp.int32

GRID_W = 64
CONV_DIM = 256
POOL_DIM = 256
POOL_WINDOWS = (2, 4, 8, 16)
HEAD_DIM = 64
GQA_HEADS = 4
GQA_KV_HEADS = 2
MLA_HEADS = 4
MLA_NOPE_DIM = 64
MLA_ROPE_DIM = 32
MLA_QK_DIM = MLA_NOPE_DIM + MLA_ROPE_DIM
MLA_V_DIM = 64
MLA_Q_RANK = 256
MLA_KV_RANK = 128
N_EXPERTS = 16
EXPERTS_PER_GROUP = 4
ROPE_THETA = 10000.0
NORM_EPS = 1e-6
LOG2E = 1.4426950408889634
GQA_SCALE = HEAD_DIM ** -0.5
MLA_SCALE = MLA_QK_DIM ** -0.5

LANES = 128
SUBLANES = 8
BF16_ROWS = 16
VMEM_LIMIT = 56 * 1024 * 1024

TM = 256
ATT_ROWS = 1024
ATT_TK = 256
BM = 512
ROW_COPY_UNROLL = 8
POOL_EXT = 512
POOL_HALO = 8

ZC_B, ZC_C, ZC_U, ZC_P = 0, 256, 512, 768
ZC_GQ = 1024
ZC_GK = ZC_GQ + GQA_HEADS * LANES
ZC_MQ = ZC_GK + GQA_KV_HEADS * LANES
ZC_MKV = ZC_MQ + MLA_Q_RANK
ZC_MKR = ZC_MKV + MLA_KV_RANK
ZC_END = ZC_MKR + LANES

HIGHEST = lax.Precision.HIGHEST


def _cparams(sem, vmem=VMEM_LIMIT):
    return pltpu.CompilerParams(dimension_semantics=sem, vmem_limit_bytes=vmem)


def _sigmoid(v):
    return 1.0 / (1.0 + jnp.exp(-v))


def _adaln_kernel(c_ref, w_ref, b_ref, o_ref):
    c = c_ref[...]
    s = c * _sigmoid(c)
    o_ref[0] = jnp.dot(s, w_ref[0], preferred_element_type=F32, precision=HIGHEST) + b_ref[0]


def _adaln(cvec, w_mod, b_mod):
    depth, d, n6 = w_mod.shape
    tn = 1536 if n6 % 1536 == 0 else n6
    rows = cvec.shape[0]
    return pl.pallas_call(
        _adaln_kernel,
        grid=(depth, n6 // tn),
        in_specs=[
            pl.BlockSpec((rows, d), lambda i, n: (0, 0)),
            pl.BlockSpec((1, d, tn), lambda i, n: (i, 0, n)),
            pl.BlockSpec((1, 1, tn), lambda i, n: (i, 0, n)),
        ],
        out_specs=pl.BlockSpec((1, rows, tn), lambda i, n: (i, 0, n)),
        out_shape=jax.ShapeDtypeStruct((depth, rows, n6), F32),
        compiler_params=_cparams(("arbitrary", "arbitrary")),
        name="adaln",
    )(cvec, w_mod, b_mod.reshape(depth, 1, n6))


def _tile_geometry(j, geo):
    lt, nct, n_lat = geo["lt"], geo["nct"], geo["n_lat_tiles"]
    is_ctx = j >= n_lat
    jc = j - n_lat
    sample = jnp.where(is_ctx, jc // nct, j // lt)
    jt = jnp.where(is_ctx, jc % nct, j % lt)
    return {
        "is_ctx": is_ctx,
        "mod_row": jnp.where(is_ctx, geo["nbatch"], sample),
        "first": jt == 0,
        "last": jt == jnp.where(is_ctx, nct, lt) - 1,
        "pos0": jt * TM,
        "seg_len": jnp.where(is_ctx, geo["nc"], geo["nl"]),
    }


def _norm_rope(slab, gain, cos, sin, n_valid, first_half, half):
    ms = jnp.sum(slab * slab, axis=-1, keepdims=True) * (1.0 / n_valid)
    y = slab * lax.rsqrt(ms + NORM_EPS) * gain
    partner = jnp.where(first_half, pltpu.roll(y, LANES - half, 1), pltpu.roll(y, half, 1))
    return y * cos + partner * sin


def _inproj_kernel(x_ref, mods_ref, n1_ref, cg_ref, sg_ref, cm_ref, sm_ref, win_ref,
                   gq_ref, gk_ref, mqn_ref, mkvn_ref, qkq_ref, qkk_ref, wuq_ref, wuk_ref,
                   wgvt_ref, wuvt_ref,
                   cb_ref, cv_ref, zp_ref, qg_ref, kg_ref, vgt_ref, qm_ref, km_ref, vmt_ref,
                   *, d_model, geo):
    d = d_model
    row = _tile_geometry(pl.program_id(0), geo)["mod_row"]
    shift = mods_ref[0, pl.ds(row, 1), 0:d]
    scale = mods_ref[0, pl.ds(row, 1), d:2 * d]
    x = x_ref[...]
    ms = jnp.mean(x * x, axis=-1, keepdims=True)
    h = ((x * lax.rsqrt(ms + NORM_EPS)) * (n1_ref[0] * (1.0 + scale)) + shift).astype(BF16)
    z = jnp.dot(h, win_ref[0], preferred_element_type=F32)
    nt_dims = (((1,), (1,)), ((), ()))

    def with_ones_row(vt):
        srow = lax.broadcasted_iota(I32, vt.shape, 0)
        return jnp.where(jnp.bitwise_and(srow, LANES - 1) == HEAD_DIM, 1.0, vt).astype(BF16)

    vgt_ref[0] = with_ones_row(lax.dot_general(wgvt_ref[0], h, nt_dims, preferred_element_type=F32))

    cb_ref[...] = z[:, ZC_B:ZC_B + CONV_DIM].astype(BF16)
    cv_ref[...] = (z[:, ZC_C:ZC_C + CONV_DIM] * z[:, ZC_U:ZC_U + CONV_DIM]).astype(BF16)
    zp_ref[...] = z[:, ZC_P:ZC_P + POOL_DIM]

    tm = x.shape[0]
    lane = lax.broadcasted_iota(I32, (tm, LANES), 1)
    cg, sg = cg_ref[...], sg_ref[...]
    g_first = lane < HEAD_DIM // 2
    for hd in range(GQA_HEADS):
        slab = z[:, ZC_GQ + hd * LANES:ZC_GQ + (hd + 1) * LANES]
        qg_ref[:, hd * LANES:(hd + 1) * LANES] = _norm_rope(
            slab, gq_ref[0], cg, sg, HEAD_DIM, g_first, HEAD_DIM // 2).astype(BF16)
    for hd in range(GQA_KV_HEADS):
        slab = z[:, ZC_GK + hd * LANES:ZC_GK + (hd + 1) * LANES]
        kg_ref[:, hd * LANES:(hd + 1) * LANES] = _norm_rope(
            slab, gk_ref[0], cg, sg, HEAD_DIM, g_first, HEAD_DIM // 2).astype(BF16)

    cm, sm = cm_ref[...], sm_ref[...]
    m_first = lane < MLA_NOPE_DIM + MLA_ROPE_DIM // 2
    zq = z[:, ZC_MQ:ZC_MQ + MLA_Q_RANK]
    cq = zq * lax.rsqrt(jnp.mean(zq * zq, axis=-1, keepdims=True) + NORM_EPS) * mqn_ref[0]
    qpre = jnp.dot(cq.astype(BF16), wuq_ref[0], preferred_element_type=F32)
    zkv = z[:, ZC_MKV:ZC_MKV + MLA_KV_RANK]
    ckv = zkv * lax.rsqrt(jnp.mean(zkv * zkv, axis=-1, keepdims=True) + NORM_EPS) * mkvn_ref[0]
    ckv = ckv.astype(BF16)
    kvp = jnp.dot(ckv, wuk_ref[0], preferred_element_type=F32)
    vmt_ref[0] = with_ones_row(lax.dot_general(wuvt_ref[0], ckv, nt_dims, preferred_element_type=F32))
    zkr = z[:, ZC_MKR:ZC_MKR + LANES]
    for hd in range(MLA_HEADS):
        sl = slice(hd * LANES, (hd + 1) * LANES)
        qm_ref[:, sl] = _norm_rope(qpre[:, sl], qkq_ref[0], cm, sm, MLA_QK_DIM,
                                   m_first, MLA_ROPE_DIM // 2).astype(BF16)
        km_ref[:, sl] = _norm_rope(kvp[:, sl] + zkr, qkk_ref[0], cm, sm, MLA_QK_DIM,
                                   m_first, MLA_ROPE_DIM // 2).astype(BF16)


def _inproj(xa, mods, layer, p, geo):
    r, d = xa.shape
    lt, n_lat = geo["lt"], geo["n_lat_tiles"]
    rowmap = lambda j: (j, 0)
    posmap = lambda j: (jnp.where(j >= n_lat, lt, j % lt), 0)
    lay3 = lambda j: (layer, 0, 0)
    nct, tps = geo["nct"], geo["lt"] + geo["nct"]

    def kv_tile(j):
        jc = j - n_lat
        return jnp.where(j >= n_lat, (jc // nct) * tps + lt + jc % nct, (j // lt) * tps + j % lt)

    def full3(a):
        return pl.BlockSpec((1,) + a.shape[1:], lay3)

    outs = [("cb", CONV_DIM, BF16, "rows"), ("cv", CONV_DIM, BF16, "rows"), ("zp", POOL_DIM, F32, "rows"),
            ("qg", GQA_HEADS * LANES, BF16, "rows"), ("kg", GQA_KV_HEADS * LANES, BF16, "keys"),
            ("vgt", GQA_KV_HEADS * LANES, BF16, "keys_t"), ("qm", MLA_HEADS * LANES, BF16, "rows"),
            ("km", MLA_HEADS * LANES, BF16, "keys"), ("vmt", MLA_HEADS * LANES, BF16, "keys_t")]

    def out_spec(w, layout):
        if layout == "keys_t":
            return pl.BlockSpec((1, w, TM), lambda j: (kv_tile(j), 0, 0))
        return pl.BlockSpec((TM, w), rowmap if layout == "rows" else (lambda j: (kv_tile(j), 0)))

    def out_shape(w, dt, layout):
        return jax.ShapeDtypeStruct((r // TM, w, TM) if layout == "keys_t" else (r, w), dt)

    res = pl.pallas_call(
        functools.partial(_inproj_kernel, d_model=d, geo=geo),
        grid=(geo["n_tiles"],),
        in_specs=[
            pl.BlockSpec((TM, d), rowmap),
            full3(mods), full3(p["norm1"]),
            pl.BlockSpec((TM, LANES), posmap), pl.BlockSpec((TM, LANES), posmap),
            pl.BlockSpec((TM, LANES), posmap), pl.BlockSpec((TM, LANES), posmap),
            full3(p["w_in"]), full3(p["gq"]), full3(p["gk"]), full3(p["mqn"]), full3(p["mkvn"]),
            full3(p["qkq"]), full3(p["qkk"]), full3(p["wuq"]), full3(p["wuk"]),
            full3(p["wgvt"]), full3(p["wuvt"]),
        ],
        out_specs=[out_spec(w, layout) for _, w, _, layout in outs],
        out_shape=[out_shape(w, dt, layout) for _, w, dt, layout in outs],
        compiler_params=_cparams(("arbitrary",)),
        name="inproj",
    )(xa, mods, p["norm1"], p["cos_g"], p["sin_g"], p["cos_m"], p["sin_m"], p["w_in"],
      p["gq"], p["gk"], p["mqn"], p["mkvn"], p["qkq"], p["qkk"], p["wuq"], p["wuk"],
      p["wgvt"], p["wuvt"])
    return {name: a for (name, _, _, _), a in zip(outs, res)}


def _attn_kernel(q_ref, k_ref, vt_ref, o_ref, q_st, s_buf, p_buf, a_buf, m_ref, acc_ref,
                 *, shared_kv, n_steps, tk):
    tq = q_ref.shape[0]
    if shared_kv:
        q_st[0] = jnp.concatenate([q_ref[:, 0:LANES], q_ref[:, LANES:2 * LANES]], axis=0)
        cols = [0]
    else:
        q_st[0] = q_ref[:, 0:LANES]
        q_st[1] = q_ref[:, LANES:2 * LANES]
        cols = [0, LANES]
    for si, col in enumerate(cols):
        q_s, m_s, acc_s = q_st.at[si], m_ref.at[si], acc_ref.at[si]
        m_s[...] = jnp.full(m_s.shape, -1e30, F32)
        acc_s[...] = jnp.zeros(acc_s.shape, F32)
        p_buf[1] = jnp.zeros(p_buf.shape[1:], BF16)
        a_buf[1] = jnp.ones(a_buf.shape[1:], F32)

        def key_tile(i):
            return jnp.clip(i, 0, n_steps - 1)

        def scores(i, slot, q_s=q_s, col=col):
            k = k_ref[pl.ds(pl.multiple_of(key_tile(i) * tk, tk), tk), col:col + LANES]
            s_buf[slot] = lax.dot_general(k, q_s[...], (((1,), (1,)), ((), ())),
                                          preferred_element_type=F32)

        def softmax(i, slot, m_s=m_s):
            s = s_buf[slot]
            m_old = m_s[...]
            m_new = jnp.maximum(m_old, jnp.max(s, axis=0, keepdims=True))
            a_buf[slot] = jnp.exp2(m_old - m_new)
            m_eff = m_new[0:1] + jnp.where(i < n_steps, 0.0, 1e9)
            p_buf[slot] = jnp.exp2(s - m_eff).astype(BF16)
            m_s[...] = m_new

        def accumulate(i, slot, acc_s=acc_s, col=col):
            vt = vt_ref[key_tile(i), col:col + LANES, :]
            acc_s[...] = a_buf[slot][0:1] * acc_s[...] + jnp.dot(vt, p_buf[slot],
                                                                 preferred_element_type=F32)

        scores(0, 0)

        def pair(t, carry):
            i = 2 * t
            scores(i + 1, 1)
            softmax(i, 0)
            accumulate(i - 1, 1)
            scores(i + 2, 0)
            softmax(i + 1, 1)
            accumulate(i, 0)
            return carry

        lax.fori_loop(0, (n_steps + 2) // 2, pair, 0)
    def finish(acc_t):
        return (acc_t * (1.0 / acc_t[HEAD_DIM:HEAD_DIM + 1])).T

    if shared_kv:
        o_both = finish(acc_ref[0])
        o0, o1 = o_both[0:tq], o_both[tq:2 * tq]
    else:
        o0, o1 = finish(acc_ref[0]), finish(acc_ref[1])
    lane = lax.broadcasted_iota(I32, (tq, LANES), 1)
    o_ref[...] = jnp.where(lane < HEAD_DIM, o0, pltpu.roll(o1, HEAD_DIM, 1)).astype(BF16)


def _attention(q, k, vt, shared_kv, geo, ctx_queries):
    nb, nc, nl = geo["nbatch"], geo["nc"], geo["nl"]
    na = nl + nc
    kw = LANES if shared_kv else 2 * LANES
    if ctx_queries:
        tq, q_per, q_blk0 = nc, 1, nb * nl // nc
        kv_rows = nc
        kv_blk = lambda b: b * (na // nc) + nl // nc
    else:
        tq = min(ATT_ROWS, nl) // (2 if shared_kv else 1)
        q_per, q_blk0 = nl // tq, 0
        kv_rows = na
        kv_blk = lambda b: b
    n_streams, rows = (1, 2 * tq) if shared_kv else (2, tq)
    n_steps = kv_rows // ATT_TK
    return pl.pallas_call(
        functools.partial(_attn_kernel, shared_kv=shared_kv, n_steps=n_steps, tk=ATT_TK),
        grid=(nb, 2, q_per),
        in_specs=[
            pl.BlockSpec((tq, 2 * LANES), lambda b, g, t: (q_blk0 + b * q_per + t, g)),
            pl.BlockSpec((kv_rows, kw), lambda b, g, t: (kv_blk(b), g)),
            pl.BlockSpec((n_steps, kw, ATT_TK), lambda b, g, t: (kv_blk(b), g, 0)),
        ],
        out_specs=pl.BlockSpec((tq, LANES), lambda b, g, t: (b * q_per + t, g)),
        out_shape=jax.ShapeDtypeStruct((nb * q_per * tq, 2 * LANES), BF16),
        scratch_shapes=[pltpu.VMEM((n_streams, rows, LANES), BF16),
                        pltpu.VMEM((2, ATT_TK, rows), F32),
                        pltpu.VMEM((2, ATT_TK, rows), BF16),
                        pltpu.VMEM((2, SUBLANES, rows), F32),
                        pltpu.VMEM((n_streams, SUBLANES, rows), F32),
                        pltpu.VMEM((n_streams, LANES, rows), F32)],
        compiler_params=_cparams(("arbitrary", "arbitrary", "arbitrary")),
        name=("attn_gqa" if shared_kv else "attn_mla") + ("_ctx" if ctx_queries else ""),
    )(q, k, vt)


def _top2_sum(a, b, c, d):
    hi_ab, lo_ab = jnp.maximum(a, b), jnp.minimum(a, b)
    hi_cd, lo_cd = jnp.maximum(c, d), jnp.minimum(c, d)
    first = jnp.maximum(hi_ab, hi_cd)
    second = jnp.maximum(jnp.minimum(hi_ab, hi_cd), jnp.maximum(lo_ab, lo_cd))
    return first + second


def _outproj_kernel(x_ref, cb_ref, cv_ref, cvp_ref, cvn_ref, zp_ref, zpp_ref, zpn_ref,
                    ygl_ref, ygc_ref, yml_ref, ymc_ref,
                    mods_ref, convw_ref, band_ref, wpool_ref, pscale_ref,
                    wout_ref, n2_ref, wrt_ref, br_ref, tri_ref,
                    xo_ref, h2_ref, info_ref, cnt_ref,
                    *, d_model, geo):
    tile = pl.program_id(0)
    d = d_model
    tm = x_ref.shape[0]
    tg = _tile_geometry(tile, geo)
    is_ctx = tg["is_ctx"]
    keep_prev = jnp.where(tg["first"], 0.0, 1.0)
    keep_next = jnp.where(tg["last"], 0.0, 1.0)
    row = tg["mod_row"]
    gate1 = mods_ref[0, pl.ds(row, 1), 2 * d:3 * d]
    shift2 = mods_ref[0, pl.ds(row, 1), 3 * d:4 * d]
    scale2 = mods_ref[0, pl.ds(row, 1), 4 * d:5 * d]

    v = cv_ref[...].astype(F32)
    prev_row = cvp_ref[...].astype(F32)[BF16_ROWS - 1:BF16_ROWS] * keep_prev
    next_row = cvn_ref[...].astype(F32)[0:1] * keep_next
    rid = lax.broadcasted_iota(I32, (tm, CONV_DIM), 0)
    vm1 = jnp.where(rid == 0, prev_row, pltpu.roll(v, 1, 0))
    vp1 = jnp.where(rid == tm - 1, next_row, pltpu.roll(v, tm - 1, 0))
    cw = convw_ref[0]
    y_conv = cb_ref[...].astype(F32) * (vm1 * cw[0:1] + v * cw[1:2] + vp1 * cw[2:3])

    zp = zp_ref[...]
    ext = jnp.concatenate(
        [zpp_ref[...] * keep_prev, zp, zpn_ref[...] * keep_next,
         jnp.zeros((POOL_EXT - tm - 2 * POOL_HALO, POOL_DIM), F32)], axis=0).astype(BF16)
    ext_a, ext_b = ext[:, 0:LANES], ext[:, LANES:2 * LANES]
    lane = lax.broadcasted_iota(I32, (tm, LANES), 1)
    low = lane < POOL_DIM // 4
    sum_a = jnp.where(low, jnp.dot(band_ref[0], ext_a, preferred_element_type=F32),
                      jnp.dot(band_ref[1], ext_a, preferred_element_type=F32))
    sum_b = jnp.where(low, jnp.dot(band_ref[2], ext_b, preferred_element_type=F32),
                      jnp.dot(band_ref[3], ext_b, preferred_element_type=F32))
    sums = jnp.concatenate([sum_a, sum_b], axis=1)
    lane_p = lax.broadcasted_iota(I32, (tm, POOL_DIM), 1)
    half_w = jnp.left_shift(1, jnp.right_shift(lane_p, int(math.log2(POOL_DIM // 4))))
    pos = tg["pos0"] + rid
    cnt = (jnp.minimum(pos + half_w, tg["seg_len"]) - jnp.maximum(pos - half_w, 0)).astype(F32)
    dlt = sums / cnt - zp
    y_pool = jnp.dot(dlt.astype(BF16), wpool_ref[0], preferred_element_type=F32) * pscale_ref[0]

    y_gqa = jnp.where(is_ctx, ygc_ref[...], ygl_ref[...])
    y_mla = jnp.where(is_ctx, ymc_ref[...], yml_ref[...])
    ycat = jnp.concatenate([y_conv.astype(BF16), y_pool.astype(BF16), y_gqa, y_mla], axis=1)
    y = jnp.dot(ycat, wout_ref[0], preferred_element_type=F32)
    xn = x_ref[...] + gate1 * y
    xo_ref[...] = xn
    ms = jnp.mean(xn * xn, axis=-1, keepdims=True)
    h2 = (xn * lax.rsqrt(ms + NORM_EPS)) * (n2_ref[0] * (1.0 + scale2)) + shift2
    h2_ref[...] = h2

    logits = lax.dot_general(wrt_ref[...], h2, (((1,), (1,)), ((), ())),
                             preferred_element_type=F32, precision=HIGHEST)
    scores = _sigmoid(logits)
    sel = scores + br_ref[...]
    epg = EXPERTS_PER_GROUP
    n_groups = N_EXPERTS // epg
    srow = [sel[e:e + 1] for e in range(N_EXPERTS)]
    crow = [scores[e:e + 1] for e in range(N_EXPERTS)]
    gscore = [_top2_sum(*srow[g * epg:(g + 1) * epg]) for g in range(n_groups)]
    gbest = jnp.zeros_like(gscore[0]).astype(I32)
    best = gscore[0]
    for g in range(1, n_groups):
        upd = gscore[g] > best
        gbest = jnp.where(upd, g, gbest)
        best = jnp.where(upd, gscore[g], best)

    def pick(rows_, j):
        out = rows_[(n_groups - 1) * epg + j]
        for g in range(n_groups - 2, -1, -1):
            out = jnp.where(gbest == g, rows_[g * epg + j], out)
        return out

    sv = [pick(srow, j) for j in range(epg)]
    cv_ = [pick(crow, j) for j in range(epg)]
    i1 = jnp.zeros_like(gbest)
    b1 = sv[0]
    for j in range(1, epg):
        upd = sv[j] > b1
        i1 = jnp.where(upd, j, i1)
        b1 = jnp.where(upd, sv[j], b1)
    i2 = jnp.zeros_like(gbest)
    b2 = jnp.full_like(b1, -jnp.inf)
    for j in range(epg):
        upd = jnp.logical_and(i1 != j, sv[j] > b2)
        i2 = jnp.where(upd, j, i2)
        b2 = jnp.where(upd, sv[j], b2)
    s1 = cv_[epg - 1]
    s2 = cv_[epg - 1]
    for j in range(epg - 2, -1, -1):
        s1 = jnp.where(i1 == j, cv_[j], s1)
        s2 = jnp.where(i2 == j, cv_[j], s2)
    inv = 1.0 / (s1 + s2)
    e1 = gbest * epg + i1
    e2 = gbest * epg + i2

    @pl.when(tile == 0)
    def _():
        cnt_ref[...] = jnp.zeros_like(cnt_ref)

    erow = lax.broadcasted_iota(I32, (N_EXPERTS, tm), 0)
    hit1 = erow == e1
    hit2 = erow == e2
    onehot = jnp.where(hit1, 1.0, 0.0) + jnp.where(hit2, 1.0, 0.0)
    before = jnp.dot(onehot.astype(BF16), tri_ref[...], preferred_element_type=F32)
    tot = cnt_ref[:, 0:1] + before
    rank1 = jnp.sum(jnp.where(hit1, tot, 0.0), axis=0, keepdims=True).astype(I32)
    rank2 = jnp.sum(jnp.where(hit2, tot, 0.0), axis=0, keepdims=True).astype(I32)
    cnt_ref[...] = cnt_ref[...] + jnp.sum(onehot, axis=1, keepdims=True)

    w1 = lax.bitcast_convert_type(s1 * inv, I32)
    w2 = lax.bitcast_convert_type(s2 * inv, I32)
    irow = lax.broadcasted_iota(I32, (SUBLANES, tm), 0)
    info = jnp.where(irow == 0, e1, jnp.where(irow == 1, e2, jnp.where(
        irow == 2, rank1, jnp.where(irow == 3, rank2, jnp.where(
            irow == 4, w1, jnp.where(irow == 5, w2, 0))))))
    info_ref[0] = info


def _outproj(xa, mix, yg, ygc, ym, ymc, mods, layer, p, geo, n_tiles):
    d = xa.shape[1]
    nt_all, n_lat = geo["n_tiles"], geo["n_lat_tiles"]
    r = n_tiles * TM
    nt = n_tiles
    rowmap = lambda j: (j, 0)
    latmap = lambda j: (jnp.minimum(j, n_lat - 1), 0)
    ctxmap = lambda j: (jnp.clip(j - n_lat, 0, ygc.shape[0] // TM - 1), 0)
    lay3 = lambda j: (layer, 0, 0)
    c0 = lambda j: (0, 0)
    c3 = lambda j: (0, 0, 0)
    bf_blocks = TM // BF16_ROWS
    f_blocks = TM // SUBLANES

    def full3(a):
        return pl.BlockSpec((1,) + a.shape[1:], lay3)

    return pl.pallas_call(
        functools.partial(_outproj_kernel, d_model=d, geo=geo),
        grid=(n_tiles,),
        in_specs=[
            pl.BlockSpec((TM, d), rowmap),
            pl.BlockSpec((TM, CONV_DIM), rowmap),
            pl.BlockSpec((TM, CONV_DIM), rowmap),
            pl.BlockSpec((BF16_ROWS, CONV_DIM), lambda j: (jnp.maximum(j * bf_blocks - 1, 0), 0)),
            pl.BlockSpec((BF16_ROWS, CONV_DIM),
                         lambda j: (jnp.minimum((j + 1) * bf_blocks, nt_all * bf_blocks - 1), 0)),
            pl.BlockSpec((TM, POOL_DIM), rowmap),
            pl.BlockSpec((SUBLANES, POOL_DIM), lambda j: (jnp.maximum(j * f_blocks - 1, 0), 0)),
            pl.BlockSpec((SUBLANES, POOL_DIM),
                         lambda j: (jnp.minimum((j + 1) * f_blocks, nt_all * f_blocks - 1), 0)),
            pl.BlockSpec((TM, 2 * LANES), latmap),
            pl.BlockSpec((TM, 2 * LANES), ctxmap),
            pl.BlockSpec((TM, 2 * LANES), latmap),
            pl.BlockSpec((TM, 2 * LANES), ctxmap),
            full3(mods), full3(p["conv_w"]),
            pl.BlockSpec(p["band"].shape, c3),
            full3(p["wpool"]), full3(p["pscale"]), full3(p["w_out"]), full3(p["norm2"]),
            pl.BlockSpec(p["wrt"].shape, c0), pl.BlockSpec(p["br"].shape, c0),
            pl.BlockSpec(p["tri"].shape, c0),
        ],
        out_specs=[
            pl.BlockSpec((TM, d), rowmap),
            pl.BlockSpec((TM, d), rowmap),
            pl.BlockSpec((1, SUBLANES, TM), lambda j: (j, 0, 0)),
            pl.BlockSpec((N_EXPERTS, LANES), c0),
        ],
        out_shape=[
            jax.ShapeDtypeStruct((r, d), F32),
            jax.ShapeDtypeStruct((r, d), F32),
            jax.ShapeDtypeStruct((nt, SUBLANES, TM), I32),
            jax.ShapeDtypeStruct((N_EXPERTS, LANES), F32),
        ],
        compiler_params=_cparams(("arbitrary",)),
        name="outproj",
    )(xa, mix["cb"], mix["cv"], mix["cv"], mix["cv"], mix["zp"], mix["zp"], mix["zp"],
      yg, ygc, ym, ymc, mods, p["conv_w"], p["band"], p["wpool"], p["pscale"], p["w_out"],
      p["norm2"], p["wrt"], p["br"], p["tri"])


def _row_copies(idx_ref, n_rows, make):
    def body(r, carry):
        make(0, r, idx_ref[0, r]).start()
        make(1, r, idx_ref[1, r]).start()
        return carry
    lax.fori_loop(0, n_rows, body, 0, unroll=ROW_COPY_UNROLL)


def _scatter_kernel(ps_ref, zs_ref, has_ref, nu_ref, info_ref, h2_ref, dest_ref, xs_ref,
                    dsm_ref, zbuf_ref, stage_ref, sem_ref, *, first_spare, n_blocks, n_tiles):
    j = pl.program_id(0)
    tm = h2_ref.shape[0]

    @pl.when(j == 0)
    def _():
        zbuf_ref[...] = jnp.zeros_like(zbuf_ref)

        def zero_copy(start):
            return pltpu.make_async_copy(
                zbuf_ref, xs_ref.at[pl.ds(pl.multiple_of(start, BM), BM)], sem_ref.at[2])

        for act in ("start", "wait"):
            for e in range(N_EXPERTS):
                @pl.when(has_ref[e] > 0)
                def _():
                    getattr(zero_copy(zs_ref[e]), act)()
            for jb in range(first_spare, n_blocks):
                @pl.when(jb >= nu_ref[0])
                def _():
                    getattr(zero_copy(jb * BM), act)()

    info = info_ref[0]
    e1, e2 = info[0:1], info[1:2]
    d1, d2 = info[2:3], info[3:4]
    for e in range(N_EXPERTS):
        d1 = d1 + jnp.where(e1 == e, ps_ref[e], 0)
        d2 = d2 + jnp.where(e2 == e, ps_ref[e], 0)
    irow = lax.broadcasted_iota(I32, (SUBLANES, tm), 0)
    dest_ref[0] = jnp.where(irow == 0, d1, jnp.where(irow == 1, d2, 0))
    to_smem = pltpu.make_async_copy(dest_ref.at[0], dsm_ref, sem_ref.at[3])
    to_smem.start()
    to_smem.wait()

    par = j % 2
    stage_ref[par] = h2_ref[...]

    def row_copy(_, r, slot):
        return pltpu.make_async_copy(stage_ref.at[par, pl.ds(r, 1)], xs_ref.at[pl.ds(slot, 1)],
                                     sem_ref.at[par])

    _row_copies(dsm_ref, tm, row_copy)

    def retire(which):
        for _ in range(2):
            pltpu.make_async_copy(stage_ref.at[which], xs_ref.at[pl.ds(0, tm)],
                                  sem_ref.at[which]).wait()

    @pl.when(j > 0)
    def _():
        retire(1 - par)

    @pl.when(j == n_tiles - 1)
    def _():
        retire(par)


def _scatter(info, h2, pad_start, zero_start, has_rows, n_used, n_slots):
    r, d = h2.shape
    nt = r // TM
    return pl.pallas_call(
        functools.partial(_scatter_kernel, first_spare=-(-2 * r // BM), n_blocks=n_slots // BM,
                          n_tiles=nt),
        grid_spec=pltpu.PrefetchScalarGridSpec(
            num_scalar_prefetch=4,
            grid=(nt,),
            in_specs=[
                pl.BlockSpec((1, SUBLANES, TM), lambda j, *_: (j, 0, 0)),
                pl.BlockSpec((TM, d), lambda j, *_: (j, 0)),
            ],
            out_specs=[
                pl.BlockSpec((1, SUBLANES, TM), lambda j, *_: (j, 0, 0)),
                pl.BlockSpec(memory_space=pl.ANY),
            ],
            scratch_shapes=[
                pltpu.SMEM((SUBLANES, TM), I32),
                pltpu.VMEM((BM, d), F32),
                pltpu.VMEM((2, TM, d), F32),
                pltpu.SemaphoreType.DMA((4,)),
            ],
        ),
        out_shape=[
            jax.ShapeDtypeStruct((nt, SUBLANES, TM), I32),
            jax.ShapeDtypeStruct((n_slots, d), F32),
        ],
        compiler_params=_cparams(("arbitrary",)),
        name="moe_scatter",
    )(pad_start, zero_start, has_rows, n_used, info, h2)


def _expert_kernel(be_ref, nu_ref, xs_ref, wg_ref, wu_ref, wd_ref, y_ref, wgb, wub, wdb):
    j = pl.program_id(0)

    @pl.when(j < nu_ref[0])
    def _():
        e = be_ref[j]
        prev = be_ref[jnp.maximum(j - 1, 0)]

        @pl.when(jnp.logical_or(j == 0, e != prev))
        def _():
            wgb[...] = wg_ref[0].astype(BF16)
            wub[...] = wu_ref[0].astype(BF16)
            wdb[...] = wd_ref[0].astype(BF16)

        x = xs_ref[...].astype(BF16)
        a = jnp.dot(x, wgb[...], preferred_element_type=F32)
        u = jnp.dot(x, wub[...], preferred_element_type=F32)
        hmid = (a * _sigmoid(a) * u).astype(BF16)
        y_ref[...] = jnp.dot(hmid, wdb[...], preferred_element_type=F32)

    @pl.when(j >= nu_ref[0])
    def _():
        y_ref[...] = jnp.zeros_like(y_ref)


def _experts(xs, block_exp, n_used, w_gate, w_up, w_down, layer):
    n_slots, d = xs.shape
    de = w_gate.shape[-1]
    nbm = n_slots // BM
    n_exp = w_gate.shape[1]

    def blk(j, be, nu):
        return (jnp.minimum(j, nu[0] - 1), 0)

    def wmap(j, be, nu):
        return (layer * n_exp + be[jnp.minimum(j, nu[0] - 1)], 0, 0)

    wg = w_gate.reshape((-1,) + w_gate.shape[2:])
    wu = w_up.reshape((-1,) + w_up.shape[2:])
    wd = w_down.reshape((-1,) + w_down.shape[2:])
    return pl.pallas_call(
        _expert_kernel,
        grid_spec=pltpu.PrefetchScalarGridSpec(
            num_scalar_prefetch=2,
            grid=(nbm,),
            in_specs=[
                pl.BlockSpec((BM, d), blk),
                pl.BlockSpec((1, d, de), wmap),
                pl.BlockSpec((1, d, de), wmap),
                pl.BlockSpec((1, de, d), wmap),
            ],
            out_specs=pl.BlockSpec((BM, d), lambda j, be, nu: (j, 0)),
            scratch_shapes=[
                pltpu.VMEM((d, de), BF16),
                pltpu.VMEM((d, de), BF16),
                pltpu.VMEM((de, d), BF16),
            ],
        ),
        out_shape=jax.ShapeDtypeStruct((n_slots, d), F32),
        compiler_params=_cparams(("arbitrary",)),
        name="moe_experts",
    )(block_exp, n_used, xs, wg, wu, wd)


def _combine_kernel(dest_ref, dest_next_ref, info_ref, x_ref, mods_ref, y_ref, o_ref,
                    dsm_ref, ybuf_ref, sem_ref, *, d_model, geo, n_steps):
    j = pl.program_id(0)
    d = d_model
    tm = x_ref.shape[0]
    row = _tile_geometry(j, geo)["mod_row"]
    gate2 = mods_ref[0, pl.ds(row, 1), 5 * d:6 * d]
    cur = j % 2

    def gather(idx_block_ref, buf):
        to_smem = pltpu.make_async_copy(idx_block_ref.at[0], dsm_ref, sem_ref.at[2])
        to_smem.start()
        to_smem.wait()

        def row_copy(k, r, slot):
            return pltpu.make_async_copy(y_ref.at[pl.ds(slot, 1)],
                                         ybuf_ref.at[buf, k, pl.ds(r, 1)], sem_ref.at[buf])

        _row_copies(dsm_ref, tm, row_copy)

    @pl.when(j == 0)
    def _():
        gather(dest_ref, 0)

    @pl.when(j + 1 < n_steps)
    def _():
        gather(dest_next_ref, 1 - cur)

    for k in range(2):
        pltpu.make_async_copy(y_ref.at[pl.ds(0, tm)], ybuf_ref.at[cur, k], sem_ref.at[cur]).wait()

    info = info_ref[0]
    wrow = lax.broadcasted_iota(I32, (LANES, tm), 0)
    w_lanes = jnp.where(wrow == 0, lax.bitcast_convert_type(info[4:5], F32),
                        jnp.where(wrow == 1, lax.bitcast_convert_type(info[5:6], F32), 0.0))
    w_rows = w_lanes.T
    mixed = w_rows[:, 0:1] * ybuf_ref[cur, 0] + w_rows[:, 1:2] * ybuf_ref[cur, 1]
    o_ref[...] = x_ref[...] + gate2 * mixed


def _combine(dest, info, xn, mods, y, layer, geo):
    r, d = xn.shape
    n_steps = r // TM
    return pl.pallas_call(
        functools.partial(_combine_kernel, d_model=d, geo=geo, n_steps=n_steps),
        grid=(n_steps,),
        in_specs=[
            pl.BlockSpec((1, SUBLANES, TM), lambda j: (j, 0, 0)),
            pl.BlockSpec((1, SUBLANES, TM), lambda j: (jnp.minimum(j + 1, n_steps - 1), 0, 0)),
            pl.BlockSpec((1, SUBLANES, TM), lambda j: (j, 0, 0)),
            pl.BlockSpec((TM, d), lambda j: (j, 0)),
            pl.BlockSpec((1,) + mods.shape[1:], lambda j: (layer, 0, 0)),
            pl.BlockSpec(memory_space=pl.ANY),
        ],
        out_specs=pl.BlockSpec((TM, d), lambda j: (j, 0)),
        out_shape=jax.ShapeDtypeStruct((n_steps * TM, d), F32),
        scratch_shapes=[
            pltpu.SMEM((SUBLANES, TM), I32),
            pltpu.VMEM((2, 2, TM, d), F32),
            pltpu.SemaphoreType.DMA((3,)),
        ],
        compiler_params=_cparams(("arbitrary",)),
        name="moe_combine",
    )(dest, dest, info, xn, mods, y)


def _pad_heads(w, n_heads, width):
    lead = w.shape[:-1]
    w = w.reshape(lead + (n_heads, width))
    w = jnp.pad(w, [(0, 0)] * len(lead) + [(0, 0), (0, LANES - width)])
    return w.reshape(lead + (n_heads * LANES,))


def _rope_tables(nc, nl):
    t = jnp.arange(nl)
    row_id = (t // GRID_W).astype(F32)
    col_id = (t % GRID_W).astype(F32)

    def angles(rot_dim):
        n_freq = rot_dim // 4
        inv_freq = jnp.power(ROPE_THETA, -jnp.arange(n_freq, dtype=F32) / n_freq)
        return jnp.concatenate([row_id[:, None] * inv_freq, col_id[:, None] * inv_freq], axis=-1)

    def with_ctx(tab, fill):
        return jnp.concatenate([tab, jnp.full((TM, LANES), fill, F32)], axis=0)

    ag = angles(HEAD_DIM)
    one_g = jnp.ones((nl, LANES - HEAD_DIM), F32)
    cos_g = jnp.concatenate([jnp.cos(ag), jnp.cos(ag), one_g], axis=-1)
    sin_g = jnp.concatenate([-jnp.sin(ag), jnp.sin(ag), 0.0 * one_g], axis=-1)
    am = angles(MLA_ROPE_DIM)
    one_n = jnp.ones((nl, MLA_NOPE_DIM), F32)
    one_t = jnp.ones((nl, LANES - MLA_QK_DIM), F32)
    cos_m = jnp.concatenate([one_n, jnp.cos(am), jnp.cos(am), one_t], axis=-1)
    sin_m = jnp.concatenate([0.0 * one_n, -jnp.sin(am), jnp.sin(am), 0.0 * one_t], axis=-1)
    return with_ctx(cos_g, 1.0), with_ctx(sin_g, 0.0), with_ctx(cos_m, 1.0), with_ctx(sin_m, 0.0)


def _pool_band():
    t = np.arange(TM)[:, None]
    src = np.arange(POOL_EXT)[None, :] - POOL_HALO
    live = np.arange(POOL_EXT)[None, :] < TM + 2 * POOL_HALO
    mats = [((src >= t - w // 2) & (src < t + w // 2) & live) for w in POOL_WINDOWS]
    return jnp.asarray(np.stack(mats).astype(np.float32), dtype=BF16)


def _prep_params(w_in, norm1, norm2, conv_w, w_pool, pool_scale, gqa_q_norm, gqa_k_norm,
                 mla_q_norm, mla_kv_norm, mla_w_uq, mla_w_uk, mla_w_uv, mla_qk_q_norm,
                 mla_qk_k_norm, w_out, w_router, b_router, nc, nl):
    dep = w_in.shape[0]
    o = 0
    pieces = {}
    for name, n in (("conv", 3 * CONV_DIM), ("pool", POOL_DIM), ("gq", GQA_HEADS * HEAD_DIM),
                    ("gk", GQA_KV_HEADS * HEAD_DIM), ("gv", GQA_KV_HEADS * HEAD_DIM),
                    ("mq", MLA_Q_RANK), ("mkv", MLA_KV_RANK), ("mkr", MLA_ROPE_DIM)):
        pieces[name] = w_in[..., o:o + n]
        o += n
    mkr = jnp.pad(pieces["mkr"], ((0, 0), (0, 0), (MLA_NOPE_DIM, LANES - MLA_QK_DIM)))
    w_in_p = jnp.concatenate([
        pieces["conv"], pieces["pool"], _pad_heads(pieces["gq"], GQA_HEADS, HEAD_DIM),
        _pad_heads(pieces["gk"], GQA_KV_HEADS, HEAD_DIM),
        pieces["mq"], pieces["mkv"], mkr], axis=-1).astype(BF16)
    wgvt = jnp.swapaxes(_pad_heads(pieces["gv"], GQA_KV_HEADS, HEAD_DIM), 1, 2).astype(BF16)
    wuvt = jnp.swapaxes(_pad_heads(mla_w_uv, MLA_HEADS, MLA_V_DIM), 1, 2).astype(BF16)
    eye = jnp.eye(len(POOL_WINDOWS), dtype=F32)
    wpool = jnp.einsum("gh,dgij->dgihj", eye, w_pool).reshape(dep, POOL_DIM, POOL_DIM).astype(BF16)
    cos_g, sin_g, cos_m, sin_m = _rope_tables(nc, nl)
    tri = np.triu(np.ones((TM, TM), np.float32), 1)

    def row3(a):
        return a.reshape(dep, 1, a.shape[-1])

    return {
        "w_in": w_in_p,
        "norm1": row3(norm1), "norm2": row3(norm2),
        "gq": row3(_pad_heads(gqa_q_norm * (GQA_SCALE * LOG2E), 1, HEAD_DIM)),
        "gk": row3(_pad_heads(gqa_k_norm, 1, HEAD_DIM)),
        "mqn": row3(mla_q_norm), "mkvn": row3(mla_kv_norm),
        "qkq": row3(_pad_heads(mla_qk_q_norm * (MLA_SCALE * LOG2E), 1, MLA_QK_DIM)),
        "qkk": row3(_pad_heads(mla_qk_k_norm, 1, MLA_QK_DIM)),
        "wuq": _pad_heads(mla_w_uq, MLA_HEADS, MLA_QK_DIM).astype(BF16),
        "wuk": _pad_heads(mla_w_uk, MLA_HEADS, MLA_NOPE_DIM).astype(BF16),
        "wgvt": wgvt, "wuvt": wuvt,
        "cos_g": cos_g, "sin_g": sin_g, "cos_m": cos_m, "sin_m": sin_m,
        "conv_w": jnp.pad(conv_w, ((0, 0), (0, SUBLANES - conv_w.shape[1]), (0, 0))),
        "band": _pool_band(),
        "wpool": wpool, "pscale": row3(pool_scale),
        "w_out": w_out.astype(BF16),
        "wrt": w_router.T, "br": b_router.reshape(-1, 1),
        "tri": jnp.asarray(tri, dtype=BF16),
    }


def _moe_plan(counts):
    counts = counts.astype(I32)
    padded = ((counts + BM - 1) // BM) * BM
    pad_end = jnp.cumsum(padded)
    pad_start = pad_end - padded
    return pad_start, pad_end, padded


def kernel(x, c, ctx, c_ctx, w_mod, b_mod, norm1, norm2, w_in, conv_w, w_pool, pool_scale,
           gqa_q_norm, gqa_k_norm, mla_q_norm, mla_kv_norm, mla_w_uq, mla_w_uk, mla_w_uv,
           mla_qk_q_norm, mla_qk_k_norm, w_out, w_router, b_router, w_gate, w_up, w_down):
    nb, nl, d = x.shape
    nc = ctx.shape[1]
    depth = w_mod.shape[0]
    assert nc % TM == 0 and nl % TM == 0 and nl % GRID_W == 0 and nb < SUBLANES
    lt, nct = nl // TM, nc // TM
    geo = {"nbatch": nb, "nc": nc, "nl": nl, "lt": lt, "nct": nct,
           "n_lat_tiles": nb * lt, "n_tiles": nb * (lt + nct)}

    p = _prep_params(w_in, norm1, norm2, conv_w, w_pool, pool_scale, gqa_q_norm, gqa_k_norm,
                     mla_q_norm, mla_kv_norm, mla_w_uq, mla_w_uk, mla_w_uv, mla_qk_q_norm,
                     mla_qk_k_norm, w_out, w_router, b_router, nc, nl)
    cvec = jnp.concatenate([c, c_ctx[None, :], jnp.zeros((SUBLANES - nb - 1, d), F32)], axis=0)
    mods = _adaln(cvec, w_mod, b_mod)

    xa = jnp.concatenate([x.reshape(nb * nl, d), ctx.reshape(nb * nc, d)], axis=0)
    for i in range(depth):
        last = i == depth - 1
        mix = _inproj(xa, mods, i, p, geo)
        yg = _attention(mix["qg"], mix["kg"], mix["vgt"], True, geo, False)
        ym = _attention(mix["qm"], mix["km"], mix["vmt"], False, geo, False)
        if last:
            n_tiles, ygc, ymc = geo["n_lat_tiles"], yg, ym
        else:
            n_tiles = geo["n_tiles"]
            ygc = _attention(mix["qg"], mix["kg"], mix["vgt"], True, geo, True)
            ymc = _attention(mix["qm"], mix["km"], mix["vmt"], False, geo, True)
        xn, h2, info, cnt = _outproj(xa, mix, yg, ygc, ym, ymc, mods, i, p, geo, n_tiles)
        n_blocks = -(-2 * n_tiles * TM // BM) + N_EXPERTS
        n_slots = n_blocks * BM
        pad_start, pad_end, padded = _moe_plan(cnt[:, 0])
        n_used = (pad_end[-1:] // BM).astype(I32)
        block_row0 = jnp.arange(n_blocks, dtype=I32) * BM
        block_exp = jnp.minimum(jnp.sum((pad_end[None, :] <= block_row0[:, None]).astype(I32), axis=1),
                                N_EXPERTS - 1)
        dest, xs = _scatter(info, h2, pad_start, jnp.maximum(pad_end - BM, 0),
                            (padded > 0).astype(I32), n_used, n_slots)
        y = _experts(xs, block_exp, n_used, w_gate, w_up, w_down, i)
        xa = _combine(dest, info, xn, mods, y, i, geo)
    return xa.reshape(nb, nl, d)
```

```python
import functools
import math

import numpy as np
import jax
import jax.numpy as jnp
from jax import lax
from jax.experimental import pallas as pl
from jax.experimental.pallas import tpu as pltpu

F32 = jnp.float32
BF16 = jnp.bfloat16
I32 = jnp.int32

GRID_W = 64
CONV_DIM = 256
POOL_DIM = 256
POOL_WINDOWS = (2, 4, 8, 16)
HEAD_DIM = 64
GQA_HEADS = 4
GQA_KV_HEADS = 2
MLA_HEADS = 4
MLA_NOPE_DIM = 64
MLA_ROPE_DIM = 32
MLA_QK_DIM = MLA_NOPE_DIM + MLA_ROPE_DIM
MLA_V_DIM = 64
MLA_Q_RANK = 256
MLA_KV_RANK = 128
N_EXPERTS = 16
EXPERTS_PER_GROUP = 4
ROPE_THETA = 10000.0
NORM_EPS = 1e-6
LOG2E = 1.4426950408889634
GQA_SCALE = HEAD_DIM ** -0.5
MLA_SCALE = MLA_QK_DIM ** -0.5

LANES = 128
SUBLANES = 8
BF16_ROWS = 16
VMEM_LIMIT = 56 * 1024 * 1024

TM = 256
ATT_ROWS = 1024
ATT_TK = 256
BM = 512
ATT_PAIR_UNROLL = 4
ROW_COPY_UNROLL = 8
POOL_EXT = 512
POOL_HALO = 8

ZC_B, ZC_C, ZC_U, ZC_P = 0, 256, 512, 768
ZC_GQ = 1024
ZC_GK = ZC_GQ + GQA_HEADS * LANES
ZC_MQ = ZC_GK + GQA_KV_HEADS * LANES
ZC_MKV = ZC_MQ + MLA_Q_RANK
ZC_MKR = ZC_MKV + MLA_KV_RANK
ZC_END = ZC_MKR + LANES

HIGHEST = lax.Precision.HIGHEST


def _cparams(sem, vmem=VMEM_LIMIT):
    return pltpu.CompilerParams(dimension_semantics=sem, vmem_limit_bytes=vmem)


def _sigmoid(v):
    return 1.0 / (1.0 + jnp.exp(-v))


def _adaln_kernel(c_ref, w_ref, b_ref, o_ref):
    c = c_ref[...]
    s = c * _sigmoid(c)
    o_ref[0] = jnp.dot(s, w_ref[0], preferred_element_type=F32, precision=HIGHEST) + b_ref[0]


def _adaln(cvec, w_mod, b_mod):
    depth, d, n6 = w_mod.shape
    tn = 1536 if n6 % 1536 == 0 else n6
    rows = cvec.shape[0]
    return pl.pallas_call(
        _adaln_kernel,
        grid=(depth, n6 // tn),
        in_specs=[
            pl.BlockSpec((rows, d), lambda i, n: (0, 0)),
            pl.BlockSpec((1, d, tn), lambda i, n: (i, 0, n)),
            pl.BlockSpec((1, 1, tn), lambda i, n: (i, 0, n)),
        ],
        out_specs=pl.BlockSpec((1, rows, tn), lambda i, n: (i, 0, n)),
        out_shape=jax.ShapeDtypeStruct((depth, rows, n6), F32),
        compiler_params=_cparams(("arbitrary", "arbitrary")),
        name="adaln",
    )(cvec, w_mod, b_mod.reshape(depth, 1, n6))


def _tile_geometry(j, geo):
    lt, nct, n_lat = geo["lt"], geo["nct"], geo["n_lat_tiles"]
    is_ctx = j >= n_lat
    jc = j - n_lat
    sample = jnp.where(is_ctx, jc // nct, j // lt)
    jt = jnp.where(is_ctx, jc % nct, j % lt)
    return {
        "is_ctx": is_ctx,
        "mod_row": jnp.where(is_ctx, geo["nbatch"], sample),
        "first": jt == 0,
        "last": jt == jnp.where(is_ctx, nct, lt) - 1,
        "pos0": jt * TM,
        "seg_len": jnp.where(is_ctx, geo["nc"], geo["nl"]),
    }


def _norm_rope(slab, gain, cos, sin, n_valid, first_half, half):
    ms = jnp.sum(slab * slab, axis=-1, keepdims=True) * (1.0 / n_valid)
    y = slab * lax.rsqrt(ms + NORM_EPS) * gain
    partner = jnp.where(first_half, pltpu.roll(y, LANES - half, 1), pltpu.roll(y, half, 1))
    return y * cos + partner * sin


def _inproj_kernel(x_ref, mods_ref, n1_ref, cg_ref, sg_ref, cm_ref, sm_ref, win_ref,
                   gq_ref, gk_ref, mqn_ref, mkvn_ref, qkq_ref, qkk_ref, wuq_ref, wuk_ref,
                   wgvt_ref, wuvt_ref,
                   cb_ref, cv_ref, zp_ref, qg_ref, kg_ref, vgt_ref, qm_ref, km_ref, vmt_ref,
                   *, d_model, geo):
    d = d_model
    row = _tile_geometry(pl.program_id(0), geo)["mod_row"]
    shift = mods_ref[0, pl.ds(row, 1), 0:d]
    scale = mods_ref[0, pl.ds(row, 1), d:2 * d]
    x = x_ref[...]
    ms = jnp.mean(x * x, axis=-1, keepdims=True)
    h = ((x * lax.rsqrt(ms + NORM_EPS)) * (n1_ref[0] * (1.0 + scale)) + shift).astype(BF16)
    nt_dims = (((1,), (1,)), ((), ()))

    def proj(c0, width):
        return jnp.dot(h, win_ref[0, :, c0:c0 + width], preferred_element_type=F32)

    def with_ones_row(vt):
        srow = lax.broadcasted_iota(I32, vt.shape, 0)
        return jnp.where(jnp.bitwise_and(srow, LANES - 1) == HEAD_DIM, 1.0, vt).astype(BF16)

    tm = x.shape[0]
    lane = lax.broadcasted_iota(I32, (tm, LANES), 1)

    cm, sm = cm_ref[...], sm_ref[...]
    m_first = lane < MLA_NOPE_DIM + MLA_ROPE_DIM // 2
    zq = proj(ZC_MQ, MLA_Q_RANK)
    cq = zq * lax.rsqrt(jnp.mean(zq * zq, axis=-1, keepdims=True) + NORM_EPS) * mqn_ref[0]
    qpre = jnp.dot(cq.astype(BF16), wuq_ref[0], preferred_element_type=F32)
    zk = proj(ZC_MKV, MLA_KV_RANK + LANES)
    zkv, zkr = zk[:, 0:MLA_KV_RANK], zk[:, MLA_KV_RANK:]
    ckv = zkv * lax.rsqrt(jnp.mean(zkv * zkv, axis=-1, keepdims=True) + NORM_EPS) * mkvn_ref[0]
    ckv = ckv.astype(BF16)
    kvp = jnp.dot(ckv, wuk_ref[0], preferred_element_type=F32)
    vmt_ref[0] = with_ones_row(lax.dot_general(wuvt_ref[0], ckv, nt_dims, preferred_element_type=F32))
    for hd in range(MLA_HEADS):
        sl = slice(hd * LANES, (hd + 1) * LANES)
        qm_ref[:, sl] = _norm_rope(qpre[:, sl], qkq_ref[0], cm, sm, MLA_QK_DIM,
                                   m_first, MLA_ROPE_DIM // 2).astype(BF16)
        km_ref[:, sl] = _norm_rope(kvp[:, sl] + zkr, qkk_ref[0], cm, sm, MLA_QK_DIM,
                                   m_first, MLA_ROPE_DIM // 2).astype(BF16)

    cg, sg = cg_ref[...], sg_ref[...]
    g_first = lane < HEAD_DIM // 2
    zg = proj(ZC_GQ, (GQA_HEADS + GQA_KV_HEADS) * LANES)
    for hd in range(GQA_HEADS):
        qg_ref[:, hd * LANES:(hd + 1) * LANES] = _norm_rope(
            zg[:, hd * LANES:(hd + 1) * LANES], gq_ref[0], cg, sg, HEAD_DIM, g_first,
            HEAD_DIM // 2).astype(BF16)
    for hd in range(GQA_KV_HEADS):
        slab = zg[:, (GQA_HEADS + hd) * LANES:(GQA_HEADS + hd + 1) * LANES]
        kg_ref[:, hd * LANES:(hd + 1) * LANES] = _norm_rope(
            slab, gk_ref[0], cg, sg, HEAD_DIM, g_first, HEAD_DIM // 2).astype(BF16)
    vgt_ref[0] = with_ones_row(lax.dot_general(wgvt_ref[0], h, nt_dims, preferred_element_type=F32))

    zc = proj(ZC_B, 3 * CONV_DIM + POOL_DIM)
    cb_ref[...] = zc[:, ZC_B:ZC_B + CONV_DIM].astype(BF16)
    cv_ref[...] = (zc[:, ZC_C:ZC_C + CONV_DIM] * zc[:, ZC_U:ZC_U + CONV_DIM]).astype(BF16)
    zp_ref[...] = zc[:, ZC_P:ZC_P + POOL_DIM]


def _inproj(xa, mods, layer, p, geo):
    r, d = xa.shape
    lt, n_lat = geo["lt"], geo["n_lat_tiles"]
    rowmap = lambda j: (j, 0)
    posmap = lambda j: (jnp.where(j >= n_lat, lt, j % lt), 0)
    lay3 = lambda j: (layer, 0, 0)
    nct, tps = geo["nct"], geo["lt"] + geo["nct"]

    def kv_tile(j):
        jc = j - n_lat
        return jnp.where(j >= n_lat, (jc // nct) * tps + lt + jc % nct, (j // lt) * tps + j % lt)

    def full3(a):
        return pl.BlockSpec((1,) + a.shape[1:], lay3)

    outs = [("cb", CONV_DIM, BF16, "rows"), ("cv", CONV_DIM, BF16, "rows"), ("zp", POOL_DIM, F32, "rows"),
            ("qg", GQA_HEADS * LANES, BF16, "rows"), ("kg", GQA_KV_HEADS * LANES, BF16, "keys"),
            ("vgt", GQA_KV_HEADS * LANES, BF16, "keys_t"), ("qm", MLA_HEADS * LANES, BF16, "rows"),
            ("km", MLA_HEADS * LANES, BF16, "keys"), ("vmt", MLA_HEADS * LANES, BF16, "keys_t")]

    def out_spec(w, layout):
        if layout == "keys_t":
            return pl.BlockSpec((1, w, TM), lambda j: (kv_tile(j), 0, 0))
        return pl.BlockSpec((TM, w), rowmap if layout == "rows" else (lambda j: (kv_tile(j), 0)))

    def out_shape(w, dt, layout):
        return jax.ShapeDtypeStruct((r // TM, w, TM) if layout == "keys_t" else (r, w), dt)

    res = pl.pallas_call(
        functools.partial(_inproj_kernel, d_model=d, geo=geo),
        grid=(geo["n_tiles"],),
        in_specs=[
            pl.BlockSpec((TM, d), rowmap),
            full3(mods), full3(p["norm1"]),
            pl.BlockSpec((TM, LANES), posmap), pl.BlockSpec((TM, LANES), posmap),
            pl.BlockSpec((TM, LANES), posmap), pl.BlockSpec((TM, LANES), posmap),
            full3(p["w_in"]), full3(p["gq"]), full3(p["gk"]), full3(p["mqn"]), full3(p["mkvn"]),
            full3(p["qkq"]), full3(p["qkk"]), full3(p["wuq"]), full3(p["wuk"]),
            full3(p["wgvt"]), full3(p["wuvt"]),
        ],
        out_specs=[out_spec(w, layout) for _, w, _, layout in outs],
        out_shape=[out_shape(w, dt, layout) for _, w, dt, layout in outs],
        compiler_params=_cparams(("arbitrary",)),
        name="inproj",
    )(xa, mods, p["norm1"], p["cos_g"], p["sin_g"], p["cos_m"], p["sin_m"], p["w_in"],
      p["gq"], p["gk"], p["mqn"], p["mkvn"], p["qkq"], p["qkk"], p["wuq"], p["wuk"],
      p["wgvt"], p["wuvt"])
    return {name: a for (name, _, _, _), a in zip(outs, res)}


def _attn_kernel(q_ref, k_ref, vt_ref, o_ref, q_st, s_buf, p_buf, a_buf, m_ref, acc_ref,
                 *, shared_kv, n_steps, tk):
    tq = q_ref.shape[0]
    if shared_kv:
        q_st[0] = jnp.concatenate([q_ref[:, 0:LANES], q_ref[:, LANES:2 * LANES]], axis=0)
        cols = [0]
    else:
        q_st[0] = q_ref[:, 0:LANES]
        q_st[1] = q_ref[:, LANES:2 * LANES]
        cols = [0, LANES]
    for si, col in enumerate(cols):
        q_s, m_s, acc_s = q_st.at[si], m_ref.at[si], acc_ref.at[si]
        m_s[...] = jnp.full(m_s.shape, -1e30, F32)
        acc_s[...] = jnp.zeros(acc_s.shape, F32)
        p_buf[1] = jnp.zeros(p_buf.shape[1:], BF16)
        a_buf[1] = jnp.ones(a_buf.shape[1:], F32)

        def key_tile(i):
            return jnp.clip(i, 0, n_steps - 1)

        def scores(i, slot, q_s=q_s, col=col):
            k = k_ref[pl.ds(pl.multiple_of(key_tile(i) * tk, tk), tk), col:col + LANES]
            s_buf[slot] = lax.dot_general(k, q_s[...], (((1,), (1,)), ((), ())),
                                          preferred_element_type=F32)

        def softmax(i, slot, m_s=m_s):
            s = s_buf[slot]
            m_old = m_s[...]
            m_new = jnp.maximum(m_old, jnp.max(s, axis=0, keepdims=True))
            a_buf[slot] = jnp.exp2(m_old - m_new)
            m_eff = m_new[0:1] + jnp.where(i < n_steps, 0.0, 1e9)
            p_buf[slot] = jnp.exp2(s - m_eff).astype(BF16)
            m_s[...] = m_new

        def accumulate(i, slot, acc_s=acc_s, col=col):
            vt = vt_ref[key_tile(i), col:col + LANES, :]
            acc_s[...] = a_buf[slot][0:1] * acc_s[...] + jnp.dot(vt, p_buf[slot],
                                                                 preferred_element_type=F32)

        scores(0, 0)

        def pair(t, carry):
            i = 2 * t
            scores(i + 1, 1)
            softmax(i, 0)
            accumulate(i - 1, 1)
            scores(i + 2, 0)
            softmax(i + 1, 1)
            accumulate(i, 0)
            return carry

        lax.fori_loop(0, (n_steps + 2) // 2, pair, 0, unroll=ATT_PAIR_UNROLL)
    def finish(acc_t):
        return (acc_t * (1.0 / acc_t[HEAD_DIM:HEAD_DIM + 1])).T

    if shared_kv:
        o_both = finish(acc_ref[0])
        o0, o1 = o_both[0:tq], o_both[tq:2 * tq]
    else:
        o0, o1 = finish(acc_ref[0]), finish(acc_ref[1])
    lane = lax.broadcasted_iota(I32, (tq, LANES), 1)
    o_ref[...] = jnp.where(lane < HEAD_DIM, o0, pltpu.roll(o1, HEAD_DIM, 1)).astype(BF16)


def _attention(q, k, vt, shared_kv, geo, ctx_queries):
    nb, nc, nl = geo["nbatch"], geo["nc"], geo["nl"]
    na = nl + nc
    kw = LANES if shared_kv else 2 * LANES
    if ctx_queries:
        tq, q_per, q_blk0 = nc, 1, nb * nl // nc
        kv_rows = nc
        kv_blk = lambda b: b * (na // nc) + nl // nc
    else:
        tq = min(ATT_ROWS, nl) // (2 if shared_kv else 1)
        q_per, q_blk0 = nl // tq, 0
        kv_rows = na
        kv_blk = lambda b: b
    n_streams, rows = (1, 2 * tq) if shared_kv else (2, tq)
    n_steps = kv_rows // ATT_TK
    return pl.pallas_call(
        functools.partial(_attn_kernel, shared_kv=shared_kv, n_steps=n_steps, tk=ATT_TK),
        grid=(nb, 2, q_per),
        in_specs=[
            pl.BlockSpec((tq, 2 * LANES), lambda b, g, t: (q_blk0 + b * q_per + t, g)),
            pl.BlockSpec((kv_rows, kw), lambda b, g, t: (kv_blk(b), g)),
            pl.BlockSpec((n_steps, kw, ATT_TK), lambda b, g, t: (kv_blk(b), g, 0)),
        ],
        out_specs=pl.BlockSpec((tq, LANES), lambda b, g, t: (b * q_per + t, g)),
        out_shape=jax.ShapeDtypeStruct((nb * q_per * tq, 2 * LANES), BF16),
        scratch_shapes=[pltpu.VMEM((n_streams, rows, LANES), BF16),
                        pltpu.VMEM((2, ATT_TK, rows), F32),
                        pltpu.VMEM((2, ATT_TK, rows), BF16),
                        pltpu.VMEM((2, SUBLANES, rows), F32),
                        pltpu.VMEM((n_streams, SUBLANES, rows), F32),
                        pltpu.VMEM((n_streams, LANES, rows), F32)],
        compiler_params=_cparams(("arbitrary", "arbitrary", "arbitrary")),
        name=("attn_gqa" if shared_kv else "attn_mla") + ("_ctx" if ctx_queries else ""),
    )(q, k, vt)


def _top2_sum(a, b, c, d):
    hi_ab, lo_ab = jnp.maximum(a, b), jnp.minimum(a, b)
    hi_cd, lo_cd = jnp.maximum(c, d), jnp.minimum(c, d)
    first = jnp.maximum(hi_ab, hi_cd)
    second = jnp.maximum(jnp.minimum(hi_ab, hi_cd), jnp.maximum(lo_ab, lo_cd))
    return first + second


def _outproj_kernel(x_ref, cb_ref, cv_ref, cvp_ref, cvn_ref, zp_ref, zpp_ref, zpn_ref,
                    ygl_ref, ygc_ref, yml_ref, ymc_ref,
                    mods_ref, convw_ref, band_ref, wpool_ref, pscale_ref,
                    wout_ref, n2_ref, wrt_ref, br_ref, tri_ref,
                    xo_ref, h2_ref, info_ref, cnt_ref,
                    *, d_model, geo):
    tile = pl.program_id(0)
    d = d_model
    tm = x_ref.shape[0]
    tg = _tile_geometry(tile, geo)
    is_ctx = tg["is_ctx"]
    keep_prev = jnp.where(tg["first"], 0.0, 1.0)
    keep_next = jnp.where(tg["last"], 0.0, 1.0)
    row = tg["mod_row"]
    gate1 = mods_ref[0, pl.ds(row, 1), 2 * d:3 * d]
    shift2 = mods_ref[0, pl.ds(row, 1), 3 * d:4 * d]
    scale2 = mods_ref[0, pl.ds(row, 1), 4 * d:5 * d]

    v = cv_ref[...].astype(F32)
    prev_row = cvp_ref[...].astype(F32)[BF16_ROWS - 1:BF16_ROWS] * keep_prev
    next_row = cvn_ref[...].astype(F32)[0:1] * keep_next
    rid = lax.broadcasted_iota(I32, (tm, CONV_DIM), 0)
    vm1 = jnp.where(rid == 0, prev_row, pltpu.roll(v, 1, 0))
    vp1 = jnp.where(rid == tm - 1, next_row, pltpu.roll(v, tm - 1, 0))
    cw = convw_ref[0]
    y_conv = cb_ref[...].astype(F32) * (vm1 * cw[0:1] + v * cw[1:2] + vp1 * cw[2:3])

    zp = zp_ref[...]
    ext = jnp.concatenate(
        [zpp_ref[...] * keep_prev, zp, zpn_ref[...] * keep_next,
         jnp.zeros((POOL_EXT - tm - 2 * POOL_HALO, POOL_DIM), F32)], axis=0).astype(BF16)
    ext_a, ext_b = ext[:, 0:LANES], ext[:, LANES:2 * LANES]
    lane = lax.broadcasted_iota(I32, (tm, LANES), 1)
    low = lane < POOL_DIM // 4
    sum_a = jnp.where(low, jnp.dot(band_ref[0], ext_a, preferred_element_type=F32),
                      jnp.dot(band_ref[1], ext_a, preferred_element_type=F32))
    sum_b = jnp.where(low, jnp.dot(band_ref[2], ext_b, preferred_element_type=F32),
                      jnp.dot(band_ref[3], ext_b, preferred_element_type=F32))
    sums = jnp.concatenate([sum_a, sum_b], axis=1)
    lane_p = lax.broadcasted_iota(I32, (tm, POOL_DIM), 1)
    half_w = jnp.left_shift(1, jnp.right_shift(lane_p, int(math.log2(POOL_DIM // 4))))
    pos = tg["pos0"] + rid
    cnt = (jnp.minimum(pos + half_w, tg["seg_len"]) - jnp.maximum(pos - half_w, 0)).astype(F32)
    dlt = sums / cnt - zp
    y_pool = jnp.dot(dlt.astype(BF16), wpool_ref[0], preferred_element_type=F32) * pscale_ref[0]

    y_gqa = jnp.where(is_ctx, ygc_ref[...], ygl_ref[...])
    y_mla = jnp.where(is_ctx, ymc_ref[...], yml_ref[...])
    ycat = jnp.concatenate([y_conv.astype(BF16), y_pool.astype(BF16), y_gqa, y_mla], axis=1)
    y = jnp.dot(ycat, wout_ref[0], preferred_element_type=F32)
    xn = x_ref[...] + gate1 * y
    xo_ref[...] = xn
    ms = jnp.mean(xn * xn, axis=-1, keepdims=True)
    h2 = (xn * lax.rsqrt(ms + NORM_EPS)) * (n2_ref[0] * (1.0 + scale2)) + shift2
    h2_ref[...] = h2

    logits = lax.dot_general(wrt_ref[...], h2, (((1,), (1,)), ((), ())),
                             preferred_element_type=F32, precision=HIGHEST)
    scores = _sigmoid(logits)
    sel = scores + br_ref[...]
    epg = EXPERTS_PER_GROUP
    n_groups = N_EXPERTS // epg
    srow = [sel[e:e + 1] for e in range(N_EXPERTS)]
    crow = [scores[e:e + 1] for e in range(N_EXPERTS)]
    gscore = [_top2_sum(*srow[g * epg:(g + 1) * epg]) for g in range(n_groups)]
    gbest = jnp.zeros_like(gscore[0]).astype(I32)
    best = gscore[0]
    for g in range(1, n_groups):
        upd = gscore[g] > best
        gbest = jnp.where(upd, g, gbest)
        best = jnp.where(upd, gscore[g], best)

    def pick(rows_, j):
        out = rows_[(n_groups - 1) * epg + j]
        for g in range(n_groups - 2, -1, -1):
            out = jnp.where(gbest == g, rows_[g * epg + j], out)
        return out

    sv = [pick(srow, j) for j in range(epg)]
    cv_ = [pick(crow, j) for j in range(epg)]
    i1 = jnp.zeros_like(gbest)
    b1 = sv[0]
    for j in range(1, epg):
        upd = sv[j] > b1
        i1 = jnp.where(upd, j, i1)
        b1 = jnp.where(upd, sv[j], b1)
    i2 = jnp.zeros_like(gbest)
    b2 = jnp.full_like(b1, -jnp.inf)
    for j in range(epg):
        upd = jnp.logical_and(i1 != j, sv[j] > b2)
        i2 = jnp.where(upd, j, i2)
        b2 = jnp.where(upd, sv[j], b2)
    s1 = cv_[epg - 1]
    s2 = cv_[epg - 1]
    for j in range(epg - 2, -1, -1):
        s1 = jnp.where(i1 == j, cv_[j], s1)
        s2 = jnp.where(i2 == j, cv_[j], s2)
    inv = 1.0 / (s1 + s2)
    e1 = gbest * epg + i1
    e2 = gbest * epg + i2

    @pl.when(tile == 0)
    def _():
        cnt_ref[...] = jnp.zeros_like(cnt_ref)

    erow = lax.broadcasted_iota(I32, (N_EXPERTS, tm), 0)
    hit1 = erow == e1
    hit2 = erow == e2
    onehot = jnp.where(hit1, 1.0, 0.0) + jnp.where(hit2, 1.0, 0.0)
    before = jnp.dot(onehot.astype(BF16), tri_ref[...], preferred_element_type=F32)
    tot = cnt_ref[:, 0:1] + before
    rank1 = jnp.sum(jnp.where(hit1, tot, 0.0), axis=0, keepdims=True).astype(I32)
    rank2 = jnp.sum(jnp.where(hit2, tot, 0.0), axis=0, keepdims=True).astype(I32)
    cnt_ref[...] = cnt_ref[...] + jnp.sum(onehot, axis=1, keepdims=True)

    w1 = lax.bitcast_convert_type(s1 * inv, I32)
    w2 = lax.bitcast_convert_type(s2 * inv, I32)
    irow = lax.broadcasted_iota(I32, (SUBLANES, tm), 0)
    info = jnp.where(irow == 0, e1, jnp.where(irow == 1, e2, jnp.where(
        irow == 2, rank1, jnp.where(irow == 3, rank2, jnp.where(
            irow == 4, w1, jnp.where(irow == 5, w2, 0))))))
    info_ref[0] = info


def _outproj(xa, mix, yg, ygc, ym, ymc, mods, layer, p, geo, n_tiles):
    d = xa.shape[1]
    nt_all, n_lat = geo["n_tiles"], geo["n_lat_tiles"]
    r = n_tiles * TM
    nt = n_tiles
    rowmap = lambda j: (j, 0)
    latmap = lambda j: (jnp.minimum(j, n_lat - 1), 0)
    ctxmap = lambda j: (jnp.clip(j - n_lat, 0, ygc.shape[0] // TM - 1), 0)
    lay3 = lambda j: (layer, 0, 0)
    c0 = lambda j: (0, 0)
    c3 = lambda j: (0, 0, 0)
    bf_blocks = TM // BF16_ROWS
    f_blocks = TM // SUBLANES

    def full3(a):
        return pl.BlockSpec((1,) + a.shape[1:], lay3)

    return pl.pallas_call(
        functools.partial(_outproj_kernel, d_model=d, geo=geo),
        grid=(n_tiles,),
        in_specs=[
            pl.BlockSpec((TM, d), rowmap),
            pl.BlockSpec((TM, CONV_DIM), rowmap),
            pl.BlockSpec((TM, CONV_DIM), rowmap),
            pl.BlockSpec((BF16_ROWS, CONV_DIM), lambda j: (jnp.maximum(j * bf_blocks - 1, 0), 0)),
            pl.BlockSpec((BF16_ROWS, CONV_DIM),
                         lambda j: (jnp.minimum((j + 1) * bf_blocks, nt_all * bf_blocks - 1), 0)),
            pl.BlockSpec((TM, POOL_DIM), rowmap),
            pl.BlockSpec((SUBLANES, POOL_DIM), lambda j: (jnp.maximum(j * f_blocks - 1, 0), 0)),
            pl.BlockSpec((SUBLANES, POOL_DIM),
                         lambda j: (jnp.minimum((j + 1) * f_blocks, nt_all * f_blocks - 1), 0)),
            pl.BlockSpec((TM, 2 * LANES), latmap),
            pl.BlockSpec((TM, 2 * LANES), ctxmap),
            pl.BlockSpec((TM, 2 * LANES), latmap),
            pl.BlockSpec((TM, 2 * LANES), ctxmap),
            full3(mods), full3(p["conv_w"]),
            pl.BlockSpec(p["band"].shape, c3),
            full3(p["wpool"]), full3(p["pscale"]), full3(p["w_out"]), full3(p["norm2"]),
            pl.BlockSpec(p["wrt"].shape, c0), pl.BlockSpec(p["br"].shape, c0),
            pl.BlockSpec(p["tri"].shape, c0),
        ],
        out_specs=[
            pl.BlockSpec((TM, d), rowmap),
            pl.BlockSpec((TM, d), rowmap),
            pl.BlockSpec((1, SUBLANES, TM), lambda j: (j, 0, 0)),
            pl.BlockSpec((N_EXPERTS, LANES), c0),
        ],
        out_shape=[
            jax.ShapeDtypeStruct((r, d), F32),
            jax.ShapeDtypeStruct((r, d), F32),
            jax.ShapeDtypeStruct((nt, SUBLANES, TM), I32),
            jax.ShapeDtypeStruct((N_EXPERTS, LANES), F32),
        ],
        compiler_params=_cparams(("arbitrary",)),
        name="outproj",
    )(xa, mix["cb"], mix["cv"], mix["cv"], mix["cv"], mix["zp"], mix["zp"], mix["zp"],
      yg, ygc, ym, ymc, mods, p["conv_w"], p["band"], p["wpool"], p["pscale"], p["w_out"],
      p["norm2"], p["wrt"], p["br"], p["tri"])


def _row_copies(idx_ref, n_rows, make):
    def body(r, carry):
        make(0, r, idx_ref[0, r]).start()
        make(1, r, idx_ref[1, r]).start()
        return carry
    lax.fori_loop(0, n_rows, body, 0, unroll=ROW_COPY_UNROLL)


def _scatter_kernel(ps_ref, zs_ref, has_ref, nu_ref, info_ref, h2_ref, dest_ref, xs_ref,
                    dsm_ref, zbuf_ref, stage_ref, sem_ref, *, first_spare, n_blocks, n_tiles):
    j = pl.program_id(0)
    tm = h2_ref.shape[0]

    @pl.when(j == 0)
    def _():
        zbuf_ref[...] = jnp.zeros_like(zbuf_ref)

        def zero_copy(start):
            return pltpu.make_async_copy(
                zbuf_ref, xs_ref.at[pl.ds(pl.multiple_of(start, BM), BM)], sem_ref.at[2])

        for act in ("start", "wait"):
            for e in range(N_EXPERTS):
                @pl.when(has_ref[e] > 0)
                def _():
                    getattr(zero_copy(zs_ref[e]), act)()
            for jb in range(first_spare, n_blocks):
                @pl.when(jb >= nu_ref[0])
                def _():
                    getattr(zero_copy(jb * BM), act)()

    info = info_ref[0]
    e1, e2 = info[0:1], info[1:2]
    d1, d2 = info[2:3], info[3:4]
    for e in range(N_EXPERTS):
        d1 = d1 + jnp.where(e1 == e, ps_ref[e], 0)
        d2 = d2 + jnp.where(e2 == e, ps_ref[e], 0)
    irow = lax.broadcasted_iota(I32, (SUBLANES, tm), 0)
    dest_ref[0] = jnp.where(irow == 0, d1, jnp.where(irow == 1, d2, 0))
    to_smem = pltpu.make_async_copy(dest_ref.at[0], dsm_ref, sem_ref.at[3])
    to_smem.start()
    to_smem.wait()

    par = j % 2
    stage_ref[par] = h2_ref[...]

    def row_copy(_, r, slot):
        return pltpu.make_async_copy(stage_ref.at[par, pl.ds(r, 1)], xs_ref.at[pl.ds(slot, 1)],
                                     sem_ref.at[par])

    _row_copies(dsm_ref, tm, row_copy)

    def retire(which):
        for _ in range(2):
            pltpu.make_async_copy(stage_ref.at[which], xs_ref.at[pl.ds(0, tm)],
                                  sem_ref.at[which]).wait()

    @pl.when(j > 0)
    def _():
        retire(1 - par)

    @pl.when(j == n_tiles - 1)
    def _():
        retire(par)


def _scatter(info, h2, pad_start, zero_start, has_rows, n_used, n_slots):
    r, d = h2.shape
    nt = r // TM
    return pl.pallas_call(
        functools.partial(_scatter_kernel, first_spare=-(-2 * r // BM), n_blocks=n_slots // BM,
                          n_tiles=nt),
        grid_spec=pltpu.PrefetchScalarGridSpec(
            num_scalar_prefetch=4,
            grid=(nt,),
            in_specs=[
                pl.BlockSpec((1, SUBLANES, TM), lambda j, *_: (j, 0, 0)),
                pl.BlockSpec((TM, d), lambda j, *_: (j, 0)),
            ],
            out_specs=[
                pl.BlockSpec((1, SUBLANES, TM), lambda j, *_: (j, 0, 0)),
                pl.BlockSpec(memory_space=pl.ANY),
            ],
            scratch_shapes=[
                pltpu.SMEM((SUBLANES, TM), I32),
                pltpu.VMEM((BM, d), F32),
                pltpu.VMEM((2, TM, d), F32),
                pltpu.SemaphoreType.DMA((4,)),
            ],
        ),
        out_shape=[
            jax.ShapeDtypeStruct((nt, SUBLANES, TM), I32),
            jax.ShapeDtypeStruct((n_slots, d), F32),
        ],
        compiler_params=_cparams(("arbitrary",)),
        name="moe_scatter",
    )(pad_start, zero_start, has_rows, n_used, info, h2)


def _expert_kernel(be_ref, nu_ref, xs_ref, wg_ref, wu_ref, wd_ref, y_ref, wgb, wub, wdb):
    j = pl.program_id(0)

    @pl.when(j < nu_ref[0])
    def _():
        e = be_ref[j]
        prev = be_ref[jnp.maximum(j - 1, 0)]

        @pl.when(jnp.logical_or(j == 0, e != prev))
        def _():
            wgb[...] = wg_ref[0].astype(BF16)
            wub[...] = wu_ref[0].astype(BF16)
            wdb[...] = wd_ref[0].astype(BF16)

        x = xs_ref[...].astype(BF16)
        a = jnp.dot(x, wgb[...], preferred_element_type=F32)
        u = jnp.dot(x, wub[...], preferred_element_type=F32)
        hmid = (a * _sigmoid(a) * u).astype(BF16)
        y_ref[...] = jnp.dot(hmid, wdb[...], preferred_element_type=F32)

    @pl.when(j >= nu_ref[0])
    def _():
        y_ref[...] = jnp.zeros_like(y_ref)


def _experts(xs, block_exp, n_used, w_gate, w_up, w_down, layer):
    n_slots, d = xs.shape
    de = w_gate.shape[-1]
    nbm = n_slots // BM
    n_exp = w_gate.shape[1]

    def blk(j, be, nu):
        return (jnp.minimum(j, nu[0] - 1), 0)

    def wmap(j, be, nu):
        return (layer * n_exp + be[jnp.minimum(j, nu[0] - 1)], 0, 0)

    wg = w_gate.reshape((-1,) + w_gate.shape[2:])
    wu = w_up.reshape((-1,) + w_up.shape[2:])
    wd = w_down.reshape((-1,) + w_down.shape[2:])
    return pl.pallas_call(
        _expert_kernel,
        grid_spec=pltpu.PrefetchScalarGridSpec(
            num_scalar_prefetch=2,
            grid=(nbm,),
            in_specs=[
                pl.BlockSpec((BM, d), blk),
                pl.BlockSpec((1, d, de), wmap),
                pl.BlockSpec((1, d, de), wmap),
                pl.BlockSpec((1, de, d), wmap),
            ],
            out_specs=pl.BlockSpec((BM, d), lambda j, be, nu: (j, 0)),
            scratch_shapes=[
                pltpu.VMEM((d, de), BF16),
                pltpu.VMEM((d, de), BF16),
                pltpu.VMEM((de, d), BF16),
            ],
        ),
        out_shape=jax.ShapeDtypeStruct((n_slots, d), F32),
        compiler_params=_cparams(("arbitrary",)),
        name="moe_experts",
    )(block_exp, n_used, xs, wg, wu, wd)


def _combine_kernel(dest_ref, dest_next_ref, info_ref, x_ref, mods_ref, y_ref, o_ref,
                    dsm_ref, ybuf_ref, sem_ref, *, d_model, geo, n_steps):
    j = pl.program_id(0)
    d = d_model
    tm = x_ref.shape[0]
    row = _tile_geometry(j, geo)["mod_row"]
    gate2 = mods_ref[0, pl.ds(row, 1), 5 * d:6 * d]
    cur = j % 2

    def gather(idx_block_ref, buf):
        to_smem = pltpu.make_async_copy(idx_block_ref.at[0], dsm_ref, sem_ref.at[2])
        to_smem.start()
        to_smem.wait()

        def row_copy(k, r, slot):
            return pltpu.make_async_copy(y_ref.at[pl.ds(slot, 1)],
                                         ybuf_ref.at[buf, k, pl.ds(r, 1)], sem_ref.at[buf])

        _row_copies(dsm_ref, tm, row_copy)

    @pl.when(j == 0)
    def _():
        gather(dest_ref, 0)

    @pl.when(j + 1 < n_steps)
    def _():
        gather(dest_next_ref, 1 - cur)

    for k in range(2):
        pltpu.make_async_copy(y_ref.at[pl.ds(0, tm)], ybuf_ref.at[cur, k], sem_ref.at[cur]).wait()

    info = info_ref[0]
    wrow = lax.broadcasted_iota(I32, (LANES, tm), 0)
    w_lanes = jnp.where(wrow == 0, lax.bitcast_convert_type(info[4:5], F32),
                        jnp.where(wrow == 1, lax.bitcast_convert_type(info[5:6], F32), 0.0))
    w_rows = w_lanes.T
    mixed = w_rows[:, 0:1] * ybuf_ref[cur, 0] + w_rows[:, 1:2] * ybuf_ref[cur, 1]
    o_ref[...] = x_ref[...] + gate2 * mixed


def _combine(dest, info, xn, mods, y, layer, geo):
    r, d = xn.shape
    n_steps = r // TM
    return pl.pallas_call(
        functools.partial(_combine_kernel, d_model=d, geo=geo, n_steps=n_steps),
        grid=(n_steps,),
        in_specs=[
            pl.BlockSpec((1, SUBLANES, TM), lambda j: (j, 0, 0)),
            pl.BlockSpec((1, SUBLANES, TM), lambda j: (jnp.minimum(j + 1, n_steps - 1), 0, 0)),
            pl.BlockSpec((1, SUBLANES, TM), lambda j: (j, 0, 0)),
            pl.BlockSpec((TM, d), lambda j: (j, 0)),
            pl.BlockSpec((1,) + mods.shape[1:], lambda j: (layer, 0, 0)),
            pl.BlockSpec(memory_space=pl.ANY),
        ],
        out_specs=pl.BlockSpec((TM, d), lambda j: (j, 0)),
        out_shape=jax.ShapeDtypeStruct((n_steps * TM, d), F32),
        scratch_shapes=[
            pltpu.SMEM((SUBLANES, TM), I32),
            pltpu.VMEM((2, 2, TM, d), F32),
            pltpu.SemaphoreType.DMA((3,)),
        ],
        compiler_params=_cparams(("arbitrary",)),
        name="moe_combine",
    )(dest, dest, info, xn, mods, y)


def _pad_heads(w, n_heads, width):
    lead = w.shape[:-1]
    w = w.reshape(lead + (n_heads, width))
    w = jnp.pad(w, [(0, 0)] * len(lead) + [(0, 0), (0, LANES - width)])
    return w.reshape(lead + (n_heads * LANES,))


def _rope_tables(nc, nl):
    t = jnp.arange(nl)
    row_id = (t // GRID_W).astype(F32)
    col_id = (t % GRID_W).astype(F32)

    def angles(rot_dim):
        n_freq = rot_dim // 4
        inv_freq = jnp.power(ROPE_THETA, -jnp.arange(n_freq, dtype=F32) / n_freq)
        return jnp.concatenate([row_id[:, None] * inv_freq, col_id[:, None] * inv_freq], axis=-1)

    def with_ctx(tab, fill):
        return jnp.concatenate([tab, jnp.full((TM, LANES), fill, F32)], axis=0)

    ag = angles(HEAD_DIM)
    one_g = jnp.ones((nl, LANES - HEAD_DIM), F32)
    cos_g = jnp.concatenate([jnp.cos(ag), jnp.cos(ag), one_g], axis=-1)
    sin_g = jnp.concatenate([-jnp.sin(ag), jnp.sin(ag), 0.0 * one_g], axis=-1)
    am = angles(MLA_ROPE_DIM)
    one_n = jnp.ones((nl, MLA_NOPE_DIM), F32)
    one_t = jnp.ones((nl, LANES - MLA_QK_DIM), F32)
    cos_m = jnp.concatenate([one_n, jnp.cos(am), jnp.cos(am), one_t], axis=-1)
    sin_m = jnp.concatenate([0.0 * one_n, -jnp.sin(am), jnp.sin(am), 0.0 * one_t], axis=-1)
    return with_ctx(cos_g, 1.0), with_ctx(sin_g, 0.0), with_ctx(cos_m, 1.0), with_ctx(sin_m, 0.0)


def _pool_band():
    t = np.arange(TM)[:, None]
    src = np.arange(POOL_EXT)[None, :] - POOL_HALO
    live = np.arange(POOL_EXT)[None, :] < TM + 2 * POOL_HALO
    mats = [((src >= t - w // 2) & (src < t + w // 2) & live) for w in POOL_WINDOWS]
    return jnp.asarray(np.stack(mats).astype(np.float32), dtype=BF16)


def _prep_params(w_in, norm1, norm2, conv_w, w_pool, pool_scale, gqa_q_norm, gqa_k_norm,
                 mla_q_norm, mla_kv_norm, mla_w_uq, mla_w_uk, mla_w_uv, mla_qk_q_norm,
                 mla_qk_k_norm, w_out, w_router, b_router, nc, nl):
    dep = w_in.shape[0]
    o = 0
    pieces = {}
    for name, n in (("conv", 3 * CONV_DIM), ("pool", POOL_DIM), ("gq", GQA_HEADS * HEAD_DIM),
                    ("gk", GQA_KV_HEADS * HEAD_DIM), ("gv", GQA_KV_HEADS * HEAD_DIM),
                    ("mq", MLA_Q_RANK), ("mkv", MLA_KV_RANK), ("mkr", MLA_ROPE_DIM)):
        pieces[name] = w_in[..., o:o + n]
        o += n
    mkr = jnp.pad(pieces["mkr"], ((0, 0), (0, 0), (MLA_NOPE_DIM, LANES - MLA_QK_DIM)))
    w_in_p = jnp.concatenate([
        pieces["conv"], pieces["pool"], _pad_heads(pieces["gq"], GQA_HEADS, HEAD_DIM),
        _pad_heads(pieces["gk"], GQA_KV_HEADS, HEAD_DIM),
        pieces["mq"], pieces["mkv"], mkr], axis=-1).astype(BF16)
    wgvt = jnp.swapaxes(_pad_heads(pieces["gv"], GQA_KV_HEADS, HEAD_DIM), 1, 2).astype(BF16)
    wuvt = jnp.swapaxes(_pad_heads(mla_w_uv, MLA_HEADS, MLA_V_DIM), 1, 2).astype(BF16)
    eye = jnp.eye(len(POOL_WINDOWS), dtype=F32)
    wpool = jnp.einsum("gh,dgij->dgihj", eye, w_pool).reshape(dep, POOL_DIM, POOL_DIM).astype(BF16)
    cos_g, sin_g, cos_m, sin_m = _rope_tables(nc, nl)
    tri = np.triu(np.ones((TM, TM), np.float32), 1)

    def row3(a):
        return a.reshape(dep, 1, a.shape[-1])

    return {
        "w_in": w_in_p,
        "norm1": row3(norm1), "norm2": row3(norm2),
        "gq": row3(_pad_heads(gqa_q_norm * (GQA_SCALE * LOG2E), 1, HEAD_DIM)),
        "gk": row3(_pad_heads(gqa_k_norm, 1, HEAD_DIM)),
        "mqn": row3(mla_q_norm), "mkvn": row3(mla_kv_norm),
        "qkq": row3(_pad_heads(mla_qk_q_norm * (MLA_SCALE * LOG2E), 1, MLA_QK_DIM)),
        "qkk": row3(_pad_heads(mla_qk_k_norm, 1, MLA_QK_DIM)),
        "wuq": _pad_heads(mla_w_uq, MLA_HEADS, MLA_QK_DIM).astype(BF16),
        "wuk": _pad_heads(mla_w_uk, MLA_HEADS, MLA_NOPE_DIM).astype(BF16),
        "wgvt": wgvt, "wuvt": wuvt,
        "cos_g": cos_g, "sin_g": sin_g, "cos_m": cos_m, "sin_m": sin_m,
        "conv_w": jnp.pad(conv_w, ((0, 0), (0, SUBLANES - conv_w.shape[1]), (0, 0))),
        "band": _pool_band(),
        "wpool": wpool, "pscale": row3(pool_scale),
        "w_out": w_out.astype(BF16),
        "wrt": w_router.T, "br": b_router.reshape(-1, 1),
        "tri": jnp.asarray(tri, dtype=BF16),
    }


def _moe_plan(counts):
    counts = counts.astype(I32)
    padded = ((counts + BM - 1) // BM) * BM
    pad_end = jnp.cumsum(padded)
    pad_start = pad_end - padded
    return pad_start, pad_end, padded


def kernel(x, c, ctx, c_ctx, w_mod, b_mod, norm1, norm2, w_in, conv_w, w_pool, pool_scale,
           gqa_q_norm, gqa_k_norm, mla_q_norm, mla_kv_norm, mla_w_uq, mla_w_uk, mla_w_uv,
           mla_qk_q_norm, mla_qk_k_norm, w_out, w_router, b_router, w_gate, w_up, w_down):
    nb, nl, d = x.shape
    nc = ctx.shape[1]
    depth = w_mod.shape[0]
    assert nc % TM == 0 and nl % TM == 0 and nl % GRID_W == 0 and nb < SUBLANES
    lt, nct = nl // TM, nc // TM
    geo = {"nbatch": nb, "nc": nc, "nl": nl, "lt": lt, "nct": nct,
           "n_lat_tiles": nb * lt, "n_tiles": nb * (lt + nct)}

    p = _prep_params(w_in, norm1, norm2, conv_w, w_pool, pool_scale, gqa_q_norm, gqa_k_norm,
                     mla_q_norm, mla_kv_norm, mla_w_uq, mla_w_uk, mla_w_uv, mla_qk_q_norm,
                     mla_qk_k_norm, w_out, w_router, b_router, nc, nl)
    cvec = jnp.concatenate([c, c_ctx[None, :], jnp.zeros((SUBLANES - nb - 1, d), F32)], axis=0)
    mods = _adaln(cvec, w_mod, b_mod)

    xa = jnp.concatenate([x.reshape(nb * nl, d), ctx.reshape(nb * nc, d)], axis=0)
    for i in range(depth):
        last = i == depth - 1
        mix = _inproj(xa, mods, i, p, geo)
        yg = _attention(mix["qg"], mix["kg"], mix["vgt"], True, geo, False)
        ym = _attention(mix["qm"], mix["km"], mix["vmt"], False, geo, False)
        if last:
            n_tiles, ygc, ymc = geo["n_lat_tiles"], yg, ym
        else:
            n_tiles = geo["n_tiles"]
            ygc = _attention(mix["qg"], mix["kg"], mix["vgt"], True, geo, True)
            ymc = _attention(mix["qm"], mix["km"], mix["vmt"], False, geo, True)
        xn, h2, info, cnt = _outproj(xa, mix, yg, ygc, ym, ymc, mods, i, p, geo, n_tiles)
        n_blocks = -(-2 * n_tiles * TM // BM) + N_EXPERTS
        n_slots = n_blocks * BM
        pad_start, pad_end, padded = _moe_plan(cnt[:, 0])
        n_used = (pad_end[-1:] // BM).astype(I32)
        block_row0 = jnp.arange(n_blocks, dtype=I32) * BM
        block_exp = jnp.minimum(jnp.sum((pad_end[None, :] <= block_row0[:, None]).astype(I32), axis=1),
                                N_EXPERTS - 1)
        dest, xs = _scatter(info, h2, pad_start, jnp.maximum(pad_end - BM, 0),
                            (padded > 0).astype(I32), n_used, n_slots)
        y = _experts(xs, block_exp, n_used, w_gate, w_up, w_down, i)
        xa = _combine(dest, info, xn, mods, y, i, geo)
    return xa.reshape(nb, nl, d)
```

```python
import functools
import math

import numpy as np
import jax
import jax.numpy as jnp
from jax import lax
from jax.experimental import pallas as pl
from jax.experimental.pallas import tpu as pltpu

F32 = jnp.float32
BF16 = jnp.bfloat16
I32 = jnp.int32

GRID_W = 64
CONV_DIM = 256
POOL_DIM = 256
POOL_WINDOWS = (2, 4, 8, 16)
HEAD_DIM = 64
GQA_HEADS = 4
GQA_KV_HEADS = 2
MLA_HEADS = 4
MLA_NOPE_DIM = 64
MLA_ROPE_DIM = 32
MLA_QK_DIM = MLA_NOPE_DIM + MLA_ROPE_DIM
MLA_V_DIM = 64
MLA_Q_RANK = 256
MLA_KV_RANK = 128
N_EXPERTS = 16
EXPERTS_PER_GROUP = 4
ROPE_THETA = 10000.0
NORM_EPS = 1e-6
LOG2E = 1.4426950408889634
GQA_SCALE = HEAD_DIM ** -0.5
MLA_SCALE = MLA_QK_DIM ** -0.5

LANES = 128
SUBLANES = 8
BF16_ROWS = 16
VMEM_LIMIT = 56 * 1024 * 1024

TM = 256
ATT_ROWS = 2048
ATT_TK = 256
BM = 512
V_ROWS = HEAD_DIM + BF16_ROWS
ATT_PAIR_UNROLL = 4
ROW_COPY_UNROLL = 8
POOL_EXT = 512
POOL_HALO = 8

ZC_B, ZC_C, ZC_U, ZC_P = 0, 256, 512, 768
ZC_GQ = 1024
ZC_GK = ZC_GQ + GQA_HEADS * LANES
ZC_MQ = ZC_GK + GQA_KV_HEADS * LANES
ZC_MKV = ZC_MQ + MLA_Q_RANK
ZC_MKR = ZC_MKV + MLA_KV_RANK
ZC_END = ZC_MKR + LANES

HIGHEST = lax.Precision.HIGHEST


def _cparams(sem, vmem=VMEM_LIMIT):
    return pltpu.CompilerParams(dimension_semantics=sem, vmem_limit_bytes=vmem)


def _sigmoid(v):
    return 1.0 / (1.0 + jnp.exp(-v))


def _adaln_kernel(c_ref, w_ref, b_ref, o_ref):
    c = c_ref[...]
    s = c * _sigmoid(c)
    o_ref[0] = jnp.dot(s, w_ref[0], preferred_element_type=F32, precision=HIGHEST) + b_ref[0]


def _adaln(cvec, w_mod, b_mod):
    depth, d, n6 = w_mod.shape
    tn = 1536 if n6 % 1536 == 0 else n6
    rows = cvec.shape[0]
    return pl.pallas_call(
        _adaln_kernel,
        grid=(depth, n6 // tn),
        in_specs=[
            pl.BlockSpec((rows, d), lambda i, n: (0, 0)),
            pl.BlockSpec((1, d, tn), lambda i, n: (i, 0, n)),
            pl.BlockSpec((1, 1, tn), lambda i, n: (i, 0, n)),
        ],
        out_specs=pl.BlockSpec((1, rows, tn), lambda i, n: (i, 0, n)),
        out_shape=jax.ShapeDtypeStruct((depth, rows, n6), F32),
        compiler_params=_cparams(("arbitrary", "arbitrary")),
        name="adaln",
    )(cvec, w_mod, b_mod.reshape(depth, 1, n6))


def _tile_geometry(j, geo):
    lt, nct, n_lat = geo["lt"], geo["nct"], geo["n_lat_tiles"]
    is_ctx = j >= n_lat
    jc = j - n_lat
    sample = jnp.where(is_ctx, jc // nct, j // lt)
    jt = jnp.where(is_ctx, jc % nct, j % lt)
    return {
        "is_ctx": is_ctx,
        "mod_row": jnp.where(is_ctx, geo["nbatch"], sample),
        "first": jt == 0,
        "last": jt == jnp.where(is_ctx, nct, lt) - 1,
        "pos0": jt * TM,
        "seg_len": jnp.where(is_ctx, geo["nc"], geo["nl"]),
    }


def _norm_rope(slab, gain, cos, sin, n_valid, first_half, half):
    ms = jnp.sum(slab * slab, axis=-1, keepdims=True) * (1.0 / n_valid)
    y = slab * lax.rsqrt(ms + NORM_EPS) * gain
    partner = jnp.where(first_half, pltpu.roll(y, LANES - half, 1), pltpu.roll(y, half, 1))
    return y * cos + partner * sin


def _inproj_kernel(x_ref, mods_ref, n1_ref, cg_ref, sg_ref, cm_ref, sm_ref, win_ref,
                   gq_ref, gk_ref, mqn_ref, mkvn_ref, qkq_ref, qkk_ref, wuq_ref, wuk_ref,
                   wgvt_ref, wuvt_ref,
                   cb_ref, cv_ref, zp_ref, qg_ref, kg_ref, vgt_ref, qm_ref, km_ref, vmt_ref,
                   *, d_model, geo):
    d = d_model
    row = _tile_geometry(pl.program_id(0), geo)["mod_row"]
    shift = mods_ref[0, pl.ds(row, 1), 0:d]
    scale = mods_ref[0, pl.ds(row, 1), d:2 * d]
    x = x_ref[...]
    ms = jnp.mean(x * x, axis=-1, keepdims=True)
    h = ((x * lax.rsqrt(ms + NORM_EPS)) * (n1_ref[0] * (1.0 + scale)) + shift).astype(BF16)
    nt_dims = (((1,), (1,)), ((), ()))

    def proj(c0, width):
        return jnp.dot(h, win_ref[0, :, c0:c0 + width], preferred_element_type=F32)

    def with_ones_row(vt):
        srow = lax.broadcasted_iota(I32, vt.shape, 0)
        return jnp.where(jnp.bitwise_and(srow, LANES - 1) == HEAD_DIM, 1.0, vt).astype(BF16)

    tm = x.shape[0]
    lane = lax.broadcasted_iota(I32, (tm, LANES), 1)

    cm, sm = cm_ref[...], sm_ref[...]
    m_first = lane < MLA_NOPE_DIM + MLA_ROPE_DIM // 2
    zq = proj(ZC_MQ, MLA_Q_RANK)
    cq = zq * lax.rsqrt(jnp.mean(zq * zq, axis=-1, keepdims=True) + NORM_EPS) * mqn_ref[0]
    qpre = jnp.dot(cq.astype(BF16), wuq_ref[0], preferred_element_type=F32)
    zk = proj(ZC_MKV, MLA_KV_RANK + LANES)
    zkv, zkr = zk[:, 0:MLA_KV_RANK], zk[:, MLA_KV_RANK:]
    ckv = zkv * lax.rsqrt(jnp.mean(zkv * zkv, axis=-1, keepdims=True) + NORM_EPS) * mkvn_ref[0]
    ckv = ckv.astype(BF16)
    kvp = jnp.dot(ckv, wuk_ref[0], preferred_element_type=F32)
    vmt_ref[0] = with_ones_row(lax.dot_general(wuvt_ref[0], ckv, nt_dims, preferred_element_type=F32))
    for hd in range(MLA_HEADS):
        sl = slice(hd * LANES, (hd + 1) * LANES)
        qm_ref[:, sl] = _norm_rope(qpre[:, sl], qkq_ref[0], cm, sm, MLA_QK_DIM,
                                   m_first, MLA_ROPE_DIM // 2).astype(BF16)
        km_ref[:, sl] = _norm_rope(kvp[:, sl] + zkr, qkk_ref[0], cm, sm, MLA_QK_DIM,
                                   m_first, MLA_ROPE_DIM // 2).astype(BF16)

    cg, sg = cg_ref[...], sg_ref[...]
    g_first = lane < HEAD_DIM // 2
    zg = proj(ZC_GQ, (GQA_HEADS + GQA_KV_HEADS) * LANES)
    for hd in range(GQA_HEADS):
        qg_ref[:, hd * LANES:(hd + 1) * LANES] = _norm_rope(
            zg[:, hd * LANES:(hd + 1) * LANES], gq_ref[0], cg, sg, HEAD_DIM, g_first,
            HEAD_DIM // 2).astype(BF16)
    for hd in range(GQA_KV_HEADS):
        slab = zg[:, (GQA_HEADS + hd) * LANES:(GQA_HEADS + hd + 1) * LANES]
        kg_ref[:, hd * LANES:(hd + 1) * LANES] = _norm_rope(
            slab, gk_ref[0], cg, sg, HEAD_DIM, g_first, HEAD_DIM // 2).astype(BF16)
    vgt_ref[0] = with_ones_row(lax.dot_general(wgvt_ref[0], h, nt_dims, preferred_element_type=F32))

    zc = proj(ZC_B, 3 * CONV_DIM + POOL_DIM)
    cb_ref[...] = zc[:, ZC_B:ZC_B + CONV_DIM].astype(BF16)
    cv_ref[...] = (zc[:, ZC_C:ZC_C + CONV_DIM] * zc[:, ZC_U:ZC_U + CONV_DIM]).astype(BF16)
    zp_ref[...] = zc[:, ZC_P:ZC_P + POOL_DIM]


def _inproj(xa, mods, layer, p, geo):
    r, d = xa.shape
    lt, n_lat = geo["lt"], geo["n_lat_tiles"]
    rowmap = lambda j: (j, 0)
    posmap = lambda j: (jnp.where(j >= n_lat, lt, j % lt), 0)
    lay3 = lambda j: (layer, 0, 0)
    nct, tps = geo["nct"], geo["lt"] + geo["nct"]

    def kv_tile(j):
        jc = j - n_lat
        return jnp.where(j >= n_lat, (jc // nct) * tps + lt + jc % nct, (j // lt) * tps + j % lt)

    def full3(a):
        return pl.BlockSpec((1,) + a.shape[1:], lay3)

    outs = [("cb", CONV_DIM, BF16, "rows"), ("cv", CONV_DIM, BF16, "rows"), ("zp", POOL_DIM, F32, "rows"),
            ("qg", GQA_HEADS * LANES, BF16, "rows"), ("kg", GQA_KV_HEADS * LANES, BF16, "keys"),
            ("vgt", GQA_KV_HEADS * LANES, BF16, "keys_t"), ("qm", MLA_HEADS * LANES, BF16, "rows"),
            ("km", MLA_HEADS * LANES, BF16, "keys"), ("vmt", MLA_HEADS * LANES, BF16, "keys_t")]

    def out_spec(w, layout):
        if layout == "keys_t":
            return pl.BlockSpec((1, w, TM), lambda j: (kv_tile(j), 0, 0))
        return pl.BlockSpec((TM, w), rowmap if layout == "rows" else (lambda j: (kv_tile(j), 0)))

    def out_shape(w, dt, layout):
        return jax.ShapeDtypeStruct((r // TM, w, TM) if layout == "keys_t" else (r, w), dt)

    res = pl.pallas_call(
        functools.partial(_inproj_kernel, d_model=d, geo=geo),
        grid=(geo["n_tiles"],),
        in_specs=[
            pl.BlockSpec((TM, d), rowmap),
            full3(mods), full3(p["norm1"]),
            pl.BlockSpec((TM, LANES), posmap), pl.BlockSpec((TM, LANES), posmap),
            pl.BlockSpec((TM, LANES), posmap), pl.BlockSpec((TM, LANES), posmap),
            full3(p["w_in"]), full3(p["gq"]), full3(p["gk"]), full3(p["mqn"]), full3(p["mkvn"]),
            full3(p["qkq"]), full3(p["qkk"]), full3(p["wuq"]), full3(p["wuk"]),
            full3(p["wgvt"]), full3(p["wuvt"]),
        ],
        out_specs=[out_spec(w, layout) for _, w, _, layout in outs],
        out_shape=[out_shape(w, dt, layout) for _, w, dt, layout in outs],
        compiler_params=_cparams(("arbitrary",)),
        name="inproj",
    )(xa, mods, p["norm1"], p["cos_g"], p["sin_g"], p["cos_m"], p["sin_m"], p["w_in"],
      p["gq"], p["gk"], p["mqn"], p["mkvn"], p["qkq"], p["qkk"], p["wuq"], p["wuk"],
      p["wgvt"], p["wuvt"])
    return {name: a for (name, _, _, _), a in zip(outs, res)}


def _attn_kernel(q_ref, k_ref, vt_ref, o_ref, q_st, s_buf, p_buf, a_buf, m_ref, acc_ref,
                 *, shared_kv, n_steps, tk):
    tq = q_ref.shape[0]
    if shared_kv:
        q_st[0] = jnp.concatenate([q_ref[:, 0:LANES], q_ref[:, LANES:2 * LANES]], axis=0)
        cols = [0]
    else:
        q_st[0] = q_ref[:, 0:LANES]
        q_st[1] = q_ref[:, LANES:2 * LANES]
        cols = [0, LANES]
    for si, col in enumerate(cols):
        q_s, m_s, acc_s = q_st.at[si], m_ref.at[si], acc_ref.at[si]
        m_s[...] = jnp.full(m_s.shape, -1e30, F32)
        acc_s[...] = jnp.zeros(acc_s.shape, F32)
        p_buf[1] = jnp.zeros(p_buf.shape[1:], BF16)
        a_buf[1] = jnp.ones(a_buf.shape[1:], F32)

        def key_tile(i):
            return jnp.clip(i, 0, n_steps - 1)

        def scores(i, slot, q_s=q_s, col=col):
            k = k_ref[pl.ds(pl.multiple_of(key_tile(i) * tk, tk), tk), col:col + LANES]
            s_buf[slot] = lax.dot_general(k, q_s[...], (((1,), (1,)), ((), ())),
                                          preferred_element_type=F32)

        def softmax(i, slot, m_s=m_s):
            s = s_buf[slot]
            m_old = m_s[...]
            m_new = jnp.maximum(m_old, jnp.max(s, axis=0, keepdims=True))
            a_buf[slot] = jnp.exp2(m_old - m_new)
            m_eff = m_new[0:1] + jnp.where(i < n_steps, 0.0, 1e9)
            p_buf[slot] = jnp.exp2(s - m_eff).astype(BF16)
            m_s[...] = m_new

        def accumulate(i, slot, acc_s=acc_s, col=col):
            vt = vt_ref[key_tile(i), col:col + V_ROWS, :]
            acc_s[...] = a_buf[slot][0:1] * acc_s[...] + jnp.dot(vt, p_buf[slot],
                                                                 preferred_element_type=F32)

        scores(0, 0)

        def pair(t, carry):
            i = 2 * t
            scores(i + 1, 1)
            softmax(i, 0)
            accumulate(i - 1, 1)
            scores(i + 2, 0)
            softmax(i + 1, 1)
            accumulate(i, 0)
            return carry

        lax.fori_loop(0, (n_steps + 2) // 2, pair, 0, unroll=ATT_PAIR_UNROLL)
    def finish(acc_t):
        o_t = acc_t * (1.0 / acc_t[HEAD_DIM:HEAD_DIM + 1])
        pad = jnp.zeros((LANES - V_ROWS, o_t.shape[1]), F32)
        return jnp.concatenate([o_t, pad], axis=0).T

    if shared_kv:
        o_both = finish(acc_ref[0])
        o0, o1 = o_both[0:tq], o_both[tq:2 * tq]
    else:
        o0, o1 = finish(acc_ref[0]), finish(acc_ref[1])
    lane = lax.broadcasted_iota(I32, (tq, LANES), 1)
    o_ref[...] = jnp.where(lane < HEAD_DIM, o0, pltpu.roll(o1, HEAD_DIM, 1)).astype(BF16)


def _attention(q, k, vt, shared_kv, geo, ctx_queries):
    nb, nc, nl = geo["nbatch"], geo["nc"], geo["nl"]
    na = nl + nc
    kw = LANES if shared_kv else 2 * LANES
    if ctx_queries:
        tq, q_per, q_blk0 = nc, 1, nb * nl // nc
        kv_rows = nc
        kv_blk = lambda b: b * (na // nc) + nl // nc
    else:
        tq = min(ATT_ROWS, nl) // (2 if shared_kv else 1)
        q_per, q_blk0 = nl // tq, 0
        kv_rows = na
        kv_blk = lambda b: b
    n_streams, rows = (1, 2 * tq) if shared_kv else (2, tq)
    n_steps = kv_rows // ATT_TK
    return pl.pallas_call(
        functools.partial(_attn_kernel, shared_kv=shared_kv, n_steps=n_steps, tk=ATT_TK),
        grid=(nb, 2, q_per),
        in_specs=[
            pl.BlockSpec((tq, 2 * LANES), lambda b, g, t: (q_blk0 + b * q_per + t, g)),
            pl.BlockSpec((kv_rows, kw), lambda b, g, t: (kv_blk(b), g)),
            pl.BlockSpec((n_steps, kw, ATT_TK), lambda b, g, t: (kv_blk(b), g, 0)),
        ],
        out_specs=pl.BlockSpec((tq, LANES), lambda b, g, t: (b * q_per + t, g)),
        out_shape=jax.ShapeDtypeStruct((nb * q_per * tq, 2 * LANES), BF16),
        scratch_shapes=[pltpu.VMEM((n_streams, rows, LANES), BF16),
                        pltpu.VMEM((2, ATT_TK, rows), F32),
                        pltpu.VMEM((2, ATT_TK, rows), BF16),
                        pltpu.VMEM((2, SUBLANES, rows), F32),
                        pltpu.VMEM((n_streams, SUBLANES, rows), F32),
                        pltpu.VMEM((n_streams, V_ROWS, rows), F32)],
        compiler_params=_cparams(("arbitrary", "arbitrary", "arbitrary")),
        name=("attn_gqa" if shared_kv else "attn_mla") + ("_ctx" if ctx_queries else ""),
    )(q, k, vt)


def _top2_sum(a, b, c, d):
    hi_ab, lo_ab = jnp.maximum(a, b), jnp.minimum(a, b)
    hi_cd, lo_cd = jnp.maximum(c, d), jnp.minimum(c, d)
    first = jnp.maximum(hi_ab, hi_cd)
    second = jnp.maximum(jnp.minimum(hi_ab, hi_cd), jnp.maximum(lo_ab, lo_cd))
    return first + second


def _outproj_kernel(x_ref, cb_ref, cv_ref, cvp_ref, cvn_ref, zp_ref, zpp_ref, zpn_ref,
                    ygl_ref, ygc_ref, yml_ref, ymc_ref,
                    mods_ref, convw_ref, band_ref, wpool_ref, pscale_ref,
                    wout_ref, n2_ref, wrt_ref, br_ref, tri_ref,
                    xo_ref, h2_ref, info_ref, cnt_ref,
                    *, d_model, geo):
    tile = pl.program_id(0)
    d = d_model
    tm = x_ref.shape[0]
    tg = _tile_geometry(tile, geo)
    is_ctx = tg["is_ctx"]
    keep_prev = jnp.where(tg["first"], 0.0, 1.0)
    keep_next = jnp.where(tg["last"], 0.0, 1.0)
    row = tg["mod_row"]
    gate1 = mods_ref[0, pl.ds(row, 1), 2 * d:3 * d]
    shift2 = mods_ref[0, pl.ds(row, 1), 3 * d:4 * d]
    scale2 = mods_ref[0, pl.ds(row, 1), 4 * d:5 * d]

    v = cv_ref[...].astype(F32)
    prev_row = cvp_ref[...].astype(F32)[BF16_ROWS - 1:BF16_ROWS] * keep_prev
    next_row = cvn_ref[...].astype(F32)[0:1] * keep_next
    rid = lax.broadcasted_iota(I32, (tm, CONV_DIM), 0)
    vm1 = jnp.where(rid == 0, prev_row, pltpu.roll(v, 1, 0))
    vp1 = jnp.where(rid == tm - 1, next_row, pltpu.roll(v, tm - 1, 0))
    cw = convw_ref[0]
    y_conv = cb_ref[...].astype(F32) * (vm1 * cw[0:1] + v * cw[1:2] + vp1 * cw[2:3])

    zp = zp_ref[...]
    ext = jnp.concatenate(
        [zpp_ref[...] * keep_prev, zp, zpn_ref[...] * keep_next,
         jnp.zeros((POOL_EXT - tm - 2 * POOL_HALO, POOL_DIM), F32)], axis=0).astype(BF16)
    ext_a, ext_b = ext[:, 0:LANES], ext[:, LANES:2 * LANES]
    lane = lax.broadcasted_iota(I32, (tm, LANES), 1)
    low = lane < POOL_DIM // 4
    sum_a = jnp.where(low, jnp.dot(band_ref[0], ext_a, preferred_element_type=F32),
                      jnp.dot(band_ref[1], ext_a, preferred_element_type=F32))
    sum_b = jnp.where(low, jnp.dot(band_ref[2], ext_b, preferred_element_type=F32),
                      jnp.dot(band_ref[3], ext_b, preferred_element_type=F32))
    sums = jnp.concatenate([sum_a, sum_b], axis=1)
    lane_p = lax.broadcasted_iota(I32, (tm, POOL_DIM), 1)
    half_w = jnp.left_shift(1, jnp.right_shift(lane_p, int(math.log2(POOL_DIM // 4))))
    pos = tg["pos0"] + rid
    cnt = (jnp.minimum(pos + half_w, tg["seg_len"]) - jnp.maximum(pos - half_w, 0)).astype(F32)
    dlt = sums / cnt - zp
    y_pool = jnp.dot(dlt.astype(BF16), wpool_ref[0], preferred_element_type=F32) * pscale_ref[0]

    y_gqa = jnp.where(is_ctx, ygc_ref[...], ygl_ref[...])
    y_mla = jnp.where(is_ctx, ymc_ref[...], yml_ref[...])
    ycat = jnp.concatenate([y_conv.astype(BF16), y_pool.astype(BF16), y_gqa, y_mla], axis=1)
    y = jnp.dot(ycat, wout_ref[0], preferred_element_type=F32)
    xn = x_ref[...] + gate1 * y
    xo_ref[...] = xn
    ms = jnp.mean(xn * xn, axis=-1, keepdims=True)
    h2 = (xn * lax.rsqrt(ms + NORM_EPS)) * (n2_ref[0] * (1.0 + scale2)) + shift2
    h2_ref[...] = h2

    logits = lax.dot_general(wrt_ref[...], h2, (((1,), (1,)), ((), ())),
                             preferred_element_type=F32, precision=HIGHEST)
    scores = _sigmoid(logits)
    sel = scores + br_ref[...]
    epg = EXPERTS_PER_GROUP
    n_groups = N_EXPERTS // epg
    srow = [sel[e:e + 1] for e in range(N_EXPERTS)]
    crow = [scores[e:e + 1] for e in range(N_EXPERTS)]
    gscore = [_top2_sum(*srow[g * epg:(g + 1) * epg]) for g in range(n_groups)]
    gbest = jnp.zeros_like(gscore[0]).astype(I32)
    best = gscore[0]
    for g in range(1, n_groups):
        upd = gscore[g] > best
        gbest = jnp.where(upd, g, gbest)
        best = jnp.where(upd, gscore[g], best)

    def pick(rows_, j):
        out = rows_[(n_groups - 1) * epg + j]
        for g in range(n_groups - 2, -1, -1):
            out = jnp.where(gbest == g, rows_[g * epg + j], out)
        return out

    sv = [pick(srow, j) for j in range(epg)]
    cv_ = [pick(crow, j) for j in range(epg)]
    i1 = jnp.zeros_like(gbest)
    b1 = sv[0]
    for j in range(1, epg):
        upd = sv[j] > b1
        i1 = jnp.where(upd, j, i1)
        b1 = jnp.where(upd, sv[j], b1)
    i2 = jnp.zeros_like(gbest)
    b2 = jnp.full_like(b1, -jnp.inf)
    for j in range(epg):
        upd = jnp.logical_and(i1 != j, sv[j] > b2)
        i2 = jnp.where(upd, j, i2)
        b2 = jnp.where(upd, sv[j], b2)
    s1 = cv_[epg - 1]
    s2 = cv_[epg - 1]
    for j in range(epg - 2, -1, -1):
        s1 = jnp.where(i1 == j, cv_[j], s1)
        s2 = jnp.where(i2 == j, cv_[j], s2)
    inv = 1.0 / (s1 + s2)
    e1 = gbest * epg + i1
    e2 = gbest * epg + i2

    @pl.when(tile == 0)
    def _():
        cnt_ref[...] = jnp.zeros_like(cnt_ref)

    erow = lax.broadcasted_iota(I32, (N_EXPERTS, tm), 0)
    hit1 = erow == e1
    hit2 = erow == e2
    onehot = jnp.where(hit1, 1.0, 0.0) + jnp.where(hit2, 1.0, 0.0)
    before = jnp.dot(onehot.astype(BF16), tri_ref[...], preferred_element_type=F32)
    tot = cnt_ref[:, 0:1] + before
    rank1 = jnp.sum(jnp.where(hit1, tot, 0.0), axis=0, keepdims=True).astype(I32)
    rank2 = jnp.sum(jnp.where(hit2, tot, 0.0), axis=0, keepdims=True).astype(I32)
    cnt_ref[...] = cnt_ref[...] + jnp.sum(onehot, axis=1, keepdims=True)

    w1 = lax.bitcast_convert_type(s1 * inv, I32)
    w2 = lax.bitcast_convert_type(s2 * inv, I32)
    irow = lax.broadcasted_iota(I32, (SUBLANES, tm), 0)
    info = jnp.where(irow == 0, e1, jnp.where(irow == 1, e2, jnp.where(
        irow == 2, rank1, jnp.where(irow == 3, rank2, jnp.where(
            irow == 4, w1, jnp.where(irow == 5, w2, 0))))))
    info_ref[0] = info


def _outproj(xa, mix, yg, ygc, ym, ymc, mods, layer, p, geo, n_tiles):
    d = xa.shape[1]
    nt_all, n_lat = geo["n_tiles"], geo["n_lat_tiles"]
    r = n_tiles * TM
    nt = n_tiles
    rowmap = lambda j: (j, 0)
    latmap = lambda j: (jnp.minimum(j, n_lat - 1), 0)
    ctxmap = lambda j: (jnp.clip(j - n_lat, 0, ygc.shape[0] // TM - 1), 0)
    lay3 = lambda j: (layer, 0, 0)
    c0 = lambda j: (0, 0)
    c3 = lambda j: (0, 0, 0)
    bf_blocks = TM // BF16_ROWS
    f_blocks = TM // SUBLANES

    def full3(a):
        return pl.BlockSpec((1,) + a.shape[1:], lay3)

    return pl.pallas_call(
        functools.partial(_outproj_kernel, d_model=d, geo=geo),
        grid=(n_tiles,),
        in_specs=[
            pl.BlockSpec((TM, d), rowmap),
            pl.BlockSpec((TM, CONV_DIM), rowmap),
            pl.BlockSpec((TM, CONV_DIM), rowmap),
            pl.BlockSpec((BF16_ROWS, CONV_DIM), lambda j: (jnp.maximum(j * bf_blocks - 1, 0), 0)),
            pl.BlockSpec((BF16_ROWS, CONV_DIM),
                         lambda j: (jnp.minimum((j + 1) * bf_blocks, nt_all * bf_blocks - 1), 0)),
            pl.BlockSpec((TM, POOL_DIM), rowmap),
            pl.BlockSpec((SUBLANES, POOL_DIM), lambda j: (jnp.maximum(j * f_blocks - 1, 0), 0)),
            pl.BlockSpec((SUBLANES, POOL_DIM),
                         lambda j: (jnp.minimum((j + 1) * f_blocks, nt_all * f_blocks - 1), 0)),
            pl.BlockSpec((TM, 2 * LANES), latmap),
            pl.BlockSpec((TM, 2 * LANES), ctxmap),
            pl.BlockSpec((TM, 2 * LANES), latmap),
            pl.BlockSpec((TM, 2 * LANES), ctxmap),
            full3(mods), full3(p["conv_w"]),
            pl.BlockSpec(p["band"].shape, c3),
            full3(p["wpool"]), full3(p["pscale"]), full3(p["w_out"]), full3(p["norm2"]),
            pl.BlockSpec(p["wrt"].shape, c0), pl.BlockSpec(p["br"].shape, c0),
            pl.BlockSpec(p["tri"].shape, c0),
        ],
        out_specs=[
            pl.BlockSpec((TM, d), rowmap),
            pl.BlockSpec((TM, d), rowmap),
            pl.BlockSpec((1, SUBLANES, TM), lambda j: (j, 0, 0)),
            pl.BlockSpec((N_EXPERTS, LANES), c0),
        ],
        out_shape=[
            jax.ShapeDtypeStruct((r, d), F32),
            jax.ShapeDtypeStruct((r, d), F32),
            jax.ShapeDtypeStruct((nt, SUBLANES, TM), I32),
            jax.ShapeDtypeStruct((N_EXPERTS, LANES), F32),
        ],
        compiler_params=_cparams(("arbitrary",)),
        name="outproj",
    )(xa, mix["cb"], mix["cv"], mix["cv"], mix["cv"], mix["zp"], mix["zp"], mix["zp"],
      yg, ygc, ym, ymc, mods, p["conv_w"], p["band"], p["wpool"], p["pscale"], p["w_out"],
      p["norm2"], p["wrt"], p["br"], p["tri"])


def _load_slots(block_ref, idx_ref, sem):
    n = block_ref.shape[2]
    copies = [pltpu.make_async_copy(block_ref.at[0, k], idx_ref.at[pl.ds(k * n, n)], sem) for k in range(2)]
    for cp in copies:
        cp.start()
    for cp in copies:
        cp.wait()


def _row_copies(idx_ref, n_rows, make):
    def body(t, carry):
        for u in range(SUBLANES):
            r = t * SUBLANES + u
            make(0, t, u, idx_ref[r]).start()
            make(1, t, u, idx_ref[n_rows + r]).start()
        return carry
    lax.fori_loop(0, n_rows // SUBLANES, body, 0)


def _scatter_kernel(ps_ref, zs_ref, has_ref, nu_ref, info_ref, h2_ref, dest_ref, xs_ref,
                    dsm_ref, zbuf_ref, stage_ref, sem_ref, *, first_spare, n_blocks, n_tiles):
    j = pl.program_id(0)
    tm = h2_ref.shape[0]

    @pl.when(j == 0)
    def _():
        zbuf_ref[...] = jnp.zeros_like(zbuf_ref)

        def zero_copy(start):
            return pltpu.make_async_copy(
                zbuf_ref, xs_ref.at[pl.ds(pl.multiple_of(start, BM), BM)], sem_ref.at[2])

        for act in ("start", "wait"):
            for e in range(N_EXPERTS):
                @pl.when(has_ref[e] > 0)
                def _():
                    getattr(zero_copy(zs_ref[e]), act)()
            for jb in range(first_spare, n_blocks):
                @pl.when(jb >= nu_ref[0])
                def _():
                    getattr(zero_copy(jb * BM), act)()

    info = info_ref[0]
    e1, e2 = info[0:1], info[1:2]
    d1, d2 = info[2:3], info[3:4]
    for e in range(N_EXPERTS):
        d1 = d1 + jnp.where(e1 == e, ps_ref[e], 0)
        d2 = d2 + jnp.where(e2 == e, ps_ref[e], 0)
    irow = lax.broadcasted_iota(I32, (SUBLANES, tm), 0)
    dest_ref[0] = jnp.where(irow == 0, d1, jnp.where(irow == 1, d2, 0))
    _load_slots(dest_ref, dsm_ref, sem_ref.at[3])

    par = j % 2
    stage_ref[par] = h2_ref[...].reshape(stage_ref.shape[1:])

    def row_copy(_, t, u, slot):
        return pltpu.make_async_copy(stage_ref.at[par, t, pl.ds(u, 1)], xs_ref.at[pl.ds(slot, 1)],
                                     sem_ref.at[par])

    _row_copies(dsm_ref, tm, row_copy)

    def retire(which):
        for _ in range(2):
            pltpu.make_async_copy(h2_ref, xs_ref.at[pl.ds(0, tm)], sem_ref.at[which]).wait()

    @pl.when(j > 0)
    def _():
        retire(1 - par)

    @pl.when(j == n_tiles - 1)
    def _():
        retire(par)


def _scatter(info, h2, pad_start, zero_start, has_rows, n_used, n_slots):
    r, d = h2.shape
    nt = r // TM
    return pl.pallas_call(
        functools.partial(_scatter_kernel, first_spare=-(-2 * r // BM), n_blocks=n_slots // BM,
                          n_tiles=nt),
        grid_spec=pltpu.PrefetchScalarGridSpec(
            num_scalar_prefetch=4,
            grid=(nt,),
            in_specs=[
                pl.BlockSpec((1, SUBLANES, TM), lambda j, *_: (j, 0, 0)),
                pl.BlockSpec((TM, d), lambda j, *_: (j, 0)),
            ],
            out_specs=[
                pl.BlockSpec((1, SUBLANES, TM), lambda j, *_: (j, 0, 0)),
                pl.BlockSpec(memory_space=pl.ANY),
            ],
            scratch_shapes=[
                pltpu.SMEM((2 * TM,), I32),
                pltpu.VMEM((BM, d), F32),
                pltpu.VMEM((2, TM // SUBLANES, SUBLANES, d), F32),
                pltpu.SemaphoreType.DMA((4,)),
            ],
        ),
        out_shape=[
            jax.ShapeDtypeStruct((nt, SUBLANES, TM), I32),
            jax.ShapeDtypeStruct((n_slots, d), F32),
        ],
        compiler_params=_cparams(("arbitrary",)),
        name="moe_scatter",
    )(pad_start, zero_start, has_rows, n_used, info, h2)


def _expert_kernel(be_ref, nu_ref, xs_ref, wg_ref, wu_ref, wd_ref, y_ref, wgb, wub, wdb):
    j = pl.program_id(0)

    @pl.when(j < nu_ref[0])
    def _():
        e = be_ref[j]
        prev = be_ref[jnp.maximum(j - 1, 0)]

        @pl.when(jnp.logical_or(j == 0, e != prev))
        def _():
            wgb[...] = wg_ref[0].astype(BF16)
            wub[...] = wu_ref[0].astype(BF16)
            wdb[...] = wd_ref[0].astype(BF16)

        x = xs_ref[...].astype(BF16)
        a = jnp.dot(x, wgb[...], preferred_element_type=F32)
        u = jnp.dot(x, wub[...], preferred_element_type=F32)
        hmid = (a * _sigmoid(a) * u).astype(BF16)
        y_ref[...] = jnp.dot(hmid, wdb[...], preferred_element_type=F32)

    @pl.when(j >= nu_ref[0])
    def _():
        y_ref[...] = jnp.zeros_like(y_ref)


def _experts(xs, block_exp, n_used, w_gate, w_up, w_down, layer):
    n_slots, d = xs.shape
    de = w_gate.shape[-1]
    nbm = n_slots // BM
    n_exp = w_gate.shape[1]

    def blk(j, be, nu):
        return (jnp.minimum(j, nu[0] - 1), 0)

    def wmap(j, be, nu):
        return (layer * n_exp + be[jnp.minimum(j, nu[0] - 1)], 0, 0)

    wg = w_gate.reshape((-1,) + w_gate.shape[2:])
    wu = w_up.reshape((-1,) + w_up.shape[2:])
    wd = w_down.reshape((-1,) + w_down.shape[2:])
    return pl.pallas_call(
        _expert_kernel,
        grid_spec=pltpu.PrefetchScalarGridSpec(
            num_scalar_prefetch=2,
            grid=(nbm,),
            in_specs=[
                pl.BlockSpec((BM, d), blk),
                pl.BlockSpec((1, d, de), wmap),
                pl.BlockSpec((1, d, de), wmap),
                pl.BlockSpec((1, de, d), wmap),
            ],
            out_specs=pl.BlockSpec((BM, d), lambda j, be, nu: (j, 0)),
            scratch_shapes=[
                pltpu.VMEM((d, de), BF16),
                pltpu.VMEM((d, de), BF16),
                pltpu.VMEM((de, d), BF16),
            ],
        ),
        out_shape=jax.ShapeDtypeStruct((n_slots, d), F32),
        compiler_params=_cparams(("arbitrary",)),
        name="moe_experts",
    )(block_exp, n_used, xs, wg, wu, wd)


def _combine_kernel(dest_ref, dest_next_ref, info_ref, x_ref, mods_ref, y_ref, o_ref,
                    dsm_ref, ybuf_ref, sem_ref, *, d_model, geo, n_steps):
    j = pl.program_id(0)
    d = d_model
    tm = x_ref.shape[0]
    row = _tile_geometry(j, geo)["mod_row"]
    gate2 = mods_ref[0, pl.ds(row, 1), 5 * d:6 * d]
    cur = j % 2

    def gather(idx_block_ref, buf):
        _load_slots(idx_block_ref, dsm_ref, sem_ref.at[2])

        def row_copy(k, t, u, slot):
            return pltpu.make_async_copy(y_ref.at[pl.ds(slot, 1)],
                                         ybuf_ref.at[buf, k, t, pl.ds(u, 1)], sem_ref.at[buf])

        _row_copies(dsm_ref, tm, row_copy)

    @pl.when(j == 0)
    def _():
        gather(dest_ref, 0)

    @pl.when(j + 1 < n_steps)
    def _():
        gather(dest_next_ref, 1 - cur)

    for k in range(2):
        pltpu.make_async_copy(y_ref.at[pl.ds(0, tm)], o_ref, sem_ref.at[cur]).wait()

    info = info_ref[0]
    wrow = lax.broadcasted_iota(I32, (LANES, tm), 0)
    w_lanes = jnp.where(wrow == 0, lax.bitcast_convert_type(info[4:5], F32),
                        jnp.where(wrow == 1, lax.bitcast_convert_type(info[5:6], F32), 0.0))
    w_rows = w_lanes.T
    y1 = ybuf_ref[cur, 0].reshape(tm, d)
    y2 = ybuf_ref[cur, 1].reshape(tm, d)
    o_ref[...] = x_ref[...] + gate2 * (w_rows[:, 0:1] * y1 + w_rows[:, 1:2] * y2)


def _combine(dest, info, xn, mods, y, layer, geo):
    r, d = xn.shape
    n_steps = r // TM
    return pl.pallas_call(
        functools.partial(_combine_kernel, d_model=d, geo=geo, n_steps=n_steps),
        grid=(n_steps,),
        in_specs=[
            pl.BlockSpec((1, SUBLANES, TM), lambda j: (j, 0, 0)),
            pl.BlockSpec((1, SUBLANES, TM), lambda j: (jnp.minimum(j + 1, n_steps - 1), 0, 0)),
            pl.BlockSpec((1, SUBLANES, TM), lambda j: (j, 0, 0)),
            pl.BlockSpec((TM, d), lambda j: (j, 0)),
            pl.BlockSpec((1,) + mods.shape[1:], lambda j: (layer, 0, 0)),
            pl.BlockSpec(memory_space=pl.ANY),
        ],
        out_specs=pl.BlockSpec((TM, d), lambda j: (j, 0)),
        out_shape=jax.ShapeDtypeStruct((n_steps * TM, d), F32),
        scratch_shapes=[
            pltpu.SMEM((2 * TM,), I32),
            pltpu.VMEM((2, 2, TM // SUBLANES, SUBLANES, d), F32),
            pltpu.SemaphoreType.DMA((3,)),
        ],
        compiler_params=_cparams(("arbitrary",)),
        name="moe_combine",
    )(dest, dest, info, xn, mods, y)


def _pad_heads(w, n_heads, width):
    lead = w.shape[:-1]
    w = w.reshape(lead + (n_heads, width))
    w = jnp.pad(w, [(0, 0)] * len(lead) + [(0, 0), (0, LANES - width)])
    return w.reshape(lead + (n_heads * LANES,))


def _rope_tables(nc, nl):
    t = jnp.arange(nl)
    row_id = (t // GRID_W).astype(F32)
    col_id = (t % GRID_W).astype(F32)

    def angles(rot_dim):
        n_freq = rot_dim // 4
        inv_freq = jnp.power(ROPE_THETA, -jnp.arange(n_freq, dtype=F32) / n_freq)
        return jnp.concatenate([row_id[:, None] * inv_freq, col_id[:, None] * inv_freq], axis=-1)

    def with_ctx(tab, fill):
        return jnp.concatenate([tab, jnp.full((TM, LANES), fill, F32)], axis=0)

    ag = angles(HEAD_DIM)
    one_g = jnp.ones((nl, LANES - HEAD_DIM), F32)
    cos_g = jnp.concatenate([jnp.cos(ag), jnp.cos(ag), one_g], axis=-1)
    sin_g = jnp.concatenate([-jnp.sin(ag), jnp.sin(ag), 0.0 * one_g], axis=-1)
    am = angles(MLA_ROPE_DIM)
    one_n = jnp.ones((nl, MLA_NOPE_DIM), F32)
    one_t = jnp.ones((nl, LANES - MLA_QK_DIM), F32)
    cos_m = jnp.concatenate([one_n, jnp.cos(am), jnp.cos(am), one_t], axis=-1)
    sin_m = jnp.concatenate([0.0 * one_n, -jnp.sin(am), jnp.sin(am), 0.0 * one_t], axis=-1)
    return with_ctx(cos_g, 1.0), with_ctx(sin_g, 0.0), with_ctx(cos_m, 1.0), with_ctx(sin_m, 0.0)


def _pool_band():
    t = np.arange(TM)[:, None]
    src = np.arange(POOL_EXT)[None, :] - POOL_HALO
    live = np.arange(POOL_EXT)[None, :] < TM + 2 * POOL_HALO
    mats = [((src >= t - w // 2) & (src < t + w // 2) & live) for w in POOL_WINDOWS]
    return jnp.asarray(np.stack(mats).astype(np.float32), dtype=BF16)


def _prep_params(w_in, norm1, norm2, conv_w, w_pool, pool_scale, gqa_q_norm, gqa_k_norm,
                 mla_q_norm, mla_kv_norm, mla_w_uq, mla_w_uk, mla_w_uv, mla_qk_q_norm,
                 mla_qk_k_norm, w_out, w_router, b_router, nc, nl):
    dep = w_in.shape[0]
    o = 0
    pieces = {}
    for name, n in (("conv", 3 * CONV_DIM), ("pool", POOL_DIM), ("gq", GQA_HEADS * HEAD_DIM),
                    ("gk", GQA_KV_HEADS * HEAD_DIM), ("gv", GQA_KV_HEADS * HEAD_DIM),
                    ("mq", MLA_Q_RANK), ("mkv", MLA_KV_RANK), ("mkr", MLA_ROPE_DIM)):
        pieces[name] = w_in[..., o:o + n]
        o += n
    mkr = jnp.pad(pieces["mkr"], ((0, 0), (0, 0), (MLA_NOPE_DIM, LANES - MLA_QK_DIM)))
    w_in_p = jnp.concatenate([
        pieces["conv"], pieces["pool"], _pad_heads(pieces["gq"], GQA_HEADS, HEAD_DIM),
        _pad_heads(pieces["gk"], GQA_KV_HEADS, HEAD_DIM),
        pieces["mq"], pieces["mkv"], mkr], axis=-1).astype(BF16)
    wgvt = jnp.swapaxes(_pad_heads(pieces["gv"], GQA_KV_HEADS, HEAD_DIM), 1, 2).astype(BF16)
    wuvt = jnp.swapaxes(_pad_heads(mla_w_uv, MLA_HEADS, MLA_V_DIM), 1, 2).astype(BF16)
    eye = jnp.eye(len(POOL_WINDOWS), dtype=F32)
    wpool = jnp.einsum("gh,dgij->dgihj", eye, w_pool).reshape(dep, POOL_DIM, POOL_DIM).astype(BF16)
    cos_g, sin_g, cos_m, sin_m = _rope_tables(nc, nl)
    tri = np.triu(np.ones((TM, TM), np.float32), 1)

    def row3(a):
        return a.reshape(dep, 1, a.shape[-1])

    return {
        "w_in": w_in_p,
        "norm1": row3(norm1), "norm2": row3(norm2),
        "gq": row3(_pad_heads(gqa_q_norm * (GQA_SCALE * LOG2E), 1, HEAD_DIM)),
        "gk": row3(_pad_heads(gqa_k_norm, 1, HEAD_DIM)),
        "mqn": row3(mla_q_norm), "mkvn": row3(mla_kv_norm),
        "qkq": row3(_pad_heads(mla_qk_q_norm * (MLA_SCALE * LOG2E), 1, MLA_QK_DIM)),
        "qkk": row3(_pad_heads(mla_qk_k_norm, 1, MLA_QK_DIM)),
        "wuq": _pad_heads(mla_w_uq, MLA_HEADS, MLA_QK_DIM).astype(BF16),
        "wuk": _pad_heads(mla_w_uk, MLA_HEADS, MLA_NOPE_DIM).astype(BF16),
        "wgvt": wgvt, "wuvt": wuvt,
        "cos_g": cos_g, "sin_g": sin_g, "cos_m": cos_m, "sin_m": sin_m,
        "conv_w": jnp.pad(conv_w, ((0, 0), (0, SUBLANES - conv_w.shape[1]), (0, 0))),
        "band": _pool_band(),
        "wpool": wpool, "pscale": row3(pool_scale),
        "w_out": w_out.astype(BF16),
        "wrt": w_router.T, "br": b_router.reshape(-1, 1),
        "tri": jnp.asarray(tri, dtype=BF16),
    }


def _moe_plan(counts):
    counts = counts.astype(I32)
    padded = ((counts + BM - 1) // BM) * BM
    pad_end = jnp.cumsum(padded)
    pad_start = pad_end - padded
    return pad_start, pad_end, padded


def kernel(x, c, ctx, c_ctx, w_mod, b_mod, norm1, norm2, w_in, conv_w, w_pool, pool_scale,
           gqa_q_norm, gqa_k_norm, mla_q_norm, mla_kv_norm, mla_w_uq, mla_w_uk, mla_w_uv,
           mla_qk_q_norm, mla_qk_k_norm, w_out, w_router, b_router, w_gate, w_up, w_down):
    nb, nl, d = x.shape
    nc = ctx.shape[1]
    depth = w_mod.shape[0]
    assert nc % TM == 0 and nl % TM == 0 and nl % GRID_W == 0 and nb < SUBLANES
    lt, nct = nl // TM, nc // TM
    geo = {"nbatch": nb, "nc": nc, "nl": nl, "lt": lt, "nct": nct,
           "n_lat_tiles": nb * lt, "n_tiles": nb * (lt + nct)}

    p = _prep_params(w_in, norm1, norm2, conv_w, w_pool, pool_scale, gqa_q_norm, gqa_k_norm,
                     mla_q_norm, mla_kv_norm, mla_w_uq, mla_w_uk, mla_w_uv, mla_qk_q_norm,
                     mla_qk_k_norm, w_out, w_router, b_router, nc, nl)
    cvec = jnp.concatenate([c, c_ctx[None, :], jnp.zeros((SUBLANES - nb - 1, d), F32)], axis=0)
    mods = _adaln(cvec, w_mod, b_mod)

    xa = jnp.concatenate([x.reshape(nb * nl, d), ctx.reshape(nb * nc, d)], axis=0)
    for i in range(depth):
        last = i == depth - 1
        mix = _inproj(xa, mods, i, p, geo)
        yg = _attention(mix["qg"], mix["kg"], mix["vgt"], True, geo, False)
        ym = _attention(mix["qm"], mix["km"], mix["vmt"], False, geo, False)
        if last:
            n_tiles, ygc, ymc = geo["n_lat_tiles"], yg, ym
        else:
            n_tiles = geo["n_tiles"]
            ygc = _attention(mix["qg"], mix["kg"], mix["vgt"], True, geo, True)
            ymc = _attention(mix["qm"], mix["km"], mix["vmt"], False, geo, True)
        xn, h2, info, cnt = _outproj(xa, mix, yg, ygc, ym, ymc, mods, i, p, geo, n_tiles)
        n_blocks = -(-2 * n_tiles * TM // BM) + N_EXPERTS
        n_slots = n_blocks * BM
        pad_start, pad_end, padded = _moe_plan(cnt[:, 0])
        n_used = (pad_end[-1:] // BM).astype(I32)
        block_row0 = jnp.arange(n_blocks, dtype=I32) * BM
        block_exp = jnp.minimum(jnp.sum((pad_end[None, :] <= block_row0[:, None]).astype(I32), axis=1),
                                N_EXPERTS - 1)
        dest, xs = _scatter(info, h2, pad_start, jnp.maximum(pad_end - BM, 0),
                            (padded > 0).astype(I32), n_used, n_slots)
        y = _experts(xs, block_exp, n_used, w_gate, w_up, w_down, i)
        xa = _combine(dest, info, xn, mods, y, i, geo)
    return xa.reshape(nb, nl, d)
```

```python
import functools
import math

import numpy as np
import jax
import jax.numpy as jnp
from jax import lax
from jax.experimental import pallas as pl
from jax.experimental.pallas import tpu as pltpu

F32 = jnp.float32
BF16 = jnp.bfloat16
I32 = jnp.int32

GRID_W = 64
CONV_DIM = 256
POOL_DIM = 256
POOL_WINDOWS = (2, 4, 8, 16)
HEAD_DIM = 64
GQA_HEADS = 4
GQA_KV_HEADS = 2
MLA_HEADS = 4
MLA_NOPE_DIM = 64
MLA_ROPE_DIM = 32
MLA_QK_DIM = MLA_NOPE_DIM + MLA_ROPE_DIM
MLA_V_DIM = 64
MLA_Q_RANK = 256
MLA_KV_RANK = 128
N_EXPERTS = 16
EXPERTS_PER_GROUP = 4
ROPE_THETA = 10000.0
NORM_EPS = 1e-6
LOG2E = 1.4426950408889634
GQA_SCALE = HEAD_DIM ** -0.5
MLA_SCALE = MLA_QK_DIM ** -0.5

LANES = 128
SUBLANES = 8
BF16_ROWS = 16
VMEM_LIMIT = 56 * 1024 * 1024

TM = 256
ATT_ROWS = 2048
INPROJ_ROWS = 256
BM = 512
V_ROWS = HEAD_DIM + BF16_ROWS
ATT_PAIR_UNROLL = 4
ROW_COPY_UNROLL = 8
POOL_EXT = 512
POOL_HALO = 8

ZC_B, ZC_C, ZC_U, ZC_P = 0, 256, 512, 768
ZC_GQ = 1024
ZC_GK = ZC_GQ + GQA_HEADS * LANES
ZC_MQ = ZC_GK + GQA_KV_HEADS * LANES
ZC_MKV = ZC_MQ + MLA_Q_RANK
ZC_MKR = ZC_MKV + MLA_KV_RANK
ZC_END = ZC_MKR + LANES

HIGHEST = lax.Precision.HIGHEST


def _cparams(sem, vmem=VMEM_LIMIT):
    return pltpu.CompilerParams(dimension_semantics=sem, vmem_limit_bytes=vmem)


def _sigmoid(v):
    return 1.0 / (1.0 + jnp.exp(-v))


def _adaln_kernel(c_ref, w_ref, b_ref, o_ref):
    c = c_ref[...]
    s = c * _sigmoid(c)
    o_ref[0] = jnp.dot(s, w_ref[0], preferred_element_type=F32, precision=HIGHEST) + b_ref[0]


def _adaln(cvec, w_mod, b_mod):
    depth, d, n6 = w_mod.shape
    tn = 1536 if n6 % 1536 == 0 else n6
    rows = cvec.shape[0]
    return pl.pallas_call(
        _adaln_kernel,
        grid=(depth, n6 // tn),
        in_specs=[
            pl.BlockSpec((rows, d), lambda i, n: (0, 0)),
            pl.BlockSpec((1, d, tn), lambda i, n: (i, 0, n)),
            pl.BlockSpec((1, 1, tn), lambda i, n: (i, 0, n)),
        ],
        out_specs=pl.BlockSpec((1, rows, tn), lambda i, n: (i, 0, n)),
        out_shape=jax.ShapeDtypeStruct((depth, rows, n6), F32),
        compiler_params=_cparams(("arbitrary", "arbitrary")),
        name="adaln",
    )(cvec, w_mod, b_mod.reshape(depth, 1, n6))


def _tile_geometry(j, geo):
    lt, nct, n_lat = geo["lt"], geo["nct"], geo["n_lat_tiles"]
    is_ctx = j >= n_lat
    jc = j - n_lat
    sample = jnp.where(is_ctx, jc // nct, j // lt)
    jt = jnp.where(is_ctx, jc % nct, j % lt)
    return {
        "is_ctx": is_ctx,
        "mod_row": jnp.where(is_ctx, geo["nbatch"], sample),
        "first": jt == 0,
        "last": jt == jnp.where(is_ctx, nct, lt) - 1,
        "pos0": jt * TM,
        "seg_len": jnp.where(is_ctx, geo["nc"], geo["nl"]),
    }


def _norm_rope(slab, gain, cos, sin, n_valid, first_half, half):
    ms = jnp.sum(slab * slab, axis=-1, keepdims=True) * (1.0 / n_valid)
    y = slab * lax.rsqrt(ms + NORM_EPS) * gain
    partner = jnp.where(first_half, pltpu.roll(y, LANES - half, 1), pltpu.roll(y, half, 1))
    return y * cos + partner * sin


def _inproj_kernel(x_ref, mods_ref, n1_ref, cg_ref, sg_ref, cm_ref, sm_ref, win_ref,
                   gq_ref, gk_ref, mqn_ref, mkvn_ref, qkq_ref, qkk_ref, wuq_ref, wuk_ref,
                   wgvt_ref, wuvt_ref,
                   cb_ref, cv_ref, zp_ref, qg_ref, kg_ref, vgt_ref, qm_ref, km_ref, vmt_ref,
                   *, d_model, nbatch, n_lat_steps, lat_steps_per_sample):
    d = d_model
    step = pl.program_id(0)
    row = jnp.where(step >= n_lat_steps, nbatch, step // lat_steps_per_sample)
    shift = mods_ref[0, pl.ds(row, 1), 0:d]
    scale = mods_ref[0, pl.ds(row, 1), d:2 * d]
    x = x_ref[...]
    ms = jnp.mean(x * x, axis=-1, keepdims=True)
    h = ((x * lax.rsqrt(ms + NORM_EPS)) * (n1_ref[0] * (1.0 + scale)) + shift).astype(BF16)
    nt_dims = (((1,), (1,)), ((), ()))

    def proj(c0, width):
        return jnp.dot(h, win_ref[0, :, c0:c0 + width], preferred_element_type=F32)

    def store_transposed(vt_ref, vt):
        srow = lax.broadcasted_iota(I32, vt.shape, 0)
        vt = jnp.where(jnp.bitwise_and(srow, LANES - 1) == HEAD_DIM, 1.0, vt).astype(BF16)
        for c in range(vt_ref.shape[0]):
            vt_ref[c] = vt[:, c * TM:(c + 1) * TM]

    tm = x.shape[0]
    lane = lax.broadcasted_iota(I32, (tm, LANES), 1)

    cm, sm = cm_ref[...], sm_ref[...]
    m_first = lane < MLA_NOPE_DIM + MLA_ROPE_DIM // 2
    zq = proj(ZC_MQ, MLA_Q_RANK)
    cq = zq * lax.rsqrt(jnp.mean(zq * zq, axis=-1, keepdims=True) + NORM_EPS) * mqn_ref[0]
    qpre = jnp.dot(cq.astype(BF16), wuq_ref[0], preferred_element_type=F32)
    zk = proj(ZC_MKV, MLA_KV_RANK + LANES)
    zkv, zkr = zk[:, 0:MLA_KV_RANK], zk[:, MLA_KV_RANK:]
    ckv = zkv * lax.rsqrt(jnp.mean(zkv * zkv, axis=-1, keepdims=True) + NORM_EPS) * mkvn_ref[0]
    ckv = ckv.astype(BF16)
    kvp = jnp.dot(ckv, wuk_ref[0], preferred_element_type=F32)
    store_transposed(vmt_ref, lax.dot_general(wuvt_ref[0], ckv, nt_dims, preferred_element_type=F32))
    for hd in range(MLA_HEADS):
        sl = slice(hd * LANES, (hd + 1) * LANES)
        qm_ref[:, sl] = _norm_rope(qpre[:, sl], qkq_ref[0], cm, sm, MLA_QK_DIM,
                                   m_first, MLA_ROPE_DIM // 2).astype(BF16)
        km_ref[:, sl] = _norm_rope(kvp[:, sl] + zkr, qkk_ref[0], cm, sm, MLA_QK_DIM,
                                   m_first, MLA_ROPE_DIM // 2).astype(BF16)

    cg, sg = cg_ref[...], sg_ref[...]
    g_first = lane < HEAD_DIM // 2
    zg = proj(ZC_GQ, (GQA_HEADS + GQA_KV_HEADS) * LANES)
    for hd in range(GQA_HEADS):
        qg_ref[:, hd * LANES:(hd + 1) * LANES] = _norm_rope(
            zg[:, hd * LANES:(hd + 1) * LANES], gq_ref[0], cg, sg, HEAD_DIM, g_first,
            HEAD_DIM // 2).astype(BF16)
    for hd in range(GQA_KV_HEADS):
        slab = zg[:, (GQA_HEADS + hd) * LANES:(GQA_HEADS + hd + 1) * LANES]
        kg_ref[:, hd * LANES:(hd + 1) * LANES] = _norm_rope(
            slab, gk_ref[0], cg, sg, HEAD_DIM, g_first, HEAD_DIM // 2).astype(BF16)
    store_transposed(vgt_ref, lax.dot_general(wgvt_ref[0], h, nt_dims, preferred_element_type=F32))

    zc = proj(ZC_B, 3 * CONV_DIM + POOL_DIM)
    cb_ref[...] = zc[:, ZC_B:ZC_B + CONV_DIM].astype(BF16)
    cv_ref[...] = (zc[:, ZC_C:ZC_C + CONV_DIM] * zc[:, ZC_U:ZC_U + CONV_DIM]).astype(BF16)
    zp_ref[...] = zc[:, ZC_P:ZC_P + POOL_DIM]


def _inproj(xa, mods, layer, p, geo):
    r, d = xa.shape
    tmi = geo["inproj_rows"]
    n_lat_steps = geo["nbatch"] * geo["nl"] // tmi
    lat_steps_per_sample = geo["nl"] // tmi
    rowmap = lambda j: (j, 0)
    posmap = lambda j: (jnp.where(j >= n_lat_steps, lat_steps_per_sample, j % lat_steps_per_sample), 0)
    lay3 = lambda j: (layer, 0, 0)

    def full3(a):
        return pl.BlockSpec((1,) + a.shape[1:], lay3)

    outs = [("cb", CONV_DIM, BF16, False), ("cv", CONV_DIM, BF16, False), ("zp", POOL_DIM, F32, False),
            ("qg", GQA_HEADS * LANES, BF16, False), ("kg", GQA_KV_HEADS * LANES, BF16, False),
            ("vgt", GQA_KV_HEADS * LANES, BF16, True), ("qm", MLA_HEADS * LANES, BF16, False),
            ("km", MLA_HEADS * LANES, BF16, False), ("vmt", MLA_HEADS * LANES, BF16, True)]

    def out_spec(w, transposed):
        if transposed:
            return pl.BlockSpec((tmi // TM, w, TM), lambda j: (j, 0, 0))
        return pl.BlockSpec((tmi, w), rowmap)

    def out_shape(w, dt, transposed):
        return jax.ShapeDtypeStruct((r // TM, w, TM) if transposed else (r, w), dt)

    res = pl.pallas_call(
        functools.partial(_inproj_kernel, d_model=d, nbatch=geo["nbatch"], n_lat_steps=n_lat_steps,
                          lat_steps_per_sample=lat_steps_per_sample),
        grid=(r // tmi,),
        in_specs=[
            pl.BlockSpec((tmi, d), rowmap),
            full3(mods), full3(p["norm1"]),
            pl.BlockSpec((tmi, LANES), posmap), pl.BlockSpec((tmi, LANES), posmap),
            pl.BlockSpec((tmi, LANES), posmap), pl.BlockSpec((tmi, LANES), posmap),
            full3(p["w_in"]), full3(p["gq"]), full3(p["gk"]), full3(p["mqn"]), full3(p["mkvn"]),
            full3(p["qkq"]), full3(p["qkk"]), full3(p["wuq"]), full3(p["wuk"]),
            full3(p["wgvt"]), full3(p["wuvt"]),
        ],
        out_specs=[out_spec(w, t) for _, w, _, t in outs],
        out_shape=[out_shape(w, dt, t) for _, w, dt, t in outs],
        compiler_params=_cparams(("arbitrary",)),
        name="inproj",
    )(xa, mods, p["norm1"], p["cos_g"], p["sin_g"], p["cos_m"], p["sin_m"], p["w_in"],
      p["gq"], p["gk"], p["mqn"], p["mkvn"], p["qkq"], p["qkk"], p["wuq"], p["wuk"],
      p["wgvt"], p["wuvt"])
    return {name: a for (name, _, _, _), a in zip(outs, res)}


def _attn_kernel(q_ref, kl_ref, kc_ref, vtl_ref, vtc_ref, o_ref, q_st, s_buf, p_buf, a_buf, m_ref, acc_ref,
                 *, shared_kv, n_lat, n_ctx, tk):
    tq = q_ref.shape[0]
    n_steps = n_lat + n_ctx
    if shared_kv:
        q_st[0] = jnp.concatenate([q_ref[:, 0:LANES], q_ref[:, LANES:2 * LANES]], axis=0)
        cols = [0]
    else:
        q_st[0] = q_ref[:, 0:LANES]
        q_st[1] = q_ref[:, LANES:2 * LANES]
        cols = [0, LANES]
    for si, col in enumerate(cols):
        q_s, m_s, acc_s = q_st.at[si], m_ref.at[si], acc_ref.at[si]
        m_s[...] = jnp.full(m_s.shape, -1e30, F32)
        acc_s[...] = jnp.zeros(acc_s.shape, F32)
        p_buf[1] = jnp.zeros(p_buf.shape[1:], BF16)
        a_buf[1] = jnp.ones(a_buf.shape[1:], F32)

        def pick(i, lat_fn, ctx_fn):
            ctx = ctx_fn(jnp.clip(i - n_lat, 0, n_ctx - 1))
            if n_lat == 0:
                return ctx
            return jnp.where(i >= n_lat, ctx, lat_fn(jnp.clip(i, 0, n_lat - 1)))

        def scores(i, slot, q_s=q_s, col=col):
            k = pick(i, lambda t: kl_ref[pl.ds(pl.multiple_of(t * tk, tk), tk), col:col + LANES],
                     lambda t: kc_ref[pl.ds(pl.multiple_of(t * tk, tk), tk), col:col + LANES])
            s_buf[slot] = lax.dot_general(k, q_s[...], (((1,), (1,)), ((), ())),
                                          preferred_element_type=F32)

        def softmax(i, slot, m_s=m_s):
            s = s_buf[slot]
            m_old = m_s[...]
            m_new = jnp.maximum(m_old, jnp.max(s, axis=0, keepdims=True))
            a_buf[slot] = jnp.exp2(m_old - m_new)
            m_eff = m_new[0:1] + jnp.where(i < n_steps, 0.0, 1e9)
            p_buf[slot] = jnp.exp2(s - m_eff).astype(BF16)
            m_s[...] = m_new

        def accumulate(i, slot, acc_s=acc_s, col=col):
            vt = pick(i, lambda t: vtl_ref[t, col:col + V_ROWS, :],
                      lambda t: vtc_ref[t, col:col + V_ROWS, :])
            acc_s[...] = a_buf[slot][0:1] * acc_s[...] + jnp.dot(vt, p_buf[slot],
                                                                 preferred_element_type=F32)

        scores(0, 0)

        def pair(t, carry):
            i = 2 * t
            scores(i + 1, 1)
            softmax(i, 0)
            accumulate(i - 1, 1)
            scores(i + 2, 0)
            softmax(i + 1, 1)
            accumulate(i, 0)
            return carry

        lax.fori_loop(0, (n_steps + 2) // 2, pair, 0, unroll=ATT_PAIR_UNROLL)
    def finish(acc_t):
        o_t = acc_t * (1.0 / acc_t[HEAD_DIM:HEAD_DIM + 1])
        pad = jnp.zeros((LANES - V_ROWS, o_t.shape[1]), F32)
        return jnp.concatenate([o_t, pad], axis=0).T

    if shared_kv:
        o_both = finish(acc_ref[0])
        o0, o1 = o_both[0:tq], o_both[tq:2 * tq]
    else:
        o0, o1 = finish(acc_ref[0]), finish(acc_ref[1])
    lane = lax.broadcasted_iota(I32, (tq, LANES), 1)
    o_ref[...] = jnp.where(lane < HEAD_DIM, o0, pltpu.roll(o1, HEAD_DIM, 1)).astype(BF16)


def _attention(q, k, vt, shared_kv, geo, ctx_queries):
    nb, nc, nl = geo["nbatch"], geo["nc"], geo["nl"]
    kw = LANES if shared_kv else 2 * LANES
    tk = TM
    ctx_blk0 = nb * nl // nc
    ctx_k = pl.BlockSpec((nc, kw), lambda b, g, t: (ctx_blk0 + b, g))
    ctx_vt = pl.BlockSpec((nc // tk, kw, tk), lambda b, g, t: (ctx_blk0 + b, g, 0))
    if ctx_queries:
        tq, q_per, q_blk0, n_lat = nc, 1, ctx_blk0, 0
        lat_k, lat_vt = ctx_k, ctx_vt
    else:
        tq = min(ATT_ROWS, nl) // (2 if shared_kv else 1)
        q_per, q_blk0, n_lat = nl // tq, 0, nl // tk
        lat_k = pl.BlockSpec((nl, kw), lambda b, g, t: (b, g))
        lat_vt = pl.BlockSpec((nl // tk, kw, tk), lambda b, g, t: (b, g, 0))
    n_streams, rows = (1, 2 * tq) if shared_kv else (2, tq)
    return pl.pallas_call(
        functools.partial(_attn_kernel, shared_kv=shared_kv, n_lat=n_lat, n_ctx=nc // tk, tk=tk),
        grid=(nb, 2, q_per),
        in_specs=[
            pl.BlockSpec((tq, 2 * LANES), lambda b, g, t: (q_blk0 + b * q_per + t, g)),
            lat_k, ctx_k, lat_vt, ctx_vt,
        ],
        out_specs=pl.BlockSpec((tq, LANES), lambda b, g, t: (b * q_per + t, g)),
        out_shape=jax.ShapeDtypeStruct((nb * q_per * tq, 2 * LANES), BF16),
        scratch_shapes=[pltpu.VMEM((n_streams, rows, LANES), BF16),
                        pltpu.VMEM((2, tk, rows), F32),
                        pltpu.VMEM((2, tk, rows), BF16),
                        pltpu.VMEM((2, SUBLANES, rows), F32),
                        pltpu.VMEM((n_streams, SUBLANES, rows), F32),
                        pltpu.VMEM((n_streams, V_ROWS, rows), F32)],
        compiler_params=_cparams(("arbitrary", "arbitrary", "arbitrary")),
        name=("attn_gqa" if shared_kv else "attn_mla") + ("_ctx" if ctx_queries else ""),
    )(q, k, k, vt, vt)


def _top2_sum(a, b, c, d):
    hi_ab, lo_ab = jnp.maximum(a, b), jnp.minimum(a, b)
    hi_cd, lo_cd = jnp.maximum(c, d), jnp.minimum(c, d)
    first = jnp.maximum(hi_ab, hi_cd)
    second = jnp.maximum(jnp.minimum(hi_ab, hi_cd), jnp.maximum(lo_ab, lo_cd))
    return first + second


def _outproj_kernel(x_ref, cb_ref, cv_ref, cvp_ref, cvn_ref, zp_ref, zpp_ref, zpn_ref,
                    ygl_ref, ygc_ref, yml_ref, ymc_ref,
                    mods_ref, convw_ref, band_ref, wpool_ref, pscale_ref,
                    wout_ref, n2_ref, wr_ref, br_ref, tri_ref,
                    xo_ref, h2_ref, info_ref, cnt_ref,
                    *, d_model, geo):
    tile = pl.program_id(0)
    d = d_model
    tm = x_ref.shape[0]
    tg = _tile_geometry(tile, geo)
    is_ctx = tg["is_ctx"]
    keep_prev = jnp.where(tg["first"], 0.0, 1.0)
    keep_next = jnp.where(tg["last"], 0.0, 1.0)
    row = tg["mod_row"]
    gate1 = mods_ref[0, pl.ds(row, 1), 2 * d:3 * d]
    shift2 = mods_ref[0, pl.ds(row, 1), 3 * d:4 * d]
    scale2 = mods_ref[0, pl.ds(row, 1), 4 * d:5 * d]

    v = cv_ref[...].astype(F32)
    prev_row = cvp_ref[...].astype(F32)[BF16_ROWS - 1:BF16_ROWS] * keep_prev
    next_row = cvn_ref[...].astype(F32)[0:1] * keep_next
    rid = lax.broadcasted_iota(I32, (tm, CONV_DIM), 0)
    vm1 = jnp.where(rid == 0, prev_row, pltpu.roll(v, 1, 0))
    vp1 = jnp.where(rid == tm - 1, next_row, pltpu.roll(v, tm - 1, 0))
    cw = convw_ref[0]
    y_conv = cb_ref[...].astype(F32) * (vm1 * cw[0:1] + v * cw[1:2] + vp1 * cw[2:3])

    zp = zp_ref[...]
    ext = jnp.concatenate(
        [zpp_ref[...] * keep_prev, zp, zpn_ref[...] * keep_next,
         jnp.zeros((POOL_EXT - tm - 2 * POOL_HALO, POOL_DIM), F32)], axis=0).astype(BF16)
    ext_a, ext_b = ext[:, 0:LANES], ext[:, LANES:2 * LANES]
    lane = lax.broadcasted_iota(I32, (tm, LANES), 1)
    low = lane < POOL_DIM // 4
    sum_a = jnp.where(low, jnp.dot(band_ref[0], ext_a, preferred_element_type=F32),
                      jnp.dot(band_ref[1], ext_a, preferred_element_type=F32))
    sum_b = jnp.where(low, jnp.dot(band_ref[2], ext_b, preferred_element_type=F32),
                      jnp.dot(band_ref[3], ext_b, preferred_element_type=F32))
    sums = jnp.concatenate([sum_a, sum_b], axis=1)
    lane_p = lax.broadcasted_iota(I32, (tm, POOL_DIM), 1)
    half_w = jnp.left_shift(1, jnp.right_shift(lane_p, int(math.log2(POOL_DIM // 4))))
    pos = tg["pos0"] + rid
    cnt = (jnp.minimum(pos + half_w, tg["seg_len"]) - jnp.maximum(pos - half_w, 0)).astype(F32)
    dlt = sums / cnt - zp
    y_pool = jnp.dot(dlt.astype(BF16), wpool_ref[0], preferred_element_type=F32) * pscale_ref[0]

    y_gqa = jnp.where(is_ctx, ygc_ref[...], ygl_ref[...])
    y_mla = jnp.where(is_ctx, ymc_ref[...], yml_ref[...])
    ycat = jnp.concatenate([y_conv.astype(BF16), y_pool.astype(BF16), y_gqa, y_mla], axis=1)
    y = jnp.dot(ycat, wout_ref[0], preferred_element_type=F32)
    xn = x_ref[...] + gate1 * y
    xo_ref[...] = xn
    ms = jnp.mean(xn * xn, axis=-1, keepdims=True)
    h2 = (xn * lax.rsqrt(ms + NORM_EPS)) * (n2_ref[0] * (1.0 + scale2)) + shift2
    h2_ref[...] = h2

    h_hi = h2.astype(BF16)
    h_lo = (h2 - h_hi.astype(F32)).astype(BF16)
    wr = wr_ref[...]
    part = jnp.dot(h_hi, wr, preferred_element_type=F32)
    small = part[:, LANES:] + jnp.dot(h_lo, wr[:, 0:LANES], preferred_element_type=F32)
    logits = (part[:, 0:LANES] + small).T[0:N_EXPERTS]
    scores = _sigmoid(logits)
    sel = scores + br_ref[...]
    epg = EXPERTS_PER_GROUP
    n_groups = N_EXPERTS // epg
    srow = [sel[e:e + 1] for e in range(N_EXPERTS)]
    crow = [scores[e:e + 1] for e in range(N_EXPERTS)]
    gscore = [_top2_sum(*srow[g * epg:(g + 1) * epg]) for g in range(n_groups)]
    gbest = jnp.zeros_like(gscore[0]).astype(I32)
    best = gscore[0]
    for g in range(1, n_groups):
        upd = gscore[g] > best
        gbest = jnp.where(upd, g, gbest)
        best = jnp.where(upd, gscore[g], best)

    def pick(rows_, j):
        out = rows_[(n_groups - 1) * epg + j]
        for g in range(n_groups - 2, -1, -1):
            out = jnp.where(gbest == g, rows_[g * epg + j], out)
        return out

    sv = [pick(srow, j) for j in range(epg)]
    cv_ = [pick(crow, j) for j in range(epg)]
    i1 = jnp.zeros_like(gbest)
    b1 = sv[0]
    for j in range(1, epg):
        upd = sv[j] > b1
        i1 = jnp.where(upd, j, i1)
        b1 = jnp.where(upd, sv[j], b1)
    i2 = jnp.zeros_like(gbest)
    b2 = jnp.full_like(b1, -jnp.inf)
    for j in range(epg):
        upd = jnp.logical_and(i1 != j, sv[j] > b2)
        i2 = jnp.where(upd, j, i2)
        b2 = jnp.where(upd, sv[j], b2)
    s1 = cv_[epg - 1]
    s2 = cv_[epg - 1]
    for j in range(epg - 2, -1, -1):
        s1 = jnp.where(i1 == j, cv_[j], s1)
        s2 = jnp.where(i2 == j, cv_[j], s2)
    inv = 1.0 / (s1 + s2)
    e1 = gbest * epg + i1
    e2 = gbest * epg + i2

    @pl.when(tile == 0)
    def _():
        cnt_ref[...] = jnp.zeros_like(cnt_ref)

    erow = lax.broadcasted_iota(I32, (N_EXPERTS, tm), 0)
    hit1 = erow == e1
    hit2 = erow == e2
    onehot = jnp.where(hit1, 1.0, 0.0) + jnp.where(hit2, 1.0, 0.0)
    before = jnp.dot(onehot.astype(BF16), tri_ref[...], preferred_element_type=F32)
    tot = cnt_ref[:, 0:1] + before
    rank1 = jnp.sum(jnp.where(hit1, tot, 0.0), axis=0, keepdims=True).astype(I32)
    rank2 = jnp.sum(jnp.where(hit2, tot, 0.0), axis=0, keepdims=True).astype(I32)
    cnt_ref[...] = cnt_ref[...] + jnp.sum(onehot, axis=1, keepdims=True)

    w1 = lax.bitcast_convert_type(s1 * inv, I32)
    w2 = lax.bitcast_convert_type(s2 * inv, I32)
    irow = lax.broadcasted_iota(I32, (SUBLANES, tm), 0)
    info = jnp.where(irow == 0, e1, jnp.where(irow == 1, e2, jnp.where(
        irow == 2, rank1, jnp.where(irow == 3, rank2, jnp.where(
            irow == 4, w1, jnp.where(irow == 5, w2, 0))))))
    info_ref[0] = info


def _outproj(xa, mix, yg, ygc, ym, ymc, mods, layer, p, geo, n_tiles):
    d = xa.shape[1]
    nt_all, n_lat = geo["n_tiles"], geo["n_lat_tiles"]
    r = n_tiles * TM
    nt = n_tiles
    rowmap = lambda j: (j, 0)
    latmap = lambda j: (jnp.minimum(j, n_lat - 1), 0)
    ctxmap = lambda j: (jnp.clip(j - n_lat, 0, ygc.shape[0] // TM - 1), 0)
    lay3 = lambda j: (layer, 0, 0)
    c0 = lambda j: (0, 0)
    c3 = lambda j: (0, 0, 0)
    bf_blocks = TM // BF16_ROWS
    f_blocks = TM // SUBLANES

    def full3(a):
        return pl.BlockSpec((1,) + a.shape[1:], lay3)

    return pl.pallas_call(
        functools.partial(_outproj_kernel, d_model=d, geo=geo),
        grid=(n_tiles,),
        in_specs=[
            pl.BlockSpec((TM, d), rowmap),
            pl.BlockSpec((TM, CONV_DIM), rowmap),
            pl.BlockSpec((TM, CONV_DIM), rowmap),
            pl.BlockSpec((BF16_ROWS, CONV_DIM), lambda j: (jnp.maximum(j * bf_blocks - 1, 0), 0)),
            pl.BlockSpec((BF16_ROWS, CONV_DIM),
                         lambda j: (jnp.minimum((j + 1) * bf_blocks, nt_all * bf_blocks - 1), 0)),
            pl.BlockSpec((TM, POOL_DIM), rowmap),
            pl.BlockSpec((SUBLANES, POOL_DIM), lambda j: (jnp.maximum(j * f_blocks - 1, 0), 0)),
            pl.BlockSpec((SUBLANES, POOL_DIM),
                         lambda j: (jnp.minimum((j + 1) * f_blocks, nt_all * f_blocks - 1), 0)),
            pl.BlockSpec((TM, 2 * LANES), latmap),
            pl.BlockSpec((TM, 2 * LANES), ctxmap),
            pl.BlockSpec((TM, 2 * LANES), latmap),
            pl.BlockSpec((TM, 2 * LANES), ctxmap),
            full3(mods), full3(p["conv_w"]),
            pl.BlockSpec(p["band"].shape, c3),
            full3(p["wpool"]), full3(p["pscale"]), full3(p["w_out"]), full3(p["norm2"]),
            pl.BlockSpec(p["wr"].shape, c0), pl.BlockSpec(p["br"].shape, c0),
            pl.BlockSpec(p["tri"].shape, c0),
        ],
        out_specs=[
            pl.BlockSpec((TM, d), rowmap),
            pl.BlockSpec((TM, d), rowmap),
            pl.BlockSpec((1, SUBLANES, TM), lambda j: (j, 0, 0)),
            pl.BlockSpec((N_EXPERTS, LANES), c0),
        ],
        out_shape=[
            jax.ShapeDtypeStruct((r, d), F32),
            jax.ShapeDtypeStruct((r, d), F32),
            jax.ShapeDtypeStruct((nt, SUBLANES, TM), I32),
            jax.ShapeDtypeStruct((N_EXPERTS, LANES), F32),
        ],
        compiler_params=_cparams(("arbitrary",)),
        name="outproj",
    )(xa, mix["cb"], mix["cv"], mix["cv"], mix["cv"], mix["zp"], mix["zp"], mix["zp"],
      yg, ygc, ym, ymc, mods, p["conv_w"], p["band"], p["wpool"], p["pscale"], p["w_out"],
      p["norm2"], p["wr"], p["br"], p["tri"])


def _load_slots(block_ref, idx_ref, sem):
    n = block_ref.shape[2]
    copies = [pltpu.make_async_copy(block_ref.at[0, k], idx_ref.at[pl.ds(k * n, n)], sem) for k in range(2)]
    for cp in copies:
        cp.start()
    for cp in copies:
        cp.wait()


def _row_copies(idx_ref, n_rows, make):
    def body(t, carry):
        for u in range(SUBLANES):
            r = t * SUBLANES + u
            make(0, t, u, idx_ref[r]).start()
            make(1, t, u, idx_ref[n_rows + r]).start()
        return carry
    lax.fori_loop(0, n_rows // SUBLANES, body, 0)


def _scatter_kernel(ps_ref, zs_ref, has_ref, nu_ref, info_ref, h2_ref, dest_ref, xs_ref,
                    dsm_ref, zbuf_ref, stage_ref, sem_ref, *, first_spare, n_blocks, n_tiles):
    j = pl.program_id(0)
    tm = h2_ref.shape[0]

    @pl.when(j == 0)
    def _():
        zbuf_ref[...] = jnp.zeros_like(zbuf_ref)

        def zero_copy(start):
            return pltpu.make_async_copy(
                zbuf_ref, xs_ref.at[pl.ds(pl.multiple_of(start, BM), BM)], sem_ref.at[2])

        for act in ("start", "wait"):
            for e in range(N_EXPERTS):
                @pl.when(has_ref[e] > 0)
                def _():
                    getattr(zero_copy(zs_ref[e]), act)()
            for jb in range(first_spare, n_blocks):
                @pl.when(jb >= nu_ref[0])
                def _():
                    getattr(zero_copy(jb * BM), act)()

    info = info_ref[0]
    e1, e2 = info[0:1], info[1:2]
    d1, d2 = info[2:3], info[3:4]
    for e in range(N_EXPERTS):
        d1 = d1 + jnp.where(e1 == e, ps_ref[e], 0)
        d2 = d2 + jnp.where(e2 == e, ps_ref[e], 0)
    irow = lax.broadcasted_iota(I32, (SUBLANES, tm), 0)
    dest_ref[0] = jnp.where(irow == 0, d1, jnp.where(irow == 1, d2, 0))
    _load_slots(dest_ref, dsm_ref, sem_ref.at[3])

    par = j % 2
    stage_ref[par] = h2_ref[...].reshape(stage_ref.shape[1:])

    def row_copy(_, t, u, slot):
        return pltpu.make_async_copy(stage_ref.at[par, t, pl.ds(u, 1)], xs_ref.at[pl.ds(slot, 1)],
                                     sem_ref.at[par])

    _row_copies(dsm_ref, tm, row_copy)

    def retire(which):
        for _ in range(2):
            pltpu.make_async_copy(h2_ref, xs_ref.at[pl.ds(0, tm)], sem_ref.at[which]).wait()

    @pl.when(j > 0)
    def _():
        retire(1 - par)

    @pl.when(j == n_tiles - 1)
    def _():
        retire(par)


def _scatter(info, h2, pad_start, zero_start, has_rows, n_used, n_slots):
    r, d = h2.shape
    nt = r // TM
    return pl.pallas_call(
        functools.partial(_scatter_kernel, first_spare=-(-2 * r // BM), n_blocks=n_slots // BM,
                          n_tiles=nt),
        grid_spec=pltpu.PrefetchScalarGridSpec(
            num_scalar_prefetch=4,
            grid=(nt,),
            in_specs=[
                pl.BlockSpec((1, SUBLANES, TM), lambda j, *_: (j, 0, 0)),
                pl.BlockSpec((TM, d), lambda j, *_: (j, 0)),
            ],
            out_specs=[
                pl.BlockSpec((1, SUBLANES, TM), lambda j, *_: (j, 0, 0)),
                pl.BlockSpec(memory_space=pl.ANY),
            ],
            scratch_shapes=[
                pltpu.SMEM((2 * TM,), I32),
                pltpu.VMEM((BM, d), F32),
                pltpu.VMEM((2, TM // SUBLANES, SUBLANES, d), F32),
                pltpu.SemaphoreType.DMA((4,)),
            ],
        ),
        out_shape=[
            jax.ShapeDtypeStruct((nt, SUBLANES, TM), I32),
            jax.ShapeDtypeStruct((n_slots, d), F32),
        ],
        compiler_params=_cparams(("arbitrary",)),
        name="moe_scatter",
    )(pad_start, zero_start, has_rows, n_used, info, h2)


def _expert_kernel(be_ref, nu_ref, xs_ref, wg_ref, wu_ref, wd_ref, y_ref, wgb, wub, wdb):
    j = pl.program_id(0)

    @pl.when(j < nu_ref[0])
    def _():
        e = be_ref[j]
        prev = be_ref[jnp.maximum(j - 1, 0)]

        @pl.when(jnp.logical_or(j == 0, e != prev))
        def _():
            wgb[...] = wg_ref[0].astype(BF16)
            wub[...] = wu_ref[0].astype(BF16)
            wdb[...] = wd_ref[0].astype(BF16)

        x = xs_ref[...].astype(BF16)
        a = jnp.dot(x, wgb[...], preferred_element_type=F32)
        u = jnp.dot(x, wub[...], preferred_element_type=F32)
        hmid = (a * _sigmoid(a) * u).astype(BF16)
        y_ref[...] = jnp.dot(hmid, wdb[...], preferred_element_type=F32)

    @pl.when(j >= nu_ref[0])
    def _():
        y_ref[...] = jnp.zeros_like(y_ref)


def _experts(xs, block_exp, n_used, w_gate, w_up, w_down, layer):
    n_slots, d = xs.shape
    de = w_gate.shape[-1]
    nbm = n_slots // BM
    n_exp = w_gate.shape[1]

    def blk(j, be, nu):
        return (jnp.minimum(j, nu[0] - 1), 0)

    def wmap(j, be, nu):
        return (layer * n_exp + be[jnp.minimum(j, nu[0] - 1)], 0, 0)

    wg = w_gate.reshape((-1,) + w_gate.shape[2:])
    wu = w_up.reshape((-1,) + w_up.shape[2:])
    wd = w_down.reshape((-1,) + w_down.shape[2:])
    return pl.pallas_call(
        _expert_kernel,
        grid_spec=pltpu.PrefetchScalarGridSpec(
            num_scalar_prefetch=2,
            grid=(nbm,),
            in_specs=[
                pl.BlockSpec((BM, d), blk),
                pl.BlockSpec((1, d, de), wmap),
                pl.BlockSpec((1, d, de), wmap),
                pl.BlockSpec((1, de, d), wmap),
            ],
            out_specs=pl.BlockSpec((BM, d), lambda j, be, nu: (j, 0)),
            scratch_shapes=[
                pltpu.VMEM((d, de), BF16),
                pltpu.VMEM((d, de), BF16),
                pltpu.VMEM((de, d), BF16),
            ],
        ),
        out_shape=jax.ShapeDtypeStruct((n_slots, d), F32),
        compiler_params=_cparams(("arbitrary",)),
        name="moe_experts",
    )(block_exp, n_used, xs, wg, wu, wd)


def _combine_kernel(dest_ref, dest_next_ref, info_ref, x_ref, mods_ref, y_ref, o_ref,
                    dsm_ref, ybuf_ref, sem_ref, *, d_model, geo, n_steps):
    j = pl.program_id(0)
    d = d_model
    tm = x_ref.shape[0]
    row = _tile_geometry(j, geo)["mod_row"]
    gate2 = mods_ref[0, pl.ds(row, 1), 5 * d:6 * d]
    cur = j % 2

    def gather(idx_block_ref, buf):
        _load_slots(idx_block_ref, dsm_ref, sem_ref.at[2])

        def row_copy(k, t, u, slot):
            return pltpu.make_async_copy(y_ref.at[pl.ds(slot, 1)],
                                         ybuf_ref.at[buf, k, t, pl.ds(u, 1)], sem_ref.at[buf])

        _row_copies(dsm_ref, tm, row_copy)

    @pl.when(j == 0)
    def _():
        gather(dest_ref, 0)

    @pl.when(j + 1 < n_steps)
    def _():
        gather(dest_next_ref, 1 - cur)

    for k in range(2):
        pltpu.make_async_copy(y_ref.at[pl.ds(0, tm)], o_ref, sem_ref.at[cur]).wait()

    info = info_ref[0]
    wrow = lax.broadcasted_iota(I32, (LANES, tm), 0)
    w_lanes = jnp.where(wrow == 0, lax.bitcast_convert_type(info[4:5], F32),
                        jnp.where(wrow == 1, lax.bitcast_convert_type(info[5:6], F32), 0.0))
    w_rows = w_lanes.T
    y1 = ybuf_ref[cur, 0].reshape(tm, d)
    y2 = ybuf_ref[cur, 1].reshape(tm, d)
    o_ref[...] = x_ref[...] + gate2 * (w_rows[:, 0:1] * y1 + w_rows[:, 1:2] * y2)


def _combine(dest, info, xn, mods, y, layer, geo):
    r, d = xn.shape
    n_steps = r // TM
    return pl.pallas_call(
        functools.partial(_combine_kernel, d_model=d, geo=geo, n_steps=n_steps),
        grid=(n_steps,),
        in_specs=[
            pl.BlockSpec((1, SUBLANES, TM), lambda j: (j, 0, 0)),
            pl.BlockSpec((1, SUBLANES, TM), lambda j: (jnp.minimum(j + 1, n_steps - 1), 0, 0)),
            pl.BlockSpec((1, SUBLANES, TM), lambda j: (j, 0, 0)),
            pl.BlockSpec((TM, d), lambda j: (j, 0)),
            pl.BlockSpec((1,) + mods.shape[1:], lambda j: (layer, 0, 0)),
            pl.BlockSpec(memory_space=pl.ANY),
        ],
        out_specs=pl.BlockSpec((TM, d), lambda j: (j, 0)),
        out_shape=jax.ShapeDtypeStruct((n_steps * TM, d), F32),
        scratch_shapes=[
            pltpu.SMEM((2 * TM,), I32),
            pltpu.VMEM((2, 2, TM // SUBLANES, SUBLANES, d), F32),
            pltpu.SemaphoreType.DMA((3,)),
        ],
        compiler_params=_cparams(("arbitrary",)),
        name="moe_combine",
    )(dest, dest, info, xn, mods, y)


def _pad_heads(w, n_heads, width):
    lead = w.shape[:-1]
    w = w.reshape(lead + (n_heads, width))
    w = jnp.pad(w, [(0, 0)] * len(lead) + [(0, 0), (0, LANES - width)])
    return w.reshape(lead + (n_heads * LANES,))


def _rope_tables(id_rows, nl):
    t = jnp.arange(nl)
    row_id = (t // GRID_W).astype(F32)
    col_id = (t % GRID_W).astype(F32)

    def angles(rot_dim):
        n_freq = rot_dim // 4
        inv_freq = jnp.power(ROPE_THETA, -jnp.arange(n_freq, dtype=F32) / n_freq)
        return jnp.concatenate([row_id[:, None] * inv_freq, col_id[:, None] * inv_freq], axis=-1)

    def with_ctx(tab, fill):
        return jnp.concatenate([tab, jnp.full((id_rows, LANES), fill, F32)], axis=0)

    ag = angles(HEAD_DIM)
    one_g = jnp.ones((nl, LANES - HEAD_DIM), F32)
    cos_g = jnp.concatenate([jnp.cos(ag), jnp.cos(ag), one_g], axis=-1)
    sin_g = jnp.concatenate([-jnp.sin(ag), jnp.sin(ag), 0.0 * one_g], axis=-1)
    am = angles(MLA_ROPE_DIM)
    one_n = jnp.ones((nl, MLA_NOPE_DIM), F32)
    one_t = jnp.ones((nl, LANES - MLA_QK_DIM), F32)
    cos_m = jnp.concatenate([one_n, jnp.cos(am), jnp.cos(am), one_t], axis=-1)
    sin_m = jnp.concatenate([0.0 * one_n, -jnp.sin(am), jnp.sin(am), 0.0 * one_t], axis=-1)
    return with_ctx(cos_g, 1.0), with_ctx(sin_g, 0.0), with_ctx(cos_m, 1.0), with_ctx(sin_m, 0.0)


def _pool_band():
    t = np.arange(TM)[:, None]
    src = np.arange(POOL_EXT)[None, :] - POOL_HALO
    live = np.arange(POOL_EXT)[None, :] < TM + 2 * POOL_HALO
    mats = [((src >= t - w // 2) & (src < t + w // 2) & live) for w in POOL_WINDOWS]
    return jnp.asarray(np.stack(mats).astype(np.float32), dtype=BF16)


def _prep_params(w_in, norm1, norm2, conv_w, w_pool, pool_scale, gqa_q_norm, gqa_k_norm,
                 mla_q_norm, mla_kv_norm, mla_w_uq, mla_w_uk, mla_w_uv, mla_qk_q_norm,
                 mla_qk_k_norm, w_out, w_router, b_router, id_rows, nl):
    dep = w_in.shape[0]
    o = 0
    pieces = {}
    for name, n in (("conv", 3 * CONV_DIM), ("pool", POOL_DIM), ("gq", GQA_HEADS * HEAD_DIM),
                    ("gk", GQA_KV_HEADS * HEAD_DIM), ("gv", GQA_KV_HEADS * HEAD_DIM),
                    ("mq", MLA_Q_RANK), ("mkv", MLA_KV_RANK), ("mkr", MLA_ROPE_DIM)):
        pieces[name] = w_in[..., o:o + n]
        o += n
    mkr = jnp.pad(pieces["mkr"], ((0, 0), (0, 0), (MLA_NOPE_DIM, LANES - MLA_QK_DIM)))
    w_in_p = jnp.concatenate([
        pieces["conv"], pieces["pool"], _pad_heads(pieces["gq"], GQA_HEADS, HEAD_DIM),
        _pad_heads(pieces["gk"], GQA_KV_HEADS, HEAD_DIM),
        pieces["mq"], pieces["mkv"], mkr], axis=-1).astype(BF16)
    wgvt = jnp.swapaxes(_pad_heads(pieces["gv"], GQA_KV_HEADS, HEAD_DIM), 1, 2).astype(BF16)
    wuvt = jnp.swapaxes(_pad_heads(mla_w_uv, MLA_HEADS, MLA_V_DIM), 1, 2).astype(BF16)
    eye = jnp.eye(len(POOL_WINDOWS), dtype=F32)
    wpool = jnp.einsum("gh,dgij->dgihj", eye, w_pool).reshape(dep, POOL_DIM, POOL_DIM).astype(BF16)
    cos_g, sin_g, cos_m, sin_m = _rope_tables(id_rows, nl)
    tri = np.triu(np.ones((TM, TM), np.float32), 1)
    wr_pad = jnp.pad(w_router, ((0, 0), (0, LANES - w_router.shape[1])))
    wr_hi = wr_pad.astype(BF16)
    wr_split = jnp.concatenate([wr_hi, (wr_pad - wr_hi.astype(F32)).astype(BF16)], axis=1)

    def row3(a):
        return a.reshape(dep, 1, a.shape[-1])

    return {
        "w_in": w_in_p,
        "norm1": row3(norm1), "norm2": row3(norm2),
        "gq": row3(_pad_heads(gqa_q_norm * (GQA_SCALE * LOG2E), 1, HEAD_DIM)),
        "gk": row3(_pad_heads(gqa_k_norm, 1, HEAD_DIM)),
        "mqn": row3(mla_q_norm), "mkvn": row3(mla_kv_norm),
        "qkq": row3(_pad_heads(mla_qk_q_norm * (MLA_SCALE * LOG2E), 1, MLA_QK_DIM)),
        "qkk": row3(_pad_heads(mla_qk_k_norm, 1, MLA_QK_DIM)),
        "wuq": _pad_heads(mla_w_uq, MLA_HEADS, MLA_QK_DIM).astype(BF16),
        "wuk": _pad_heads(mla_w_uk, MLA_HEADS, MLA_NOPE_DIM).astype(BF16),
        "wgvt": wgvt, "wuvt": wuvt,
        "cos_g": cos_g, "sin_g": sin_g, "cos_m": cos_m, "sin_m": sin_m,
        "conv_w": jnp.pad(conv_w, ((0, 0), (0, SUBLANES - conv_w.shape[1]), (0, 0))),
        "band": _pool_band(),
        "wpool": wpool, "pscale": row3(pool_scale),
        "w_out": w_out.astype(BF16),
        "wr": wr_split, "br": b_router.reshape(-1, 1),
        "tri": jnp.asarray(tri, dtype=BF16),
    }


def _moe_plan(counts):
    counts = counts.astype(I32)
    padded = ((counts + BM - 1) // BM) * BM
    pad_end = jnp.cumsum(padded)
    pad_start = pad_end - padded
    return pad_start, pad_end, padded


def kernel(x, c, ctx, c_ctx, w_mod, b_mod, norm1, norm2, w_in, conv_w, w_pool, pool_scale,
           gqa_q_norm, gqa_k_norm, mla_q_norm, mla_kv_norm, mla_w_uq, mla_w_uk, mla_w_uv,
           mla_qk_q_norm, mla_qk_k_norm, w_out, w_router, b_router, w_gate, w_up, w_down):
    nb, nl, d = x.shape
    nc = ctx.shape[1]
    depth = w_mod.shape[0]
    assert nc % TM == 0 and nl % TM == 0 and nl % GRID_W == 0 and nb < SUBLANES
    lt, nct = nl // TM, nc // TM
    wide = nl % INPROJ_ROWS == 0 and (nb * nc) % INPROJ_ROWS == 0
    geo = {"nbatch": nb, "nc": nc, "nl": nl, "lt": lt, "nct": nct,
           "n_lat_tiles": nb * lt, "n_tiles": nb * (lt + nct),
           "inproj_rows": INPROJ_ROWS if wide else TM}

    p = _prep_params(w_in, norm1, norm2, conv_w, w_pool, pool_scale, gqa_q_norm, gqa_k_norm,
                     mla_q_norm, mla_kv_norm, mla_w_uq, mla_w_uk, mla_w_uv, mla_qk_q_norm,
                     mla_qk_k_norm, w_out, w_router, b_router, geo["inproj_rows"], nl)
    cvec = jnp.concatenate([c, c_ctx[None, :], jnp.zeros((SUBLANES - nb - 1, d), F32)], axis=0)
    mods = _adaln(cvec, w_mod, b_mod)

    xa = jnp.concatenate([x.reshape(nb * nl, d), ctx.reshape(nb * nc, d)], axis=0)
    for i in range(depth):
        last = i == depth - 1
        mix = _inproj(xa, mods, i, p, geo)
        yg = _attention(mix["qg"], mix["kg"], mix["vgt"], True, geo, False)
        ym = _attention(mix["qm"], mix["km"], mix["vmt"], False, geo, False)
        if last:
            n_tiles, ygc, ymc = geo["n_lat_tiles"], yg, ym
        else:
            n_tiles = geo["n_tiles"]
            ygc = _attention(mix["qg"], mix["kg"], mix["vgt"], True, geo, True)
            ymc = _attention(mix["qm"], mix["km"], mix["vmt"], False, geo, True)
        xn, h2, info, cnt = _outproj(xa, mix, yg, ygc, ym, ymc, mods, i, p, geo, n_tiles)
        n_blocks = -(-2 * n_tiles * TM // BM) + N_EXPERTS
        n_slots = n_blocks * BM
        pad_start, pad_end, padded = _moe_plan(cnt[:, 0])
        n_used = (pad_end[-1:] // BM).astype(I32)
        block_row0 = jnp.arange(n_blocks, dtype=I32) * BM
        block_exp = jnp.minimum(jnp.sum((pad_end[None, :] <= block_row0[:, None]).astype(I32), axis=1),
                                N_EXPERTS - 1)
        dest, xs = _scatter(info, h2, pad_start, jnp.maximum(pad_end - BM, 0),
                            (padded > 0).astype(I32), n_used, n_slots)
        y = _experts(xs, block_exp, n_used, w_gate, w_up, w_down, i)
        xa = _combine(dest, info, xn, mods, y, i, geo)
    return xa.reshape(nb, nl, d)
```

```python
import functools
import math

import numpy as np
import jax
import jax.numpy as jnp
from jax import lax
from jax.experimental import pallas as pl
from jax.experimental.pallas import tpu as pltpu

F32 = jnp.float32
BF16 = jnp.bfloat16
I32 = jnp.int32

GRID_W = 64
CONV_DIM = 256
POOL_DIM = 256
POOL_WINDOWS = (2, 4, 8, 16)
HEAD_DIM = 64
GQA_HEADS = 4
GQA_KV_HEADS = 2
MLA_HEADS = 4
MLA_NOPE_DIM = 64
MLA_ROPE_DIM = 32
MLA_QK_DIM = MLA_NOPE_DIM + MLA_ROPE_DIM
MLA_V_DIM = 64
MLA_Q_RANK = 256
MLA_KV_RANK = 128
N_EXPERTS = 16
EXPERTS_PER_GROUP = 4
ROPE_THETA = 10000.0
NORM_EPS = 1e-6
LOG2E = 1.4426950408889634
GQA_SCALE = HEAD_DIM ** -0.5
MLA_SCALE = MLA_QK_DIM ** -0.5

LANES = 128
SUBLANES = 8
BF16_ROWS = 16
VMEM_LIMIT = 56 * 1024 * 1024

TM = 256
ATT_ROWS = 2048
INPROJ_ROWS = 256
BM = 512
V_ROWS = HEAD_DIM + BF16_ROWS
ATT_PAIR_UNROLL = 4
ROW_COPY_UNROLL = 8
POOL_EXT = 512
POOL_HALO = 8

ZC_B, ZC_C, ZC_U, ZC_P = 0, 256, 512, 768
ZC_GQ = 1024
ZC_GK = ZC_GQ + GQA_HEADS * LANES
ZC_MQ = ZC_GK + GQA_KV_HEADS * LANES
ZC_MKV = ZC_MQ + MLA_Q_RANK
ZC_MKR = ZC_MKV + MLA_KV_RANK
ZC_END = ZC_MKR + LANES

HIGHEST = lax.Precision.HIGHEST


def _cparams(sem, vmem=VMEM_LIMIT):
    return pltpu.CompilerParams(dimension_semantics=sem, vmem_limit_bytes=vmem)


def _sigmoid(v):
    return 1.0 / (1.0 + jnp.exp(-v))


def _adaln_kernel(c_ref, w_ref, b_ref, o_ref):
    c = c_ref[...]
    s = c * _sigmoid(c)
    o_ref[0] = jnp.dot(s, w_ref[0], preferred_element_type=F32, precision=HIGHEST) + b_ref[0]


def _adaln(cvec, w_mod, b_mod):
    depth, d, n6 = w_mod.shape
    tn = 1536 if n6 % 1536 == 0 else n6
    rows = cvec.shape[0]
    return pl.pallas_call(
        _adaln_kernel,
        grid=(depth, n6 // tn),
        in_specs=[
            pl.BlockSpec((rows, d), lambda i, n: (0, 0)),
            pl.BlockSpec((1, d, tn), lambda i, n: (i, 0, n)),
            pl.BlockSpec((1, 1, tn), lambda i, n: (i, 0, n)),
        ],
        out_specs=pl.BlockSpec((1, rows, tn), lambda i, n: (i, 0, n)),
        out_shape=jax.ShapeDtypeStruct((depth, rows, n6), F32),
        compiler_params=_cparams(("arbitrary", "arbitrary")),
        name="adaln",
    )(cvec, w_mod, b_mod.reshape(depth, 1, n6))


def _tile_geometry(j, geo):
    lt, nct, n_lat = geo["lt"], geo["nct"], geo["n_lat_tiles"]
    is_ctx = j >= n_lat
    jc = j - n_lat
    sample = jnp.where(is_ctx, jc // nct, j // lt)
    jt = jnp.where(is_ctx, jc % nct, j % lt)
    return {
        "is_ctx": is_ctx,
        "mod_row": jnp.where(is_ctx, geo["nbatch"], sample),
        "first": jt == 0,
        "last": jt == jnp.where(is_ctx, nct, lt) - 1,
        "pos0": jt * TM,
        "seg_len": jnp.where(is_ctx, geo["nc"], geo["nl"]),
    }


def _norm_rope(slab, gain, cos, sin, n_valid, first_half, half):
    ms = jnp.sum(slab * slab, axis=-1, keepdims=True) * (1.0 / n_valid)
    y = slab * lax.rsqrt(ms + NORM_EPS) * gain
    partner = jnp.where(first_half, pltpu.roll(y, LANES - half, 1), pltpu.roll(y, half, 1))
    return y * cos + partner * sin


def _inproj_kernel(x_ref, mods_ref, n1_ref, cg_ref, sg_ref, cm_ref, sm_ref, win_ref,
                   gq_ref, gk_ref, mqn_ref, mkvn_ref, qkq_ref, qkk_ref, wuq_ref, wuk_ref,
                   wgvt_ref, wuvt_ref,
                   cb_ref, cv_ref, zp_ref, qg_ref, kg_ref, vgt_ref, qm_ref, km_ref, vmt_ref,
                   *, d_model, nbatch, n_lat_steps, lat_steps_per_sample):
    d = d_model
    step = pl.program_id(0)
    row = jnp.where(step >= n_lat_steps, nbatch, step // lat_steps_per_sample)
    shift = mods_ref[0, pl.ds(row, 1), 0:d]
    scale = mods_ref[0, pl.ds(row, 1), d:2 * d]
    x = x_ref[...]
    ms = jnp.mean(x * x, axis=-1, keepdims=True)
    h = ((x * lax.rsqrt(ms + NORM_EPS)) * (n1_ref[0] * (1.0 + scale)) + shift).astype(BF16)
    nt_dims = (((1,), (1,)), ((), ()))

    def proj(c0, width):
        return jnp.dot(h, win_ref[0, :, c0:c0 + width], preferred_element_type=F32)

    def store_transposed(vt_ref, vt):
        srow = lax.broadcasted_iota(I32, vt.shape, 0)
        vt = jnp.where(jnp.bitwise_and(srow, LANES - 1) == HEAD_DIM, 1.0, vt).astype(BF16)
        for c in range(vt_ref.shape[0]):
            vt_ref[c] = vt[:, c * TM:(c + 1) * TM]

    tm = x.shape[0]
    lane = lax.broadcasted_iota(I32, (tm, LANES), 1)

    cm, sm = cm_ref[...], sm_ref[...]
    m_first = lane < MLA_NOPE_DIM + MLA_ROPE_DIM // 2
    zq = proj(ZC_MQ, MLA_Q_RANK)
    cq = zq * lax.rsqrt(jnp.mean(zq * zq, axis=-1, keepdims=True) + NORM_EPS) * mqn_ref[0]
    qpre = jnp.dot(cq.astype(BF16), wuq_ref[0], preferred_element_type=F32)
    zk = proj(ZC_MKV, MLA_KV_RANK + LANES)
    zkv, zkr = zk[:, 0:MLA_KV_RANK], zk[:, MLA_KV_RANK:]
    ckv = zkv * lax.rsqrt(jnp.mean(zkv * zkv, axis=-1, keepdims=True) + NORM_EPS) * mkvn_ref[0]
    ckv = ckv.astype(BF16)
    kvp = jnp.dot(ckv, wuk_ref[0], preferred_element_type=F32)
    store_transposed(vmt_ref, lax.dot_general(wuvt_ref[0], ckv, nt_dims, preferred_element_type=F32))
    for hd in range(MLA_HEADS):
        sl = slice(hd * LANES, (hd + 1) * LANES)
        qm_ref[:, sl] = _norm_rope(qpre[:, sl], qkq_ref[0], cm, sm, MLA_QK_DIM,
                                   m_first, MLA_ROPE_DIM // 2).astype(BF16)
        km_ref[:, sl] = _norm_rope(kvp[:, sl] + zkr, qkk_ref[0], cm, sm, MLA_QK_DIM,
                                   m_first, MLA_ROPE_DIM // 2).astype(BF16)

    cg, sg = cg_ref[...], sg_ref[...]
    g_first = lane < HEAD_DIM // 2
    zg = proj(ZC_GQ, (GQA_HEADS + GQA_KV_HEADS) * LANES)
    for hd in range(GQA_HEADS):
        qg_ref[:, hd * LANES:(hd + 1) * LANES] = _norm_rope(
            zg[:, hd * LANES:(hd + 1) * LANES], gq_ref[0], cg, sg, HEAD_DIM, g_first,
            HEAD_DIM // 2).astype(BF16)
    for hd in range(GQA_KV_HEADS):
        slab = zg[:, (GQA_HEADS + hd) * LANES:(GQA_HEADS + hd + 1) * LANES]
        kg_ref[:, hd * LANES:(hd + 1) * LANES] = _norm_rope(
            slab, gk_ref[0], cg, sg, HEAD_DIM, g_first, HEAD_DIM // 2).astype(BF16)
    store_transposed(vgt_ref, lax.dot_general(wgvt_ref[0], h, nt_dims, preferred_element_type=F32))

    zc = proj(ZC_B, 3 * CONV_DIM + POOL_DIM)
    cb_ref[...] = zc[:, ZC_B:ZC_B + CONV_DIM].astype(BF16)
    cv_ref[...] = (zc[:, ZC_C:ZC_C + CONV_DIM] * zc[:, ZC_U:ZC_U + CONV_DIM]).astype(BF16)
    zp_ref[...] = zc[:, ZC_P:ZC_P + POOL_DIM]


def _inproj(xa, mods, layer, p, geo):
    r, d = xa.shape
    tmi = geo["inproj_rows"]
    n_lat_steps = geo["nbatch"] * geo["nl"] // tmi
    lat_steps_per_sample = geo["nl"] // tmi
    rowmap = lambda j: (j, 0)
    posmap = lambda j: (jnp.where(j >= n_lat_steps, lat_steps_per_sample, j % lat_steps_per_sample), 0)
    lay3 = lambda j: (layer, 0, 0)

    def full3(a):
        return pl.BlockSpec((1,) + a.shape[1:], lay3)

    outs = [("cb", CONV_DIM, BF16, False), ("cv", CONV_DIM, BF16, False), ("zp", POOL_DIM, F32, False),
            ("qg", GQA_HEADS * LANES, BF16, False), ("kg", GQA_KV_HEADS * LANES, BF16, False),
            ("vgt", GQA_KV_HEADS * LANES, BF16, True), ("qm", MLA_HEADS * LANES, BF16, False),
            ("km", MLA_HEADS * LANES, BF16, False), ("vmt", MLA_HEADS * LANES, BF16, True)]

    def out_spec(w, transposed):
        if transposed:
            return pl.BlockSpec((tmi // TM, w, TM), lambda j: (j, 0, 0))
        return pl.BlockSpec((tmi, w), rowmap)

    def out_shape(w, dt, transposed):
        return jax.ShapeDtypeStruct((r // TM, w, TM) if transposed else (r, w), dt)

    res = pl.pallas_call(
        functools.partial(_inproj_kernel, d_model=d, nbatch=geo["nbatch"], n_lat_steps=n_lat_steps,
                          lat_steps_per_sample=lat_steps_per_sample),
        grid=(r // tmi,),
        in_specs=[
            pl.BlockSpec((tmi, d), rowmap),
            full3(mods), full3(p["norm1"]),
            pl.BlockSpec((tmi, LANES), posmap), pl.BlockSpec((tmi, LANES), posmap),
            pl.BlockSpec((tmi, LANES), posmap), pl.BlockSpec((tmi, LANES), posmap),
            full3(p["w_in"]), full3(p["gq"]), full3(p["gk"]), full3(p["mqn"]), full3(p["mkvn"]),
            full3(p["qkq"]), full3(p["qkk"]), full3(p["wuq"]), full3(p["wuk"]),
            full3(p["wgvt"]), full3(p["wuvt"]),
        ],
        out_specs=[out_spec(w, t) for _, w, _, t in outs],
        out_shape=[out_shape(w, dt, t) for _, w, dt, t in outs],
        compiler_params=_cparams(("arbitrary",)),
        name="inproj",
    )(xa, mods, p["norm1"], p["cos_g"], p["sin_g"], p["cos_m"], p["sin_m"], p["w_in"],
      p["gq"], p["gk"], p["mqn"], p["mkvn"], p["qkq"], p["qkk"], p["wuq"], p["wuk"],
      p["wgvt"], p["wuvt"])
    return {name: a for (name, _, _, _), a in zip(outs, res)}


def _attn_kernel(q_ref, kl_ref, kc_ref, vtl_ref, vtc_ref, o_ref, q_st, s_buf, p_buf, a_buf, m_ref, acc_ref,
                 *, shared_kv, n_lat, n_ctx, tk):
    tq = q_ref.shape[0]
    n_steps = n_lat + n_ctx
    if shared_kv:
        q_st[0] = jnp.concatenate([q_ref[:, 0:LANES], q_ref[:, LANES:2 * LANES]], axis=0)
        cols = [0]
    else:
        q_st[0] = q_ref[:, 0:LANES]
        q_st[1] = q_ref[:, LANES:2 * LANES]
        cols = [0, LANES]
    for si, col in enumerate(cols):
        q_s, m_s, acc_s = q_st.at[si], m_ref.at[si], acc_ref.at[si]
        m_s[...] = jnp.full(m_s.shape, -1e30, F32)
        acc_s[...] = jnp.zeros(acc_s.shape, F32)
        p_buf[1] = jnp.zeros(p_buf.shape[1:], BF16)
        a_buf[1] = jnp.ones(a_buf.shape[1:], F32)

        def pick(i, lat_fn, ctx_fn):
            ctx = ctx_fn(jnp.clip(i - n_lat, 0, n_ctx - 1))
            if n_lat == 0:
                return ctx
            return jnp.where(i >= n_lat, ctx, lat_fn(jnp.clip(i, 0, n_lat - 1)))

        def scores(i, slot, q_s=q_s, col=col):
            k = pick(i, lambda t: kl_ref[pl.ds(pl.multiple_of(t * tk, tk), tk), col:col + LANES],
                     lambda t: kc_ref[pl.ds(pl.multiple_of(t * tk, tk), tk), col:col + LANES])
            s_buf[slot] = lax.dot_general(k, q_s[...], (((1,), (1,)), ((), ())),
                                          preferred_element_type=F32)

        def softmax(i, slot, m_s=m_s):
            s = s_buf[slot]
            m_old = m_s[...]
            m_new = jnp.maximum(m_old, jnp.max(s, axis=0, keepdims=True))
            a_buf[slot] = jnp.exp2(m_old - m_new)
            m_eff = m_new[0:1] + jnp.where(i < n_steps, 0.0, 1e9)
            p_buf[slot] = jnp.exp2(s - m_eff).astype(BF16)
            m_s[...] = m_new

        def accumulate(i, slot, acc_s=acc_s, col=col):
            vt = pick(i, lambda t: vtl_ref[t, col:col + V_ROWS, :],
                      lambda t: vtc_ref[t, col:col + V_ROWS, :])
            acc_s[...] = a_buf[slot][0:1] * acc_s[...] + jnp.dot(vt, p_buf[slot],
                                                                 preferred_element_type=F32)

        scores(0, 0)

        def pair(t, carry):
            i = 2 * t
            scores(i + 1, 1)
            softmax(i, 0)
            accumulate(i - 1, 1)
            scores(i + 2, 0)
            softmax(i + 1, 1)
            accumulate(i, 0)
            return carry

        lax.fori_loop(0, (n_steps + 2) // 2, pair, 0, unroll=ATT_PAIR_UNROLL)
    def finish(acc_t):
        o_t = acc_t * (1.0 / acc_t[HEAD_DIM:HEAD_DIM + 1])
        pad = jnp.zeros((LANES - V_ROWS, o_t.shape[1]), F32)
        return jnp.concatenate([o_t, pad], axis=0).T

    if shared_kv:
        o_both = finish(acc_ref[0])
        o0, o1 = o_both[0:tq], o_both[tq:2 * tq]
    else:
        o0, o1 = finish(acc_ref[0]), finish(acc_ref[1])
    lane = lax.broadcasted_iota(I32, (tq, LANES), 1)
    o_ref[...] = jnp.where(lane < HEAD_DIM, o0, pltpu.roll(o1, HEAD_DIM, 1)).astype(BF16)


def _attention(q, k, vt, shared_kv, geo, ctx_queries):
    nb, nc, nl = geo["nbatch"], geo["nc"], geo["nl"]
    kw = LANES if shared_kv else 2 * LANES
    tk = TM
    ctx_blk0 = nb * nl // nc
    ctx_k = pl.BlockSpec((nc, kw), lambda b, g, t: (ctx_blk0 + b, g))
    ctx_vt = pl.BlockSpec((nc // tk, kw, tk), lambda b, g, t: (ctx_blk0 + b, g, 0))
    if ctx_queries:
        tq, q_per, q_blk0, n_lat = nc, 1, ctx_blk0, 0
        lat_k, lat_vt = ctx_k, ctx_vt
    else:
        tq = min(ATT_ROWS, nl) // (2 if shared_kv else 1)
        q_per, q_blk0, n_lat = nl // tq, 0, nl // tk
        lat_k = pl.BlockSpec((nl, kw), lambda b, g, t: (b, g))
        lat_vt = pl.BlockSpec((nl // tk, kw, tk), lambda b, g, t: (b, g, 0))
    n_streams, rows = (1, 2 * tq) if shared_kv else (2, tq)
    return pl.pallas_call(
        functools.partial(_attn_kernel, shared_kv=shared_kv, n_lat=n_lat, n_ctx=nc // tk, tk=tk),
        grid=(nb, 2, q_per),
        in_specs=[
            pl.BlockSpec((tq, 2 * LANES), lambda b, g, t: (q_blk0 + b * q_per + t, g)),
            lat_k, ctx_k, lat_vt, ctx_vt,
        ],
        out_specs=pl.BlockSpec((tq, LANES), lambda b, g, t: (b * q_per + t, g)),
        out_shape=jax.ShapeDtypeStruct((nb * q_per * tq, 2 * LANES), BF16),
        scratch_shapes=[pltpu.VMEM((n_streams, rows, LANES), BF16),
                        pltpu.VMEM((2, tk, rows), F32),
                        pltpu.VMEM((2, tk, rows), BF16),
                        pltpu.VMEM((2, SUBLANES, rows), F32),
                        pltpu.VMEM((n_streams, SUBLANES, rows), F32),
                        pltpu.VMEM((n_streams, V_ROWS, rows), F32)],
        compiler_params=_cparams(("arbitrary", "arbitrary", "arbitrary")),
        name=("attn_gqa" if shared_kv else "attn_mla") + ("_ctx" if ctx_queries else ""),
    )(q, k, k, vt, vt)


def _top2_sum(a, b, c, d):
    hi_ab, lo_ab = jnp.maximum(a, b), jnp.minimum(a, b)
    hi_cd, lo_cd = jnp.maximum(c, d), jnp.minimum(c, d)
    first = jnp.maximum(hi_ab, hi_cd)
    second = jnp.maximum(jnp.minimum(hi_ab, hi_cd), jnp.maximum(lo_ab, lo_cd))
    return first + second


def _outproj_kernel(x_ref, cb_ref, cv_ref, cvp_ref, cvn_ref, zp_ref, zpp_ref, zpn_ref,
                    ygl_ref, ygc_ref, yml_ref, ymc_ref,
                    mods_ref, convw_ref, band_ref, wpool_ref, pscale_ref,
                    wout_ref, n2_ref, wr_ref, br_ref, tri_ref,
                    xo_ref, h2_ref, info_ref, cnt_ref,
                    *, d_model, geo):
    tile = pl.program_id(0)
    d = d_model
    tm = x_ref.shape[0]
    tg = _tile_geometry(tile, geo)
    is_ctx = tg["is_ctx"]
    keep_prev = jnp.where(tg["first"], 0.0, 1.0)
    keep_next = jnp.where(tg["last"], 0.0, 1.0)
    row = tg["mod_row"]
    gate1 = mods_ref[0, pl.ds(row, 1), 2 * d:3 * d]
    shift2 = mods_ref[0, pl.ds(row, 1), 3 * d:4 * d]
    scale2 = mods_ref[0, pl.ds(row, 1), 4 * d:5 * d]

    v = cv_ref[...].astype(F32)
    prev_row = cvp_ref[...].astype(F32)[BF16_ROWS - 1:BF16_ROWS] * keep_prev
    next_row = cvn_ref[...].astype(F32)[0:1] * keep_next
    rid = lax.broadcasted_iota(I32, (tm, CONV_DIM), 0)
    vm1 = jnp.where(rid == 0, prev_row, pltpu.roll(v, 1, 0))
    vp1 = jnp.where(rid == tm - 1, next_row, pltpu.roll(v, tm - 1, 0))
    cw = convw_ref[0]
    y_conv = cb_ref[...].astype(F32) * (vm1 * cw[0:1] + v * cw[1:2] + vp1 * cw[2:3])

    zp = zp_ref[...]
    ext = jnp.concatenate(
        [zpp_ref[...] * keep_prev, zp, zpn_ref[...] * keep_next,
         jnp.zeros((POOL_EXT - tm - 2 * POOL_HALO, POOL_DIM), F32)], axis=0).astype(BF16)
    ext_a, ext_b = ext[:, 0:LANES], ext[:, LANES:2 * LANES]
    lane = lax.broadcasted_iota(I32, (tm, LANES), 1)
    low = lane < POOL_DIM // 4
    sum_a = jnp.where(low, jnp.dot(band_ref[0], ext_a, preferred_element_type=F32),
                      jnp.dot(band_ref[1], ext_a, preferred_element_type=F32))
    sum_b = jnp.where(low, jnp.dot(band_ref[2], ext_b, preferred_element_type=F32),
                      jnp.dot(band_ref[3], ext_b, preferred_element_type=F32))
    sums = jnp.concatenate([sum_a, sum_b], axis=1)
    lane_p = lax.broadcasted_iota(I32, (tm, POOL_DIM), 1)
    half_w = jnp.left_shift(1, jnp.right_shift(lane_p, int(math.log2(POOL_DIM // 4))))
    pos = tg["pos0"] + rid
    cnt = (jnp.minimum(pos + half_w, tg["seg_len"]) - jnp.maximum(pos - half_w, 0)).astype(F32)
    dlt = sums / cnt - zp
    y_pool = jnp.dot(dlt.astype(BF16), wpool_ref[0], preferred_element_type=F32) * pscale_ref[0]

    y_gqa = jnp.where(is_ctx, ygc_ref[...], ygl_ref[...])
    y_mla = jnp.where(is_ctx, ymc_ref[...], yml_ref[...])
    ycat = jnp.concatenate([y_conv.astype(BF16), y_pool.astype(BF16), y_gqa, y_mla], axis=1)
    y = jnp.dot(ycat, wout_ref[0], preferred_element_type=F32)
    xn = x_ref[...] + gate1 * y
    xo_ref[...] = xn
    ms = jnp.mean(xn * xn, axis=-1, keepdims=True)
    h2 = (xn * lax.rsqrt(ms + NORM_EPS)) * (n2_ref[0] * (1.0 + scale2)) + shift2
    h2_ref[...] = h2

    h_hi = h2.astype(BF16)
    h_lo = (h2 - h_hi.astype(F32)).astype(BF16)
    wr = wr_ref[...]
    part = jnp.dot(h_hi, wr, preferred_element_type=F32)
    small = part[:, LANES:] + jnp.dot(h_lo, wr[:, 0:LANES], preferred_element_type=F32)
    logits = (part[:, 0:LANES] + small).T[0:N_EXPERTS]
    scores = _sigmoid(logits)
    sel = scores + br_ref[...]
    epg = EXPERTS_PER_GROUP
    n_groups = N_EXPERTS // epg
    srow = [sel[e:e + 1] for e in range(N_EXPERTS)]
    crow = [scores[e:e + 1] for e in range(N_EXPERTS)]
    gscore = [_top2_sum(*srow[g * epg:(g + 1) * epg]) for g in range(n_groups)]
    gbest = jnp.zeros_like(gscore[0]).astype(I32)
    best = gscore[0]
    for g in range(1, n_groups):
        upd = gscore[g] > best
        gbest = jnp.where(upd, g, gbest)
        best = jnp.where(upd, gscore[g], best)

    def pick(rows_, j):
        out = rows_[(n_groups - 1) * epg + j]
        for g in range(n_groups - 2, -1, -1):
            out = jnp.where(gbest == g, rows_[g * epg + j], out)
        return out

    sv = [pick(srow, j) for j in range(epg)]
    cv_ = [pick(crow, j) for j in range(epg)]
    i1 = jnp.zeros_like(gbest)
    b1 = sv[0]
    for j in range(1, epg):
        upd = sv[j] > b1
        i1 = jnp.where(upd, j, i1)
        b1 = jnp.where(upd, sv[j], b1)
    i2 = jnp.zeros_like(gbest)
    b2 = jnp.full_like(b1, -jnp.inf)
    for j in range(epg):
        upd = jnp.logical_and(i1 != j, sv[j] > b2)
        i2 = jnp.where(upd, j, i2)
        b2 = jnp.where(upd, sv[j], b2)
    s1 = cv_[epg - 1]
    s2 = cv_[epg - 1]
    for j in range(epg - 2, -1, -1):
        s1 = jnp.where(i1 == j, cv_[j], s1)
        s2 = jnp.where(i2 == j, cv_[j], s2)
    inv = 1.0 / (s1 + s2)
    e1 = gbest * epg + i1
    e2 = gbest * epg + i2

    @pl.when(tile == 0)
    def _():
        cnt_ref[...] = jnp.zeros_like(cnt_ref)

    erow = lax.broadcasted_iota(I32, (N_EXPERTS, tm), 0)
    hit1 = erow == e1
    hit2 = erow == e2
    onehot = jnp.where(hit1, 1.0, 0.0) + jnp.where(hit2, 1.0, 0.0)
    before = jnp.dot(onehot.astype(BF16), tri_ref[...], preferred_element_type=F32)
    tot = cnt_ref[:, 0:1] + before
    rank1 = jnp.sum(jnp.where(hit1, tot, 0.0), axis=0, keepdims=True).astype(I32)
    rank2 = jnp.sum(jnp.where(hit2, tot, 0.0), axis=0, keepdims=True).astype(I32)
    cnt_ref[...] = cnt_ref[...] + jnp.sum(onehot, axis=1, keepdims=True)

    w1 = lax.bitcast_convert_type(s1 * inv, I32)
    w2 = lax.bitcast_convert_type(s2 * inv, I32)
    irow = lax.broadcasted_iota(I32, (SUBLANES, tm), 0)
    info = jnp.where(irow == 0, e1, jnp.where(irow == 1, e2, jnp.where(
        irow == 2, rank1, jnp.where(irow == 3, rank2, jnp.where(
            irow == 4, w1, jnp.where(irow == 5, w2, 0))))))
    info_ref[0] = info


def _outproj(xa, mix, yg, ygc, ym, ymc, mods, layer, p, geo, n_tiles):
    d = xa.shape[1]
    nt_all, n_lat = geo["n_tiles"], geo["n_lat_tiles"]
    r = n_tiles * TM
    nt = n_tiles
    rowmap = lambda j: (j, 0)
    latmap = lambda j: (jnp.minimum(j, n_lat - 1), 0)
    ctxmap = lambda j: (jnp.clip(j - n_lat, 0, ygc.shape[0] // TM - 1), 0)
    lay3 = lambda j: (layer, 0, 0)
    c0 = lambda j: (0, 0)
    c3 = lambda j: (0, 0, 0)
    bf_blocks = TM // BF16_ROWS
    f_blocks = TM // SUBLANES

    def full3(a):
        return pl.BlockSpec((1,) + a.shape[1:], lay3)

    return pl.pallas_call(
        functools.partial(_outproj_kernel, d_model=d, geo=geo),
        grid=(n_tiles,),
        in_specs=[
            pl.BlockSpec((TM, d), rowmap),
            pl.BlockSpec((TM, CONV_DIM), rowmap),
            pl.BlockSpec((TM, CONV_DIM), rowmap),
            pl.BlockSpec((BF16_ROWS, CONV_DIM), lambda j: (jnp.maximum(j * bf_blocks - 1, 0), 0)),
            pl.BlockSpec((BF16_ROWS, CONV_DIM),
                         lambda j: (jnp.minimum((j + 1) * bf_blocks, nt_all * bf_blocks - 1), 0)),
            pl.BlockSpec((TM, POOL_DIM), rowmap),
            pl.BlockSpec((SUBLANES, POOL_DIM), lambda j: (jnp.maximum(j * f_blocks - 1, 0), 0)),
            pl.BlockSpec((SUBLANES, POOL_DIM),
                         lambda j: (jnp.minimum((j + 1) * f_blocks, nt_all * f_blocks - 1), 0)),
            pl.BlockSpec((TM, 2 * LANES), latmap),
            pl.BlockSpec((TM, 2 * LANES), ctxmap),
            pl.BlockSpec((TM, 2 * LANES), latmap),
            pl.BlockSpec((TM, 2 * LANES), ctxmap),
            full3(mods), full3(p["conv_w"]),
            pl.BlockSpec(p["band"].shape, c3),
            full3(p["wpool"]), full3(p["pscale"]), full3(p["w_out"]), full3(p["norm2"]),
            pl.BlockSpec(p["wr"].shape, c0), pl.BlockSpec(p["br"].shape, c0),
            pl.BlockSpec(p["tri"].shape, c0),
        ],
        out_specs=[
            pl.BlockSpec((TM, d), rowmap),
            pl.BlockSpec((TM, d), rowmap),
            pl.BlockSpec((1, SUBLANES, TM), lambda j: (j, 0, 0)),
            pl.BlockSpec((N_EXPERTS, LANES), c0),
        ],
        out_shape=[
            jax.ShapeDtypeStruct((r, d), F32),
            jax.ShapeDtypeStruct((r, d), F32),
            jax.ShapeDtypeStruct((nt, SUBLANES, TM), I32),
            jax.ShapeDtypeStruct((N_EXPERTS, LANES), F32),
        ],
        compiler_params=_cparams(("arbitrary",)),
        name="outproj",
    )(xa, mix["cb"], mix["cv"], mix["cv"], mix["cv"], mix["zp"], mix["zp"], mix["zp"],
      yg, ygc, ym, ymc, mods, p["conv_w"], p["band"], p["wpool"], p["pscale"], p["w_out"],
      p["norm2"], p["wr"], p["br"], p["tri"])


def _load_slots(block_ref, idx_ref, sem):
    n = block_ref.shape[2]
    copies = [pltpu.make_async_copy(block_ref.at[0, k], idx_ref.at[pl.ds(k * n, n)], sem) for k in range(2)]
    for cp in copies:
        cp.start()
    for cp in copies:
        cp.wait()


def _row_copies(idx_ref, n_rows, make):
    def body(t, carry):
        for u in range(SUBLANES):
            r = t * SUBLANES + u
            make(0, t, u, idx_ref[r]).start(priority=0)
            make(1, t, u, idx_ref[n_rows + r]).start(priority=1)
        return carry
    lax.fori_loop(0, n_rows // SUBLANES, body, 0)


def _scatter_kernel(ps_ref, zs_ref, has_ref, nu_ref, info_ref, h2_ref, dest_ref, xs_ref,
                    dsm_ref, zbuf_ref, stage_ref, sem_ref, *, first_spare, n_blocks, n_tiles):
    j = pl.program_id(0)
    tm = h2_ref.shape[0]

    @pl.when(j == 0)
    def _():
        zbuf_ref[...] = jnp.zeros_like(zbuf_ref)

        def zero_copy(start):
            return pltpu.make_async_copy(
                zbuf_ref, xs_ref.at[pl.ds(pl.multiple_of(start, BM), BM)], sem_ref.at[2])

        for act in ("start", "wait"):
            for e in range(N_EXPERTS):
                @pl.when(has_ref[e] > 0)
                def _():
                    getattr(zero_copy(zs_ref[e]), act)()
            for jb in range(first_spare, n_blocks):
                @pl.when(jb >= nu_ref[0])
                def _():
                    getattr(zero_copy(jb * BM), act)()

    info = info_ref[0]
    e1, e2 = info[0:1], info[1:2]
    d1, d2 = info[2:3], info[3:4]
    for e in range(N_EXPERTS):
        d1 = d1 + jnp.where(e1 == e, ps_ref[e], 0)
        d2 = d2 + jnp.where(e2 == e, ps_ref[e], 0)
    irow = lax.broadcasted_iota(I32, (SUBLANES, tm), 0)
    dest_ref[0] = jnp.where(irow == 0, d1, jnp.where(irow == 1, d2, 0))
    _load_slots(dest_ref, dsm_ref, sem_ref.at[3])

    par = j % 2
    stage_ref[par] = h2_ref[...].reshape(stage_ref.shape[1:])

    def row_copy(_, t, u, slot):
        return pltpu.make_async_copy(stage_ref.at[par, t, pl.ds(u, 1)], xs_ref.at[pl.ds(slot, 1)],
                                     sem_ref.at[par])

    _row_copies(dsm_ref, tm, row_copy)

    def retire(which):
        for _ in range(2):
            pltpu.make_async_copy(h2_ref, xs_ref.at[pl.ds(0, tm)], sem_ref.at[which]).wait()

    @pl.when(j > 0)
    def _():
        retire(1 - par)

    @pl.when(j == n_tiles - 1)
    def _():
        retire(par)


def _scatter(info, h2, pad_start, zero_start, has_rows, n_used, n_slots):
    r, d = h2.shape
    nt = r // TM
    return pl.pallas_call(
        functools.partial(_scatter_kernel, first_spare=-(-2 * r // BM), n_blocks=n_slots // BM,
                          n_tiles=nt),
        grid_spec=pltpu.PrefetchScalarGridSpec(
            num_scalar_prefetch=4,
            grid=(nt,),
            in_specs=[
                pl.BlockSpec((1, SUBLANES, TM), lambda j, *_: (j, 0, 0)),
                pl.BlockSpec((TM, d), lambda j, *_: (j, 0)),
            ],
            out_specs=[
                pl.BlockSpec((1, SUBLANES, TM), lambda j, *_: (j, 0, 0)),
                pl.BlockSpec(memory_space=pl.ANY),
            ],
            scratch_shapes=[
                pltpu.SMEM((2 * TM,), I32),
                pltpu.VMEM((BM, d), F32),
                pltpu.VMEM((2, TM // SUBLANES, SUBLANES, d), F32),
                pltpu.SemaphoreType.DMA((4,)),
            ],
        ),
        out_shape=[
            jax.ShapeDtypeStruct((nt, SUBLANES, TM), I32),
            jax.ShapeDtypeStruct((n_slots, d), F32),
        ],
        compiler_params=_cparams(("arbitrary",)),
        name="moe_scatter",
    )(pad_start, zero_start, has_rows, n_used, info, h2)


def _expert_kernel(be_ref, nu_ref, xs_ref, wg_ref, wu_ref, wd_ref, y_ref, wgb, wub, wdb):
    j = pl.program_id(0)

    @pl.when(j < nu_ref[0])
    def _():
        e = be_ref[j]
        prev = be_ref[jnp.maximum(j - 1, 0)]

        @pl.when(jnp.logical_or(j == 0, e != prev))
        def _():
            wgb[...] = wg_ref[0].astype(BF16)
            wub[...] = wu_ref[0].astype(BF16)
            wdb[...] = wd_ref[0].astype(BF16)

        x = xs_ref[...].astype(BF16)
        a = jnp.dot(x, wgb[...], preferred_element_type=F32)
        u = jnp.dot(x, wub[...], preferred_element_type=F32)
        hmid = (a * _sigmoid(a) * u).astype(BF16)
        y_ref[...] = jnp.dot(hmid, wdb[...], preferred_element_type=F32)

    @pl.when(j >= nu_ref[0])
    def _():
        y_ref[...] = jnp.zeros_like(y_ref)


def _experts(xs, block_exp, n_used, w_gate, w_up, w_down, layer):
    n_slots, d = xs.shape
    de = w_gate.shape[-1]
    nbm = n_slots // BM
    n_exp = w_gate.shape[1]

    def blk(j, be, nu):
        return (jnp.minimum(j, nu[0] - 1), 0)

    def wmap(j, be, nu):
        return (layer * n_exp + be[jnp.minimum(j, nu[0] - 1)], 0, 0)

    wg = w_gate.reshape((-1,) + w_gate.shape[2:])
    wu = w_up.reshape((-1,) + w_up.shape[2:])
    wd = w_down.reshape((-1,) + w_down.shape[2:])
    return pl.pallas_call(
        _expert_kernel,
        grid_spec=pltpu.PrefetchScalarGridSpec(
            num_scalar_prefetch=2,
            grid=(nbm,),
            in_specs=[
                pl.BlockSpec((BM, d), blk),
                pl.BlockSpec((1, d, de), wmap),
                pl.BlockSpec((1, d, de), wmap),
                pl.BlockSpec((1, de, d), wmap),
            ],
            out_specs=pl.BlockSpec((BM, d), lambda j, be, nu: (j, 0)),
            scratch_shapes=[
                pltpu.VMEM((d, de), BF16),
                pltpu.VMEM((d, de), BF16),
                pltpu.VMEM((de, d), BF16),
            ],
        ),
        out_shape=jax.ShapeDtypeStruct((n_slots, d), F32),
        compiler_params=_cparams(("arbitrary",)),
        name="moe_experts",
    )(block_exp, n_used, xs, wg, wu, wd)


def _combine_kernel(dest_ref, dest_next_ref, info_ref, x_ref, mods_ref, y_ref, o_ref,
                    dsm_ref, ybuf_ref, sem_ref, *, d_model, geo, n_steps):
    j = pl.program_id(0)
    d = d_model
    tm = x_ref.shape[0]
    row = _tile_geometry(j, geo)["mod_row"]
    gate2 = mods_ref[0, pl.ds(row, 1), 5 * d:6 * d]
    cur = j % 2

    def gather(idx_block_ref, buf):
        _load_slots(idx_block_ref, dsm_ref, sem_ref.at[2])

        def row_copy(k, t, u, slot):
            return pltpu.make_async_copy(y_ref.at[pl.ds(slot, 1)],
                                         ybuf_ref.at[buf, k, t, pl.ds(u, 1)], sem_ref.at[buf])

        _row_copies(dsm_ref, tm, row_copy)

    @pl.when(j == 0)
    def _():
        gather(dest_ref, 0)

    @pl.when(j + 1 < n_steps)
    def _():
        gather(dest_next_ref, 1 - cur)

    for k in range(2):
        pltpu.make_async_copy(y_ref.at[pl.ds(0, tm)], o_ref, sem_ref.at[cur]).wait()

    info = info_ref[0]
    wrow = lax.broadcasted_iota(I32, (LANES, tm), 0)
    w_lanes = jnp.where(wrow == 0, lax.bitcast_convert_type(info[4:5], F32),
                        jnp.where(wrow == 1, lax.bitcast_convert_type(info[5:6], F32), 0.0))
    w_rows = w_lanes.T
    y1 = ybuf_ref[cur, 0].reshape(tm, d)
    y2 = ybuf_ref[cur, 1].reshape(tm, d)
    o_ref[...] = x_ref[...] + gate2 * (w_rows[:, 0:1] * y1 + w_rows[:, 1:2] * y2)


def _combine(dest, info, xn, mods, y, layer, geo):
    r, d = xn.shape
    n_steps = r // TM
    return pl.pallas_call(
        functools.partial(_combine_kernel, d_model=d, geo=geo, n_steps=n_steps),
        grid=(n_steps,),
        in_specs=[
            pl.BlockSpec((1, SUBLANES, TM), lambda j: (j, 0, 0)),
            pl.BlockSpec((1, SUBLANES, TM), lambda j: (jnp.minimum(j + 1, n_steps - 1), 0, 0)),
            pl.BlockSpec((1, SUBLANES, TM), lambda j: (j, 0, 0)),
            pl.BlockSpec((TM, d), lambda j: (j, 0)),
            pl.BlockSpec((1,) + mods.shape[1:], lambda j: (layer, 0, 0)),
            pl.BlockSpec(memory_space=pl.ANY),
        ],
        out_specs=pl.BlockSpec((TM, d), lambda j: (j, 0)),
        out_shape=jax.ShapeDtypeStruct((n_steps * TM, d), F32),
        scratch_shapes=[
            pltpu.SMEM((2 * TM,), I32),
            pltpu.VMEM((2, 2, TM // SUBLANES, SUBLANES, d), F32),
            pltpu.SemaphoreType.DMA((3,)),
        ],
        compiler_params=_cparams(("arbitrary",)),
        name="moe_combine",
    )(dest, dest, info, xn, mods, y)


def _pad_heads(w, n_heads, width):
    lead = w.shape[:-1]
    w = w.reshape(lead + (n_heads, width))
    w = jnp.pad(w, [(0, 0)] * len(lead) + [(0, 0), (0, LANES - width)])
    return w.reshape(lead + (n_heads * LANES,))


def _rope_tables(id_rows, nl):
    t = jnp.arange(nl)
    row_id = (t // GRID_W).astype(F32)
    col_id = (t % GRID_W).astype(F32)

    def angles(rot_dim):
        n_freq = rot_dim // 4
        inv_freq = jnp.power(ROPE_THETA, -jnp.arange(n_freq, dtype=F32) / n_freq)
        return jnp.concatenate([row_id[:, None] * inv_freq, col_id[:, None] * inv_freq], axis=-1)

    def with_ctx(tab, fill):
        return jnp.concatenate([tab, jnp.full((id_rows, LANES), fill, F32)], axis=0)

    ag = angles(HEAD_DIM)
    one_g = jnp.ones((nl, LANES - HEAD_DIM), F32)
    cos_g = jnp.concatenate([jnp.cos(ag), jnp.cos(ag), one_g], axis=-1)
    sin_g = jnp.concatenate([-jnp.sin(ag), jnp.sin(ag), 0.0 * one_g], axis=-1)
    am = angles(MLA_ROPE_DIM)
    one_n = jnp.ones((nl, MLA_NOPE_DIM), F32)
    one_t = jnp.ones((nl, LANES - MLA_QK_DIM), F32)
    cos_m = jnp.concatenate([one_n, jnp.cos(am), jnp.cos(am), one_t], axis=-1)
    sin_m = jnp.concatenate([0.0 * one_n, -jnp.sin(am), jnp.sin(am), 0.0 * one_t], axis=-1)
    return with_ctx(cos_g, 1.0), with_ctx(sin_g, 0.0), with_ctx(cos_m, 1.0), with_ctx(sin_m, 0.0)


def _pool_band():
    t = np.arange(TM)[:, None]
    src = np.arange(POOL_EXT)[None, :] - POOL_HALO
    live = np.arange(POOL_EXT)[None, :] < TM + 2 * POOL_HALO
    mats = [((src >= t - w // 2) & (src < t + w // 2) & live) for w in POOL_WINDOWS]
    return jnp.asarray(np.stack(mats).astype(np.float32), dtype=BF16)


def _prep_params(w_in, norm1, norm2, conv_w, w_pool, pool_scale, gqa_q_norm, gqa_k_norm,
                 mla_q_norm, mla_kv_norm, mla_w_uq, mla_w_uk, mla_w_uv, mla_qk_q_norm,
                 mla_qk_k_norm, w_out, w_router, b_router, id_rows, nl):
    dep = w_in.shape[0]
    o = 0
    pieces = {}
    for name, n in (("conv", 3 * CONV_DIM), ("pool", POOL_DIM), ("gq", GQA_HEADS * HEAD_DIM),
                    ("gk", GQA_KV_HEADS * HEAD_DIM), ("gv", GQA_KV_HEADS * HEAD_DIM),
                    ("mq", MLA_Q_RANK), ("mkv", MLA_KV_RANK), ("mkr", MLA_ROPE_DIM)):
        pieces[name] = w_in[..., o:o + n]
        o += n
    mkr = jnp.pad(pieces["mkr"], ((0, 0), (0, 0), (MLA_NOPE_DIM, LANES - MLA_QK_DIM)))
    w_in_p = jnp.concatenate([
        pieces["conv"], pieces["pool"], _pad_heads(pieces["gq"], GQA_HEADS, HEAD_DIM),
        _pad_heads(pieces["gk"], GQA_KV_HEADS, HEAD_DIM),
        pieces["mq"], pieces["mkv"], mkr], axis=-1).astype(BF16)
    wgvt = jnp.swapaxes(_pad_heads(pieces["gv"], GQA_KV_HEADS, HEAD_DIM), 1, 2).astype(BF16)
    wuvt = jnp.swapaxes(_pad_heads(mla_w_uv, MLA_HEADS, MLA_V_DIM), 1, 2).astype(BF16)
    eye = jnp.eye(len(POOL_WINDOWS), dtype=F32)
    wpool = jnp.einsum("gh,dgij->dgihj", eye, w_pool).reshape(dep, POOL_DIM, POOL_DIM).astype(BF16)
    cos_g, sin_g, cos_m, sin_m = _rope_tables(id_rows, nl)
    tri = np.triu(np.ones((TM, TM), np.float32), 1)
    wr_pad = jnp.pad(w_router, ((0, 0), (0, LANES - w_router.shape[1])))
    wr_hi = wr_pad.astype(BF16)
    wr_split = jnp.concatenate([wr_hi, (wr_pad - wr_hi.astype(F32)).astype(BF16)], axis=1)

    def row3(a):
        return a.reshape(dep, 1, a.shape[-1])

    return {
        "w_in": w_in_p,
        "norm1": row3(norm1), "norm2": row3(norm2),
        "gq": row3(_pad_heads(gqa_q_norm * (GQA_SCALE * LOG2E), 1, HEAD_DIM)),
        "gk": row3(_pad_heads(gqa_k_norm, 1, HEAD_DIM)),
        "mqn": row3(mla_q_norm), "mkvn": row3(mla_kv_norm),
        "qkq": row3(_pad_heads(mla_qk_q_norm * (MLA_SCALE * LOG2E), 1, MLA_QK_DIM)),
        "qkk": row3(_pad_heads(mla_qk_k_norm, 1, MLA_QK_DIM)),
        "wuq": _pad_heads(mla_w_uq, MLA_HEADS, MLA_QK_DIM).astype(BF16),
        "wuk": _pad_heads(mla_w_uk, MLA_HEADS, MLA_NOPE_DIM).astype(BF16),
        "wgvt": wgvt, "wuvt": wuvt,
        "cos_g": cos_g, "sin_g": sin_g, "cos_m": cos_m, "sin_m": sin_m,
        "conv_w": jnp.pad(conv_w, ((0, 0), (0, SUBLANES - conv_w.shape[1]), (0, 0))),
        "band": _pool_band(),
        "wpool": wpool, "pscale": row3(pool_scale),
        "w_out": w_out.astype(BF16),
        "wr": wr_split, "br": b_router.reshape(-1, 1),
        "tri": jnp.asarray(tri, dtype=BF16),
    }


def _moe_plan(counts):
    counts = counts.astype(I32)
    padded = ((counts + BM - 1) // BM) * BM
    pad_end = jnp.cumsum(padded)
    pad_start = pad_end - padded
    return pad_start, pad_end, padded


def kernel(x, c, ctx, c_ctx, w_mod, b_mod, norm1, norm2, w_in, conv_w, w_pool, pool_scale,
           gqa_q_norm, gqa_k_norm, mla_q_norm, mla_kv_norm, mla_w_uq, mla_w_uk, mla_w_uv,
           mla_qk_q_norm, mla_qk_k_norm, w_out, w_router, b_router, w_gate, w_up, w_down):
    nb, nl, d = x.shape
    nc = ctx.shape[1]
    depth = w_mod.shape[0]
    assert nc % TM == 0 and nl % TM == 0 and nl % GRID_W == 0 and nb < SUBLANES
    lt, nct = nl // TM, nc // TM
    wide = nl % INPROJ_ROWS == 0 and (nb * nc) % INPROJ_ROWS == 0
    geo = {"nbatch": nb, "nc": nc, "nl": nl, "lt": lt, "nct": nct,
           "n_lat_tiles": nb * lt, "n_tiles": nb * (lt + nct),
           "inproj_rows": INPROJ_ROWS if wide else TM}

    p = _prep_params(w_in, norm1, norm2, conv_w, w_pool, pool_scale, gqa_q_norm, gqa_k_norm,
                     mla_q_norm, mla_kv_norm, mla_w_uq, mla_w_uk, mla_w_uv, mla_qk_q_norm,
                     mla_qk_k_norm, w_out, w_router, b_router, geo["inproj_rows"], nl)
    cvec = jnp.concatenate([c, c_ctx[None, :], jnp.zeros((SUBLANES - nb - 1, d), F32)], axis=0)
    mods = _adaln(cvec, w_mod, b_mod)

    xa = jnp.concatenate([x.reshape(nb * nl, d), ctx.reshape(nb * nc, d)], axis=0)
    for i in range(depth):
        last = i == depth - 1
        mix = _inproj(xa, mods, i, p, geo)
        yg = _attention(mix["qg"], mix["kg"], mix["vgt"], True, geo, False)
        ym = _attention(mix["qm"], mix["km"], mix["vmt"], False, geo, False)
        if last:
            n_tiles, ygc, ymc = geo["n_lat_tiles"], yg, ym
        else:
            n_tiles = geo["n_tiles"]
            ygc = _attention(mix["qg"], mix["kg"], mix["vgt"], True, geo, True)
            ymc = _attention(mix["qm"], mix["km"], mix["vmt"], False, geo, True)
        xn, h2, info, cnt = _outproj(xa, mix, yg, ygc, ym, ymc, mods, i, p, geo, n_tiles)
        n_blocks = -(-2 * n_tiles * TM // BM) + N_EXPERTS
        n_slots = n_blocks * BM
        pad_start, pad_end, padded = _moe_plan(cnt[:, 0])
        n_used = (pad_end[-1:] // BM).astype(I32)
        block_row0 = jnp.arange(n_blocks, dtype=I32) * BM
        block_exp = jnp.minimum(jnp.sum((pad_end[None, :] <= block_row0[:, None]).astype(I32), axis=1),
                                N_EXPERTS - 1)
        dest, xs = _scatter(info, h2, pad_start, jnp.maximum(pad_end - BM, 0),
                            (padded > 0).astype(I32), n_used, n_slots)
        y = _experts(xs, block_exp, n_used, w_gate, w_up, w_down, i)
        xa = _combine(dest, info, xn, mods, y, i, geo)
    return xa.reshape(nb, nl, d)
```

```python
import functools
import math

import numpy as np
import jax
import jax.numpy as jnp
from jax import lax
from jax.experimental import pallas as pl
from jax.experimental.pallas import tpu as pltpu

F32 = jnp.float32
BF16 = jnp.bfloat16
I32 = jnp.int32

GRID_W = 64
CONV_DIM = 256
POOL_DIM = 256
POOL_WINDOWS = (2, 4, 8, 16)
HEAD_DIM = 64
GQA_HEADS = 4
GQA_KV_HEADS = 2
MLA_HEADS = 4
MLA_NOPE_DIM = 64
MLA_ROPE_DIM = 32
MLA_QK_DIM = MLA_NOPE_DIM + MLA_ROPE_DIM
MLA_V_DIM = 64
MLA_Q_RANK = 256
MLA_KV_RANK = 128
N_EXPERTS = 16
EXPERTS_PER_GROUP = 4
ROPE_THETA = 10000.0
NORM_EPS = 1e-6
LOG2E = 1.4426950408889634
GQA_SCALE = HEAD_DIM ** -0.5
MLA_SCALE = MLA_QK_DIM ** -0.5

LANES = 128
SUBLANES = 8
BF16_ROWS = 16
VMEM_LIMIT = 56 * 1024 * 1024

TM = 256
ATT_ROWS = 2048
INPROJ_ROWS = 512
BM = 512
V_ROWS = HEAD_DIM + BF16_ROWS
ATT_PAIR_UNROLL = 4
ROW_COPY_UNROLL = 8
POOL_EXT = 512
POOL_HALO = 8

ZC_B, ZC_C, ZC_U, ZC_P = 0, 256, 512, 768
ZC_GQ = 1024
ZC_GK = ZC_GQ + GQA_HEADS * LANES
ZC_MQ = ZC_GK + GQA_KV_HEADS * LANES
ZC_MKV = ZC_MQ + MLA_Q_RANK
ZC_MKR = ZC_MKV + MLA_KV_RANK
ZC_END = ZC_MKR + LANES

HIGHEST = lax.Precision.HIGHEST


def _cparams(sem, vmem=VMEM_LIMIT):
    return pltpu.CompilerParams(dimension_semantics=sem, vmem_limit_bytes=vmem)


def _sigmoid(v):
    return 1.0 / (1.0 + jnp.exp(-v))


def _adaln_kernel(c_ref, w_ref, b_ref, o_ref):
    c = c_ref[...]
    s = c * _sigmoid(c)
    o_ref[0] = jnp.dot(s, w_ref[0], preferred_element_type=F32, precision=HIGHEST) + b_ref[0]


def _adaln(cvec, w_mod, b_mod):
    depth, d, n6 = w_mod.shape
    tn = 1536 if n6 % 1536 == 0 else n6
    rows = cvec.shape[0]
    return pl.pallas_call(
        _adaln_kernel,
        grid=(depth, n6 // tn),
        in_specs=[
            pl.BlockSpec((rows, d), lambda i, n: (0, 0)),
            pl.BlockSpec((1, d, tn), lambda i, n: (i, 0, n)),
            pl.BlockSpec((1, 1, tn), lambda i, n: (i, 0, n)),
        ],
        out_specs=pl.BlockSpec((1, rows, tn), lambda i, n: (i, 0, n)),
        out_shape=jax.ShapeDtypeStruct((depth, rows, n6), F32),
        compiler_params=_cparams(("arbitrary", "arbitrary")),
        name="adaln",
    )(cvec, w_mod, b_mod.reshape(depth, 1, n6))


def _tile_geometry(j, geo):
    lt, nct, n_lat = geo["lt"], geo["nct"], geo["n_lat_tiles"]
    is_ctx = j >= n_lat
    jc = j - n_lat
    sample = jnp.where(is_ctx, jc // nct, j // lt)
    jt = jnp.where(is_ctx, jc % nct, j % lt)
    return {
        "is_ctx": is_ctx,
        "mod_row": jnp.where(is_ctx, geo["nbatch"], sample),
        "first": jt == 0,
        "last": jt == jnp.where(is_ctx, nct, lt) - 1,
        "pos0": jt * TM,
        "seg_len": jnp.where(is_ctx, geo["nc"], geo["nl"]),
    }


def _norm_rope(slab, gain, cos, sin, n_valid, first_half, half):
    ssq = jnp.dot((slab * slab).astype(BF16), jnp.ones((LANES, LANES), BF16), preferred_element_type=F32)
    y = slab * lax.rsqrt(ssq * (1.0 / n_valid) + NORM_EPS) * gain
    partner = jnp.where(first_half, pltpu.roll(y, LANES - half, 1), pltpu.roll(y, half, 1))
    return y * cos + partner * sin


def _inproj_kernel(x_ref, mods_ref, n1_ref, cg_ref, sg_ref, cm_ref, sm_ref, win_ref,
                   gq_ref, gk_ref, mqn_ref, mkvn_ref, qkq_ref, qkk_ref, wuq_ref, wuk_ref,
                   wgvt_ref, wuvt_ref,
                   cb_ref, cv_ref, zp_ref, qg_ref, kg_ref, vgt_ref, qm_ref, km_ref, vmt_ref,
                   *, d_model, nbatch, n_lat_steps, lat_steps_per_sample):
    d = d_model
    step = pl.program_id(0)
    row = jnp.where(step >= n_lat_steps, nbatch, step // lat_steps_per_sample)
    shift = mods_ref[0, pl.ds(row, 1), 0:d]
    scale = mods_ref[0, pl.ds(row, 1), d:2 * d]
    x = x_ref[...]
    ms = jnp.mean(x * x, axis=-1, keepdims=True)
    h = ((x * lax.rsqrt(ms + NORM_EPS)) * (n1_ref[0] * (1.0 + scale)) + shift).astype(BF16)
    nt_dims = (((1,), (1,)), ((), ()))

    def proj(c0, width):
        return jnp.dot(h, win_ref[0, :, c0:c0 + width], preferred_element_type=F32)

    def store_transposed(vt_ref, vt):
        srow = lax.broadcasted_iota(I32, vt.shape, 0)
        vt = jnp.where(jnp.bitwise_and(srow, LANES - 1) == HEAD_DIM, 1.0, vt).astype(BF16)
        for c in range(vt_ref.shape[0]):
            vt_ref[c] = vt[:, c * TM:(c + 1) * TM]

    tm = x.shape[0]
    lane = lax.broadcasted_iota(I32, (tm, LANES), 1)

    cm, sm = cm_ref[...], sm_ref[...]
    m_first = lane < MLA_NOPE_DIM + MLA_ROPE_DIM // 2
    zq = proj(ZC_MQ, MLA_Q_RANK)
    cq = zq * lax.rsqrt(jnp.mean(zq * zq, axis=-1, keepdims=True) + NORM_EPS) * mqn_ref[0]
    qpre = jnp.dot(cq.astype(BF16), wuq_ref[0], preferred_element_type=F32)
    zk = proj(ZC_MKV, MLA_KV_RANK + LANES)
    zkv, zkr = zk[:, 0:MLA_KV_RANK], zk[:, MLA_KV_RANK:]
    ckv = zkv * lax.rsqrt(jnp.mean(zkv * zkv, axis=-1, keepdims=True) + NORM_EPS) * mkvn_ref[0]
    ckv = ckv.astype(BF16)
    kvp = jnp.dot(ckv, wuk_ref[0], preferred_element_type=F32)
    store_transposed(vmt_ref, lax.dot_general(wuvt_ref[0], ckv, nt_dims, preferred_element_type=F32))
    for hd in range(MLA_HEADS):
        sl = slice(hd * LANES, (hd + 1) * LANES)
        qm_ref[:, sl] = _norm_rope(qpre[:, sl], qkq_ref[0], cm, sm, MLA_QK_DIM,
                                   m_first, MLA_ROPE_DIM // 2).astype(BF16)
        km_ref[:, sl] = _norm_rope(kvp[:, sl] + zkr, qkk_ref[0], cm, sm, MLA_QK_DIM,
                                   m_first, MLA_ROPE_DIM // 2).astype(BF16)

    cg, sg = cg_ref[...], sg_ref[...]
    g_first = lane < HEAD_DIM // 2
    zg = proj(ZC_GQ, (GQA_HEADS + GQA_KV_HEADS) * LANES)
    for hd in range(GQA_HEADS):
        qg_ref[:, hd * LANES:(hd + 1) * LANES] = _norm_rope(
            zg[:, hd * LANES:(hd + 1) * LANES], gq_ref[0], cg, sg, HEAD_DIM, g_first,
            HEAD_DIM // 2).astype(BF16)
    for hd in range(GQA_KV_HEADS):
        slab = zg[:, (GQA_HEADS + hd) * LANES:(GQA_HEADS + hd + 1) * LANES]
        kg_ref[:, hd * LANES:(hd + 1) * LANES] = _norm_rope(
            slab, gk_ref[0], cg, sg, HEAD_DIM, g_first, HEAD_DIM // 2).astype(BF16)
    store_transposed(vgt_ref, lax.dot_general(wgvt_ref[0], h, nt_dims, preferred_element_type=F32))

    zc = proj(ZC_B, 3 * CONV_DIM + POOL_DIM)
    cb_ref[...] = zc[:, ZC_B:ZC_B + CONV_DIM].astype(BF16)
    cv_ref[...] = (zc[:, ZC_C:ZC_C + CONV_DIM] * zc[:, ZC_U:ZC_U + CONV_DIM]).astype(BF16)
    zp_ref[...] = zc[:, ZC_P:ZC_P + POOL_DIM]


def _inproj(xa, mods, layer, p, geo):
    r, d = xa.shape
    tmi = geo["inproj_rows"]
    n_lat_steps = geo["nbatch"] * geo["nl"] // tmi
    lat_steps_per_sample = geo["nl"] // tmi
    rowmap = lambda j: (j, 0)
    posmap = lambda j: (jnp.where(j >= n_lat_steps, lat_steps_per_sample, j % lat_steps_per_sample), 0)
    lay3 = lambda j: (layer, 0, 0)

    def full3(a):
        return pl.BlockSpec((1,) + a.shape[1:], lay3)

    outs = [("cb", CONV_DIM, BF16, False), ("cv", CONV_DIM, BF16, False), ("zp", POOL_DIM, F32, False),
            ("qg", GQA_HEADS * LANES, BF16, False), ("kg", GQA_KV_HEADS * LANES, BF16, False),
            ("vgt", GQA_KV_HEADS * LANES, BF16, True), ("qm", MLA_HEADS * LANES, BF16, False),
            ("km", MLA_HEADS * LANES, BF16, False), ("vmt", MLA_HEADS * LANES, BF16, True)]

    def out_spec(w, transposed):
        if transposed:
            return pl.BlockSpec((tmi // TM, w, TM), lambda j: (j, 0, 0))
        return pl.BlockSpec((tmi, w), rowmap)

    def out_shape(w, dt, transposed):
        return jax.ShapeDtypeStruct((r // TM, w, TM) if transposed else (r, w), dt)

    res = pl.pallas_call(
        functools.partial(_inproj_kernel, d_model=d, nbatch=geo["nbatch"], n_lat_steps=n_lat_steps,
                          lat_steps_per_sample=lat_steps_per_sample),
        grid=(r // tmi,),
        in_specs=[
            pl.BlockSpec((tmi, d), rowmap),
            full3(mods), full3(p["norm1"]),
            pl.BlockSpec((tmi, LANES), posmap), pl.BlockSpec((tmi, LANES), posmap),
            pl.BlockSpec((tmi, LANES), posmap), pl.BlockSpec((tmi, LANES), posmap),
            full3(p["w_in"]), full3(p["gq"]), full3(p["gk"]), full3(p["mqn"]), full3(p["mkvn"]),
            full3(p["qkq"]), full3(p["qkk"]), full3(p["wuq"]), full3(p["wuk"]),
            full3(p["wgvt"]), full3(p["wuvt"]),
        ],
        out_specs=[out_spec(w, t) for _, w, _, t in outs],
        out_shape=[out_shape(w, dt, t) for _, w, dt, t in outs],
        compiler_params=_cparams(("arbitrary",)),
        name="inproj",
    )(xa, mods, p["norm1"], p["cos_g"], p["sin_g"], p["cos_m"], p["sin_m"], p["w_in"],
      p["gq"], p["gk"], p["mqn"], p["mkvn"], p["qkq"], p["qkk"], p["wuq"], p["wuk"],
      p["wgvt"], p["wuvt"])
    return {name: a for (name, _, _, _), a in zip(outs, res)}


def _attn_kernel(q_ref, kl_ref, kc_ref, vtl_ref, vtc_ref, o_ref, q_st, s_buf, p_buf, a_buf, m_ref, acc_ref,
                 *, shared_kv, n_lat, n_ctx, tk):
    tq = q_ref.shape[0]
    n_steps = n_lat + n_ctx
    if shared_kv:
        q_st[0] = jnp.concatenate([q_ref[:, 0:LANES], q_ref[:, LANES:2 * LANES]], axis=0)
        cols = [0]
    else:
        q_st[0] = q_ref[:, 0:LANES]
        q_st[1] = q_ref[:, LANES:2 * LANES]
        cols = [0, LANES]
    for si, col in enumerate(cols):
        q_s, m_s, acc_s = q_st.at[si], m_ref.at[si], acc_ref.at[si]
        m_s[...] = jnp.full(m_s.shape, -1e30, F32)
        acc_s[...] = jnp.zeros(acc_s.shape, F32)
        p_buf[1] = jnp.zeros(p_buf.shape[1:], BF16)
        a_buf[1] = jnp.ones(a_buf.shape[1:], F32)

        def pick(i, lat_fn, ctx_fn):
            ctx = ctx_fn(jnp.clip(i - n_lat, 0, n_ctx - 1))
            if n_lat == 0:
                return ctx
            return jnp.where(i >= n_lat, ctx, lat_fn(jnp.clip(i, 0, n_lat - 1)))

        def scores(i, slot, q_s=q_s, col=col):
            k = pick(i, lambda t: kl_ref[pl.ds(pl.multiple_of(t * tk, tk), tk), col:col + LANES],
                     lambda t: kc_ref[pl.ds(pl.multiple_of(t * tk, tk), tk), col:col + LANES])
            s_buf[slot] = lax.dot_general(k, q_s[...], (((1,), (1,)), ((), ())),
                                          preferred_element_type=F32)

        def softmax(i, slot, m_s=m_s):
            s = s_buf[slot]
            m_old = m_s[...]
            m_new = jnp.maximum(m_old, jnp.max(s, axis=0, keepdims=True))
            a_buf[slot] = jnp.exp2(m_old - m_new)
            m_eff = m_new[0:1] + jnp.where(i < n_steps, 0.0, 1e9)
            p_buf[slot] = jnp.exp2(s - m_eff).astype(BF16)
            m_s[...] = m_new

        def accumulate(i, slot, acc_s=acc_s, col=col):
            vt = pick(i, lambda t: vtl_ref[t, col:col + V_ROWS, :],
                      lambda t: vtc_ref[t, col:col + V_ROWS, :])
            acc_s[...] = a_buf[slot][0:1] * acc_s[...] + jnp.dot(vt, p_buf[slot],
                                                                 preferred_element_type=F32)

        scores(0, 0)

        def pair(t, carry):
            i = 2 * t
            scores(i + 1, 1)
            softmax(i, 0)
            accumulate(i - 1, 1)
            scores(i + 2, 0)
            softmax(i + 1, 1)
            accumulate(i, 0)
            return carry

        lax.fori_loop(0, (n_steps + 2) // 2, pair, 0, unroll=ATT_PAIR_UNROLL)
    def finish(acc_t):
        o_t = acc_t * (1.0 / acc_t[HEAD_DIM:HEAD_DIM + 1])
        pad = jnp.zeros((LANES - V_ROWS, o_t.shape[1]), F32)
        return jnp.concatenate([o_t, pad], axis=0).T

    if shared_kv:
        o_both = finish(acc_ref[0])
        o0, o1 = o_both[0:tq], o_both[tq:2 * tq]
    else:
        o0, o1 = finish(acc_ref[0]), finish(acc_ref[1])
    lane = lax.broadcasted_iota(I32, (tq, LANES), 1)
    o_ref[...] = jnp.where(lane < HEAD_DIM, o0, pltpu.roll(o1, HEAD_DIM, 1)).astype(BF16)


def _attention(q, k, vt, shared_kv, geo, ctx_queries):
    nb, nc, nl = geo["nbatch"], geo["nc"], geo["nl"]
    kw = LANES if shared_kv else 2 * LANES
    tk = TM
    ctx_blk0 = nb * nl // nc
    ctx_k = pl.BlockSpec((nc, kw), lambda b, g, t: (ctx_blk0 + b, g))
    ctx_vt = pl.BlockSpec((nc // tk, kw, tk), lambda b, g, t: (ctx_blk0 + b, g, 0))
    if ctx_queries:
        tq, q_per, q_blk0, n_lat = nc, 1, ctx_blk0, 0
        lat_k, lat_vt = ctx_k, ctx_vt
    else:
        tq = min(ATT_ROWS, nl) // (2 if shared_kv else 1)
        q_per, q_blk0, n_lat = nl // tq, 0, nl // tk
        lat_k = pl.BlockSpec((nl, kw), lambda b, g, t: (b, g))
        lat_vt = pl.BlockSpec((nl // tk, kw, tk), lambda b, g, t: (b, g, 0))
    n_streams, rows = (1, 2 * tq) if shared_kv else (2, tq)
    return pl.pallas_call(
        functools.partial(_attn_kernel, shared_kv=shared_kv, n_lat=n_lat, n_ctx=nc // tk, tk=tk),
        grid=(nb, 2, q_per),
        in_specs=[
            pl.BlockSpec((tq, 2 * LANES), lambda b, g, t: (q_blk0 + b * q_per + t, g)),
            lat_k, ctx_k, lat_vt, ctx_vt,
        ],
        out_specs=pl.BlockSpec((tq, LANES), lambda b, g, t: (b * q_per + t, g)),
        out_shape=jax.ShapeDtypeStruct((nb * q_per * tq, 2 * LANES), BF16),
        scratch_shapes=[pltpu.VMEM((n_streams, rows, LANES), BF16),
                        pltpu.VMEM((2, tk, rows), F32),
                        pltpu.VMEM((2, tk, rows), BF16),
                        pltpu.VMEM((2, SUBLANES, rows), F32),
                        pltpu.VMEM((n_streams, SUBLANES, rows), F32),
                        pltpu.VMEM((n_streams, V_ROWS, rows), F32)],
        compiler_params=_cparams(("arbitrary", "arbitrary", "arbitrary")),
        name=("attn_gqa" if shared_kv else "attn_mla") + ("_ctx" if ctx_queries else ""),
    )(q, k, k, vt, vt)


def _top2_sum(a, b, c, d):
    hi_ab, lo_ab = jnp.maximum(a, b), jnp.minimum(a, b)
    hi_cd, lo_cd = jnp.maximum(c, d), jnp.minimum(c, d)
    first = jnp.maximum(hi_ab, hi_cd)
    second = jnp.maximum(jnp.minimum(hi_ab, hi_cd), jnp.maximum(lo_ab, lo_cd))
    return first + second


def _outproj_kernel(x_ref, cb_ref, cv_ref, cvp_ref, cvn_ref, zp_ref, zpp_ref, zpn_ref,
                    ygl_ref, ygc_ref, yml_ref, ymc_ref,
                    mods_ref, convw_ref, band_ref, wpool_ref, pscale_ref,
                    wout_ref, n2_ref, wr_ref, br_ref, tri_ref,
                    xo_ref, h2_ref, info_ref, cnt_ref,
                    *, d_model, geo):
    tile = pl.program_id(0)
    d = d_model
    tm = x_ref.shape[0]
    tg = _tile_geometry(tile, geo)
    is_ctx = tg["is_ctx"]
    keep_prev = jnp.where(tg["first"], 0.0, 1.0)
    keep_next = jnp.where(tg["last"], 0.0, 1.0)
    row = tg["mod_row"]
    gate1 = mods_ref[0, pl.ds(row, 1), 2 * d:3 * d]
    shift2 = mods_ref[0, pl.ds(row, 1), 3 * d:4 * d]
    scale2 = mods_ref[0, pl.ds(row, 1), 4 * d:5 * d]

    v = cv_ref[...].astype(F32)
    prev_row = cvp_ref[...].astype(F32)[BF16_ROWS - 1:BF16_ROWS] * keep_prev
    next_row = cvn_ref[...].astype(F32)[0:1] * keep_next
    rid = lax.broadcasted_iota(I32, (tm, CONV_DIM), 0)
    vm1 = jnp.where(rid == 0, prev_row, pltpu.roll(v, 1, 0))
    vp1 = jnp.where(rid == tm - 1, next_row, pltpu.roll(v, tm - 1, 0))
    cw = convw_ref[0]
    y_conv = cb_ref[...].astype(F32) * (vm1 * cw[0:1] + v * cw[1:2] + vp1 * cw[2:3])

    zp = zp_ref[...]
    ext = jnp.concatenate(
        [zpp_ref[...] * keep_prev, zp, zpn_ref[...] * keep_next,
         jnp.zeros((POOL_EXT - tm - 2 * POOL_HALO, POOL_DIM), F32)], axis=0).astype(BF16)
    ext_a, ext_b = ext[:, 0:LANES], ext[:, LANES:2 * LANES]
    lane = lax.broadcasted_iota(I32, (tm, LANES), 1)
    low = lane < POOL_DIM // 4
    sum_a = jnp.where(low, jnp.dot(band_ref[0], ext_a, preferred_element_type=F32),
                      jnp.dot(band_ref[1], ext_a, preferred_element_type=F32))
    sum_b = jnp.where(low, jnp.dot(band_ref[2], ext_b, preferred_element_type=F32),
                      jnp.dot(band_ref[3], ext_b, preferred_element_type=F32))
    sums = jnp.concatenate([sum_a, sum_b], axis=1)
    lane_p = lax.broadcasted_iota(I32, (tm, POOL_DIM), 1)
    half_w = jnp.left_shift(1, jnp.right_shift(lane_p, int(math.log2(POOL_DIM // 4))))
    pos = tg["pos0"] + rid
    cnt = (jnp.minimum(pos + half_w, tg["seg_len"]) - jnp.maximum(pos - half_w, 0)).astype(F32)
    dlt = sums / cnt - zp
    y_pool = jnp.dot(dlt.astype(BF16), wpool_ref[0], preferred_element_type=F32) * pscale_ref[0]

    y_gqa = jnp.where(is_ctx, ygc_ref[...], ygl_ref[...])
    y_mla = jnp.where(is_ctx, ymc_ref[...], yml_ref[...])
    ycat = jnp.concatenate([y_conv.astype(BF16), y_pool.astype(BF16), y_gqa, y_mla], axis=1)
    y = jnp.dot(ycat, wout_ref[0], preferred_element_type=F32)
    xn = x_ref[...] + gate1 * y
    xo_ref[...] = xn
    ms = jnp.mean(xn * xn, axis=-1, keepdims=True)
    h2 = (xn * lax.rsqrt(ms + NORM_EPS)) * (n2_ref[0] * (1.0 + scale2)) + shift2
    h2_ref[...] = h2

    h_hi = h2.astype(BF16)
    h_lo = (h2 - h_hi.astype(F32)).astype(BF16)
    wr = wr_ref[...]
    part = jnp.dot(h_hi, wr, preferred_element_type=F32)
    small = part[:, LANES:] + jnp.dot(h_lo, wr[:, 0:LANES], preferred_element_type=F32)
    logits = (part[:, 0:LANES] + small).T[0:N_EXPERTS]
    scores = _sigmoid(logits)
    sel = scores + br_ref[...]
    epg = EXPERTS_PER_GROUP
    n_groups = N_EXPERTS // epg
    srow = [sel[e:e + 1] for e in range(N_EXPERTS)]
    crow = [scores[e:e + 1] for e in range(N_EXPERTS)]
    gscore = [_top2_sum(*srow[g * epg:(g + 1) * epg]) for g in range(n_groups)]
    gbest = jnp.zeros_like(gscore[0]).astype(I32)
    best = gscore[0]
    for g in range(1, n_groups):
        upd = gscore[g] > best
        gbest = jnp.where(upd, g, gbest)
        best = jnp.where(upd, gscore[g], best)

    def pick(rows_, j):
        out = rows_[(n_groups - 1) * epg + j]
        for g in range(n_groups - 2, -1, -1):
            out = jnp.where(gbest == g, rows_[g * epg + j], out)
        return out

    sv = [pick(srow, j) for j in range(epg)]
    cv_ = [pick(crow, j) for j in range(epg)]
    i1 = jnp.zeros_like(gbest)
    b1 = sv[0]
    for j in range(1, epg):
        upd = sv[j] > b1
        i1 = jnp.where(upd, j, i1)
        b1 = jnp.where(upd, sv[j], b1)
    i2 = jnp.zeros_like(gbest)
    b2 = jnp.full_like(b1, -jnp.inf)
    for j in range(epg):
        upd = jnp.logical_and(i1 != j, sv[j] > b2)
        i2 = jnp.where(upd, j, i2)
        b2 = jnp.where(upd, sv[j], b2)
    s1 = cv_[epg - 1]
    s2 = cv_[epg - 1]
    for j in range(epg - 2, -1, -1):
        s1 = jnp.where(i1 == j, cv_[j], s1)
        s2 = jnp.where(i2 == j, cv_[j], s2)
    inv = 1.0 / (s1 + s2)
    e1 = gbest * epg + i1
    e2 = gbest * epg + i2

    @pl.when(tile == 0)
    def _():
        cnt_ref[...] = jnp.zeros_like(cnt_ref)

    erow = lax.broadcasted_iota(I32, (N_EXPERTS, tm), 0)
    hit1 = erow == e1
    hit2 = erow == e2
    onehot = jnp.where(hit1, 1.0, 0.0) + jnp.where(hit2, 1.0, 0.0)
    before = jnp.dot(onehot.astype(BF16), tri_ref[...], preferred_element_type=F32)
    tot = cnt_ref[:, 0:1] + before
    rank1 = jnp.sum(jnp.where(hit1, tot, 0.0), axis=0, keepdims=True).astype(I32)
    rank2 = jnp.sum(jnp.where(hit2, tot, 0.0), axis=0, keepdims=True).astype(I32)
    cnt_ref[...] = cnt_ref[...] + jnp.sum(onehot, axis=1, keepdims=True)

    w1 = lax.bitcast_convert_type(s1 * inv, I32)
    w2 = lax.bitcast_convert_type(s2 * inv, I32)
    irow = lax.broadcasted_iota(I32, (SUBLANES, tm), 0)
    info = jnp.where(irow == 0, e1, jnp.where(irow == 1, e2, jnp.where(
        irow == 2, rank1, jnp.where(irow == 3, rank2, jnp.where(
            irow == 4, w1, jnp.where(irow == 5, w2, 0))))))
    info_ref[0] = info


def _outproj(xa, mix, yg, ygc, ym, ymc, mods, layer, p, geo, n_tiles):
    d = xa.shape[1]
    nt_all, n_lat = geo["n_tiles"], geo["n_lat_tiles"]
    r = n_tiles * TM
    nt = n_tiles
    rowmap = lambda j: (j, 0)
    latmap = lambda j: (jnp.minimum(j, n_lat - 1), 0)
    ctxmap = lambda j: (jnp.clip(j - n_lat, 0, ygc.shape[0] // TM - 1), 0)
    lay3 = lambda j: (layer, 0, 0)
    c0 = lambda j: (0, 0)
    c3 = lambda j: (0, 0, 0)
    bf_blocks = TM // BF16_ROWS
    f_blocks = TM // SUBLANES

    def full3(a):
        return pl.BlockSpec((1,) + a.shape[1:], lay3)

    return pl.pallas_call(
        functools.partial(_outproj_kernel, d_model=d, geo=geo),
        grid=(n_tiles,),
        in_specs=[
            pl.BlockSpec((TM, d), rowmap),
            pl.BlockSpec((TM, CONV_DIM), rowmap),
            pl.BlockSpec((TM, CONV_DIM), rowmap),
            pl.BlockSpec((BF16_ROWS, CONV_DIM), lambda j: (jnp.maximum(j * bf_blocks - 1, 0), 0)),
            pl.BlockSpec((BF16_ROWS, CONV_DIM),
                         lambda j: (jnp.minimum((j + 1) * bf_blocks, nt_all * bf_blocks - 1), 0)),
            pl.BlockSpec((TM, POOL_DIM), rowmap),
            pl.BlockSpec((SUBLANES, POOL_DIM), lambda j: (jnp.maximum(j * f_blocks - 1, 0), 0)),
            pl.BlockSpec((SUBLANES, POOL_DIM),
                         lambda j: (jnp.minimum((j + 1) * f_blocks, nt_all * f_blocks - 1), 0)),
            pl.BlockSpec((TM, 2 * LANES), latmap),
            pl.BlockSpec((TM, 2 * LANES), ctxmap),
            pl.BlockSpec((TM, 2 * LANES), latmap),
            pl.BlockSpec((TM, 2 * LANES), ctxmap),
            full3(mods), full3(p["conv_w"]),
            pl.BlockSpec(p["band"].shape, c3),
            full3(p["wpool"]), full3(p["pscale"]), full3(p["w_out"]), full3(p["norm2"]),
            pl.BlockSpec(p["wr"].shape, c0), pl.BlockSpec(p["br"].shape, c0),
            pl.BlockSpec(p["tri"].shape, c0),
        ],
        out_specs=[
            pl.BlockSpec((TM, d), rowmap),
            pl.BlockSpec((TM, d), rowmap),
            pl.BlockSpec((1, SUBLANES, TM), lambda j: (j, 0, 0)),
            pl.BlockSpec((N_EXPERTS, LANES), c0),
        ],
        out_shape=[
            jax.ShapeDtypeStruct((r, d), F32),
            jax.ShapeDtypeStruct((r, d), F32),
            jax.ShapeDtypeStruct((nt, SUBLANES, TM), I32),
            jax.ShapeDtypeStruct((N_EXPERTS, LANES), F32),
        ],
        compiler_params=_cparams(("arbitrary",)),
        name="outproj",
    )(xa, mix["cb"], mix["cv"], mix["cv"], mix["cv"], mix["zp"], mix["zp"], mix["zp"],
      yg, ygc, ym, ymc, mods, p["conv_w"], p["band"], p["wpool"], p["pscale"], p["w_out"],
      p["norm2"], p["wr"], p["br"], p["tri"])


def _load_slots(block_ref, idx_ref, sem):
    n = block_ref.shape[2]
    copies = [pltpu.make_async_copy(block_ref.at[0, k], idx_ref.at[pl.ds(k * n, n)], sem) for k in range(2)]
    for cp in copies:
        cp.start()
    for cp in copies:
        cp.wait()


def _row_copies(idx_ref, n_rows, make):
    def body(t, carry):
        for u in range(SUBLANES):
            r = t * SUBLANES + u
            make(0, t, u, idx_ref[r]).start()
            make(1, t, u, idx_ref[n_rows + r]).start()
        return carry
    lax.fori_loop(0, n_rows // SUBLANES, body, 0)


def _scatter_kernel(ps_ref, zs_ref, has_ref, nu_ref, info_ref, h2_ref, dest_ref, xs_ref,
                    dsm_ref, zbuf_ref, stage_ref, sem_ref, *, first_spare, n_blocks, n_tiles):
    j = pl.program_id(0)
    tm = h2_ref.shape[0]

    @pl.when(j == 0)
    def _():
        zbuf_ref[...] = jnp.zeros_like(zbuf_ref)

        def zero_copy(start):
            return pltpu.make_async_copy(
                zbuf_ref, xs_ref.at[pl.ds(pl.multiple_of(start, BM), BM)], sem_ref.at[2])

        for act in ("start", "wait"):
            for e in range(N_EXPERTS):
                @pl.when(has_ref[e] > 0)
                def _():
                    getattr(zero_copy(zs_ref[e]), act)()
            for jb in range(first_spare, n_blocks):
                @pl.when(jb >= nu_ref[0])
                def _():
                    getattr(zero_copy(jb * BM), act)()

    info = info_ref[0]
    e1, e2 = info[0:1], info[1:2]
    d1, d2 = info[2:3], info[3:4]
    for e in range(N_EXPERTS):
        d1 = d1 + jnp.where(e1 == e, ps_ref[e], 0)
        d2 = d2 + jnp.where(e2 == e, ps_ref[e], 0)
    irow = lax.broadcasted_iota(I32, (SUBLANES, tm), 0)
    dest_ref[0] = jnp.where(irow == 0, d1, jnp.where(irow == 1, d2, 0))
    _load_slots(dest_ref, dsm_ref, sem_ref.at[3])

    par = j % 2
    stage_ref[par] = h2_ref[...].reshape(stage_ref.shape[1:])

    def row_copy(_, t, u, slot):
        return pltpu.make_async_copy(stage_ref.at[par, t, pl.ds(u, 1)], xs_ref.at[pl.ds(slot, 1)],
                                     sem_ref.at[par])

    _row_copies(dsm_ref, tm, row_copy)

    def retire(which):
        for _ in range(2):
            pltpu.make_async_copy(h2_ref, xs_ref.at[pl.ds(0, tm)], sem_ref.at[which]).wait()

    @pl.when(j > 0)
    def _():
        retire(1 - par)

    @pl.when(j == n_tiles - 1)
    def _():
        retire(par)


def _scatter(info, h2, pad_start, zero_start, has_rows, n_used, n_slots):
    r, d = h2.shape
    nt = r // TM
    return pl.pallas_call(
        functools.partial(_scatter_kernel, first_spare=-(-2 * r // BM), n_blocks=n_slots // BM,
                          n_tiles=nt),
        grid_spec=pltpu.PrefetchScalarGridSpec(
            num_scalar_prefetch=4,
            grid=(nt,),
            in_specs=[
                pl.BlockSpec((1, SUBLANES, TM), lambda j, *_: (j, 0, 0)),
                pl.BlockSpec((TM, d), lambda j, *_: (j, 0)),
            ],
            out_specs=[
                pl.BlockSpec((1, SUBLANES, TM), lambda j, *_: (j, 0, 0)),
                pl.BlockSpec(memory_space=pl.ANY),
            ],
            scratch_shapes=[
                pltpu.SMEM((2 * TM,), I32),
                pltpu.VMEM((BM, d), F32),
                pltpu.VMEM((2, TM // SUBLANES, SUBLANES, d), F32),
                pltpu.SemaphoreType.DMA((4,)),
            ],
        ),
        out_shape=[
            jax.ShapeDtypeStruct((nt, SUBLANES, TM), I32),
            jax.ShapeDtypeStruct((n_slots, d), F32),
        ],
        compiler_params=_cparams(("arbitrary",)),
        name="moe_scatter",
    )(pad_start, zero_start, has_rows, n_used, info, h2)


def _expert_kernel(be_ref, nu_ref, xs_ref, wg_ref, wu_ref, wd_ref, y_ref, wgb, wub, wdb):
    j = pl.program_id(0)

    @pl.when(j < nu_ref[0])
    def _():
        e = be_ref[j]
        prev = be_ref[jnp.maximum(j - 1, 0)]

        @pl.when(jnp.logical_or(j == 0, e != prev))
        def _():
            wgb[...] = wg_ref[0].astype(BF16)
            wub[...] = wu_ref[0].astype(BF16)
            wdb[...] = wd_ref[0].astype(BF16)

        x = xs_ref[...].astype(BF16)
        a = jnp.dot(x, wgb[...], preferred_element_type=F32)
        u = jnp.dot(x, wub[...], preferred_element_type=F32)
        hmid = (a * _sigmoid(a) * u).astype(BF16)
        y_ref[...] = jnp.dot(hmid, wdb[...], preferred_element_type=F32)

    @pl.when(j >= nu_ref[0])
    def _():
        y_ref[...] = jnp.zeros_like(y_ref)


def _experts(xs, block_exp, n_used, w_gate, w_up, w_down, layer):
    n_slots, d = xs.shape
    de = w_gate.shape[-1]
    nbm = n_slots // BM
    n_exp = w_gate.shape[1]

    def blk(j, be, nu):
        return (jnp.minimum(j, nu[0] - 1), 0)

    def wmap(j, be, nu):
        return (layer * n_exp + be[jnp.minimum(j, nu[0] - 1)], 0, 0)

    wg = w_gate.reshape((-1,) + w_gate.shape[2:])
    wu = w_up.reshape((-1,) + w_up.shape[2:])
    wd = w_down.reshape((-1,) + w_down.shape[2:])
    return pl.pallas_call(
        _expert_kernel,
        grid_spec=pltpu.PrefetchScalarGridSpec(
            num_scalar_prefetch=2,
            grid=(nbm,),
            in_specs=[
                pl.BlockSpec((BM, d), blk),
                pl.BlockSpec((1, d, de), wmap),
                pl.BlockSpec((1, d, de), wmap),
                pl.BlockSpec((1, de, d), wmap),
            ],
            out_specs=pl.BlockSpec((BM, d), lambda j, be, nu: (j, 0)),
            scratch_shapes=[
                pltpu.VMEM((d, de), BF16),
                pltpu.VMEM((d, de), BF16),
                pltpu.VMEM((de, d), BF16),
            ],
        ),
        out_shape=jax.ShapeDtypeStruct((n_slots, d), F32),
        compiler_params=_cparams(("arbitrary",)),
        name="moe_experts",
    )(block_exp, n_used, xs, wg, wu, wd)


def _combine_kernel(dest_ref, dest_next_ref, info_ref, x_ref, mods_ref, y_ref, o_ref,
                    dsm_ref, ybuf_ref, sem_ref, *, d_model, geo, n_steps):
    j = pl.program_id(0)
    d = d_model
    tm = x_ref.shape[0]
    row = _tile_geometry(j, geo)["mod_row"]
    gate2 = mods_ref[0, pl.ds(row, 1), 5 * d:6 * d]
    cur = j % 2

    def gather(idx_block_ref, buf):
        _load_slots(idx_block_ref, dsm_ref, sem_ref.at[2])

        def row_copy(k, t, u, slot):
            return pltpu.make_async_copy(y_ref.at[pl.ds(slot, 1)],
                                         ybuf_ref.at[buf, k, t, pl.ds(u, 1)], sem_ref.at[buf])

        _row_copies(dsm_ref, tm, row_copy)

    @pl.when(j == 0)
    def _():
        gather(dest_ref, 0)

    @pl.when(j + 1 < n_steps)
    def _():
        gather(dest_next_ref, 1 - cur)

    for k in range(2):
        pltpu.make_async_copy(y_ref.at[pl.ds(0, tm)], o_ref, sem_ref.at[cur]).wait()

    info = info_ref[0]
    wrow = lax.broadcasted_iota(I32, (LANES, tm), 0)
    w_lanes = jnp.where(wrow == 0, lax.bitcast_convert_type(info[4:5], F32),
                        jnp.where(wrow == 1, lax.bitcast_convert_type(info[5:6], F32), 0.0))
    w_rows = w_lanes.T
    y1 = ybuf_ref[cur, 0].reshape(tm, d)
    y2 = ybuf_ref[cur, 1].reshape(tm, d)
    o_ref[...] = x_ref[...] + gate2 * (w_rows[:, 0:1] * y1 + w_rows[:, 1:2] * y2)


def _combine(dest, info, xn, mods, y, layer, geo):
    r, d = xn.shape
    n_steps = r // TM
    return pl.pallas_call(
        functools.partial(_combine_kernel, d_model=d, geo=geo, n_steps=n_steps),
        grid=(n_steps,),
        in_specs=[
            pl.BlockSpec((1, SUBLANES, TM), lambda j: (j, 0, 0)),
            pl.BlockSpec((1, SUBLANES, TM), lambda j: (jnp.minimum(j + 1, n_steps - 1), 0, 0)),
            pl.BlockSpec((1, SUBLANES, TM), lambda j: (j, 0, 0)),
            pl.BlockSpec((TM, d), lambda j: (j, 0)),
            pl.BlockSpec((1,) + mods.shape[1:], lambda j: (layer, 0, 0)),
            pl.BlockSpec(memory_space=pl.ANY),
        ],
        out_specs=pl.BlockSpec((TM, d), lambda j: (j, 0)),
        out_shape=jax.ShapeDtypeStruct((n_steps * TM, d), F32),
        scratch_shapes=[
            pltpu.SMEM((2 * TM,), I32),
            pltpu.VMEM((2, 2, TM // SUBLANES, SUBLANES, d), F32),
            pltpu.SemaphoreType.DMA((3,)),
        ],
        compiler_params=_cparams(("arbitrary",)),
        name="moe_combine",
    )(dest, dest, info, xn, mods, y)


def _pad_heads(w, n_heads, width):
    lead = w.shape[:-1]
    w = w.reshape(lead + (n_heads, width))
    w = jnp.pad(w, [(0, 0)] * len(lead) + [(0, 0), (0, LANES - width)])
    return w.reshape(lead + (n_heads * LANES,))


def _rope_tables(id_rows, nl):
    t = jnp.arange(nl)
    row_id = (t // GRID_W).astype(F32)
    col_id = (t % GRID_W).astype(F32)

    def angles(rot_dim):
        n_freq = rot_dim // 4
        inv_freq = jnp.power(ROPE_THETA, -jnp.arange(n_freq, dtype=F32) / n_freq)
        return jnp.concatenate([row_id[:, None] * inv_freq, col_id[:, None] * inv_freq], axis=-1)

    def with_ctx(tab, fill):
        return jnp.concatenate([tab, jnp.full((id_rows, LANES), fill, F32)], axis=0)

    ag = angles(HEAD_DIM)
    one_g = jnp.ones((nl, LANES - HEAD_DIM), F32)
    cos_g = jnp.concatenate([jnp.cos(ag), jnp.cos(ag), one_g], axis=-1)
    sin_g = jnp.concatenate([-jnp.sin(ag), jnp.sin(ag), 0.0 * one_g], axis=-1)
    am = angles(MLA_ROPE_DIM)
    one_n = jnp.ones((nl, MLA_NOPE_DIM), F32)
    one_t = jnp.ones((nl, LANES - MLA_QK_DIM), F32)
    cos_m = jnp.concatenate([one_n, jnp.cos(am), jnp.cos(am), one_t], axis=-1)
    sin_m = jnp.concatenate([0.0 * one_n, -jnp.sin(am), jnp.sin(am), 0.0 * one_t], axis=-1)
    return with_ctx(cos_g, 1.0), with_ctx(sin_g, 0.0), with_ctx(cos_m, 1.0), with_ctx(sin_m, 0.0)


def _pool_band():
    t = np.arange(TM)[:, None]
    src = np.arange(POOL_EXT)[None, :] - POOL_HALO
    live = np.arange(POOL_EXT)[None, :] < TM + 2 * POOL_HALO
    mats = [((src >= t - w // 2) & (src < t + w // 2) & live) for w in POOL_WINDOWS]
    return jnp.asarray(np.stack(mats).astype(np.float32), dtype=BF16)


def _prep_params(w_in, norm1, norm2, conv_w, w_pool, pool_scale, gqa_q_norm, gqa_k_norm,
                 mla_q_norm, mla_kv_norm, mla_w_uq, mla_w_uk, mla_w_uv, mla_qk_q_norm,
                 mla_qk_k_norm, w_out, w_router, b_router, id_rows, nl):
    dep = w_in.shape[0]
    o = 0
    pieces = {}
    for name, n in (("conv", 3 * CONV_DIM), ("pool", POOL_DIM), ("gq", GQA_HEADS * HEAD_DIM),
                    ("gk", GQA_KV_HEADS * HEAD_DIM), ("gv", GQA_KV_HEADS * HEAD_DIM),
                    ("mq", MLA_Q_RANK), ("mkv", MLA_KV_RANK), ("mkr", MLA_ROPE_DIM)):
        pieces[name] = w_in[..., o:o + n]
        o += n
    mkr = jnp.pad(pieces["mkr"], ((0, 0), (0, 0), (MLA_NOPE_DIM, LANES - MLA_QK_DIM)))
    w_in_p = jnp.concatenate([
        pieces["conv"], pieces["pool"], _pad_heads(pieces["gq"], GQA_HEADS, HEAD_DIM),
        _pad_heads(pieces["gk"], GQA_KV_HEADS, HEAD_DIM),
        pieces["mq"], pieces["mkv"], mkr], axis=-1).astype(BF16)
    wgvt = jnp.swapaxes(_pad_heads(pieces["gv"], GQA_KV_HEADS, HEAD_DIM), 1, 2).astype(BF16)
    wuvt = jnp.swapaxes(_pad_heads(mla_w_uv, MLA_HEADS, MLA_V_DIM), 1, 2).astype(BF16)
    eye = jnp.eye(len(POOL_WINDOWS), dtype=F32)
    wpool = jnp.einsum("gh,dgij->dgihj", eye, w_pool).reshape(dep, POOL_DIM, POOL_DIM).astype(BF16)
    cos_g, sin_g, cos_m, sin_m = _rope_tables(id_rows, nl)
    tri = np.triu(np.ones((TM, TM), np.float32), 1)
    wr_pad = jnp.pad(w_router, ((0, 0), (0, LANES - w_router.shape[1])))
    wr_hi = wr_pad.astype(BF16)
    wr_split = jnp.concatenate([wr_hi, (wr_pad - wr_hi.astype(F32)).astype(BF16)], axis=1)

    def row3(a):
        return a.reshape(dep, 1, a.shape[-1])

    return {
        "w_in": w_in_p,
        "norm1": row3(norm1), "norm2": row3(norm2),
        "gq": row3(_pad_heads(gqa_q_norm * (GQA_SCALE * LOG2E), 1, HEAD_DIM)),
        "gk": row3(_pad_heads(gqa_k_norm, 1, HEAD_DIM)),
        "mqn": row3(mla_q_norm), "mkvn": row3(mla_kv_norm),
        "qkq": row3(_pad_heads(mla_qk_q_norm * (MLA_SCALE * LOG2E), 1, MLA_QK_DIM)),
        "qkk": row3(_pad_heads(mla_qk_k_norm, 1, MLA_QK_DIM)),
        "wuq": _pad_heads(mla_w_uq, MLA_HEADS, MLA_QK_DIM).astype(BF16),
        "wuk": _pad_heads(mla_w_uk, MLA_HEADS, MLA_NOPE_DIM).astype(BF16),
        "wgvt": wgvt, "wuvt": wuvt,
        "cos_g": cos_g, "sin_g": sin_g, "cos_m": cos_m, "sin_m": sin_m,
        "conv_w": jnp.pad(conv_w, ((0, 0), (0, SUBLANES - conv_w.shape[1]), (0, 0))),
        "band": _pool_band(),
        "wpool": wpool, "pscale": row3(pool_scale),
        "w_out": w_out.astype(BF16),
        "wr": wr_split, "br": b_router.reshape(-1, 1),
        "tri": jnp.asarray(tri, dtype=BF16),
    }


def _moe_plan(counts):
    counts = counts.astype(I32)
    padded = ((counts + BM - 1) // BM) * BM
    pad_end = jnp.cumsum(padded)
    pad_start = pad_end - padded
    return pad_start, pad_end, padded


def kernel(x, c, ctx, c_ctx, w_mod, b_mod, norm1, norm2, w_in, conv_w, w_pool, pool_scale,
           gqa_q_norm, gqa_k_norm, mla_q_norm, mla_kv_norm, mla_w_uq, mla_w_uk, mla_w_uv,
           mla_qk_q_norm, mla_qk_k_norm, w_out, w_router, b_router, w_gate, w_up, w_down):
    nb, nl, d = x.shape
    nc = ctx.shape[1]
    depth = w_mod.shape[0]
    assert nc % TM == 0 and nl % TM == 0 and nl % GRID_W == 0 and nb < SUBLANES
    lt, nct = nl // TM, nc // TM
    wide = nl % INPROJ_ROWS == 0 and (nb * nc) % INPROJ_ROWS == 0
    geo = {"nbatch": nb, "nc": nc, "nl": nl, "lt": lt, "nct": nct,
           "n_lat_tiles": nb * lt, "n_tiles": nb * (lt + nct),
           "inproj_rows": INPROJ_ROWS if wide else TM}

    p = _prep_params(w_in, norm1, norm2, conv_w, w_pool, pool_scale, gqa_q_norm, gqa_k_norm,
                     mla_q_norm, mla_kv_norm, mla_w_uq, mla_w_uk, mla_w_uv, mla_qk_q_norm,
                     mla_qk_k_norm, w_out, w_router, b_router, geo["inproj_rows"], nl)
    cvec = jnp.concatenate([c, c_ctx[None, :], jnp.zeros((SUBLANES - nb - 1, d), F32)], axis=0)
    mods = _adaln(cvec, w_mod, b_mod)

    xa = jnp.concatenate([x.reshape(nb * nl, d), ctx.reshape(nb * nc, d)], axis=0)
    for i in range(depth):
        last = i == depth - 1
        mix = _inproj(xa, mods, i, p, geo)
        yg = _attention(mix["qg"], mix["kg"], mix["vgt"], True, geo, False)
        ym = _attention(mix["qm"], mix["km"], mix["vmt"], False, geo, False)
        if last:
            n_tiles, ygc, ymc = geo["n_lat_tiles"], yg, ym
        else:
            n_tiles = geo["n_tiles"]
            ygc = _attention(mix["qg"], mix["kg"], mix["vgt"], True, geo, True)
            ymc = _attention(mix["qm"], mix["km"], mix["vmt"], False, geo, True)
        xn, h2, info, cnt = _outproj(xa, mix, yg, ygc, ym, ymc, mods, i, p, geo, n_tiles)
        n_blocks = -(-2 * n_tiles * TM // BM) + N_EXPERTS
        n_slots = n_blocks * BM
        pad_start, pad_end, padded = _moe_plan(cnt[:, 0])
        n_used = (pad_end[-1:] // BM).astype(I32)
        block_row0 = jnp.arange(n_blocks, dtype=I32) * BM
        block_exp = jnp.minimum(jnp.sum((pad_end[None, :] <= block_row0[:, None]).astype(I32), axis=1),
                                N_EXPERTS - 1)
        dest, xs = _scatter(info, h2, pad_start, jnp.maximum(pad_end - BM, 0),
                            (padded > 0).astype(I32), n_used, n_slots)
        y = _experts(xs, block_exp, n_used, w_gate, w_up, w_down, i)
        xa = _combine(dest, info, xn, mods, y, i, geo)
    return xa.reshape(nb, nl, d)
```

```python
import functools
import math

import numpy as np
import jax
import jax.numpy as jnp
from jax import lax
from jax.experimental import pallas as pl
from jax.experimental.pallas import tpu as pltpu

F32 = jnp.float32
BF16 = jnp.bfloat16
I32 = jnp.int32

GRID_W = 64
CONV_DIM = 256
POOL_DIM = 256
POOL_WINDOWS = (2, 4, 8, 16)
HEAD_DIM = 64
GQA_HEADS = 4
GQA_KV_HEADS = 2
MLA_HEADS = 4
MLA_NOPE_DIM = 64
MLA_ROPE_DIM = 32
MLA_QK_DIM = MLA_NOPE_DIM + MLA_ROPE_DIM
MLA_V_DIM = 64
MLA_Q_RANK = 256
MLA_KV_RANK = 128
N_EXPERTS = 16
EXPERTS_PER_GROUP = 4
ROPE_THETA = 10000.0
NORM_EPS = 1e-6
LOG2E = 1.4426950408889634
GQA_SCALE = HEAD_DIM ** -0.5
MLA_SCALE = MLA_QK_DIM ** -0.5

LANES = 128
SUBLANES = 8
BF16_ROWS = 16
VMEM_LIMIT = 56 * 1024 * 1024

TM = 256
ATT_ROWS = 2048
INPROJ_ROWS = 512
BM = 512
V_ROWS = HEAD_DIM + BF16_ROWS
ATT_PAIR_UNROLL = 4
ROW_COPY_UNROLL = 8
POOL_EXT = 512
POOL_HALO = 8

ZC_B, ZC_C, ZC_U, ZC_P = 0, 256, 512, 768
ZC_GQ = 1024
ZC_GK = ZC_GQ + GQA_HEADS * LANES
ZC_MQ = ZC_GK + GQA_KV_HEADS * LANES
ZC_MKV = ZC_MQ + MLA_Q_RANK
ZC_MKR = ZC_MKV + MLA_KV_RANK
ZC_END = ZC_MKR + LANES

HIGHEST = lax.Precision.HIGHEST


def _cparams(sem, vmem=VMEM_LIMIT):
    return pltpu.CompilerParams(dimension_semantics=sem, vmem_limit_bytes=vmem)


def _sigmoid(v):
    return 1.0 / (1.0 + jnp.exp(-v))


def _adaln_kernel(c_ref, w_ref, b_ref, o_ref):
    c = c_ref[...]
    s = c * _sigmoid(c)
    o_ref[0] = jnp.dot(s, w_ref[0], preferred_element_type=F32, precision=HIGHEST) + b_ref[0]


def _adaln(cvec, w_mod, b_mod):
    depth, d, n6 = w_mod.shape
    tn = 1536 if n6 % 1536 == 0 else n6
    rows = cvec.shape[0]
    return pl.pallas_call(
        _adaln_kernel,
        grid=(depth, n6 // tn),
        in_specs=[
            pl.BlockSpec((rows, d), lambda i, n: (0, 0)),
            pl.BlockSpec((1, d, tn), lambda i, n: (i, 0, n)),
            pl.BlockSpec((1, 1, tn), lambda i, n: (i, 0, n)),
        ],
        out_specs=pl.BlockSpec((1, rows, tn), lambda i, n: (i, 0, n)),
        out_shape=jax.ShapeDtypeStruct((depth, rows, n6), F32),
        compiler_params=_cparams(("arbitrary", "arbitrary")),
        name="adaln",
    )(cvec, w_mod, b_mod.reshape(depth, 1, n6))


def _tile_geometry(j, geo):
    lt, nct, n_lat = geo["lt"], geo["nct"], geo["n_lat_tiles"]
    is_ctx = j >= n_lat
    jc = j - n_lat
    sample = jnp.where(is_ctx, jc // nct, j // lt)
    jt = jnp.where(is_ctx, jc % nct, j % lt)
    return {
        "is_ctx": is_ctx,
        "mod_row": jnp.where(is_ctx, geo["nbatch"], sample),
        "first": jt == 0,
        "last": jt == jnp.where(is_ctx, nct, lt) - 1,
        "pos0": jt * TM,
        "seg_len": jnp.where(is_ctx, geo["nc"], geo["nl"]),
    }


def _norm_rope(slab, gain, cos, sin, n_valid, first_half, half):
    ssq = jnp.dot((slab * slab).astype(BF16), jnp.ones((LANES, LANES), BF16), preferred_element_type=F32)
    y = slab * lax.rsqrt(ssq * (1.0 / n_valid) + NORM_EPS) * gain
    partner = jnp.where(first_half, pltpu.roll(y, LANES - half, 1), pltpu.roll(y, half, 1))
    return y * cos + partner * sin


def _inproj_kernel(x_ref, mods_ref, n1_ref, cg_ref, sg_ref, cm_ref, sm_ref, win_ref,
                   gq_ref, gk_ref, mqn_ref, mkvn_ref, qkq_ref, qkk_ref, wuq_ref, wuk_ref,
                   wgvt_ref, wuvt_ref,
                   cb_ref, cv_ref, zp_ref, qg_ref, kg_ref, vgt_ref, qm_ref, km_ref, vmt_ref,
                   *, d_model, nbatch, n_lat_steps, lat_steps_per_sample):
    d = d_model
    step = pl.program_id(0)
    row = jnp.where(step >= n_lat_steps, nbatch, step // lat_steps_per_sample)
    shift = mods_ref[0, pl.ds(row, 1), 0:d]
    scale = mods_ref[0, pl.ds(row, 1), d:2 * d]
    x = x_ref[...]
    ms = jnp.mean(x * x, axis=-1, keepdims=True)
    h = ((x * lax.rsqrt(ms + NORM_EPS)) * (n1_ref[0] * (1.0 + scale)) + shift).astype(BF16)
    nt_dims = (((1,), (1,)), ((), ()))

    def proj(c0, width):
        return jnp.dot(h, win_ref[0, :, c0:c0 + width], preferred_element_type=F32)

    def store_transposed(vt_ref, vt):
        srow = lax.broadcasted_iota(I32, vt.shape, 0)
        vt = jnp.where(jnp.bitwise_and(srow, LANES - 1) == HEAD_DIM, 1.0, vt).astype(BF16)
        for c in range(vt_ref.shape[0]):
            vt_ref[c] = vt[:, c * TM:(c + 1) * TM]

    tm = x.shape[0]
    lane = lax.broadcasted_iota(I32, (tm, LANES), 1)

    cm, sm = cm_ref[...], sm_ref[...]
    m_first = lane < MLA_NOPE_DIM + MLA_ROPE_DIM // 2
    zq = proj(ZC_MQ, MLA_Q_RANK)
    cq = zq * lax.rsqrt(jnp.mean(zq * zq, axis=-1, keepdims=True) + NORM_EPS) * mqn_ref[0]
    qpre = jnp.dot(cq.astype(BF16), wuq_ref[0], preferred_element_type=F32)
    zk = proj(ZC_MKV, MLA_KV_RANK + LANES)
    zkv, zkr = zk[:, 0:MLA_KV_RANK], zk[:, MLA_KV_RANK:]
    ckv = zkv * lax.rsqrt(jnp.mean(zkv * zkv, axis=-1, keepdims=True) + NORM_EPS) * mkvn_ref[0]
    ckv = ckv.astype(BF16)
    kvp = jnp.dot(ckv, wuk_ref[0], preferred_element_type=F32)
    store_transposed(vmt_ref, lax.dot_general(wuvt_ref[0], ckv, nt_dims, preferred_element_type=F32))
    for hd in range(MLA_HEADS):
        sl = slice(hd * LANES, (hd + 1) * LANES)
        qm_ref[:, sl] = _norm_rope(qpre[:, sl], qkq_ref[0], cm, sm, MLA_QK_DIM,
                                   m_first, MLA_ROPE_DIM // 2).astype(BF16)
        km_ref[:, sl] = _norm_rope(kvp[:, sl] + zkr, qkk_ref[0], cm, sm, MLA_QK_DIM,
                                   m_first, MLA_ROPE_DIM // 2).astype(BF16)

    cg, sg = cg_ref[...], sg_ref[...]
    g_first = lane < HEAD_DIM // 2
    zg = proj(ZC_GQ, (GQA_HEADS + GQA_KV_HEADS) * LANES)
    for hd in range(GQA_HEADS):
        qg_ref[:, hd * LANES:(hd + 1) * LANES] = _norm_rope(
            zg[:, hd * LANES:(hd + 1) * LANES], gq_ref[0], cg, sg, HEAD_DIM, g_first,
            HEAD_DIM // 2).astype(BF16)
    for hd in range(GQA_KV_HEADS):
        slab = zg[:, (GQA_HEADS + hd) * LANES:(GQA_HEADS + hd + 1) * LANES]
        kg_ref[:, hd * LANES:(hd + 1) * LANES] = _norm_rope(
            slab, gk_ref[0], cg, sg, HEAD_DIM, g_first, HEAD_DIM // 2).astype(BF16)
    store_transposed(vgt_ref, lax.dot_general(wgvt_ref[0], h, nt_dims, preferred_element_type=F32))

    zc = proj(ZC_B, 3 * CONV_DIM + POOL_DIM)
    cb_ref[...] = zc[:, ZC_B:ZC_B + CONV_DIM].astype(BF16)
    cv_ref[...] = (zc[:, ZC_C:ZC_C + CONV_DIM] * zc[:, ZC_U:ZC_U + CONV_DIM]).astype(BF16)
    zp_ref[...] = zc[:, ZC_P:ZC_P + POOL_DIM]


def _inproj(xa, mods, layer, p, geo):
    r, d = xa.shape
    tmi = geo["inproj_rows"]
    n_lat_steps = geo["nbatch"] * geo["nl"] // tmi
    lat_steps_per_sample = geo["nl"] // tmi
    rowmap = lambda j: (j, 0)
    posmap = lambda j: (jnp.where(j >= n_lat_steps, lat_steps_per_sample, j % lat_steps_per_sample), 0)
    lay3 = lambda j: (layer, 0, 0)

    def full3(a):
        return pl.BlockSpec((1,) + a.shape[1:], lay3)

    outs = [("cb", CONV_DIM, BF16, False), ("cv", CONV_DIM, BF16, False), ("zp", POOL_DIM, F32, False),
            ("qg", GQA_HEADS * LANES, BF16, False), ("kg", GQA_KV_HEADS * LANES, BF16, False),
            ("vgt", GQA_KV_HEADS * LANES, BF16, True), ("qm", MLA_HEADS * LANES, BF16, False),
            ("km", MLA_HEADS * LANES, BF16, False), ("vmt", MLA_HEADS * LANES, BF16, True)]

    def out_spec(w, transposed):
        if transposed:
            return pl.BlockSpec((tmi // TM, w, TM), lambda j: (j, 0, 0))
        return pl.BlockSpec((tmi, w), rowmap)

    def out_shape(w, dt, transposed):
        return jax.ShapeDtypeStruct((r // TM, w, TM) if transposed else (r, w), dt)

    res = pl.pallas_call(
        functools.partial(_inproj_kernel, d_model=d, nbatch=geo["nbatch"], n_lat_steps=n_lat_steps,
                          lat_steps_per_sample=lat_steps_per_sample),
        grid=(r // tmi,),
        in_specs=[
            pl.BlockSpec((tmi, d), rowmap),
            full3(mods), full3(p["norm1"]),
            pl.BlockSpec((tmi, LANES), posmap), pl.BlockSpec((tmi, LANES), posmap),
            pl.BlockSpec((tmi, LANES), posmap), pl.BlockSpec((tmi, LANES), posmap),
            full3(p["w_in"]), full3(p["gq"]), full3(p["gk"]), full3(p["mqn"]), full3(p["mkvn"]),
            full3(p["qkq"]), full3(p["qkk"]), full3(p["wuq"]), full3(p["wuk"]),
            full3(p["wgvt"]), full3(p["wuvt"]),
        ],
        out_specs=[out_spec(w, t) for _, w, _, t in outs],
        out_shape=[out_shape(w, dt, t) for _, w, dt, t in outs],
        compiler_params=_cparams(("arbitrary",)),
        name="inproj",
    )(xa, mods, p["norm1"], p["cos_g"], p["sin_g"], p["cos_m"], p["sin_m"], p["w_in"],
      p["gq"], p["gk"], p["mqn"], p["mkvn"], p["qkq"], p["qkk"], p["wuq"], p["wuk"],
      p["wgvt"], p["wuvt"])
    return {name: a for (name, _, _, _), a in zip(outs, res)}


def _attn_kernel(q_ref, kl_ref, kc_ref, vtl_ref, vtc_ref, o_ref, q_st, s_buf, p_buf, a_buf, m_ref, acc_ref,
                 *, shared_kv, n_lat, n_ctx, tk):
    tq = q_ref.shape[0]
    n_steps = n_lat + n_ctx
    if shared_kv:
        q_st[0] = jnp.concatenate([q_ref[:, 0:LANES], q_ref[:, LANES:2 * LANES]], axis=0)
        cols = [0]
    else:
        q_st[0] = q_ref[:, 0:LANES]
        q_st[1] = q_ref[:, LANES:2 * LANES]
        cols = [0, LANES]
    for si, col in enumerate(cols):
        q_s, m_s, acc_s = q_st.at[si], m_ref.at[si], acc_ref.at[si]
        m_s[...] = jnp.full(m_s.shape, -1e30, F32)
        acc_s[...] = jnp.zeros(acc_s.shape, F32)
        p_buf[1] = jnp.zeros(p_buf.shape[1:], BF16)
        a_buf[1] = jnp.ones(a_buf.shape[1:], F32)

        def pick(i, lat_fn, ctx_fn):
            ctx = ctx_fn(jnp.clip(i - n_lat, 0, n_ctx - 1))
            if n_lat == 0:
                return ctx
            return jnp.where(i >= n_lat, ctx, lat_fn(jnp.clip(i, 0, n_lat - 1)))

        def scores(i, slot, q_s=q_s, col=col):
            k = pick(i, lambda t: kl_ref[pl.ds(pl.multiple_of(t * tk, tk), tk), col:col + LANES],
                     lambda t: kc_ref[pl.ds(pl.multiple_of(t * tk, tk), tk), col:col + LANES])
            s_buf[slot] = lax.dot_general(k, q_s[...], (((1,), (1,)), ((), ())),
                                          preferred_element_type=F32)

        def softmax(slot, m_s=m_s):
            s = s_buf[slot]
            m_old = m_s[...]
            m_new = jnp.maximum(m_old, jnp.max(s, axis=0, keepdims=True))
            a_buf[slot] = jnp.exp2(m_old - m_new)
            p_buf[slot] = jnp.exp2(s - m_new[0:1]).astype(BF16)
            m_s[...] = m_new

        def accumulate(i, slot, acc_s=acc_s, col=col):
            vt = pick(i, lambda t: vtl_ref[t, col:col + V_ROWS, :],
                      lambda t: vtc_ref[t, col:col + V_ROWS, :])
            acc_s[...] = a_buf[slot][0:1] * acc_s[...] + jnp.dot(vt, p_buf[slot],
                                                                 preferred_element_type=F32)

        scores(0, 0)

        def pair(t, carry):
            i = 2 * t
            scores(i + 1, 1)
            softmax(0)
            accumulate(i - 1, 1)
            scores(i + 2, 0)
            softmax(1)
            accumulate(i, 0)
            return carry

        n_pairs = (n_steps - 1) // 2
        lax.fori_loop(0, n_pairs, pair, 0, unroll=ATT_PAIR_UNROLL)
        for i in range(2 * n_pairs, n_steps):
            if i + 1 < n_steps:
                scores(i + 1, (i + 1) % 2)
            softmax(i % 2)
            if i >= 1:
                accumulate(i - 1, (i - 1) % 2)
        accumulate(n_steps - 1, (n_steps - 1) % 2)
    def finish(acc_t):
        o_t = acc_t * (1.0 / acc_t[HEAD_DIM:HEAD_DIM + 1])
        pad = jnp.zeros((LANES - V_ROWS, o_t.shape[1]), F32)
        return jnp.concatenate([o_t, pad], axis=0).T

    if shared_kv:
        o_both = finish(acc_ref[0])
        o0, o1 = o_both[0:tq], o_both[tq:2 * tq]
    else:
        o0, o1 = finish(acc_ref[0]), finish(acc_ref[1])
    lane = lax.broadcasted_iota(I32, (tq, LANES), 1)
    o_ref[...] = jnp.where(lane < HEAD_DIM, o0, pltpu.roll(o1, HEAD_DIM, 1)).astype(BF16)


def _attention(q, k, vt, shared_kv, geo, ctx_queries):
    nb, nc, nl = geo["nbatch"], geo["nc"], geo["nl"]
    kw = LANES if shared_kv else 2 * LANES
    tk = TM
    ctx_blk0 = nb * nl // nc
    ctx_k = pl.BlockSpec((nc, kw), lambda b, g, t: (ctx_blk0 + b, g))
    ctx_vt = pl.BlockSpec((nc // tk, kw, tk), lambda b, g, t: (ctx_blk0 + b, g, 0))
    if ctx_queries:
        tq, q_per, q_blk0, n_lat = nc, 1, ctx_blk0, 0
        lat_k, lat_vt = ctx_k, ctx_vt
    else:
        tq = min(ATT_ROWS, nl) // (2 if shared_kv else 1)
        q_per, q_blk0, n_lat = nl // tq, 0, nl // tk
        lat_k = pl.BlockSpec((nl, kw), lambda b, g, t: (b, g))
        lat_vt = pl.BlockSpec((nl // tk, kw, tk), lambda b, g, t: (b, g, 0))
    n_streams, rows = (1, 2 * tq) if shared_kv else (2, tq)
    return pl.pallas_call(
        functools.partial(_attn_kernel, shared_kv=shared_kv, n_lat=n_lat, n_ctx=nc // tk, tk=tk),
        grid=(nb, 2, q_per),
        in_specs=[
            pl.BlockSpec((tq, 2 * LANES), lambda b, g, t: (q_blk0 + b * q_per + t, g)),
            lat_k, ctx_k, lat_vt, ctx_vt,
        ],
        out_specs=pl.BlockSpec((tq, LANES), lambda b, g, t: (b * q_per + t, g)),
        out_shape=jax.ShapeDtypeStruct((nb * q_per * tq, 2 * LANES), BF16),
        scratch_shapes=[pltpu.VMEM((n_streams, rows, LANES), BF16),
                        pltpu.VMEM((2, tk, rows), F32),
                        pltpu.VMEM((2, tk, rows), BF16),
                        pltpu.VMEM((2, SUBLANES, rows), F32),
                        pltpu.VMEM((n_streams, SUBLANES, rows), F32),
                        pltpu.VMEM((n_streams, V_ROWS, rows), F32)],
        compiler_params=_cparams(("arbitrary", "arbitrary", "arbitrary")),
        name=("attn_gqa" if shared_kv else "attn_mla") + ("_ctx" if ctx_queries else ""),
    )(q, k, k, vt, vt)


def _top2_sum(a, b, c, d):
    hi_ab, lo_ab = jnp.maximum(a, b), jnp.minimum(a, b)
    hi_cd, lo_cd = jnp.maximum(c, d), jnp.minimum(c, d)
    first = jnp.maximum(hi_ab, hi_cd)
    second = jnp.maximum(jnp.minimum(hi_ab, hi_cd), jnp.maximum(lo_ab, lo_cd))
    return first + second


def _outproj_kernel(x_ref, cb_ref, cv_ref, cvp_ref, cvn_ref, zp_ref, zpp_ref, zpn_ref,
                    ygl_ref, ygc_ref, yml_ref, ymc_ref,
                    mods_ref, convw_ref, band_ref, wpool_ref, pscale_ref,
                    wout_ref, n2_ref, wr_ref, br_ref, tri_ref,
                    xo_ref, h2_ref, info_ref, cnt_ref,
                    *, d_model, geo):
    tile = pl.program_id(0)
    d = d_model
    tm = x_ref.shape[0]
    tg = _tile_geometry(tile, geo)
    is_ctx = tg["is_ctx"]
    keep_prev = jnp.where(tg["first"], 0.0, 1.0)
    keep_next = jnp.where(tg["last"], 0.0, 1.0)
    row = tg["mod_row"]
    gate1 = mods_ref[0, pl.ds(row, 1), 2 * d:3 * d]
    shift2 = mods_ref[0, pl.ds(row, 1), 3 * d:4 * d]
    scale2 = mods_ref[0, pl.ds(row, 1), 4 * d:5 * d]

    v = cv_ref[...].astype(F32)
    prev_row = cvp_ref[...].astype(F32)[BF16_ROWS - 1:BF16_ROWS] * keep_prev
    next_row = cvn_ref[...].astype(F32)[0:1] * keep_next
    rid = lax.broadcasted_iota(I32, (tm, CONV_DIM), 0)
    vm1 = jnp.where(rid == 0, prev_row, pltpu.roll(v, 1, 0))
    vp1 = jnp.where(rid == tm - 1, next_row, pltpu.roll(v, tm - 1, 0))
    cw = convw_ref[0]
    y_conv = cb_ref[...].astype(F32) * (vm1 * cw[0:1] + v * cw[1:2] + vp1 * cw[2:3])

    zp = zp_ref[...]
    ext = jnp.concatenate(
        [zpp_ref[...] * keep_prev, zp, zpn_ref[...] * keep_next,
         jnp.zeros((POOL_EXT - tm - 2 * POOL_HALO, POOL_DIM), F32)], axis=0).astype(BF16)
    ext_a, ext_b = ext[:, 0:LANES], ext[:, LANES:2 * LANES]
    lane = lax.broadcasted_iota(I32, (tm, LANES), 1)
    low = lane < POOL_DIM // 4
    sum_a = jnp.where(low, jnp.dot(band_ref[0], ext_a, preferred_element_type=F32),
                      jnp.dot(band_ref[1], ext_a, preferred_element_type=F32))
    sum_b = jnp.where(low, jnp.dot(band_ref[2], ext_b, preferred_element_type=F32),
                      jnp.dot(band_ref[3], ext_b, preferred_element_type=F32))
    sums = jnp.concatenate([sum_a, sum_b], axis=1)
    lane_p = lax.broadcasted_iota(I32, (tm, POOL_DIM), 1)
    half_w = jnp.left_shift(1, jnp.right_shift(lane_p, int(math.log2(POOL_DIM // 4))))
    pos = tg["pos0"] + rid
    cnt = (jnp.minimum(pos + half_w, tg["seg_len"]) - jnp.maximum(pos - half_w, 0)).astype(F32)
    dlt = sums / cnt - zp
    y_pool = jnp.dot(dlt.astype(BF16), wpool_ref[0], preferred_element_type=F32) * pscale_ref[0]

    y_gqa = jnp.where(is_ctx, ygc_ref[...], ygl_ref[...])
    y_mla = jnp.where(is_ctx, ymc_ref[...], yml_ref[...])
    ycat = jnp.concatenate([y_conv.astype(BF16), y_pool.astype(BF16), y_gqa, y_mla], axis=1)
    y = jnp.dot(ycat, wout_ref[0], preferred_element_type=F32)
    xn = x_ref[...] + gate1 * y
    xo_ref[...] = xn
    ms = jnp.mean(xn * xn, axis=-1, keepdims=True)
    h2 = (xn * lax.rsqrt(ms + NORM_EPS)) * (n2_ref[0] * (1.0 + scale2)) + shift2
    h2_ref[...] = h2

    h_hi = h2.astype(BF16)
    h_lo = (h2 - h_hi.astype(F32)).astype(BF16)
    wr = wr_ref[...]
    part = jnp.dot(h_hi, wr, preferred_element_type=F32)
    small = part[:, LANES:] + jnp.dot(h_lo, wr[:, 0:LANES], preferred_element_type=F32)
    logits = (part[:, 0:LANES] + small).T[0:N_EXPERTS]
    scores = _sigmoid(logits)
    sel = scores + br_ref[...]
    epg = EXPERTS_PER_GROUP
    n_groups = N_EXPERTS // epg
    srow = [sel[e:e + 1] for e in range(N_EXPERTS)]
    crow = [scores[e:e + 1] for e in range(N_EXPERTS)]
    gscore = [_top2_sum(*srow[g * epg:(g + 1) * epg]) for g in range(n_groups)]
    gbest = jnp.zeros_like(gscore[0]).astype(I32)
    best = gscore[0]
    for g in range(1, n_groups):
        upd = gscore[g] > best
        gbest = jnp.where(upd, g, gbest)
        best = jnp.where(upd, gscore[g], best)

    def pick(rows_, j):
        out = rows_[(n_groups - 1) * epg + j]
        for g in range(n_groups - 2, -1, -1):
            out = jnp.where(gbest == g, rows_[g * epg + j], out)
        return out

    sv = [pick(srow, j) for j in range(epg)]
    cv_ = [pick(crow, j) for j in range(epg)]
    i1 = jnp.zeros_like(gbest)
    b1 = sv[0]
    for j in range(1, epg):
        upd = sv[j] > b1
        i1 = jnp.where(upd, j, i1)
        b1 = jnp.where(upd, sv[j], b1)
    i2 = jnp.zeros_like(gbest)
    b2 = jnp.full_like(b1, -jnp.inf)
    for j in range(epg):
        upd = jnp.logical_and(i1 != j, sv[j] > b2)
        i2 = jnp.where(upd, j, i2)
        b2 = jnp.where(upd, sv[j], b2)
    s1 = cv_[epg - 1]
    s2 = cv_[epg - 1]
    for j in range(epg - 2, -1, -1):
        s1 = jnp.where(i1 == j, cv_[j], s1)
        s2 = jnp.where(i2 == j, cv_[j], s2)
    inv = 1.0 / (s1 + s2)
    e1 = gbest * epg + i1
    e2 = gbest * epg + i2

    @pl.when(tile == 0)
    def _():
        cnt_ref[...] = jnp.zeros_like(cnt_ref)

    erow = lax.broadcasted_iota(I32, (N_EXPERTS, tm), 0)
    hit1 = erow == e1
    hit2 = erow == e2
    onehot = jnp.where(hit1, 1.0, 0.0) + jnp.where(hit2, 1.0, 0.0)
    before = jnp.dot(onehot.astype(BF16), tri_ref[...], preferred_element_type=F32)
    tot = cnt_ref[:, 0:1] + before
    rank1 = jnp.sum(jnp.where(hit1, tot, 0.0), axis=0, keepdims=True).astype(I32)
    rank2 = jnp.sum(jnp.where(hit2, tot, 0.0), axis=0, keepdims=True).astype(I32)
    cnt_ref[...] = cnt_ref[...] + jnp.sum(onehot, axis=1, keepdims=True)

    w1 = lax.bitcast_convert_type(s1 * inv, I32)
    w2 = lax.bitcast_convert_type(s2 * inv, I32)
    irow = lax.broadcasted_iota(I32, (SUBLANES, tm), 0)
    info = jnp.where(irow == 0, e1, jnp.where(irow == 1, e2, jnp.where(
        irow == 2, rank1, jnp.where(irow == 3, rank2, jnp.where(
            irow == 4, w1, jnp.where(irow == 5, w2, 0))))))
    info_ref[0] = info


def _outproj(xa, mix, yg, ygc, ym, ymc, mods, layer, p, geo, n_tiles):
    d = xa.shape[1]
    nt_all, n_lat = geo["n_tiles"], geo["n_lat_tiles"]
    r = n_tiles * TM
    nt = n_tiles
    rowmap = lambda j: (j, 0)
    latmap = lambda j: (jnp.minimum(j, n_lat - 1), 0)
    ctxmap = lambda j: (jnp.clip(j - n_lat, 0, ygc.shape[0] // TM - 1), 0)
    lay3 = lambda j: (layer, 0, 0)
    c0 = lambda j: (0, 0)
    c3 = lambda j: (0, 0, 0)
    bf_blocks = TM // BF16_ROWS
    f_blocks = TM // SUBLANES

    def full3(a):
        return pl.BlockSpec((1,) + a.shape[1:], lay3)

    return pl.pallas_call(
        functools.partial(_outproj_kernel, d_model=d, geo=geo),
        grid=(n_tiles,),
        in_specs=[
            pl.BlockSpec((TM, d), rowmap),
            pl.BlockSpec((TM, CONV_DIM), rowmap),
            pl.BlockSpec((TM, CONV_DIM), rowmap),
            pl.BlockSpec((BF16_ROWS, CONV_DIM), lambda j: (jnp.maximum(j * bf_blocks - 1, 0), 0)),
            pl.BlockSpec((BF16_ROWS, CONV_DIM),
                         lambda j: (jnp.minimum((j + 1) * bf_blocks, nt_all * bf_blocks - 1), 0)),
            pl.BlockSpec((TM, POOL_DIM), rowmap),
            pl.BlockSpec((SUBLANES, POOL_DIM), lambda j: (jnp.maximum(j * f_blocks - 1, 0), 0)),
            pl.BlockSpec((SUBLANES, POOL_DIM),
                         lambda j: (jnp.minimum((j + 1) * f_blocks, nt_all * f_blocks - 1), 0)),
            pl.BlockSpec((TM, 2 * LANES), latmap),
            pl.BlockSpec((TM, 2 * LANES), ctxmap),
            pl.BlockSpec((TM, 2 * LANES), latmap),
            pl.BlockSpec((TM, 2 * LANES), ctxmap),
            full3(mods), full3(p["conv_w"]),
            pl.BlockSpec(p["band"].shape, c3),
            full3(p["wpool"]), full3(p["pscale"]), full3(p["w_out"]), full3(p["norm2"]),
            pl.BlockSpec(p["wr"].shape, c0), pl.BlockSpec(p["br"].shape, c0),
            pl.BlockSpec(p["tri"].shape, c0),
        ],
        out_specs=[
            pl.BlockSpec((TM, d), rowmap),
            pl.BlockSpec((TM, d), rowmap),
            pl.BlockSpec((1, SUBLANES, TM), lambda j: (j, 0, 0)),
            pl.BlockSpec((N_EXPERTS, LANES), c0),
        ],
        out_shape=[
            jax.ShapeDtypeStruct((r, d), F32),
            jax.ShapeDtypeStruct((r, d), F32),
            jax.ShapeDtypeStruct((nt, SUBLANES, TM), I32),
            jax.ShapeDtypeStruct((N_EXPERTS, LANES), F32),
        ],
        compiler_params=_cparams(("arbitrary",)),
        name="outproj",
    )(xa, mix["cb"], mix["cv"], mix["cv"], mix["cv"], mix["zp"], mix["zp"], mix["zp"],
      yg, ygc, ym, ymc, mods, p["conv_w"], p["band"], p["wpool"], p["pscale"], p["w_out"],
      p["norm2"], p["wr"], p["br"], p["tri"])


def _load_slots(block_ref, idx_ref, sem):
    n = block_ref.shape[2]
    copies = [pltpu.make_async_copy(block_ref.at[0, k], idx_ref.at[pl.ds(k * n, n)], sem) for k in range(2)]
    for cp in copies:
        cp.start()
    for cp in copies:
        cp.wait()


def _row_copies(idx_ref, n_rows, make):
    def body(t, carry):
        for u in range(SUBLANES):
            r = t * SUBLANES + u
            make(0, t, u, idx_ref[r]).start()
            make(1, t, u, idx_ref[n_rows + r]).start()
        return carry
    lax.fori_loop(0, n_rows // SUBLANES, body, 0)


def _scatter_kernel(ps_ref, zs_ref, has_ref, nu_ref, info_ref, h2_ref, dest_ref, xs_ref,
                    dsm_ref, zbuf_ref, stage_ref, sem_ref, *, first_spare, n_blocks, n_tiles):
    j = pl.program_id(0)
    tm = h2_ref.shape[0]

    @pl.when(j == 0)
    def _():
        zbuf_ref[...] = jnp.zeros_like(zbuf_ref)

        def zero_copy(start):
            return pltpu.make_async_copy(
                zbuf_ref, xs_ref.at[pl.ds(pl.multiple_of(start, BM), BM)], sem_ref.at[2])

        for act in ("start", "wait"):
            for e in range(N_EXPERTS):
                @pl.when(has_ref[e] > 0)
                def _():
                    getattr(zero_copy(zs_ref[e]), act)()
            for jb in range(first_spare, n_blocks):
                @pl.when(jb >= nu_ref[0])
                def _():
                    getattr(zero_copy(jb * BM), act)()

    info = info_ref[0]
    e1, e2 = info[0:1], info[1:2]
    d1, d2 = info[2:3], info[3:4]
    for e in range(N_EXPERTS):
        d1 = d1 + jnp.where(e1 == e, ps_ref[e], 0)
        d2 = d2 + jnp.where(e2 == e, ps_ref[e], 0)
    irow = lax.broadcasted_iota(I32, (SUBLANES, tm), 0)
    dest_ref[0] = jnp.where(irow == 0, d1, jnp.where(irow == 1, d2, 0))
    _load_slots(dest_ref, dsm_ref, sem_ref.at[3])

    par = j % 2
    stage_ref[par] = h2_ref[...].reshape(stage_ref.shape[1:])

    def row_copy(_, t, u, slot):
        return pltpu.make_async_copy(stage_ref.at[par, t, pl.ds(u, 1)], xs_ref.at[pl.ds(slot, 1)],
                                     sem_ref.at[par])

    _row_copies(dsm_ref, tm, row_copy)

    def retire(which):
        for _ in range(2):
            pltpu.make_async_copy(h2_ref, xs_ref.at[pl.ds(0, tm)], sem_ref.at[which]).wait()

    @pl.when(j > 0)
    def _():
        retire(1 - par)

    @pl.when(j == n_tiles - 1)
    def _():
        retire(par)


def _scatter(info, h2, pad_start, zero_start, has_rows, n_used, n_slots):
    r, d = h2.shape
    nt = r // TM
    return pl.pallas_call(
        functools.partial(_scatter_kernel, first_spare=-(-2 * r // BM), n_blocks=n_slots // BM,
                          n_tiles=nt),
        grid_spec=pltpu.PrefetchScalarGridSpec(
            num_scalar_prefetch=4,
            grid=(nt,),
            in_specs=[
                pl.BlockSpec((1, SUBLANES, TM), lambda j, *_: (j, 0, 0)),
                pl.BlockSpec((TM, d), lambda j, *_: (j, 0)),
            ],
            out_specs=[
                pl.BlockSpec((1, SUBLANES, TM), lambda j, *_: (j, 0, 0)),
                pl.BlockSpec(memory_space=pl.ANY),
            ],
            scratch_shapes=[
                pltpu.SMEM((2 * TM,), I32),
                pltpu.VMEM((BM, d), F32),
                pltpu.VMEM((2, TM // SUBLANES, SUBLANES, d), F32),
                pltpu.SemaphoreType.DMA((4,)),
            ],
        ),
        out_shape=[
            jax.ShapeDtypeStruct((nt, SUBLANES, TM), I32),
            jax.ShapeDtypeStruct((n_slots, d), F32),
        ],
        compiler_params=_cparams(("arbitrary",)),
        name="moe_scatter",
    )(pad_start, zero_start, has_rows, n_used, info, h2)


def _expert_kernel(be_ref, nu_ref, xs_ref, wg_ref, wu_ref, wd_ref, y_ref, wgb, wub, wdb):
    j = pl.program_id(0)

    @pl.when(j < nu_ref[0])
    def _():
        e = be_ref[j]
        prev = be_ref[jnp.maximum(j - 1, 0)]

        @pl.when(jnp.logical_or(j == 0, e != prev))
        def _():
            wgb[...] = wg_ref[0].astype(BF16)
            wub[...] = wu_ref[0].astype(BF16)
            wdb[...] = wd_ref[0].astype(BF16)

        x = xs_ref[...].astype(BF16)
        a = jnp.dot(x, wgb[...], preferred_element_type=F32)
        u = jnp.dot(x, wub[...], preferred_element_type=F32)
        hmid = (a * _sigmoid(a) * u).astype(BF16)
        y_ref[...] = jnp.dot(hmid, wdb[...], preferred_element_type=F32)

    @pl.when(j >= nu_ref[0])
    def _():
        y_ref[...] = jnp.zeros_like(y_ref)


def _experts(xs, block_exp, n_used, w_gate, w_up, w_down, layer):
    n_slots, d = xs.shape
    de = w_gate.shape[-1]
    nbm = n_slots // BM
    n_exp = w_gate.shape[1]

    def blk(j, be, nu):
        return (jnp.minimum(j, nu[0] - 1), 0)

    def wmap(j, be, nu):
        return (layer * n_exp + be[jnp.minimum(j, nu[0] - 1)], 0, 0)

    wg = w_gate.reshape((-1,) + w_gate.shape[2:])
    wu = w_up.reshape((-1,) + w_up.shape[2:])
    wd = w_down.reshape((-1,) + w_down.shape[2:])
    return pl.pallas_call(
        _expert_kernel,
        grid_spec=pltpu.PrefetchScalarGridSpec(
            num_scalar_prefetch=2,
            grid=(nbm,),
            in_specs=[
                pl.BlockSpec((BM, d), blk),
                pl.BlockSpec((1, d, de), wmap),
                pl.BlockSpec((1, d, de), wmap),
                pl.BlockSpec((1, de, d), wmap),
            ],
            out_specs=pl.BlockSpec((BM, d), lambda j, be, nu: (j, 0)),
            scratch_shapes=[
                pltpu.VMEM((d, de), BF16),
                pltpu.VMEM((d, de), BF16),
                pltpu.VMEM((de, d), BF16),
            ],
        ),
        out_shape=jax.ShapeDtypeStruct((n_slots, d), F32),
        compiler_params=_cparams(("arbitrary",)),
        name="moe_experts",
    )(block_exp, n_used, xs, wg, wu, wd)


def _combine_kernel(dest_ref, dest_next_ref, info_ref, x_ref, mods_ref, y_ref, o_ref,
                    dsm_ref, ybuf_ref, sem_ref, *, d_model, geo, n_steps):
    j = pl.program_id(0)
    d = d_model
    tm = x_ref.shape[0]
    row = _tile_geometry(j, geo)["mod_row"]
    gate2 = mods_ref[0, pl.ds(row, 1), 5 * d:6 * d]
    cur = j % 2

    def gather(idx_block_ref, buf):
        _load_slots(idx_block_ref, dsm_ref, sem_ref.at[2])

        def row_copy(k, t, u, slot):
            return pltpu.make_async_copy(y_ref.at[pl.ds(slot, 1)],
                                         ybuf_ref.at[buf, k, t, pl.ds(u, 1)], sem_ref.at[buf])

        _row_copies(dsm_ref, tm, row_copy)

    @pl.when(j == 0)
    def _():
        gather(dest_ref, 0)

    @pl.when(j + 1 < n_steps)
    def _():
        gather(dest_next_ref, 1 - cur)

    for k in range(2):
        pltpu.make_async_copy(y_ref.at[pl.ds(0, tm)], o_ref, sem_ref.at[cur]).wait()

    info = info_ref[0]
    wrow = lax.broadcasted_iota(I32, (LANES, tm), 0)
    w_lanes = jnp.where(wrow == 0, lax.bitcast_convert_type(info[4:5], F32),
                        jnp.where(wrow == 1, lax.bitcast_convert_type(info[5:6], F32), 0.0))
    w_rows = w_lanes.T
    y1 = ybuf_ref[cur, 0].reshape(tm, d)
    y2 = ybuf_ref[cur, 1].reshape(tm, d)
    o_ref[...] = x_ref[...] + gate2 * (w_rows[:, 0:1] * y1 + w_rows[:, 1:2] * y2)


def _combine(dest, info, xn, mods, y, layer, geo):
    r, d = xn.shape
    n_steps = r // TM
    return pl.pallas_call(
        functools.partial(_combine_kernel, d_model=d, geo=geo, n_steps=n_steps),
        grid=(n_steps,),
        in_specs=[
            pl.BlockSpec((1, SUBLANES, TM), lambda j: (j, 0, 0)),
            pl.BlockSpec((1, SUBLANES, TM), lambda j: (jnp.minimum(j + 1, n_steps - 1), 0, 0)),
            pl.BlockSpec((1, SUBLANES, TM), lambda j: (j, 0, 0)),
            pl.BlockSpec((TM, d), lambda j: (j, 0)),
            pl.BlockSpec((1,) + mods.shape[1:], lambda j: (layer, 0, 0)),
            pl.BlockSpec(memory_space=pl.ANY),
        ],
        out_specs=pl.BlockSpec((TM, d), lambda j: (j, 0)),
        out_shape=jax.ShapeDtypeStruct((n_steps * TM, d), F32),
        scratch_shapes=[
            pltpu.SMEM((2 * TM,), I32),
            pltpu.VMEM((2, 2, TM // SUBLANES, SUBLANES, d), F32),
            pltpu.SemaphoreType.DMA((3,)),
        ],
        compiler_params=_cparams(("arbitrary",)),
        name="moe_combine",
    )(dest, dest, info, xn, mods, y)


def _pad_heads(w, n_heads, width):
    lead = w.shape[:-1]
    w = w.reshape(lead + (n_heads, width))
    w = jnp.pad(w, [(0, 0)] * len(lead) + [(0, 0), (0, LANES - width)])
    return w.reshape(lead + (n_heads * LANES,))


def _rope_tables(id_rows, nl):
    t = jnp.arange(nl)
    row_id = (t // GRID_W).astype(F32)
    col_id = (t % GRID_W).astype(F32)

    def angles(rot_dim):
        n_freq = rot_dim // 4
        inv_freq = jnp.power(ROPE_THETA, -jnp.arange(n_freq, dtype=F32) / n_freq)
        return jnp.concatenate([row_id[:, None] * inv_freq, col_id[:, None] * inv_freq], axis=-1)

    def with_ctx(tab, fill):
        return jnp.concatenate([tab, jnp.full((id_rows, LANES), fill, F32)], axis=0)

    ag = angles(HEAD_DIM)
    one_g = jnp.ones((nl, LANES - HEAD_DIM), F32)
    cos_g = jnp.concatenate([jnp.cos(ag), jnp.cos(ag), one_g], axis=-1)
    sin_g = jnp.concatenate([-jnp.sin(ag), jnp.sin(ag), 0.0 * one_g], axis=-1)
    am = angles(MLA_ROPE_DIM)
    one_n = jnp.ones((nl, MLA_NOPE_DIM), F32)
    one_t = jnp.ones((nl, LANES - MLA_QK_DIM), F32)
    cos_m = jnp.concatenate([one_n, jnp.cos(am), jnp.cos(am), one_t], axis=-1)
    sin_m = jnp.concatenate([0.0 * one_n, -jnp.sin(am), jnp.sin(am), 0.0 * one_t], axis=-1)
    return with_ctx(cos_g, 1.0), with_ctx(sin_g, 0.0), with_ctx(cos_m, 1.0), with_ctx(sin_m, 0.0)


def _pool_band():
    t = np.arange(TM)[:, None]
    src = np.arange(POOL_EXT)[None, :] - POOL_HALO
    live = np.arange(POOL_EXT)[None, :] < TM + 2 * POOL_HALO
    mats = [((src >= t - w // 2) & (src < t + w // 2) & live) for w in POOL_WINDOWS]
    return jnp.asarray(np.stack(mats).astype(np.float32), dtype=BF16)


def _prep_params(w_in, norm1, norm2, conv_w, w_pool, pool_scale, gqa_q_norm, gqa_k_norm,
                 mla_q_norm, mla_kv_norm, mla_w_uq, mla_w_uk, mla_w_uv, mla_qk_q_norm,
                 mla_qk_k_norm, w_out, w_router, b_router, id_rows, nl):
    dep = w_in.shape[0]
    o = 0
    pieces = {}
    for name, n in (("conv", 3 * CONV_DIM), ("pool", POOL_DIM), ("gq", GQA_HEADS * HEAD_DIM),
                    ("gk", GQA_KV_HEADS * HEAD_DIM), ("gv", GQA_KV_HEADS * HEAD_DIM),
                    ("mq", MLA_Q_RANK), ("mkv", MLA_KV_RANK), ("mkr", MLA_ROPE_DIM)):
        pieces[name] = w_in[..., o:o + n]
        o += n
    mkr = jnp.pad(pieces["mkr"], ((0, 0), (0, 0), (MLA_NOPE_DIM, LANES - MLA_QK_DIM)))
    w_in_p = jnp.concatenate([
        pieces["conv"], pieces["pool"], _pad_heads(pieces["gq"], GQA_HEADS, HEAD_DIM),
        _pad_heads(pieces["gk"], GQA_KV_HEADS, HEAD_DIM),
        pieces["mq"], pieces["mkv"], mkr], axis=-1).astype(BF16)
    wgvt = jnp.swapaxes(_pad_heads(pieces["gv"], GQA_KV_HEADS, HEAD_DIM), 1, 2).astype(BF16)
    wuvt = jnp.swapaxes(_pad_heads(mla_w_uv, MLA_HEADS, MLA_V_DIM), 1, 2).astype(BF16)
    eye = jnp.eye(len(POOL_WINDOWS), dtype=F32)
    wpool = jnp.einsum("gh,dgij->dgihj", eye, w_pool).reshape(dep, POOL_DIM, POOL_DIM).astype(BF16)
    cos_g, sin_g, cos_m, sin_m = _rope_tables(id_rows, nl)
    tri = np.triu(np.ones((TM, TM), np.float32), 1)
    wr_pad = jnp.pad(w_router, ((0, 0), (0, LANES - w_router.shape[1])))
    wr_hi = wr_pad.astype(BF16)
    wr_split = jnp.concatenate([wr_hi, (wr_pad - wr_hi.astype(F32)).astype(BF16)], axis=1)

    def row3(a):
        return a.reshape(dep, 1, a.shape[-1])

    return {
        "w_in": w_in_p,
        "norm1": row3(norm1), "norm2": row3(norm2),
        "gq": row3(_pad_heads(gqa_q_norm * (GQA_SCALE * LOG2E), 1, HEAD_DIM)),
        "gk": row3(_pad_heads(gqa_k_norm, 1, HEAD_DIM)),
        "mqn": row3(mla_q_norm), "mkvn": row3(mla_kv_norm),
        "qkq": row3(_pad_heads(mla_qk_q_norm * (MLA_SCALE * LOG2E), 1, MLA_QK_DIM)),
        "qkk": row3(_pad_heads(mla_qk_k_norm, 1, MLA_QK_DIM)),
        "wuq": _pad_heads(mla_w_uq, MLA_HEADS, MLA_QK_DIM).astype(BF16),
        "wuk": _pad_heads(mla_w_uk, MLA_HEADS, MLA_NOPE_DIM).astype(BF16),
        "wgvt": wgvt, "wuvt": wuvt,
        "cos_g": cos_g, "sin_g": sin_g, "cos_m": cos_m, "sin_m": sin_m,
        "conv_w": jnp.pad(conv_w, ((0, 0), (0, SUBLANES - conv_w.shape[1]), (0, 0))),
        "band": _pool_band(),
        "wpool": wpool, "pscale": row3(pool_scale),
        "w_out": w_out.astype(BF16),
        "wr": wr_split, "br": b_router.reshape(-1, 1),
        "tri": jnp.asarray(tri, dtype=BF16),
    }


def _moe_plan(counts):
    counts = counts.astype(I32)
    padded = ((counts + BM - 1) // BM) * BM
    pad_end = jnp.cumsum(padded)
    pad_start = pad_end - padded
    return pad_start, pad_end, padded


def kernel(x, c, ctx, c_ctx, w_mod, b_mod, norm1, norm2, w_in, conv_w, w_pool, pool_scale,
           gqa_q_norm, gqa_k_norm, mla_q_norm, mla_kv_norm, mla_w_uq, mla_w_uk, mla_w_uv,
           mla_qk_q_norm, mla_qk_k_norm, w_out, w_router, b_router, w_gate, w_up, w_down):
    nb, nl, d = x.shape
    nc = ctx.shape[1]
    depth = w_mod.shape[0]
    assert nc % TM == 0 and nl % TM == 0 and nl % GRID_W == 0 and nb < SUBLANES
    lt, nct = nl // TM, nc // TM
    wide = nl % INPROJ_ROWS == 0 and (nb * nc) % INPROJ_ROWS == 0
    geo = {"nbatch": nb, "nc": nc, "nl": nl, "lt": lt, "nct": nct,
           "n_lat_tiles": nb * lt, "n_tiles": nb * (lt + nct),
           "inproj_rows": INPROJ_ROWS if wide else TM}

    p = _prep_params(w_in, norm1, norm2, conv_w, w_pool, pool_scale, gqa_q_norm, gqa_k_norm,
                     mla_q_norm, mla_kv_norm, mla_w_uq, mla_w_uk, mla_w_uv, mla_qk_q_norm,
                     mla_qk_k_norm, w_out, w_router, b_router, geo["inproj_rows"], nl)
    cvec = jnp.concatenate([c, c_ctx[None, :], jnp.zeros((SUBLANES - nb - 1, d), F32)], axis=0)
    mods = _adaln(cvec, w_mod, b_mod)

    xa = jnp.concatenate([x.reshape(nb * nl, d), ctx.reshape(nb * nc, d)], axis=0)
    for i in range(depth):
        last = i == depth - 1
        mix = _inproj(xa, mods, i, p, geo)
        yg = _attention(mix["qg"], mix["kg"], mix["vgt"], True, geo, False)
        ym = _attention(mix["qm"], mix["km"], mix["vmt"], False, geo, False)
        if last:
            n_tiles, ygc, ymc = geo["n_lat_tiles"], yg, ym
        else:
            n_tiles = geo["n_tiles"]
            ygc = _attention(mix["qg"], mix["kg"], mix["vgt"], True, geo, True)
            ymc = _attention(mix["qm"], mix["km"], mix["vmt"], False, geo, True)
        xn, h2, info, cnt = _outproj(xa, mix, yg, ygc, ym, ymc, mods, i, p, geo, n_tiles)
        n_blocks = -(-2 * n_tiles * TM // BM) + N_EXPERTS
        n_slots = n_blocks * BM
        pad_start, pad_end, padded = _moe_plan(cnt[:, 0])
        n_used = (pad_end[-1:] // BM).astype(I32)
        block_row0 = jnp.arange(n_blocks, dtype=I32) * BM
        block_exp = jnp.minimum(jnp.sum((pad_end[None, :] <= block_row0[:, None]).astype(I32), axis=1),
                                N_EXPERTS - 1)
        dest, xs = _scatter(info, h2, pad_start, jnp.maximum(pad_end - BM, 0),
                            (padded > 0).astype(I32), n_used, n_slots)
        y = _experts(xs, block_exp, n_used, w_gate, w_up, w_down, i)
        xa = _combine(dest, info, xn, mods, y, i, geo)
    return xa.reshape(nb, nl, d)
```

```python
import functools
import math

import numpy as np
import jax
import jax.numpy as jnp
from jax import lax
from jax.experimental import pallas as pl
from jax.experimental.pallas import tpu as pltpu

F32 = jnp.float32
BF16 = jnp.bfloat16
I32 = jnp.int32

GRID_W = 64
CONV_DIM = 256
POOL_DIM = 256
POOL_WINDOWS = (2, 4, 8, 16)
HEAD_DIM = 64
GQA_HEADS = 4
GQA_KV_HEADS = 2
MLA_HEADS = 4
MLA_NOPE_DIM = 64
MLA_ROPE_DIM = 32
MLA_QK_DIM = MLA_NOPE_DIM + MLA_ROPE_DIM
MLA_V_DIM = 64
MLA_Q_RANK = 256
MLA_KV_RANK = 128
N_EXPERTS = 16
EXPERTS_PER_GROUP = 4
ROPE_THETA = 10000.0
NORM_EPS = 1e-6
LOG2E = 1.4426950408889634
GQA_SCALE = HEAD_DIM ** -0.5
MLA_SCALE = MLA_QK_DIM ** -0.5

LANES = 128
SUBLANES = 8
BF16_ROWS = 16
VMEM_LIMIT = 56 * 1024 * 1024

TM = 256
ATT_ROWS = 2048
INPROJ_ROWS = 512
BM = 512
V_ROWS = HEAD_DIM + BF16_ROWS
ATT_PAIR_UNROLL = 4
POOL_EXT = 512
POOL_HALO = 8

ZC_B, ZC_C, ZC_U, ZC_P = 0, 256, 512, 768
ZC_GQ = 1024
ZC_GK = ZC_GQ + GQA_HEADS * LANES
ZC_MQ = ZC_GK + GQA_KV_HEADS * LANES
ZC_MKV = ZC_MQ + MLA_Q_RANK
ZC_MKR = ZC_MKV + MLA_KV_RANK
ZC_END = ZC_MKR + LANES

HIGHEST = lax.Precision.HIGHEST


def _cparams(sem, vmem=VMEM_LIMIT):
    return pltpu.CompilerParams(dimension_semantics=sem, vmem_limit_bytes=vmem)


def _sigmoid(v):
    return 1.0 / (1.0 + jnp.exp(-v))


def _adaln_kernel(c_ref, w_ref, b_ref, o_ref):
    c = c_ref[...]
    s = c * _sigmoid(c)
    o_ref[0] = jnp.dot(s, w_ref[0], preferred_element_type=F32, precision=HIGHEST) + b_ref[0]


def _adaln(cvec, w_mod, b_mod):
    depth, d, n6 = w_mod.shape
    tn = 1536 if n6 % 1536 == 0 else n6
    rows = cvec.shape[0]
    return pl.pallas_call(
        _adaln_kernel,
        grid=(depth, n6 // tn),
        in_specs=[
            pl.BlockSpec((rows, d), lambda i, n: (0, 0)),
            pl.BlockSpec((1, d, tn), lambda i, n: (i, 0, n)),
            pl.BlockSpec((1, 1, tn), lambda i, n: (i, 0, n)),
        ],
        out_specs=pl.BlockSpec((1, rows, tn), lambda i, n: (i, 0, n)),
        out_shape=jax.ShapeDtypeStruct((depth, rows, n6), F32),
        compiler_params=_cparams(("arbitrary", "arbitrary")),
        name="adaln",
    )(cvec, w_mod, b_mod.reshape(depth, 1, n6))


def _tile_geometry(j, geo):
    lt, nct, n_lat = geo["lt"], geo["nct"], geo["n_lat_tiles"]
    is_ctx = j >= n_lat
    jc = j - n_lat
    sample = jnp.where(is_ctx, jc // nct, j // lt)
    jt = jnp.where(is_ctx, jc % nct, j % lt)
    return {
        "is_ctx": is_ctx,
        "mod_row": jnp.where(is_ctx, geo["nbatch"], sample),
        "first": jt == 0,
        "last": jt == jnp.where(is_ctx, nct, lt) - 1,
        "pos0": jt * TM,
        "seg_len": jnp.where(is_ctx, geo["nc"], geo["nl"]),
    }


def _norm_rope(slab, gain, cos, sin, n_valid, first_half, half):
    ssq = jnp.dot((slab * slab).astype(BF16), jnp.ones((LANES, LANES), BF16), preferred_element_type=F32)
    y = slab * lax.rsqrt(ssq * (1.0 / n_valid) + NORM_EPS) * gain
    partner = jnp.where(first_half, pltpu.roll(y, LANES - half, 1), pltpu.roll(y, half, 1))
    return y * cos + partner * sin


def _inproj_kernel(xl_ref, xc_ref, mods_ref, n1_ref, cg_ref, sg_ref, cm_ref, sm_ref, win_ref,
                   gq_ref, gk_ref, mqn_ref, mkvn_ref, qkq_ref, qkk_ref, wuq_ref, wuk_ref,
                   wgvt_ref, wuvt_ref,
                   cb_ref, cv_ref, zp_ref, qg_ref, kg_ref, vgt_ref, qm_ref, km_ref, vmt_ref,
                   *, d_model, nbatch, n_lat_steps, lat_steps_per_sample):
    d = d_model
    step = pl.program_id(0)
    row = jnp.where(step >= n_lat_steps, nbatch, step // lat_steps_per_sample)
    shift = mods_ref[0, pl.ds(row, 1), 0:d]
    scale = mods_ref[0, pl.ds(row, 1), d:2 * d]
    x = jnp.where(step >= n_lat_steps, xc_ref[...], xl_ref[...])
    ms = jnp.mean(x * x, axis=-1, keepdims=True)
    h = ((x * lax.rsqrt(ms + NORM_EPS)) * (n1_ref[0] * (1.0 + scale)) + shift).astype(BF16)
    nt_dims = (((1,), (1,)), ((), ()))

    def proj(c0, width):
        return jnp.dot(h, win_ref[0, :, c0:c0 + width], preferred_element_type=F32)

    def store_transposed(vt_ref, vt):
        srow = lax.broadcasted_iota(I32, vt.shape, 0)
        vt = jnp.where(jnp.bitwise_and(srow, LANES - 1) == HEAD_DIM, 1.0, vt).astype(BF16)
        for c in range(vt_ref.shape[0]):
            vt_ref[c] = vt[:, c * TM:(c + 1) * TM]

    tm = x.shape[0]
    lane = lax.broadcasted_iota(I32, (tm, LANES), 1)

    cm, sm = cm_ref[...], sm_ref[...]
    m_first = lane < MLA_NOPE_DIM + MLA_ROPE_DIM // 2
    zq = proj(ZC_MQ, MLA_Q_RANK)
    cq = zq * lax.rsqrt(jnp.mean(zq * zq, axis=-1, keepdims=True) + NORM_EPS) * mqn_ref[0]
    qpre = jnp.dot(cq.astype(BF16), wuq_ref[0], preferred_element_type=F32)
    zk = proj(ZC_MKV, MLA_KV_RANK + LANES)
    zkv, zkr = zk[:, 0:MLA_KV_RANK], zk[:, MLA_KV_RANK:]
    ckv = zkv * lax.rsqrt(jnp.mean(zkv * zkv, axis=-1, keepdims=True) + NORM_EPS) * mkvn_ref[0]
    ckv = ckv.astype(BF16)
    kvp = jnp.dot(ckv, wuk_ref[0], preferred_element_type=F32)
    store_transposed(vmt_ref, lax.dot_general(wuvt_ref[0], ckv, nt_dims, preferred_element_type=F32))
    for hd in range(MLA_HEADS):
        sl = slice(hd * LANES, (hd + 1) * LANES)
        qm_ref[:, sl] = _norm_rope(qpre[:, sl], qkq_ref[0], cm, sm, MLA_QK_DIM,
                                   m_first, MLA_ROPE_DIM // 2).astype(BF16)
        km_ref[:, sl] = _norm_rope(kvp[:, sl] + zkr, qkk_ref[0], cm, sm, MLA_QK_DIM,
                                   m_first, MLA_ROPE_DIM // 2).astype(BF16)

    cg, sg = cg_ref[...], sg_ref[...]
    g_first = lane < HEAD_DIM // 2
    zg = proj(ZC_GQ, (GQA_HEADS + GQA_KV_HEADS) * LANES)
    for hd in range(GQA_HEADS):
        qg_ref[:, hd * LANES:(hd + 1) * LANES] = _norm_rope(
            zg[:, hd * LANES:(hd + 1) * LANES], gq_ref[0], cg, sg, HEAD_DIM, g_first,
            HEAD_DIM // 2).astype(BF16)
    for hd in range(GQA_KV_HEADS):
        slab = zg[:, (GQA_HEADS + hd) * LANES:(GQA_HEADS + hd + 1) * LANES]
        kg_ref[:, hd * LANES:(hd + 1) * LANES] = _norm_rope(
            slab, gk_ref[0], cg, sg, HEAD_DIM, g_first, HEAD_DIM // 2).astype(BF16)
    store_transposed(vgt_ref, lax.dot_general(wgvt_ref[0], h, nt_dims, preferred_element_type=F32))

    zc = proj(ZC_B, 3 * CONV_DIM + POOL_DIM)
    cb_ref[...] = zc[:, ZC_B:ZC_B + CONV_DIM].astype(BF16)
    cv_ref[...] = (zc[:, ZC_C:ZC_C + CONV_DIM] * zc[:, ZC_U:ZC_U + CONV_DIM]).astype(BF16)
    zp_ref[...] = zc[:, ZC_P:ZC_P + POOL_DIM]


def _row_sources(src, rows_per_step, geo):
    lat, lat_row0, cx, cx_row0 = src
    n_lat = geo["nbatch"] * geo["nl"] // rows_per_step
    n_ctx = geo["nbatch"] * geo["nc"] // rows_per_step
    lat0, cx0 = lat_row0 // rows_per_step, cx_row0 // rows_per_step
    d = lat.shape[1]
    return [pl.BlockSpec((rows_per_step, d), lambda j: (lat0 + jnp.minimum(j, n_lat - 1), 0)),
            pl.BlockSpec((rows_per_step, d), lambda j: (cx0 + jnp.clip(j - n_lat, 0, n_ctx - 1), 0))]


def _inproj(src, mods, layer, p, geo):
    d = src[0].shape[1]
    r = geo["n_tiles"] * TM
    tmi = geo["inproj_rows"]
    n_lat_steps = geo["nbatch"] * geo["nl"] // tmi
    lat_steps_per_sample = geo["nl"] // tmi
    rowmap = lambda j: (j, 0)
    posmap = lambda j: (jnp.where(j >= n_lat_steps, lat_steps_per_sample, j % lat_steps_per_sample), 0)
    lay3 = lambda j: (layer, 0, 0)

    def full3(a):
        return pl.BlockSpec((1,) + a.shape[1:], lay3)

    outs = [("cb", CONV_DIM, BF16, False), ("cv", CONV_DIM, BF16, False), ("zp", POOL_DIM, F32, False),
            ("qg", GQA_HEADS * LANES, BF16, False), ("kg", GQA_KV_HEADS * LANES, BF16, False),
            ("vgt", GQA_KV_HEADS * LANES, BF16, True), ("qm", MLA_HEADS * LANES, BF16, False),
            ("km", MLA_HEADS * LANES, BF16, False), ("vmt", MLA_HEADS * LANES, BF16, True)]

    def out_spec(w, transposed):
        if transposed:
            return pl.BlockSpec((tmi // TM, w, TM), lambda j: (j, 0, 0))
        return pl.BlockSpec((tmi, w), rowmap)

    def out_shape(w, dt, transposed):
        return jax.ShapeDtypeStruct((r // TM, w, TM) if transposed else (r, w), dt)

    res = pl.pallas_call(
        functools.partial(_inproj_kernel, d_model=d, nbatch=geo["nbatch"], n_lat_steps=n_lat_steps,
                          lat_steps_per_sample=lat_steps_per_sample),
        grid=(r // tmi,),
        in_specs=_row_sources(src, tmi, geo) + [
            full3(mods), full3(p["norm1"]),
            pl.BlockSpec((tmi, LANES), posmap), pl.BlockSpec((tmi, LANES), posmap),
            pl.BlockSpec((tmi, LANES), posmap), pl.BlockSpec((tmi, LANES), posmap),
            full3(p["w_in"]), full3(p["gq"]), full3(p["gk"]), full3(p["mqn"]), full3(p["mkvn"]),
            full3(p["qkq"]), full3(p["qkk"]), full3(p["wuq"]), full3(p["wuk"]),
            full3(p["wgvt"]), full3(p["wuvt"]),
        ],
        out_specs=[out_spec(w, t) for _, w, _, t in outs],
        out_shape=[out_shape(w, dt, t) for _, w, dt, t in outs],
        compiler_params=_cparams(("arbitrary",)),
        name="inproj",
    )(src[0], src[2], mods, p["norm1"], p["cos_g"], p["sin_g"], p["cos_m"], p["sin_m"], p["w_in"],
      p["gq"], p["gk"], p["mqn"], p["mkvn"], p["qkq"], p["qkk"], p["wuq"], p["wuk"],
      p["wgvt"], p["wuvt"])
    return {name: a for (name, _, _, _), a in zip(outs, res)}


def _attn_kernel(q_ref, kl_ref, kc_ref, vtl_ref, vtc_ref, o_ref, q_st, s_buf, p_buf, a_buf, m_ref, acc_ref,
                 *, shared_kv, n_lat, n_ctx, tk):
    tq = q_ref.shape[0]
    n_steps = n_lat + n_ctx
    if shared_kv:
        q_st[0] = jnp.concatenate([q_ref[:, 0:LANES], q_ref[:, LANES:2 * LANES]], axis=0)
        cols = [0]
    else:
        q_st[0] = q_ref[:, 0:LANES]
        q_st[1] = q_ref[:, LANES:2 * LANES]
        cols = [0, LANES]
    for si, col in enumerate(cols):
        q_s, m_s, acc_s = q_st.at[si], m_ref.at[si], acc_ref.at[si]
        m_s[...] = jnp.full(m_s.shape, -1e30, F32)
        acc_s[...] = jnp.zeros(acc_s.shape, F32)
        p_buf[1] = jnp.zeros(p_buf.shape[1:], BF16)
        a_buf[1] = jnp.ones(a_buf.shape[1:], F32)

        def pick(i, lat_fn, ctx_fn):
            ctx = ctx_fn(jnp.clip(i - n_lat, 0, n_ctx - 1))
            if n_lat == 0:
                return ctx
            return jnp.where(i >= n_lat, ctx, lat_fn(jnp.clip(i, 0, n_lat - 1)))

        def scores(i, slot, q_s=q_s, col=col):
            k = pick(i, lambda t: kl_ref[pl.ds(pl.multiple_of(t * tk, tk), tk), col:col + LANES],
                     lambda t: kc_ref[pl.ds(pl.multiple_of(t * tk, tk), tk), col:col + LANES])
            s_buf[slot] = lax.dot_general(k, q_s[...], (((1,), (1,)), ((), ())),
                                          preferred_element_type=F32)

        def softmax(slot, m_s=m_s):
            s = s_buf[slot]
            m_old = m_s[...]
            m_new = jnp.maximum(m_old, jnp.max(s, axis=0, keepdims=True))
            a_buf[slot] = jnp.exp2(m_old - m_new)
            p_buf[slot] = jnp.exp2(s - m_new[0:1]).astype(BF16)
            m_s[...] = m_new

        def accumulate(i, slot, acc_s=acc_s, col=col):
            vt = pick(i, lambda t: vtl_ref[t, col:col + V_ROWS, :],
                      lambda t: vtc_ref[t, col:col + V_ROWS, :])
            acc_s[...] = a_buf[slot][0:1] * acc_s[...] + jnp.dot(vt, p_buf[slot],
                                                                 preferred_element_type=F32)

        scores(0, 0)

        def pair(t, carry):
            i = 2 * t
            scores(i + 1, 1)
            softmax(0)
            accumulate(i - 1, 1)
            scores(i + 2, 0)
            softmax(1)
            accumulate(i, 0)
            return carry

        n_pairs = (n_steps - 1) // 2
        lax.fori_loop(0, n_pairs, pair, 0, unroll=ATT_PAIR_UNROLL)
        for i in range(2 * n_pairs, n_steps):
            if i + 1 < n_steps:
                scores(i + 1, (i + 1) % 2)
            softmax(i % 2)
            if i >= 1:
                accumulate(i - 1, (i - 1) % 2)
        accumulate(n_steps - 1, (n_steps - 1) % 2)
    def finish(acc_t):
        o_t = acc_t * (1.0 / acc_t[HEAD_DIM:HEAD_DIM + 1])
        pad = jnp.zeros((LANES - V_ROWS, o_t.shape[1]), F32)
        return jnp.concatenate([o_t, pad], axis=0).T

    if shared_kv:
        o_both = finish(acc_ref[0])
        o0, o1 = o_both[0:tq], o_both[tq:2 * tq]
    else:
        o0, o1 = finish(acc_ref[0]), finish(acc_ref[1])
    lane = lax.broadcasted_iota(I32, (tq, LANES), 1)
    o_ref[...] = jnp.where(lane < HEAD_DIM, o0, pltpu.roll(o1, HEAD_DIM, 1)).astype(BF16)


def _attention(q, k, vt, shared_kv, geo, ctx_queries):
    nb, nc, nl = geo["nbatch"], geo["nc"], geo["nl"]
    kw = LANES if shared_kv else 2 * LANES
    tk = TM
    ctx_blk0 = nb * nl // nc
    ctx_k = pl.BlockSpec((nc, kw), lambda b, g, t: (ctx_blk0 + b, g))
    ctx_vt = pl.BlockSpec((nc // tk, kw, tk), lambda b, g, t: (ctx_blk0 + b, g, 0))
    if ctx_queries:
        tq, q_per, q_blk0, n_lat = nc, 1, ctx_blk0, 0
        lat_k, lat_vt = ctx_k, ctx_vt
    else:
        tq = min(ATT_ROWS, nl) // (2 if shared_kv else 1)
        q_per, q_blk0, n_lat = nl // tq, 0, nl // tk
        lat_k = pl.BlockSpec((nl, kw), lambda b, g, t: (b, g))
        lat_vt = pl.BlockSpec((nl // tk, kw, tk), lambda b, g, t: (b, g, 0))
    n_streams, rows = (1, 2 * tq) if shared_kv else (2, tq)
    return pl.pallas_call(
        functools.partial(_attn_kernel, shared_kv=shared_kv, n_lat=n_lat, n_ctx=nc // tk, tk=tk),
        grid=(nb, 2, q_per),
        in_specs=[
            pl.BlockSpec((tq, 2 * LANES), lambda b, g, t: (q_blk0 + b * q_per + t, g)),
            lat_k, ctx_k, lat_vt, ctx_vt,
        ],
        out_specs=pl.BlockSpec((tq, LANES), lambda b, g, t: (b * q_per + t, g)),
        out_shape=jax.ShapeDtypeStruct((nb * q_per * tq, 2 * LANES), BF16),
        scratch_shapes=[pltpu.VMEM((n_streams, rows, LANES), BF16),
                        pltpu.VMEM((2, tk, rows), F32),
                        pltpu.VMEM((2, tk, rows), BF16),
                        pltpu.VMEM((2, SUBLANES, rows), F32),
                        pltpu.VMEM((n_streams, SUBLANES, rows), F32),
                        pltpu.VMEM((n_streams, V_ROWS, rows), F32)],
        compiler_params=_cparams(("arbitrary", "arbitrary", "arbitrary")),
        name=("attn_gqa" if shared_kv else "attn_mla") + ("_ctx" if ctx_queries else ""),
    )(q, k, k, vt, vt)


def _top2_sum(a, b, c, d):
    hi_ab, lo_ab = jnp.maximum(a, b), jnp.minimum(a, b)
    hi_cd, lo_cd = jnp.maximum(c, d), jnp.minimum(c, d)
    first = jnp.maximum(hi_ab, hi_cd)
    second = jnp.maximum(jnp.minimum(hi_ab, hi_cd), jnp.maximum(lo_ab, lo_cd))
    return first + second


def _outproj_kernel(xl_ref, xc_ref, cb_ref, cv_ref, cvp_ref, cvn_ref, zp_ref, zpp_ref, zpn_ref,
                    ygl_ref, ygc_ref, yml_ref, ymc_ref,
                    mods_ref, convw_ref, band_ref, wpool_ref, pscale_ref,
                    wout_ref, n2_ref, wr_ref, br_ref, tri_ref,
                    xo_ref, h2_ref, info_ref, cnt_ref,
                    *, d_model, geo):
    tile = pl.program_id(0)
    d = d_model
    tm = xl_ref.shape[0]
    tg = _tile_geometry(tile, geo)
    is_ctx = tg["is_ctx"]
    keep_prev = jnp.where(tg["first"], 0.0, 1.0)
    keep_next = jnp.where(tg["last"], 0.0, 1.0)
    row = tg["mod_row"]
    gate1 = mods_ref[0, pl.ds(row, 1), 2 * d:3 * d]
    shift2 = mods_ref[0, pl.ds(row, 1), 3 * d:4 * d]
    scale2 = mods_ref[0, pl.ds(row, 1), 4 * d:5 * d]

    v = cv_ref[...].astype(F32)
    prev_row = cvp_ref[...].astype(F32)[BF16_ROWS - 1:BF16_ROWS] * keep_prev
    next_row = cvn_ref[...].astype(F32)[0:1] * keep_next
    rid = lax.broadcasted_iota(I32, (tm, CONV_DIM), 0)
    vm1 = jnp.where(rid == 0, prev_row, pltpu.roll(v, 1, 0))
    vp1 = jnp.where(rid == tm - 1, next_row, pltpu.roll(v, tm - 1, 0))
    cw = convw_ref[0]
    y_conv = cb_ref[...].astype(F32) * (vm1 * cw[0:1] + v * cw[1:2] + vp1 * cw[2:3])

    zp = zp_ref[...]
    ext = jnp.concatenate(
        [zpp_ref[...] * keep_prev, zp, zpn_ref[...] * keep_next,
         jnp.zeros((POOL_EXT - tm - 2 * POOL_HALO, POOL_DIM), F32)], axis=0).astype(BF16)
    ext_a, ext_b = ext[:, 0:LANES], ext[:, LANES:2 * LANES]
    lane = lax.broadcasted_iota(I32, (tm, LANES), 1)
    low = lane < POOL_DIM // 4
    sum_a = jnp.where(low, jnp.dot(band_ref[0], ext_a, preferred_element_type=F32),
                      jnp.dot(band_ref[1], ext_a, preferred_element_type=F32))
    sum_b = jnp.where(low, jnp.dot(band_ref[2], ext_b, preferred_element_type=F32),
                      jnp.dot(band_ref[3], ext_b, preferred_element_type=F32))
    sums = jnp.concatenate([sum_a, sum_b], axis=1)
    lane_p = lax.broadcasted_iota(I32, (tm, POOL_DIM), 1)
    half_w = jnp.left_shift(1, jnp.right_shift(lane_p, int(math.log2(POOL_DIM // 4))))
    pos = tg["pos0"] + rid
    cnt = (jnp.minimum(pos + half_w, tg["seg_len"]) - jnp.maximum(pos - half_w, 0)).astype(F32)
    dlt = sums / cnt - zp
    y_pool = jnp.dot(dlt.astype(BF16), wpool_ref[0], preferred_element_type=F32) * pscale_ref[0]

    y_gqa = jnp.where(is_ctx, ygc_ref[...], ygl_ref[...])
    y_mla = jnp.where(is_ctx, ymc_ref[...], yml_ref[...])
    ycat = jnp.concatenate([y_conv.astype(BF16), y_pool.astype(BF16), y_gqa, y_mla], axis=1)
    y = jnp.dot(ycat, wout_ref[0], preferred_element_type=F32)
    xn = jnp.where(is_ctx, xc_ref[...], xl_ref[...]) + gate1 * y
    xo_ref[...] = xn
    ms = jnp.mean(xn * xn, axis=-1, keepdims=True)
    h2 = (xn * lax.rsqrt(ms + NORM_EPS)) * (n2_ref[0] * (1.0 + scale2)) + shift2
    h2_ref[...] = h2

    h_hi = h2.astype(BF16)
    h_lo = (h2 - h_hi.astype(F32)).astype(BF16)
    wr = wr_ref[...]
    part = jnp.dot(h_hi, wr, preferred_element_type=F32)
    small = part[:, LANES:] + jnp.dot(h_lo, wr[:, 0:LANES], preferred_element_type=F32)
    logits = (part[:, 0:LANES] + small).T[0:N_EXPERTS]
    scores = _sigmoid(logits)
    sel = scores + br_ref[...]
    epg = EXPERTS_PER_GROUP
    n_groups = N_EXPERTS // epg
    srow = [sel[e:e + 1] for e in range(N_EXPERTS)]
    crow = [scores[e:e + 1] for e in range(N_EXPERTS)]
    gscore = [_top2_sum(*srow[g * epg:(g + 1) * epg]) for g in range(n_groups)]
    gbest = jnp.zeros_like(gscore[0]).astype(I32)
    best = gscore[0]
    for g in range(1, n_groups):
        upd = gscore[g] > best
        gbest = jnp.where(upd, g, gbest)
        best = jnp.where(upd, gscore[g], best)

    def pick(rows_, j):
        out = rows_[(n_groups - 1) * epg + j]
        for g in range(n_groups - 2, -1, -1):
            out = jnp.where(gbest == g, rows_[g * epg + j], out)
        return out

    sv = [pick(srow, j) for j in range(epg)]
    cv_ = [pick(crow, j) for j in range(epg)]
    i1 = jnp.zeros_like(gbest)
    b1 = sv[0]
    for j in range(1, epg):
        upd = sv[j] > b1
        i1 = jnp.where(upd, j, i1)
        b1 = jnp.where(upd, sv[j], b1)
    i2 = jnp.zeros_like(gbest)
    b2 = jnp.full_like(b1, -jnp.inf)
    for j in range(epg):
        upd = jnp.logical_and(i1 != j, sv[j] > b2)
        i2 = jnp.where(upd, j, i2)
        b2 = jnp.where(upd, sv[j], b2)
    s1 = cv_[epg - 1]
    s2 = cv_[epg - 1]
    for j in range(epg - 2, -1, -1):
        s1 = jnp.where(i1 == j, cv_[j], s1)
        s2 = jnp.where(i2 == j, cv_[j], s2)
    inv = 1.0 / (s1 + s2)
    e1 = gbest * epg + i1
    e2 = gbest * epg + i2

    @pl.when(tile == 0)
    def _():
        cnt_ref[...] = jnp.zeros_like(cnt_ref)

    erow = lax.broadcasted_iota(I32, (N_EXPERTS, tm), 0)
    hit1 = erow == e1
    hit2 = erow == e2
    onehot = jnp.where(hit1, 1.0, 0.0) + jnp.where(hit2, 1.0, 0.0)
    before = jnp.dot(onehot.astype(BF16), tri_ref[...], preferred_element_type=F32)
    tot = cnt_ref[:, 0:1] + before
    rank1 = jnp.sum(jnp.where(hit1, tot, 0.0), axis=0, keepdims=True).astype(I32)
    rank2 = jnp.sum(jnp.where(hit2, tot, 0.0), axis=0, keepdims=True).astype(I32)
    cnt_ref[...] = cnt_ref[...] + jnp.sum(onehot, axis=1, keepdims=True)

    w1 = lax.bitcast_convert_type(s1 * inv, I32)
    w2 = lax.bitcast_convert_type(s2 * inv, I32)
    irow = lax.broadcasted_iota(I32, (SUBLANES, tm), 0)
    info = jnp.where(irow == 0, e1, jnp.where(irow == 1, e2, jnp.where(
        irow == 2, rank1, jnp.where(irow == 3, rank2, jnp.where(
            irow == 4, w1, jnp.where(irow == 5, w2, 0))))))
    info_ref[0] = info


def _outproj(src, mix, yg, ygc, ym, ymc, mods, layer, p, geo, n_tiles):
    d = src[0].shape[1]
    nt_all, n_lat = geo["n_tiles"], geo["n_lat_tiles"]
    r = n_tiles * TM
    nt = n_tiles
    rowmap = lambda j: (j, 0)
    latmap = lambda j: (jnp.minimum(j, n_lat - 1), 0)
    ctxmap = lambda j: (jnp.clip(j - n_lat, 0, ygc.shape[0] // TM - 1), 0)
    lay3 = lambda j: (layer, 0, 0)
    c0 = lambda j: (0, 0)
    c3 = lambda j: (0, 0, 0)
    bf_blocks = TM // BF16_ROWS
    f_blocks = TM // SUBLANES

    def full3(a):
        return pl.BlockSpec((1,) + a.shape[1:], lay3)

    return pl.pallas_call(
        functools.partial(_outproj_kernel, d_model=d, geo=geo),
        grid=(n_tiles,),
        in_specs=_row_sources(src, TM, geo) + [
            pl.BlockSpec((TM, CONV_DIM), rowmap),
            pl.BlockSpec((TM, CONV_DIM), rowmap),
            pl.BlockSpec((BF16_ROWS, CONV_DIM), lambda j: (jnp.maximum(j * bf_blocks - 1, 0), 0)),
            pl.BlockSpec((BF16_ROWS, CONV_DIM),
                         lambda j: (jnp.minimum((j + 1) * bf_blocks, nt_all * bf_blocks - 1), 0)),
            pl.BlockSpec((TM, POOL_DIM), rowmap),
            pl.BlockSpec((SUBLANES, POOL_DIM), lambda j: (jnp.maximum(j * f_blocks - 1, 0), 0)),
            pl.BlockSpec((SUBLANES, POOL_DIM),
                         lambda j: (jnp.minimum((j + 1) * f_blocks, nt_all * f_blocks - 1), 0)),
            pl.BlockSpec((TM, 2 * LANES), latmap),
            pl.BlockSpec((TM, 2 * LANES), ctxmap),
            pl.BlockSpec((TM, 2 * LANES), latmap),
            pl.BlockSpec((TM, 2 * LANES), ctxmap),
            full3(mods), full3(p["conv_w"]),
            pl.BlockSpec(p["band"].shape, c3),
            full3(p["wpool"]), full3(p["pscale"]), full3(p["w_out"]), full3(p["norm2"]),
            pl.BlockSpec(p["wr"].shape, c0), pl.BlockSpec(p["br"].shape, c0),
            pl.BlockSpec(p["tri"].shape, c0),
        ],
        out_specs=[
            pl.BlockSpec((TM, d), rowmap),
            pl.BlockSpec((TM, d), rowmap),
            pl.BlockSpec((1, SUBLANES, TM), lambda j: (j, 0, 0)),
            pl.BlockSpec((N_EXPERTS, LANES), c0),
        ],
        out_shape=[
            jax.ShapeDtypeStruct((r, d), F32),
            jax.ShapeDtypeStruct((r, d), F32),
            jax.ShapeDtypeStruct((nt, SUBLANES, TM), I32),
            jax.ShapeDtypeStruct((N_EXPERTS, LANES), F32),
        ],
        compiler_params=_cparams(("arbitrary",)),
        name="outproj",
    )(src[0], src[2], mix["cb"], mix["cv"], mix["cv"], mix["cv"], mix["zp"], mix["zp"], mix["zp"],
      yg, ygc, ym, ymc, mods, p["conv_w"], p["band"], p["wpool"], p["pscale"], p["w_out"],
      p["norm2"], p["wr"], p["br"], p["tri"])


def _load_slots(block_ref, idx_ref, sem):
    n = block_ref.shape[2]
    copies = [pltpu.make_async_copy(block_ref.at[0, k], idx_ref.at[pl.ds(k * n, n)], sem) for k in range(2)]
    for cp in copies:
        cp.start()
    for cp in copies:
        cp.wait()


def _row_copies(idx_ref, n_rows, make):
    def body(t, carry):
        for u in range(SUBLANES):
            r = t * SUBLANES + u
            make(0, t, u, idx_ref[r]).start()
            make(1, t, u, idx_ref[n_rows + r]).start()
        return carry
    lax.fori_loop(0, n_rows // SUBLANES, body, 0)


def _scatter_kernel(ps_ref, zs_ref, has_ref, nu_ref, info_ref, h2_ref, dest_ref, xs_ref,
                    dsm_ref, zbuf_ref, stage_ref, sem_ref, *, first_spare, n_blocks, n_tiles):
    j = pl.program_id(0)
    tm = h2_ref.shape[0]

    @pl.when(j == 0)
    def _():
        zbuf_ref[...] = jnp.zeros_like(zbuf_ref)

        def zero_copy(start):
            return pltpu.make_async_copy(
                zbuf_ref, xs_ref.at[pl.ds(pl.multiple_of(start, BM), BM)], sem_ref.at[2])

        for act in ("start", "wait"):
            for e in range(N_EXPERTS):
                @pl.when(has_ref[e] > 0)
                def _():
                    getattr(zero_copy(zs_ref[e]), act)()
            for jb in range(first_spare, n_blocks):
                @pl.when(jb >= nu_ref[0])
                def _():
                    getattr(zero_copy(jb * BM), act)()

    info = info_ref[0]
    e1, e2 = info[0:1], info[1:2]
    d1, d2 = info[2:3], info[3:4]
    for e in range(N_EXPERTS):
        d1 = d1 + jnp.where(e1 == e, ps_ref[e], 0)
        d2 = d2 + jnp.where(e2 == e, ps_ref[e], 0)
    irow = lax.broadcasted_iota(I32, (SUBLANES, tm), 0)
    dest_ref[0] = jnp.where(irow == 0, d1, jnp.where(irow == 1, d2, 0))
    _load_slots(dest_ref, dsm_ref, sem_ref.at[3])

    par = j % 2
    stage_ref[par] = h2_ref[...].reshape(stage_ref.shape[1:])

    def row_copy(_, t, u, slot):
        return pltpu.make_async_copy(stage_ref.at[par, t, pl.ds(u, 1)], xs_ref.at[pl.ds(slot, 1)],
                                     sem_ref.at[par])

    _row_copies(dsm_ref, tm, row_copy)

    def retire(which):
        for _ in range(2):
            pltpu.make_async_copy(h2_ref, xs_ref.at[pl.ds(0, tm)], sem_ref.at[which]).wait()

    @pl.when(j > 0)
    def _():
        retire(1 - par)

    @pl.when(j == n_tiles - 1)
    def _():
        retire(par)


def _scatter(info, h2, pad_start, zero_start, has_rows, n_used, n_slots):
    r, d = h2.shape
    nt = r // TM
    return pl.pallas_call(
        functools.partial(_scatter_kernel, first_spare=-(-2 * r // BM), n_blocks=n_slots // BM,
                          n_tiles=nt),
        grid_spec=pltpu.PrefetchScalarGridSpec(
            num_scalar_prefetch=4,
            grid=(nt,),
            in_specs=[
                pl.BlockSpec((1, SUBLANES, TM), lambda j, *_: (j, 0, 0)),
                pl.BlockSpec((TM, d), lambda j, *_: (j, 0)),
            ],
            out_specs=[
                pl.BlockSpec((1, SUBLANES, TM), lambda j, *_: (j, 0, 0)),
                pl.BlockSpec(memory_space=pl.ANY),
            ],
            scratch_shapes=[
                pltpu.SMEM((2 * TM,), I32),
                pltpu.VMEM((BM, d), F32),
                pltpu.VMEM((2, TM // SUBLANES, SUBLANES, d), F32),
                pltpu.SemaphoreType.DMA((4,)),
            ],
        ),
        out_shape=[
            jax.ShapeDtypeStruct((nt, SUBLANES, TM), I32),
            jax.ShapeDtypeStruct((n_slots, d), F32),
        ],
        compiler_params=_cparams(("arbitrary",)),
        name="moe_scatter",
    )(pad_start, zero_start, has_rows, n_used, info, h2)


def _expert_kernel(be_ref, nu_ref, xs_ref, wg_ref, wu_ref, wd_ref, y_ref, wgb, wub, wdb):
    j = pl.program_id(0)

    @pl.when(j < nu_ref[0])
    def _():
        e = be_ref[j]
        prev = be_ref[jnp.maximum(j - 1, 0)]

        @pl.when(jnp.logical_or(j == 0, e != prev))
        def _():
            wgb[...] = wg_ref[0].astype(BF16)
            wub[...] = wu_ref[0].astype(BF16)
            wdb[...] = wd_ref[0].astype(BF16)

        x = xs_ref[...].astype(BF16)
        a = jnp.dot(x, wgb[...], preferred_element_type=F32)
        u = jnp.dot(x, wub[...], preferred_element_type=F32)
        hmid = (a * _sigmoid(a) * u).astype(BF16)
        y_ref[...] = jnp.dot(hmid, wdb[...], preferred_element_type=F32)

    @pl.when(j >= nu_ref[0])
    def _():
        y_ref[...] = jnp.zeros_like(y_ref)


def _experts(xs, block_exp, n_used, w_gate, w_up, w_down, layer):
    n_slots, d = xs.shape
    de = w_gate.shape[-1]
    nbm = n_slots // BM
    n_exp = w_gate.shape[1]

    def blk(j, be, nu):
        return (jnp.minimum(j, nu[0] - 1), 0)

    def wmap(j, be, nu):
        return (layer * n_exp + be[jnp.minimum(j, nu[0] - 1)], 0, 0)

    wg = w_gate.reshape((-1,) + w_gate.shape[2:])
    wu = w_up.reshape((-1,) + w_up.shape[2:])
    wd = w_down.reshape((-1,) + w_down.shape[2:])
    return pl.pallas_call(
        _expert_kernel,
        grid_spec=pltpu.PrefetchScalarGridSpec(
            num_scalar_prefetch=2,
            grid=(nbm,),
            in_specs=[
                pl.BlockSpec((BM, d), blk),
                pl.BlockSpec((1, d, de), wmap),
                pl.BlockSpec((1, d, de), wmap),
                pl.BlockSpec((1, de, d), wmap),
            ],
            out_specs=pl.BlockSpec((BM, d), lambda j, be, nu: (j, 0)),
            scratch_shapes=[
                pltpu.VMEM((d, de), BF16),
                pltpu.VMEM((d, de), BF16),
                pltpu.VMEM((de, d), BF16),
            ],
        ),
        out_shape=jax.ShapeDtypeStruct((n_slots, d), F32),
        compiler_params=_cparams(("arbitrary",)),
        name="moe_experts",
    )(block_exp, n_used, xs, wg, wu, wd)


def _combine_kernel(dest_ref, dest_next_ref, info_ref, x_ref, mods_ref, y_ref, o_ref,
                    dsm_ref, ybuf_ref, sem_ref, *, d_model, geo, n_steps):
    j = pl.program_id(0)
    d = d_model
    tm = x_ref.shape[0]
    row = _tile_geometry(j, geo)["mod_row"]
    gate2 = mods_ref[0, pl.ds(row, 1), 5 * d:6 * d]
    cur = j % 2

    def gather(idx_block_ref, buf):
        _load_slots(idx_block_ref, dsm_ref, sem_ref.at[2])

        def row_copy(k, t, u, slot):
            return pltpu.make_async_copy(y_ref.at[pl.ds(slot, 1)],
                                         ybuf_ref.at[buf, k, t, pl.ds(u, 1)], sem_ref.at[buf])

        _row_copies(dsm_ref, tm, row_copy)

    @pl.when(j == 0)
    def _():
        gather(dest_ref, 0)

    @pl.when(j + 1 < n_steps)
    def _():
        gather(dest_next_ref, 1 - cur)

    for k in range(2):
        pltpu.make_async_copy(y_ref.at[pl.ds(0, tm)], o_ref, sem_ref.at[cur]).wait()

    info = info_ref[0]
    wrow = lax.broadcasted_iota(I32, (LANES, tm), 0)
    w_lanes = jnp.where(wrow == 0, lax.bitcast_convert_type(info[4:5], F32),
                        jnp.where(wrow == 1, lax.bitcast_convert_type(info[5:6], F32), 0.0))
    w_rows = w_lanes.T
    y1 = ybuf_ref[cur, 0].reshape(tm, d)
    y2 = ybuf_ref[cur, 1].reshape(tm, d)
    o_ref[...] = x_ref[...] + gate2 * (w_rows[:, 0:1] * y1 + w_rows[:, 1:2] * y2)


def _combine(dest, info, xn, mods, y, layer, geo):
    r, d = xn.shape
    n_steps = r // TM
    return pl.pallas_call(
        functools.partial(_combine_kernel, d_model=d, geo=geo, n_steps=n_steps),
        grid=(n_steps,),
        in_specs=[
            pl.BlockSpec((1, SUBLANES, TM), lambda j: (j, 0, 0)),
            pl.BlockSpec((1, SUBLANES, TM), lambda j: (jnp.minimum(j + 1, n_steps - 1), 0, 0)),
            pl.BlockSpec((1, SUBLANES, TM), lambda j: (j, 0, 0)),
            pl.BlockSpec((TM, d), lambda j: (j, 0)),
            pl.BlockSpec((1,) + mods.shape[1:], lambda j: (layer, 0, 0)),
            pl.BlockSpec(memory_space=pl.ANY),
        ],
        out_specs=pl.BlockSpec((TM, d), lambda j: (j, 0)),
        out_shape=jax.ShapeDtypeStruct((n_steps * TM, d), F32),
        scratch_shapes=[
            pltpu.SMEM((2 * TM,), I32),
            pltpu.VMEM((2, 2, TM // SUBLANES, SUBLANES, d), F32),
            pltpu.SemaphoreType.DMA((3,)),
        ],
        compiler_params=_cparams(("arbitrary",)),
        name="moe_combine",
    )(dest, dest, info, xn, mods, y)


def _pad_heads(w, n_heads, width):
    lead = w.shape[:-1]
    w = w.reshape(lead + (n_heads, width))
    w = jnp.pad(w, [(0, 0)] * len(lead) + [(0, 0), (0, LANES - width)])
    return w.reshape(lead + (n_heads * LANES,))


def _rope_tables(id_rows, nl):
    t = jnp.arange(nl)
    row_id = (t // GRID_W).astype(F32)
    col_id = (t % GRID_W).astype(F32)

    def angles(rot_dim):
        n_freq = rot_dim // 4
        inv_freq = jnp.power(ROPE_THETA, -jnp.arange(n_freq, dtype=F32) / n_freq)
        return jnp.concatenate([row_id[:, None] * inv_freq, col_id[:, None] * inv_freq], axis=-1)

    def with_ctx(tab, fill):
        return jnp.concatenate([tab, jnp.full((id_rows, LANES), fill, F32)], axis=0)

    ag = angles(HEAD_DIM)
    one_g = jnp.ones((nl, LANES - HEAD_DIM), F32)
    cos_g = jnp.concatenate([jnp.cos(ag), jnp.cos(ag), one_g], axis=-1)
    sin_g = jnp.concatenate([-jnp.sin(ag), jnp.sin(ag), 0.0 * one_g], axis=-1)
    am = angles(MLA_ROPE_DIM)
    one_n = jnp.ones((nl, MLA_NOPE_DIM), F32)
    one_t = jnp.ones((nl, LANES - MLA_QK_DIM), F32)
    cos_m = jnp.concatenate([one_n, jnp.cos(am), jnp.cos(am), one_t], axis=-1)
    sin_m = jnp.concatenate([0.0 * one_n, -jnp.sin(am), jnp.sin(am), 0.0 * one_t], axis=-1)
    return with_ctx(cos_g, 1.0), with_ctx(sin_g, 0.0), with_ctx(cos_m, 1.0), with_ctx(sin_m, 0.0)


def _pool_band():
    t = np.arange(TM)[:, None]
    src = np.arange(POOL_EXT)[None, :] - POOL_HALO
    live = np.arange(POOL_EXT)[None, :] < TM + 2 * POOL_HALO
    mats = [((src >= t - w // 2) & (src < t + w // 2) & live) for w in POOL_WINDOWS]
    return jnp.asarray(np.stack(mats).astype(np.float32), dtype=BF16)


def _prep_params(w_in, norm1, norm2, conv_w, w_pool, pool_scale, gqa_q_norm, gqa_k_norm,
                 mla_q_norm, mla_kv_norm, mla_w_uq, mla_w_uk, mla_w_uv, mla_qk_q_norm,
                 mla_qk_k_norm, w_out, w_router, b_router, id_rows, nl):
    dep = w_in.shape[0]
    o = 0
    pieces = {}
    for name, n in (("conv", 3 * CONV_DIM), ("pool", POOL_DIM), ("gq", GQA_HEADS * HEAD_DIM),
                    ("gk", GQA_KV_HEADS * HEAD_DIM), ("gv", GQA_KV_HEADS * HEAD_DIM),
                    ("mq", MLA_Q_RANK), ("mkv", MLA_KV_RANK), ("mkr", MLA_ROPE_DIM)):
        pieces[name] = w_in[..., o:o + n]
        o += n
    mkr = jnp.pad(pieces["mkr"], ((0, 0), (0, 0), (MLA_NOPE_DIM, LANES - MLA_QK_DIM)))
    w_in_p = jnp.concatenate([
        pieces["conv"], pieces["pool"], _pad_heads(pieces["gq"], GQA_HEADS, HEAD_DIM),
        _pad_heads(pieces["gk"], GQA_KV_HEADS, HEAD_DIM),
        pieces["mq"], pieces["mkv"], mkr], axis=-1).astype(BF16)
    wgvt = jnp.swapaxes(_pad_heads(pieces["gv"], GQA_KV_HEADS, HEAD_DIM), 1, 2).astype(BF16)
    wuvt = jnp.swapaxes(_pad_heads(mla_w_uv, MLA_HEADS, MLA_V_DIM), 1, 2).astype(BF16)
    eye = jnp.eye(len(POOL_WINDOWS), dtype=F32)
    wpool = jnp.einsum("gh,dgij->dgihj", eye, w_pool).reshape(dep, POOL_DIM, POOL_DIM).astype(BF16)
    cos_g, sin_g, cos_m, sin_m = _rope_tables(id_rows, nl)
    tri = np.triu(np.ones((TM, TM), np.float32), 1)
    wr_pad = jnp.pad(w_router, ((0, 0), (0, LANES - w_router.shape[1])))
    wr_hi = wr_pad.astype(BF16)
    wr_split = jnp.concatenate([wr_hi, (wr_pad - wr_hi.astype(F32)).astype(BF16)], axis=1)

    def row3(a):
        return a.reshape(dep, 1, a.shape[-1])

    return {
        "w_in": w_in_p,
        "norm1": row3(norm1), "norm2": row3(norm2),
        "gq": row3(_pad_heads(gqa_q_norm * (GQA_SCALE * LOG2E), 1, HEAD_DIM)),
        "gk": row3(_pad_heads(gqa_k_norm, 1, HEAD_DIM)),
        "mqn": row3(mla_q_norm), "mkvn": row3(mla_kv_norm),
        "qkq": row3(_pad_heads(mla_qk_q_norm * (MLA_SCALE * LOG2E), 1, MLA_QK_DIM)),
        "qkk": row3(_pad_heads(mla_qk_k_norm, 1, MLA_QK_DIM)),
        "wuq": _pad_heads(mla_w_uq, MLA_HEADS, MLA_QK_DIM).astype(BF16),
        "wuk": _pad_heads(mla_w_uk, MLA_HEADS, MLA_NOPE_DIM).astype(BF16),
        "wgvt": wgvt, "wuvt": wuvt,
        "cos_g": cos_g, "sin_g": sin_g, "cos_m": cos_m, "sin_m": sin_m,
        "conv_w": jnp.pad(conv_w, ((0, 0), (0, SUBLANES - conv_w.shape[1]), (0, 0))),
        "band": _pool_band(),
        "wpool": wpool, "pscale": row3(pool_scale),
        "w_out": w_out.astype(BF16),
        "wr": wr_split, "br": b_router.reshape(-1, 1),
        "tri": jnp.asarray(tri, dtype=BF16),
    }


def _moe_plan(counts):
    counts = counts.astype(I32)
    padded = ((counts + BM - 1) // BM) * BM
    pad_end = jnp.cumsum(padded)
    pad_start = pad_end - padded
    return pad_start, pad_end, padded


def kernel(x, c, ctx, c_ctx, w_mod, b_mod, norm1, norm2, w_in, conv_w, w_pool, pool_scale,
           gqa_q_norm, gqa_k_norm, mla_q_norm, mla_kv_norm, mla_w_uq, mla_w_uk, mla_w_uv,
           mla_qk_q_norm, mla_qk_k_norm, w_out, w_router, b_router, w_gate, w_up, w_down):
    nb, nl, d = x.shape
    nc = ctx.shape[1]
    depth = w_mod.shape[0]
    assert nc % TM == 0 and nl % TM == 0 and nl % GRID_W == 0 and nb < SUBLANES
    lt, nct = nl // TM, nc // TM
    wide = nl % INPROJ_ROWS == 0 and (nb * nc) % INPROJ_ROWS == 0
    geo = {"nbatch": nb, "nc": nc, "nl": nl, "lt": lt, "nct": nct,
           "n_lat_tiles": nb * lt, "n_tiles": nb * (lt + nct),
           "inproj_rows": INPROJ_ROWS if wide else TM}

    p = _prep_params(w_in, norm1, norm2, conv_w, w_pool, pool_scale, gqa_q_norm, gqa_k_norm,
                     mla_q_norm, mla_kv_norm, mla_w_uq, mla_w_uk, mla_w_uv, mla_qk_q_norm,
                     mla_qk_k_norm, w_out, w_router, b_router, geo["inproj_rows"], nl)
    cvec = jnp.concatenate([c, c_ctx[None, :], jnp.zeros((SUBLANES - nb - 1, d), F32)], axis=0)
    mods = _adaln(cvec, w_mod, b_mod)

    src = (x.reshape(nb * nl, d), 0, ctx.reshape(nb * nc, d), 0)
    for i in range(depth):
        last = i == depth - 1
        mix = _inproj(src, mods, i, p, geo)
        yg = _attention(mix["qg"], mix["kg"], mix["vgt"], True, geo, False)
        ym = _attention(mix["qm"], mix["km"], mix["vmt"], False, geo, False)
        if last:
            n_tiles, ygc, ymc = geo["n_lat_tiles"], yg, ym
        else:
            n_tiles = geo["n_tiles"]
            ygc = _attention(mix["qg"], mix["kg"], mix["vgt"], True, geo, True)
            ymc = _attention(mix["qm"], mix["km"], mix["vmt"], False, geo, True)
        xn, h2, info, cnt = _outproj(src, mix, yg, ygc, ym, ymc, mods, i, p, geo, n_tiles)
        n_blocks = -(-2 * n_tiles * TM // BM) + N_EXPERTS
        n_slots = n_blocks * BM
        pad_start, pad_end, padded = _moe_plan(cnt[:, 0])
        n_used = (pad_end[-1:] // BM).astype(I32)
        block_row0 = jnp.arange(n_blocks, dtype=I32) * BM
        block_exp = jnp.minimum(jnp.sum((pad_end[None, :] <= block_row0[:, None]).astype(I32), axis=1),
                                N_EXPERTS - 1)
        dest, xs = _scatter(info, h2, pad_start, jnp.maximum(pad_end - BM, 0),
                            (padded > 0).astype(I32), n_used, n_slots)
        y = _experts(xs, block_exp, n_used, w_gate, w_up, w_down, i)
        xa = _combine(dest, info, xn, mods, y, i, geo)
        src = (xa, 0, xa, nb * nl)
    return xa.reshape(nb, nl, d)
```

```python
import functools
import math

import numpy as np
import jax
import jax.numpy as jnp
from jax import lax
from jax.experimental import pallas as pl
from jax.experimental.pallas import tpu as pltpu

F32 = jnp.float32
BF16 = jnp.bfloat16
I32 = jnp.int32

GRID_W = 64
CONV_DIM = 256
POOL_DIM = 256
POOL_WINDOWS = (2, 4, 8, 16)
HEAD_DIM = 64
GQA_HEADS = 4
GQA_KV_HEADS = 2
MLA_HEADS = 4
MLA_NOPE_DIM = 64
MLA_ROPE_DIM = 32
MLA_QK_DIM = MLA_NOPE_DIM + MLA_ROPE_DIM
MLA_V_DIM = 64
MLA_Q_RANK = 256
MLA_KV_RANK = 128
N_EXPERTS = 16
EXPERTS_PER_GROUP = 4
ROPE_THETA = 10000.0
NORM_EPS = 1e-6
LOG2E = 1.4426950408889634
GQA_SCALE = HEAD_DIM ** -0.5
MLA_SCALE = MLA_QK_DIM ** -0.5

LANES = 128
SUBLANES = 8
BF16_ROWS = 16
VMEM_LIMIT = 56 * 1024 * 1024

TM = 256
ATT_ROWS = 2048
INPROJ_ROWS = 512
BM = 512
V_ROWS = HEAD_DIM + BF16_ROWS
ATT_PAIR_UNROLL = 4
POOL_EXT = 512
POOL_HALO = 8

ZC_B, ZC_C, ZC_U, ZC_P = 0, 256, 512, 768
ZC_GQ = 1024
ZC_GK = ZC_GQ + GQA_HEADS * LANES
ZC_MQ = ZC_GK + GQA_KV_HEADS * LANES
ZC_MKV = ZC_MQ + MLA_Q_RANK
ZC_MKR = ZC_MKV + MLA_KV_RANK
ZC_END = ZC_MKR + LANES

HIGHEST = lax.Precision.HIGHEST


def _cparams(sem, vmem=VMEM_LIMIT):
    return pltpu.CompilerParams(dimension_semantics=sem, vmem_limit_bytes=vmem)


def _sigmoid(v):
    return 1.0 / (1.0 + jnp.exp(-v))


def _adaln_kernel(c_ref, w_ref, b_ref, o_ref):
    c = c_ref[...]
    s = c * _sigmoid(c)
    o_ref[0] = jnp.dot(s, w_ref[0], preferred_element_type=F32, precision=HIGHEST) + b_ref[0]


def _adaln(cvec, w_mod, b_mod):
    depth, d, n6 = w_mod.shape
    tn = 1536 if n6 % 1536 == 0 else n6
    rows = cvec.shape[0]
    return pl.pallas_call(
        _adaln_kernel,
        grid=(depth, n6 // tn),
        in_specs=[
            pl.BlockSpec((rows, d), lambda i, n: (0, 0)),
            pl.BlockSpec((1, d, tn), lambda i, n: (i, 0, n)),
            pl.BlockSpec((1, 1, tn), lambda i, n: (i, 0, n)),
        ],
        out_specs=pl.BlockSpec((1, rows, tn), lambda i, n: (i, 0, n)),
        out_shape=jax.ShapeDtypeStruct((depth, rows, n6), F32),
        compiler_params=_cparams(("arbitrary", "arbitrary")),
        name="adaln",
    )(cvec, w_mod, b_mod.reshape(depth, 1, n6))


def _tile_geometry(j, geo):
    lt, nct, n_lat = geo["lt"], geo["nct"], geo["n_lat_tiles"]
    is_ctx = j >= n_lat
    jc = j - n_lat
    sample = jnp.where(is_ctx, jc // nct, j // lt)
    jt = jnp.where(is_ctx, jc % nct, j % lt)
    return {
        "is_ctx": is_ctx,
        "mod_row": jnp.where(is_ctx, geo["nbatch"], sample),
        "first": jt == 0,
        "last": jt == jnp.where(is_ctx, nct, lt) - 1,
        "pos0": jt * TM,
        "seg_len": jnp.where(is_ctx, geo["nc"], geo["nl"]),
    }


def _norm_rope(slab, gain, cos, sin, n_valid, first_half, half):
    ssq = jnp.dot((slab * slab).astype(BF16), jnp.ones((LANES, LANES), BF16), preferred_element_type=F32)
    y = slab * lax.rsqrt(ssq * (1.0 / n_valid) + NORM_EPS) * gain
    partner = jnp.where(first_half, pltpu.roll(y, LANES - half, 1), pltpu.roll(y, half, 1))
    return y * cos + partner * sin


def _inproj_kernel(xl_ref, xc_ref, mods_ref, n1_ref, cg_ref, sg_ref, cm_ref, sm_ref, win_ref,
                   gq_ref, gk_ref, mqn_ref, mkvn_ref, qkq_ref, qkk_ref, wuq_ref, wuk_ref,
                   wgvt_ref, wuvt_ref,
                   cb_ref, cv_ref, zp_ref, qg_ref, kg_ref, vgt_ref, qm_ref, km_ref, vmt_ref,
                   *, d_model, nbatch, n_lat_steps, lat_steps_per_sample):
    d = d_model
    step = pl.program_id(0)
    row = jnp.where(step >= n_lat_steps, nbatch, step // lat_steps_per_sample)
    shift = mods_ref[0, pl.ds(row, 1), 0:d]
    scale = mods_ref[0, pl.ds(row, 1), d:2 * d]
    x = jnp.where(step >= n_lat_steps, xc_ref[...], xl_ref[...])
    ms = jnp.mean(x * x, axis=-1, keepdims=True)
    h = ((x * lax.rsqrt(ms + NORM_EPS)) * (n1_ref[0] * (1.0 + scale)) + shift).astype(BF16)
    nt_dims = (((1,), (1,)), ((), ()))

    def proj(c0, width):
        return jnp.dot(h, win_ref[0, :, c0:c0 + width], preferred_element_type=F32)

    def store_transposed(vt_ref, vt):
        srow = lax.broadcasted_iota(I32, vt.shape, 0)
        vt = jnp.where(jnp.bitwise_and(srow, LANES - 1) == HEAD_DIM, 1.0, vt).astype(BF16)
        for c in range(vt_ref.shape[0]):
            vt_ref[c] = vt[:, c * TM:(c + 1) * TM]

    tm = x.shape[0]
    lane = lax.broadcasted_iota(I32, (tm, LANES), 1)

    cm, sm = cm_ref[...], sm_ref[...]
    m_first = lane < MLA_NOPE_DIM + MLA_ROPE_DIM // 2
    zq = proj(ZC_MQ, MLA_Q_RANK)
    cq = zq * lax.rsqrt(jnp.mean(zq * zq, axis=-1, keepdims=True) + NORM_EPS) * mqn_ref[0]
    qpre = jnp.dot(cq.astype(BF16), wuq_ref[0], preferred_element_type=F32)
    zk = proj(ZC_MKV, MLA_KV_RANK + LANES)
    zkv, zkr = zk[:, 0:MLA_KV_RANK], zk[:, MLA_KV_RANK:]
    ckv = zkv * lax.rsqrt(jnp.mean(zkv * zkv, axis=-1, keepdims=True) + NORM_EPS) * mkvn_ref[0]
    ckv = ckv.astype(BF16)
    kvp = jnp.dot(ckv, wuk_ref[0], preferred_element_type=F32)
    store_transposed(vmt_ref, lax.dot_general(wuvt_ref[0], ckv, nt_dims, preferred_element_type=F32))
    for hd in range(MLA_HEADS):
        sl = slice(hd * LANES, (hd + 1) * LANES)
        qm_ref[:, sl] = _norm_rope(qpre[:, sl], qkq_ref[0], cm, sm, MLA_QK_DIM,
                                   m_first, MLA_ROPE_DIM // 2).astype(BF16)
        km_ref[:, sl] = _norm_rope(kvp[:, sl] + zkr, qkk_ref[0], cm, sm, MLA_QK_DIM,
                                   m_first, MLA_ROPE_DIM // 2).astype(BF16)

    cg, sg = cg_ref[...], sg_ref[...]
    g_first = lane < HEAD_DIM // 2
    zg = proj(ZC_GQ, (GQA_HEADS + GQA_KV_HEADS) * LANES)
    for hd in range(GQA_HEADS):
        qg_ref[:, hd * LANES:(hd + 1) * LANES] = _norm_rope(
            zg[:, hd * LANES:(hd + 1) * LANES], gq_ref[0], cg, sg, HEAD_DIM, g_first,
            HEAD_DIM // 2).astype(BF16)
    for hd in range(GQA_KV_HEADS):
        slab = zg[:, (GQA_HEADS + hd) * LANES:(GQA_HEADS + hd + 1) * LANES]
        kg_ref[:, hd * LANES:(hd + 1) * LANES] = _norm_rope(
            slab, gk_ref[0], cg, sg, HEAD_DIM, g_first, HEAD_DIM // 2).astype(BF16)
    store_transposed(vgt_ref, lax.dot_general(wgvt_ref[0], h, nt_dims, preferred_element_type=F32))

    zc = proj(ZC_B, 3 * CONV_DIM + POOL_DIM)
    cb_ref[...] = zc[:, ZC_B:ZC_B + CONV_DIM].astype(BF16)
    cv_ref[...] = (zc[:, ZC_C:ZC_C + CONV_DIM] * zc[:, ZC_U:ZC_U + CONV_DIM]).astype(BF16)
    zp_ref[...] = zc[:, ZC_P:ZC_P + POOL_DIM]


def _row_sources(src, rows_per_step, geo):
    lat, lat_row0, cx, cx_row0 = src
    n_lat = geo["nbatch"] * geo["nl"] // rows_per_step
    n_ctx = geo["nbatch"] * geo["nc"] // rows_per_step
    lat0, cx0 = lat_row0 // rows_per_step, cx_row0 // rows_per_step
    d = lat.shape[1]
    return [pl.BlockSpec((rows_per_step, d), lambda j: (lat0 + jnp.minimum(j, n_lat - 1), 0)),
            pl.BlockSpec((rows_per_step, d), lambda j: (cx0 + jnp.clip(j - n_lat, 0, n_ctx - 1), 0))]


def _inproj(src, mods, layer, p, geo):
    d = src[0].shape[1]
    r = geo["n_tiles"] * TM
    tmi = geo["inproj_rows"]
    n_lat_steps = geo["nbatch"] * geo["nl"] // tmi
    lat_steps_per_sample = geo["nl"] // tmi
    rowmap = lambda j: (j, 0)
    posmap = lambda j: (jnp.where(j >= n_lat_steps, lat_steps_per_sample, j % lat_steps_per_sample), 0)
    lay3 = lambda j: (layer, 0, 0)

    def full3(a):
        return pl.BlockSpec((1,) + a.shape[1:], lay3)

    outs = [("cb", CONV_DIM, BF16, False), ("cv", CONV_DIM, BF16, False), ("zp", POOL_DIM, F32, False),
            ("qg", GQA_HEADS * LANES, BF16, False), ("kg", GQA_KV_HEADS * LANES, BF16, False),
            ("vgt", GQA_KV_HEADS * LANES, BF16, True), ("qm", MLA_HEADS * LANES, BF16, False),
            ("km", MLA_HEADS * LANES, BF16, False), ("vmt", MLA_HEADS * LANES, BF16, True)]

    def out_spec(w, transposed):
        if transposed:
            return pl.BlockSpec((tmi // TM, w, TM), lambda j: (j, 0, 0))
        return pl.BlockSpec((tmi, w), rowmap)

    def out_shape(w, dt, transposed):
        return jax.ShapeDtypeStruct((r // TM, w, TM) if transposed else (r, w), dt)

    res = pl.pallas_call(
        functools.partial(_inproj_kernel, d_model=d, nbatch=geo["nbatch"], n_lat_steps=n_lat_steps,
                          lat_steps_per_sample=lat_steps_per_sample),
        grid=(r // tmi,),
        in_specs=_row_sources(src, tmi, geo) + [
            full3(mods), full3(p["norm1"]),
            pl.BlockSpec((tmi, LANES), posmap), pl.BlockSpec((tmi, LANES), posmap),
            pl.BlockSpec((tmi, LANES), posmap), pl.BlockSpec((tmi, LANES), posmap),
            full3(p["w_in"]), full3(p["gq"]), full3(p["gk"]), full3(p["mqn"]), full3(p["mkvn"]),
            full3(p["qkq"]), full3(p["qkk"]), full3(p["wuq"]), full3(p["wuk"]),
            full3(p["wgvt"]), full3(p["wuvt"]),
        ],
        out_specs=[out_spec(w, t) for _, w, _, t in outs],
        out_shape=[out_shape(w, dt, t) for _, w, dt, t in outs],
        compiler_params=_cparams(("arbitrary",)),
        name="inproj",
    )(src[0], src[2], mods, p["norm1"], p["cos_g"], p["sin_g"], p["cos_m"], p["sin_m"], p["w_in"],
      p["gq"], p["gk"], p["mqn"], p["mkvn"], p["qkq"], p["qkk"], p["wuq"], p["wuk"],
      p["wgvt"], p["wuvt"])
    return {name: a for (name, _, _, _), a in zip(outs, res)}


def _attn_kernel(q_ref, kl_ref, kc_ref, vtl_ref, vtc_ref, o_ref, q_st, s_buf, p_buf, a_buf, t_buf, m_ref, acc_ref,
                 *, shared_kv, n_lat, n_ctx, tk):
    tq = q_ref.shape[0]
    n_steps = n_lat + n_ctx
    if shared_kv:
        q_st[0] = jnp.concatenate([q_ref[:, 0:LANES], q_ref[:, LANES:2 * LANES]], axis=0)
        cols = [0]
    else:
        q_st[0] = q_ref[:, 0:LANES]
        q_st[1] = q_ref[:, LANES:2 * LANES]
        cols = [0, LANES]
    for si, col in enumerate(cols):
        q_s, m_s, acc_s = q_st.at[si], m_ref.at[si], acc_ref.at[si]
        m_s[...] = jnp.full(m_s.shape, -1e30, F32)
        acc_s[...] = jnp.zeros(acc_s.shape, F32)
        p_buf[1] = jnp.zeros(p_buf.shape[1:], BF16)
        a_buf[1] = jnp.ones(a_buf.shape[1:], F32)

        def pick(i, lat_fn, ctx_fn):
            ctx = ctx_fn(jnp.clip(i - n_lat, 0, n_ctx - 1))
            if n_lat == 0:
                return ctx
            return jnp.where(i >= n_lat, ctx, lat_fn(jnp.clip(i, 0, n_lat - 1)))

        def scores(i, slot, q_s=q_s, col=col):
            k = pick(i, lambda t: kl_ref[pl.ds(pl.multiple_of(t * tk, tk), tk), col:col + LANES],
                     lambda t: kc_ref[pl.ds(pl.multiple_of(t * tk, tk), tk), col:col + LANES])
            s = lax.dot_general(k, q_s[...], (((1,), (1,)), ((), ())),
                                preferred_element_type=F32)
            s_buf[slot] = s
            t_buf[slot] = jnp.broadcast_to(jnp.max(s, axis=0, keepdims=True), t_buf.shape[1:])

        def softmax(slot, m_s=m_s):
            s = s_buf[slot]
            m_old = m_s[...]
            m_new = jnp.maximum(m_old, t_buf[slot])
            a_buf[slot] = jnp.exp2(m_old - m_new)
            p_buf[slot] = jnp.exp2(s - m_new[0:1]).astype(BF16)
            m_s[...] = m_new

        def accumulate(i, slot, acc_s=acc_s, col=col):
            vt = pick(i, lambda t: vtl_ref[t, col:col + V_ROWS, :],
                      lambda t: vtc_ref[t, col:col + V_ROWS, :])
            acc_s[...] = a_buf[slot][0:1] * acc_s[...] + jnp.dot(vt, p_buf[slot],
                                                                 preferred_element_type=F32)

        scores(0, 0)

        def pair(t, carry):
            i = 2 * t
            scores(i + 1, 1)
            softmax(0)
            accumulate(i - 1, 1)
            scores(i + 2, 0)
            softmax(1)
            accumulate(i, 0)
            return carry

        n_pairs = (n_steps - 1) // 2
        lax.fori_loop(0, n_pairs, pair, 0, unroll=ATT_PAIR_UNROLL)
        for i in range(2 * n_pairs, n_steps):
            if i + 1 < n_steps:
                scores(i + 1, (i + 1) % 2)
            softmax(i % 2)
            if i >= 1:
                accumulate(i - 1, (i - 1) % 2)
        accumulate(n_steps - 1, (n_steps - 1) % 2)
    def finish(acc_t):
        o_t = acc_t * (1.0 / acc_t[HEAD_DIM:HEAD_DIM + 1])
        pad = jnp.zeros((LANES - V_ROWS, o_t.shape[1]), F32)
        return jnp.concatenate([o_t, pad], axis=0).T

    if shared_kv:
        o_both = finish(acc_ref[0])
        o0, o1 = o_both[0:tq], o_both[tq:2 * tq]
    else:
        o0, o1 = finish(acc_ref[0]), finish(acc_ref[1])
    lane = lax.broadcasted_iota(I32, (tq, LANES), 1)
    o_ref[...] = jnp.where(lane < HEAD_DIM, o0, pltpu.roll(o1, HEAD_DIM, 1)).astype(BF16)


def _attention(q, k, vt, shared_kv, geo, ctx_queries):
    nb, nc, nl = geo["nbatch"], geo["nc"], geo["nl"]
    kw = LANES if shared_kv else 2 * LANES
    tk = TM
    ctx_blk0 = nb * nl // nc
    ctx_k = pl.BlockSpec((nc, kw), lambda b, g, t: (ctx_blk0 + b, g))
    ctx_vt = pl.BlockSpec((nc // tk, kw, tk), lambda b, g, t: (ctx_blk0 + b, g, 0))
    if ctx_queries:
        tq, q_per, q_blk0, n_lat = nc, 1, ctx_blk0, 0
        lat_k, lat_vt = ctx_k, ctx_vt
    else:
        tq = min(ATT_ROWS, nl) // (2 if shared_kv else 1)
        q_per, q_blk0, n_lat = nl // tq, 0, nl // tk
        lat_k = pl.BlockSpec((nl, kw), lambda b, g, t: (b, g))
        lat_vt = pl.BlockSpec((nl // tk, kw, tk), lambda b, g, t: (b, g, 0))
    n_streams, rows = (1, 2 * tq) if shared_kv else (2, tq)
    return pl.pallas_call(
        functools.partial(_attn_kernel, shared_kv=shared_kv, n_lat=n_lat, n_ctx=nc // tk, tk=tk),
        grid=(nb, 2, q_per),
        in_specs=[
            pl.BlockSpec((tq, 2 * LANES), lambda b, g, t: (q_blk0 + b * q_per + t, g)),
            lat_k, ctx_k, lat_vt, ctx_vt,
        ],
        out_specs=pl.BlockSpec((tq, LANES), lambda b, g, t: (b * q_per + t, g)),
        out_shape=jax.ShapeDtypeStruct((nb * q_per * tq, 2 * LANES), BF16),
        scratch_shapes=[pltpu.VMEM((n_streams, rows, LANES), BF16),
                        pltpu.VMEM((2, tk, rows), F32),
                        pltpu.VMEM((2, tk, rows), BF16),
                        pltpu.VMEM((2, SUBLANES, rows), F32),
                        pltpu.VMEM((2, SUBLANES, rows), F32),
                        pltpu.VMEM((n_streams, SUBLANES, rows), F32),
                        pltpu.VMEM((n_streams, V_ROWS, rows), F32)],
        compiler_params=_cparams(("arbitrary", "arbitrary", "arbitrary")),
        name=("attn_gqa" if shared_kv else "attn_mla") + ("_ctx" if ctx_queries else ""),
    )(q, k, k, vt, vt)


def _top2_sum(a, b, c, d):
    hi_ab, lo_ab = jnp.maximum(a, b), jnp.minimum(a, b)
    hi_cd, lo_cd = jnp.maximum(c, d), jnp.minimum(c, d)
    first = jnp.maximum(hi_ab, hi_cd)
    second = jnp.maximum(jnp.minimum(hi_ab, hi_cd), jnp.maximum(lo_ab, lo_cd))
    return first + second


def _outproj_kernel(xl_ref, xc_ref, cb_ref, cv_ref, cvp_ref, cvn_ref, zp_ref, zpp_ref, zpn_ref,
                    ygl_ref, ygc_ref, yml_ref, ymc_ref,
                    mods_ref, convw_ref, band_ref, wpool_ref, pscale_ref,
                    wout_ref, n2_ref, wr_ref, br_ref, tri_ref,
                    xo_ref, h2_ref, info_ref, cnt_ref,
                    *, d_model, geo):
    tile = pl.program_id(0)
    d = d_model
    tm = xl_ref.shape[0]
    tg = _tile_geometry(tile, geo)
    is_ctx = tg["is_ctx"]
    keep_prev = jnp.where(tg["first"], 0.0, 1.0)
    keep_next = jnp.where(tg["last"], 0.0, 1.0)
    row = tg["mod_row"]
    gate1 = mods_ref[0, pl.ds(row, 1), 2 * d:3 * d]
    shift2 = mods_ref[0, pl.ds(row, 1), 3 * d:4 * d]
    scale2 = mods_ref[0, pl.ds(row, 1), 4 * d:5 * d]

    v = cv_ref[...].astype(F32)
    prev_row = cvp_ref[...].astype(F32)[BF16_ROWS - 1:BF16_ROWS] * keep_prev
    next_row = cvn_ref[...].astype(F32)[0:1] * keep_next
    rid = lax.broadcasted_iota(I32, (tm, CONV_DIM), 0)
    vm1 = jnp.where(rid == 0, prev_row, pltpu.roll(v, 1, 0))
    vp1 = jnp.where(rid == tm - 1, next_row, pltpu.roll(v, tm - 1, 0))
    cw = convw_ref[0]
    y_conv = cb_ref[...].astype(F32) * (vm1 * cw[0:1] + v * cw[1:2] + vp1 * cw[2:3])

    zp = zp_ref[...]
    ext = jnp.concatenate(
        [zpp_ref[...] * keep_prev, zp, zpn_ref[...] * keep_next,
         jnp.zeros((POOL_EXT - tm - 2 * POOL_HALO, POOL_DIM), F32)], axis=0).astype(BF16)
    ext_a, ext_b = ext[:, 0:LANES], ext[:, LANES:2 * LANES]
    lane = lax.broadcasted_iota(I32, (tm, LANES), 1)
    low = lane < POOL_DIM // 4
    sum_a = jnp.where(low, jnp.dot(band_ref[0], ext_a, preferred_element_type=F32),
                      jnp.dot(band_ref[1], ext_a, preferred_element_type=F32))
    sum_b = jnp.where(low, jnp.dot(band_ref[2], ext_b, preferred_element_type=F32),
                      jnp.dot(band_ref[3], ext_b, preferred_element_type=F32))
    sums = jnp.concatenate([sum_a, sum_b], axis=1)
    lane_p = lax.broadcasted_iota(I32, (tm, POOL_DIM), 1)
    half_w = jnp.left_shift(1, jnp.right_shift(lane_p, int(math.log2(POOL_DIM // 4))))
    pos = tg["pos0"] + rid
    cnt = (jnp.minimum(pos + half_w, tg["seg_len"]) - jnp.maximum(pos - half_w, 0)).astype(F32)
    dlt = sums / cnt - zp
    y_pool = jnp.dot(dlt.astype(BF16), wpool_ref[0], preferred_element_type=F32) * pscale_ref[0]

    y_gqa = jnp.where(is_ctx, ygc_ref[...], ygl_ref[...])
    y_mla = jnp.where(is_ctx, ymc_ref[...], yml_ref[...])
    ycat = jnp.concatenate([y_conv.astype(BF16), y_pool.astype(BF16), y_gqa, y_mla], axis=1)
    y = jnp.dot(ycat, wout_ref[0], preferred_element_type=F32)
    xn = jnp.where(is_ctx, xc_ref[...], xl_ref[...]) + gate1 * y
    xo_ref[...] = xn
    ms = jnp.mean(xn * xn, axis=-1, keepdims=True)
    h2 = (xn * lax.rsqrt(ms + NORM_EPS)) * (n2_ref[0] * (1.0 + scale2)) + shift2
    h2_ref[...] = h2

    h_hi = h2.astype(BF16)
    h_lo = (h2 - h_hi.astype(F32)).astype(BF16)
    wr = wr_ref[...]
    part = jnp.dot(h_hi, wr, preferred_element_type=F32)
    small = part[:, LANES:] + jnp.dot(h_lo, wr[:, 0:LANES], preferred_element_type=F32)
    logits = (part[:, 0:LANES] + small).T[0:N_EXPERTS]
    scores = _sigmoid(logits)
    sel = scores + br_ref[...]
    epg = EXPERTS_PER_GROUP
    n_groups = N_EXPERTS // epg
    srow = [sel[e:e + 1] for e in range(N_EXPERTS)]
    crow = [scores[e:e + 1] for e in range(N_EXPERTS)]
    gscore = [_top2_sum(*srow[g * epg:(g + 1) * epg]) for g in range(n_groups)]
    gbest = jnp.zeros_like(gscore[0]).astype(I32)
    best = gscore[0]
    for g in range(1, n_groups):
        upd = gscore[g] > best
        gbest = jnp.where(upd, g, gbest)
        best = jnp.where(upd, gscore[g], best)

    def pick(rows_, j):
        out = rows_[(n_groups - 1) * epg + j]
        for g in range(n_groups - 2, -1, -1):
            out = jnp.where(gbest == g, rows_[g * epg + j], out)
        return out

    sv = [pick(srow, j) for j in range(epg)]
    cv_ = [pick(crow, j) for j in range(epg)]
    i1 = jnp.zeros_like(gbest)
    b1 = sv[0]
    for j in range(1, epg):
        upd = sv[j] > b1
        i1 = jnp.where(upd, j, i1)
        b1 = jnp.where(upd, sv[j], b1)
    i2 = jnp.zeros_like(gbest)
    b2 = jnp.full_like(b1, -jnp.inf)
    for j in range(epg):
        upd = jnp.logical_and(i1 != j, sv[j] > b2)
        i2 = jnp.where(upd, j, i2)
        b2 = jnp.where(upd, sv[j], b2)
    s1 = cv_[epg - 1]
    s2 = cv_[epg - 1]
    for j in range(epg - 2, -1, -1):
        s1 = jnp.where(i1 == j, cv_[j], s1)
        s2 = jnp.where(i2 == j, cv_[j], s2)
    inv = 1.0 / (s1 + s2)
    e1 = gbest * epg + i1
    e2 = gbest * epg + i2

    @pl.when(tile == 0)
    def _():
        cnt_ref[...] = jnp.zeros_like(cnt_ref)

    erow = lax.broadcasted_iota(I32, (N_EXPERTS, tm), 0)
    hit1 = erow == e1
    hit2 = erow == e2
    onehot = jnp.where(hit1, 1.0, 0.0) + jnp.where(hit2, 1.0, 0.0)
    before = jnp.dot(onehot.astype(BF16), tri_ref[...], preferred_element_type=F32)
    tot = cnt_ref[:, 0:1] + before
    rank1 = jnp.sum(jnp.where(hit1, tot, 0.0), axis=0, keepdims=True).astype(I32)
    rank2 = jnp.sum(jnp.where(hit2, tot, 0.0), axis=0, keepdims=True).astype(I32)
    cnt_ref[...] = cnt_ref[...] + jnp.sum(onehot, axis=1, keepdims=True)

    w1 = lax.bitcast_convert_type(s1 * inv, I32)
    w2 = lax.bitcast_convert_type(s2 * inv, I32)
    irow = lax.broadcasted_iota(I32, (SUBLANES, tm), 0)
    info = jnp.where(irow == 0, e1, jnp.where(irow == 1, e2, jnp.where(
        irow == 2, rank1, jnp.where(irow == 3, rank2, jnp.where(
            irow == 4, w1, jnp.where(irow == 5, w2, 0))))))
    info_ref[0] = info


def _outproj(src, mix, yg, ygc, ym, ymc, mods, layer, p, geo, n_tiles):
    d = src[0].shape[1]
    nt_all, n_lat = geo["n_tiles"], geo["n_lat_tiles"]
    r = n_tiles * TM
    nt = n_tiles
    rowmap = lambda j: (j, 0)
    latmap = lambda j: (jnp.minimum(j, n_lat - 1), 0)
    ctxmap = lambda j: (jnp.clip(j - n_lat, 0, ygc.shape[0] // TM - 1), 0)
    lay3 = lambda j: (layer, 0, 0)
    c0 = lambda j: (0, 0)
    c3 = lambda j: (0, 0, 0)
    bf_blocks = TM // BF16_ROWS
    f_blocks = TM // SUBLANES

    def full3(a):
        return pl.BlockSpec((1,) + a.shape[1:], lay3)

    return pl.pallas_call(
        functools.partial(_outproj_kernel, d_model=d, geo=geo),
        grid=(n_tiles,),
        in_specs=_row_sources(src, TM, geo) + [
            pl.BlockSpec((TM, CONV_DIM), rowmap),
            pl.BlockSpec((TM, CONV_DIM), rowmap),
            pl.BlockSpec((BF16_ROWS, CONV_DIM), lambda j: (jnp.maximum(j * bf_blocks - 1, 0), 0)),
            pl.BlockSpec((BF16_ROWS, CONV_DIM),
                         lambda j: (jnp.minimum((j + 1) * bf_blocks, nt_all * bf_blocks - 1), 0)),
            pl.BlockSpec((TM, POOL_DIM), rowmap),
            pl.BlockSpec((SUBLANES, POOL_DIM), lambda j: (jnp.maximum(j * f_blocks - 1, 0), 0)),
            pl.BlockSpec((SUBLANES, POOL_DIM),
                         lambda j: (jnp.minimum((j + 1) * f_blocks, nt_all * f_blocks - 1), 0)),
            pl.BlockSpec((TM, 2 * LANES), latmap),
            pl.BlockSpec((TM, 2 * LANES), ctxmap),
            pl.BlockSpec((TM, 2 * LANES), latmap),
            pl.BlockSpec((TM, 2 * LANES), ctxmap),
            full3(mods), full3(p["conv_w"]),
            pl.BlockSpec(p["band"].shape, c3),
            full3(p["wpool"]), full3(p["pscale"]), full3(p["w_out"]), full3(p["norm2"]),
            pl.BlockSpec(p["wr"].shape, c0), pl.BlockSpec(p["br"].shape, c0),
            pl.BlockSpec(p["tri"].shape, c0),
        ],
        out_specs=[
            pl.BlockSpec((TM, d), rowmap),
            pl.BlockSpec((TM, d), rowmap),
            pl.BlockSpec((1, SUBLANES, TM), lambda j: (j, 0, 0)),
            pl.BlockSpec((N_EXPERTS, LANES), c0),
        ],
        out_shape=[
            jax.ShapeDtypeStruct((r, d), F32),
            jax.ShapeDtypeStruct((r, d), F32),
            jax.ShapeDtypeStruct((nt, SUBLANES, TM), I32),
            jax.ShapeDtypeStruct((N_EXPERTS, LANES), F32),
        ],
        compiler_params=_cparams(("arbitrary",)),
        name="outproj",
    )(src[0], src[2], mix["cb"], mix["cv"], mix["cv"], mix["cv"], mix["zp"], mix["zp"], mix["zp"],
      yg, ygc, ym, ymc, mods, p["conv_w"], p["band"], p["wpool"], p["pscale"], p["w_out"],
      p["norm2"], p["wr"], p["br"], p["tri"])


def _load_slots(block_ref, idx_ref, sem):
    n = block_ref.shape[2]
    copies = [pltpu.make_async_copy(block_ref.at[0, k], idx_ref.at[pl.ds(k * n, n)], sem) for k in range(2)]
    for cp in copies:
        cp.start()
    for cp in copies:
        cp.wait()


def _row_copies(idx_ref, n_rows, make):
    def body(t, carry):
        for u in range(SUBLANES):
            r = t * SUBLANES + u
            make(0, t, u, idx_ref[r]).start()
            make(1, t, u, idx_ref[n_rows + r]).start()
        return carry
    lax.fori_loop(0, n_rows // SUBLANES, body, 0)


def _scatter_kernel(ps_ref, zs_ref, has_ref, nu_ref, info_ref, h2_ref, dest_ref, xs_ref,
                    dsm_ref, zbuf_ref, stage_ref, sem_ref, *, first_spare, n_blocks, n_tiles):
    j = pl.program_id(0)
    tm = h2_ref.shape[0]

    @pl.when(j == 0)
    def _():
        zbuf_ref[...] = jnp.zeros_like(zbuf_ref)

        def zero_copy(start):
            return pltpu.make_async_copy(
                zbuf_ref, xs_ref.at[pl.ds(pl.multiple_of(start, BM), BM)], sem_ref.at[2])

        for act in ("start", "wait"):
            for e in range(N_EXPERTS):
                @pl.when(has_ref[e] > 0)
                def _():
                    getattr(zero_copy(zs_ref[e]), act)()
            for jb in range(first_spare, n_blocks):
                @pl.when(jb >= nu_ref[0])
                def _():
                    getattr(zero_copy(jb * BM), act)()

    info = info_ref[0]
    e1, e2 = info[0:1], info[1:2]
    d1, d2 = info[2:3], info[3:4]
    for e in range(N_EXPERTS):
        d1 = d1 + jnp.where(e1 == e, ps_ref[e], 0)
        d2 = d2 + jnp.where(e2 == e, ps_ref[e], 0)
    irow = lax.broadcasted_iota(I32, (SUBLANES, tm), 0)
    dest_ref[0] = jnp.where(irow == 0, d1, jnp.where(irow == 1, d2, 0))
    _load_slots(dest_ref, dsm_ref, sem_ref.at[3])

    par = j % 2
    stage_ref[par] = h2_ref[...].reshape(stage_ref.shape[1:])

    def row_copy(_, t, u, slot):
        return pltpu.make_async_copy(stage_ref.at[par, t, pl.ds(u, 1)], xs_ref.at[pl.ds(slot, 1)],
                                     sem_ref.at[par])

    _row_copies(dsm_ref, tm, row_copy)

    def retire(which):
        for _ in range(2):
            pltpu.make_async_copy(h2_ref, xs_ref.at[pl.ds(0, tm)], sem_ref.at[which]).wait()

    @pl.when(j > 0)
    def _():
        retire(1 - par)

    @pl.when(j == n_tiles - 1)
    def _():
        retire(par)


def _scatter(info, h2, pad_start, zero_start, has_rows, n_used, n_slots):
    r, d = h2.shape
    nt = r // TM
    return pl.pallas_call(
        functools.partial(_scatter_kernel, first_spare=-(-2 * r // BM), n_blocks=n_slots // BM,
                          n_tiles=nt),
        grid_spec=pltpu.PrefetchScalarGridSpec(
            num_scalar_prefetch=4,
            grid=(nt,),
            in_specs=[
                pl.BlockSpec((1, SUBLANES, TM), lambda j, *_: (j, 0, 0)),
                pl.BlockSpec((TM, d), lambda j, *_: (j, 0)),
            ],
            out_specs=[
                pl.BlockSpec((1, SUBLANES, TM), lambda j, *_: (j, 0, 0)),
                pl.BlockSpec(memory_space=pl.ANY),
            ],
            scratch_shapes=[
                pltpu.SMEM((2 * TM,), I32),
                pltpu.VMEM((BM, d), F32),
                pltpu.VMEM((2, TM // SUBLANES, SUBLANES, d), F32),
                pltpu.SemaphoreType.DMA((4,)),
            ],
        ),
        out_shape=[
            jax.ShapeDtypeStruct((nt, SUBLANES, TM), I32),
            jax.ShapeDtypeStruct((n_slots, d), F32),
        ],
        compiler_params=_cparams(("arbitrary",)),
        name="moe_scatter",
    )(pad_start, zero_start, has_rows, n_used, info, h2)


def _expert_kernel(be_ref, nu_ref, xs_ref, wg_ref, wu_ref, wd_ref, y_ref, wgb, wub, wdb):
    j = pl.program_id(0)

    @pl.when(j < nu_ref[0])
    def _():
        e = be_ref[j]
        prev = be_ref[jnp.maximum(j - 1, 0)]

        @pl.when(jnp.logical_or(j == 0, e != prev))
        def _():
            wgb[...] = wg_ref[0].astype(BF16)
            wub[...] = wu_ref[0].astype(BF16)
            wdb[...] = wd_ref[0].astype(BF16)

        x = xs_ref[...].astype(BF16)
        a = jnp.dot(x, wgb[...], preferred_element_type=F32)
        u = jnp.dot(x, wub[...], preferred_element_type=F32)
        hmid = (a * _sigmoid(a) * u).astype(BF16)
        y_ref[...] = jnp.dot(hmid, wdb[...], preferred_element_type=F32)

    @pl.when(j >= nu_ref[0])
    def _():
        y_ref[...] = jnp.zeros_like(y_ref)


def _experts(xs, block_exp, n_used, w_gate, w_up, w_down, layer):
    n_slots, d = xs.shape
    de = w_gate.shape[-1]
    nbm = n_slots // BM
    n_exp = w_gate.shape[1]

    def blk(j, be, nu):
        return (jnp.minimum(j, nu[0] - 1), 0)

    def wmap(j, be, nu):
        return (layer * n_exp + be[jnp.minimum(j, nu[0] - 1)], 0, 0)

    wg = w_gate.reshape((-1,) + w_gate.shape[2:])
    wu = w_up.reshape((-1,) + w_up.shape[2:])
    wd = w_down.reshape((-1,) + w_down.shape[2:])
    return pl.pallas_call(
        _expert_kernel,
        grid_spec=pltpu.PrefetchScalarGridSpec(
            num_scalar_prefetch=2,
            grid=(nbm,),
            in_specs=[
                pl.BlockSpec((BM, d), blk),
                pl.BlockSpec((1, d, de), wmap),
                pl.BlockSpec((1, d, de), wmap),
                pl.BlockSpec((1, de, d), wmap),
            ],
            out_specs=pl.BlockSpec((BM, d), lambda j, be, nu: (j, 0)),
            scratch_shapes=[
                pltpu.VMEM((d, de), BF16),
                pltpu.VMEM((d, de), BF16),
                pltpu.VMEM((de, d), BF16),
            ],
        ),
        out_shape=jax.ShapeDtypeStruct((n_slots, d), F32),
        compiler_params=_cparams(("arbitrary",)),
        name="moe_experts",
    )(block_exp, n_used, xs, wg, wu, wd)


def _combine_kernel(dest_ref, dest_next_ref, info_ref, x_ref, mods_ref, y_ref, o_ref,
                    dsm_ref, ybuf_ref, sem_ref, *, d_model, geo, n_steps):
    j = pl.program_id(0)
    d = d_model
    tm = x_ref.shape[0]
    row = _tile_geometry(j, geo)["mod_row"]
    gate2 = mods_ref[0, pl.ds(row, 1), 5 * d:6 * d]
    cur = j % 2

    def gather(idx_block_ref, buf):
        _load_slots(idx_block_ref, dsm_ref, sem_ref.at[2])

        def row_copy(k, t, u, slot):
            return pltpu.make_async_copy(y_ref.at[pl.ds(slot, 1)],
                                         ybuf_ref.at[buf, k, t, pl.ds(u, 1)], sem_ref.at[buf])

        _row_copies(dsm_ref, tm, row_copy)

    @pl.when(j == 0)
    def _():
        gather(dest_ref, 0)

    @pl.when(j + 1 < n_steps)
    def _():
        gather(dest_next_ref, 1 - cur)

    for k in range(2):
        pltpu.make_async_copy(y_ref.at[pl.ds(0, tm)], o_ref, sem_ref.at[cur]).wait()

    info = info_ref[0]
    wrow = lax.broadcasted_iota(I32, (LANES, tm), 0)
    w_lanes = jnp.where(wrow == 0, lax.bitcast_convert_type(info[4:5], F32),
                        jnp.where(wrow == 1, lax.bitcast_convert_type(info[5:6], F32), 0.0))
    w_rows = w_lanes.T
    y1 = ybuf_ref[cur, 0].reshape(tm, d)
    y2 = ybuf_ref[cur, 1].reshape(tm, d)
    o_ref[...] = x_ref[...] + gate2 * (w_rows[:, 0:1] * y1 + w_rows[:, 1:2] * y2)


def _combine(dest, info, xn, mods, y, layer, geo):
    r, d = xn.shape
    n_steps = r // TM
    return pl.pallas_call(
        functools.partial(_combine_kernel, d_model=d, geo=geo, n_steps=n_steps),
        grid=(n_steps,),
        in_specs=[
            pl.BlockSpec((1, SUBLANES, TM), lambda j: (j, 0, 0)),
            pl.BlockSpec((1, SUBLANES, TM), lambda j: (jnp.minimum(j + 1, n_steps - 1), 0, 0)),
            pl.BlockSpec((1, SUBLANES, TM), lambda j: (j, 0, 0)),
            pl.BlockSpec((TM, d), lambda j: (j, 0)),
            pl.BlockSpec((1,) + mods.shape[1:], lambda j: (layer, 0, 0)),
            pl.BlockSpec(memory_space=pl.ANY),
        ],
        out_specs=pl.BlockSpec((TM, d), lambda j: (j, 0)),
        out_shape=jax.ShapeDtypeStruct((n_steps * TM, d), F32),
        scratch_shapes=[
            pltpu.SMEM((2 * TM,), I32),
            pltpu.VMEM((2, 2, TM // SUBLANES, SUBLANES, d), F32),
            pltpu.SemaphoreType.DMA((3,)),
        ],
        compiler_params=_cparams(("arbitrary",)),
        name="moe_combine",
    )(dest, dest, info, xn, mods, y)


def _pad_heads(w, n_heads, width):
    lead = w.shape[:-1]
    w = w.reshape(lead + (n_heads, width))
    w = jnp.pad(w, [(0, 0)] * len(lead) + [(0, 0), (0, LANES - width)])
    return w.reshape(lead + (n_heads * LANES,))


def _rope_tables(id_rows, nl):
    t = jnp.arange(nl)
    row_id = (t // GRID_W).astype(F32)
    col_id = (t % GRID_W).astype(F32)

    def angles(rot_dim):
        n_freq = rot_dim // 4
        inv_freq = jnp.power(ROPE_THETA, -jnp.arange(n_freq, dtype=F32) / n_freq)
        return jnp.concatenate([row_id[:, None] * inv_freq, col_id[:, None] * inv_freq], axis=-1)

    def with_ctx(tab, fill):
        return jnp.concatenate([tab, jnp.full((id_rows, LANES), fill, F32)], axis=0)

    ag = angles(HEAD_DIM)
    one_g = jnp.ones((nl, LANES - HEAD_DIM), F32)
    cos_g = jnp.concatenate([jnp.cos(ag), jnp.cos(ag), one_g], axis=-1)
    sin_g = jnp.concatenate([-jnp.sin(ag), jnp.sin(ag), 0.0 * one_g], axis=-1)
    am = angles(MLA_ROPE_DIM)
    one_n = jnp.ones((nl, MLA_NOPE_DIM), F32)
    one_t = jnp.ones((nl, LANES - MLA_QK_DIM), F32)
    cos_m = jnp.concatenate([one_n, jnp.cos(am), jnp.cos(am), one_t], axis=-1)
    sin_m = jnp.concatenate([0.0 * one_n, -jnp.sin(am), jnp.sin(am), 0.0 * one_t], axis=-1)
    return with_ctx(cos_g, 1.0), with_ctx(sin_g, 0.0), with_ctx(cos_m, 1.0), with_ctx(sin_m, 0.0)


def _pool_band():
    t = np.arange(TM)[:, None]
    src = np.arange(POOL_EXT)[None, :] - POOL_HALO
    live = np.arange(POOL_EXT)[None, :] < TM + 2 * POOL_HALO
    mats = [((src >= t - w // 2) & (src < t + w // 2) & live) for w in POOL_WINDOWS]
    return jnp.asarray(np.stack(mats).astype(np.float32), dtype=BF16)


def _prep_params(w_in, norm1, norm2, conv_w, w_pool, pool_scale, gqa_q_norm, gqa_k_norm,
                 mla_q_norm, mla_kv_norm, mla_w_uq, mla_w_uk, mla_w_uv, mla_qk_q_norm,
                 mla_qk_k_norm, w_out, w_router, b_router, id_rows, nl):
    dep = w_in.shape[0]
    o = 0
    pieces = {}
    for name, n in (("conv", 3 * CONV_DIM), ("pool", POOL_DIM), ("gq", GQA_HEADS * HEAD_DIM),
                    ("gk", GQA_KV_HEADS * HEAD_DIM), ("gv", GQA_KV_HEADS * HEAD_DIM),
                    ("mq", MLA_Q_RANK), ("mkv", MLA_KV_RANK), ("mkr", MLA_ROPE_DIM)):
        pieces[name] = w_in[..., o:o + n]
        o += n
    mkr = jnp.pad(pieces["mkr"], ((0, 0), (0, 0), (MLA_NOPE_DIM, LANES - MLA_QK_DIM)))
    w_in_p = jnp.concatenate([
        pieces["conv"], pieces["pool"], _pad_heads(pieces["gq"], GQA_HEADS, HEAD_DIM),
        _pad_heads(pieces["gk"], GQA_KV_HEADS, HEAD_DIM),
        pieces["mq"], pieces["mkv"], mkr], axis=-1).astype(BF16)
    wgvt = jnp.swapaxes(_pad_heads(pieces["gv"], GQA_KV_HEADS, HEAD_DIM), 1, 2).astype(BF16)
    wuvt = jnp.swapaxes(_pad_heads(mla_w_uv, MLA_HEADS, MLA_V_DIM), 1, 2).astype(BF16)
    eye = jnp.eye(len(POOL_WINDOWS), dtype=F32)
    wpool = jnp.einsum("gh,dgij->dgihj", eye, w_pool).reshape(dep, POOL_DIM, POOL_DIM).astype(BF16)
    cos_g, sin_g, cos_m, sin_m = _rope_tables(id_rows, nl)
    tri = np.triu(np.ones((TM, TM), np.float32), 1)
    wr_pad = jnp.pad(w_router, ((0, 0), (0, LANES - w_router.shape[1])))
    wr_hi = wr_pad.astype(BF16)
    wr_split = jnp.concatenate([wr_hi, (wr_pad - wr_hi.astype(F32)).astype(BF16)], axis=1)

    def row3(a):
        return a.reshape(dep, 1, a.shape[-1])

    return {
        "w_in": w_in_p,
        "norm1": row3(norm1), "norm2": row3(norm2),
        "gq": row3(_pad_heads(gqa_q_norm * (GQA_SCALE * LOG2E), 1, HEAD_DIM)),
        "gk": row3(_pad_heads(gqa_k_norm, 1, HEAD_DIM)),
        "mqn": row3(mla_q_norm), "mkvn": row3(mla_kv_norm),
        "qkq": row3(_pad_heads(mla_qk_q_norm * (MLA_SCALE * LOG2E), 1, MLA_QK_DIM)),
        "qkk": row3(_pad_heads(mla_qk_k_norm, 1, MLA_QK_DIM)),
        "wuq": _pad_heads(mla_w_uq, MLA_HEADS, MLA_QK_DIM).astype(BF16),
        "wuk": _pad_heads(mla_w_uk, MLA_HEADS, MLA_NOPE_DIM).astype(BF16),
        "wgvt": wgvt, "wuvt": wuvt,
        "cos_g": cos_g, "sin_g": sin_g, "cos_m": cos_m, "sin_m": sin_m,
        "conv_w": jnp.pad(conv_w, ((0, 0), (0, SUBLANES - conv_w.shape[1]), (0, 0))),
        "band": _pool_band(),
        "wpool": wpool, "pscale": row3(pool_scale),
        "w_out": w_out.astype(BF16),
        "wr": wr_split, "br": b_router.reshape(-1, 1),
        "tri": jnp.asarray(tri, dtype=BF16),
    }


def _moe_plan(counts):
    counts = counts.astype(I32)
    padded = ((counts + BM - 1) // BM) * BM
    pad_end = jnp.cumsum(padded)
    pad_start = pad_end - padded
    return pad_start, pad_end, padded


def kernel(x, c, ctx, c_ctx, w_mod, b_mod, norm1, norm2, w_in, conv_w, w_pool, pool_scale,
           gqa_q_norm, gqa_k_norm, mla_q_norm, mla_kv_norm, mla_w_uq, mla_w_uk, mla_w_uv,
           mla_qk_q_norm, mla_qk_k_norm, w_out, w_router, b_router, w_gate, w_up, w_down):
    nb, nl, d = x.shape
    nc = ctx.shape[1]
    depth = w_mod.shape[0]
    assert nc % TM == 0 and nl % TM == 0 and nl % GRID_W == 0 and nb < SUBLANES
    lt, nct = nl // TM, nc // TM
    wide = nl % INPROJ_ROWS == 0 and (nb * nc) % INPROJ_ROWS == 0
    geo = {"nbatch": nb, "nc": nc, "nl": nl, "lt": lt, "nct": nct,
           "n_lat_tiles": nb * lt, "n_tiles": nb * (lt + nct),
           "inproj_rows": INPROJ_ROWS if wide else TM}

    p = _prep_params(w_in, norm1, norm2, conv_w, w_pool, pool_scale, gqa_q_norm, gqa_k_norm,
                     mla_q_norm, mla_kv_norm, mla_w_uq, mla_w_uk, mla_w_uv, mla_qk_q_norm,
                     mla_qk_k_norm, w_out, w_router, b_router, geo["inproj_rows"], nl)
    cvec = jnp.concatenate([c, c_ctx[None, :], jnp.zeros((SUBLANES - nb - 1, d), F32)], axis=0)
    mods = _adaln(cvec, w_mod, b_mod)

    src = (x.reshape(nb * nl, d), 0, ctx.reshape(nb * nc, d), 0)
    for i in range(depth):
        last = i == depth - 1
        mix = _inproj(src, mods, i, p, geo)
        yg = _attention(mix["qg"], mix["kg"], mix["vgt"], True, geo, False)
        ym = _attention(mix["qm"], mix["km"], mix["vmt"], False, geo, False)
        if last:
            n_tiles, ygc, ymc = geo["n_lat_tiles"], yg, ym
        else:
            n_tiles = geo["n_tiles"]
            ygc = _attention(mix["qg"], mix["kg"], mix["vgt"], True, geo, True)
            ymc = _attention(mix["qm"], mix["km"], mix["vmt"], False, geo, True)
        xn, h2, info, cnt = _outproj(src, mix, yg, ygc, ym, ymc, mods, i, p, geo, n_tiles)
        n_blocks = -(-2 * n_tiles * TM // BM) + N_EXPERTS
        n_slots = n_blocks * BM
        pad_start, pad_end, padded = _moe_plan(cnt[:, 0])
        n_used = (pad_end[-1:] // BM).astype(I32)
        block_row0 = jnp.arange(n_blocks, dtype=I32) * BM
        block_exp = jnp.minimum(jnp.sum((pad_end[None, :] <= block_row0[:, None]).astype(I32), axis=1),
                                N_EXPERTS - 1)
        dest, xs = _scatter(info, h2, pad_start, jnp.maximum(pad_end - BM, 0),
                            (padded > 0).astype(I32), n_used, n_slots)
        y = _experts(xs, block_exp, n_used, w_gate, w_up, w_down, i)
        xa = _combine(dest, info, xn, mods, y, i, geo)
        src = (xa, 0, xa, nb * nl)
    return xa.reshape(nb, nl, d)
```

```python
import functools
import math

import numpy as np
import jax
import jax.numpy as jnp
from jax import lax
from jax.experimental import pallas as pl
from jax.experimental.pallas import tpu as pltpu

F32 = jnp.float32
BF16 = jnp.bfloat16
I32 = jnp.int32

GRID_W = 64
CONV_DIM = 256
POOL_DIM = 256
POOL_WINDOWS = (2, 4, 8, 16)
HEAD_DIM = 64
GQA_HEADS = 4
GQA_KV_HEADS = 2
MLA_HEADS = 4
MLA_NOPE_DIM = 64
MLA_ROPE_DIM = 32
MLA_QK_DIM = MLA_NOPE_DIM + MLA_ROPE_DIM
MLA_V_DIM = 64
MLA_Q_RANK = 256
MLA_KV_RANK = 128
N_EXPERTS = 16
EXPERTS_PER_GROUP = 4
ROPE_THETA = 10000.0
NORM_EPS = 1e-6
LOG2E = 1.4426950408889634
GQA_SCALE = HEAD_DIM ** -0.5
MLA_SCALE = MLA_QK_DIM ** -0.5

LANES = 128
SUBLANES = 8
BF16_ROWS = 16
VMEM_LIMIT = 56 * 1024 * 1024

TM = 256
ATT_ROWS = 2048
INPROJ_ROWS = 512
BM = 512
V_ROWS = HEAD_DIM + BF16_ROWS
ATT_PAIR_UNROLL = 4
POOL_EXT = 512
POOL_HALO = 8

ZC_B, ZC_C, ZC_U, ZC_P = 0, 256, 512, 768
ZC_GQ = 1024
ZC_GK = ZC_GQ + GQA_HEADS * LANES
ZC_MQ = ZC_GK + GQA_KV_HEADS * LANES
ZC_MKV = ZC_MQ + MLA_Q_RANK
ZC_MKR = ZC_MKV + MLA_KV_RANK
ZC_END = ZC_MKR + LANES

HIGHEST = lax.Precision.HIGHEST


def _cparams(sem, vmem=VMEM_LIMIT):
    return pltpu.CompilerParams(dimension_semantics=sem, vmem_limit_bytes=vmem)


def _sigmoid(v):
    return 1.0 / (1.0 + jnp.exp(-v))


def _adaln_kernel(c_ref, w_ref, b_ref, o_ref):
    c = c_ref[...]
    s = c * _sigmoid(c)
    o_ref[0] = jnp.dot(s, w_ref[0], preferred_element_type=F32, precision=HIGHEST) + b_ref[0]


def _adaln(cvec, w_mod, b_mod):
    depth, d, n6 = w_mod.shape
    tn = 1536 if n6 % 1536 == 0 else n6
    rows = cvec.shape[0]
    return pl.pallas_call(
        _adaln_kernel,
        grid=(depth, n6 // tn),
        in_specs=[
            pl.BlockSpec((rows, d), lambda i, n: (0, 0)),
            pl.BlockSpec((1, d, tn), lambda i, n: (i, 0, n)),
            pl.BlockSpec((1, 1, tn), lambda i, n: (i, 0, n)),
        ],
        out_specs=pl.BlockSpec((1, rows, tn), lambda i, n: (i, 0, n)),
        out_shape=jax.ShapeDtypeStruct((depth, rows, n6), F32),
        compiler_params=_cparams(("arbitrary", "arbitrary")),
        name="adaln",
    )(cvec, w_mod, b_mod.reshape(depth, 1, n6))


def _tile_geometry(j, geo):
    lt, nct, n_lat = geo["lt"], geo["nct"], geo["n_lat_tiles"]
    is_ctx = j >= n_lat
    jc = j - n_lat
    sample = jnp.where(is_ctx, jc // nct, j // lt)
    jt = jnp.where(is_ctx, jc % nct, j % lt)
    return {
        "is_ctx": is_ctx,
        "mod_row": jnp.where(is_ctx, geo["nbatch"], sample),
        "first": jt == 0,
        "last": jt == jnp.where(is_ctx, nct, lt) - 1,
        "pos0": jt * TM,
        "seg_len": jnp.where(is_ctx, geo["nc"], geo["nl"]),
    }


def _norm_rope(slab, gain, cos, sin, n_valid, first_half, half):
    ssq = jnp.dot((slab * slab).astype(BF16), jnp.ones((LANES, LANES), BF16), preferred_element_type=F32)
    y = slab * lax.rsqrt(ssq * (1.0 / n_valid) + NORM_EPS) * gain
    partner = jnp.where(first_half, pltpu.roll(y, LANES - half, 1), pltpu.roll(y, half, 1))
    return y * cos + partner * sin


def _inproj_kernel(xl_ref, xc_ref, mods_ref, n1_ref, cg_ref, sg_ref, cm_ref, sm_ref, win_ref,
                   gq_ref, gk_ref, mqn_ref, mkvn_ref, qkq_ref, qkk_ref, wuq_ref, wuk_ref,
                   wgvt_ref, wuvt_ref,
                   cb_ref, cv_ref, zp_ref, qg_ref, kg_ref, vgt_ref, qm_ref, km_ref, vmt_ref,
                   *, d_model, nbatch, n_lat_steps, lat_steps_per_sample):
    d = d_model
    step = pl.program_id(0)
    row = jnp.where(step >= n_lat_steps, nbatch, step // lat_steps_per_sample)
    shift = mods_ref[0, pl.ds(row, 1), 0:d]
    scale = mods_ref[0, pl.ds(row, 1), d:2 * d]
    x = jnp.where(step >= n_lat_steps, xc_ref[...], xl_ref[...])
    ms = jnp.mean(x * x, axis=-1, keepdims=True)
    h = ((x * lax.rsqrt(ms + NORM_EPS)) * (n1_ref[0] * (1.0 + scale)) + shift).astype(BF16)
    nt_dims = (((1,), (1,)), ((), ()))

    def proj(c0, width):
        return jnp.dot(h, win_ref[0, :, c0:c0 + width], preferred_element_type=F32)

    def store_transposed(vt_ref, vt):
        srow = lax.broadcasted_iota(I32, vt.shape, 0)
        vt = jnp.where(jnp.bitwise_and(srow, LANES - 1) == HEAD_DIM, 1.0, vt).astype(BF16)
        for c in range(vt_ref.shape[0]):
            vt_ref[c] = vt[:, c * TM:(c + 1) * TM]

    tm = x.shape[0]
    lane = lax.broadcasted_iota(I32, (tm, LANES), 1)

    cm, sm = cm_ref[...], sm_ref[...]
    m_first = lane < MLA_NOPE_DIM + MLA_ROPE_DIM // 2
    zq = proj(ZC_MQ, MLA_Q_RANK)
    cq = zq * lax.rsqrt(jnp.mean(zq * zq, axis=-1, keepdims=True) + NORM_EPS) * mqn_ref[0]
    qpre = jnp.dot(cq.astype(BF16), wuq_ref[0], preferred_element_type=F32)
    zk = proj(ZC_MKV, MLA_KV_RANK + LANES)
    zkv, zkr = zk[:, 0:MLA_KV_RANK], zk[:, MLA_KV_RANK:]
    ckv = zkv * lax.rsqrt(jnp.mean(zkv * zkv, axis=-1, keepdims=True) + NORM_EPS) * mkvn_ref[0]
    ckv = ckv.astype(BF16)
    kvp = jnp.dot(ckv, wuk_ref[0], preferred_element_type=F32)
    store_transposed(vmt_ref, lax.dot_general(wuvt_ref[0], ckv, nt_dims, preferred_element_type=F32))
    for hd in range(MLA_HEADS):
        sl = slice(hd * LANES, (hd + 1) * LANES)
        qm_ref[:, sl] = _norm_rope(qpre[:, sl], qkq_ref[0], cm, sm, MLA_QK_DIM,
                                   m_first, MLA_ROPE_DIM // 2).astype(BF16)
        km_ref[:, sl] = _norm_rope(kvp[:, sl] + zkr, qkk_ref[0], cm, sm, MLA_QK_DIM,
                                   m_first, MLA_ROPE_DIM // 2).astype(BF16)

    cg, sg = cg_ref[...], sg_ref[...]
    g_first = lane < HEAD_DIM // 2
    zg = proj(ZC_GQ, (GQA_HEADS + GQA_KV_HEADS) * LANES)
    for hd in range(GQA_HEADS):
        qg_ref[:, hd * LANES:(hd + 1) * LANES] = _norm_rope(
            zg[:, hd * LANES:(hd + 1) * LANES], gq_ref[0], cg, sg, HEAD_DIM, g_first,
            HEAD_DIM // 2).astype(BF16)
    for hd in range(GQA_KV_HEADS):
        slab = zg[:, (GQA_HEADS + hd) * LANES:(GQA_HEADS + hd + 1) * LANES]
        kg_ref[:, hd * LANES:(hd + 1) * LANES] = _norm_rope(
            slab, gk_ref[0], cg, sg, HEAD_DIM, g_first, HEAD_DIM // 2).astype(BF16)
    store_transposed(vgt_ref, lax.dot_general(wgvt_ref[0], h, nt_dims, preferred_element_type=F32))

    zc = proj(ZC_B, 3 * CONV_DIM + POOL_DIM)
    cb_ref[...] = zc[:, ZC_B:ZC_B + CONV_DIM].astype(BF16)
    cv_ref[...] = (zc[:, ZC_C:ZC_C + CONV_DIM] * zc[:, ZC_U:ZC_U + CONV_DIM]).astype(BF16)
    zp_ref[...] = zc[:, ZC_P:ZC_P + POOL_DIM]


def _row_sources(src, rows_per_step, geo):
    lat, lat_row0, cx, cx_row0 = src
    n_lat = geo["nbatch"] * geo["nl"] // rows_per_step
    n_ctx = geo["nbatch"] * geo["nc"] // rows_per_step
    lat0, cx0 = lat_row0 // rows_per_step, cx_row0 // rows_per_step
    d = lat.shape[1]
    return [pl.BlockSpec((rows_per_step, d), lambda j: (lat0 + jnp.minimum(j, n_lat - 1), 0)),
            pl.BlockSpec((rows_per_step, d), lambda j: (cx0 + jnp.clip(j - n_lat, 0, n_ctx - 1), 0))]


def _inproj(src, mods, layer, p, geo):
    d = src[0].shape[1]
    r = geo["n_tiles"] * TM
    tmi = geo["inproj_rows"]
    n_lat_steps = geo["nbatch"] * geo["nl"] // tmi
    lat_steps_per_sample = geo["nl"] // tmi
    rowmap = lambda j: (j, 0)
    posmap = lambda j: (jnp.where(j >= n_lat_steps, lat_steps_per_sample, j % lat_steps_per_sample), 0)
    lay3 = lambda j: (layer, 0, 0)

    def full3(a):
        return pl.BlockSpec((1,) + a.shape[1:], lay3)

    outs = [("cb", CONV_DIM, BF16, False), ("cv", CONV_DIM, BF16, False), ("zp", POOL_DIM, F32, False),
            ("qg", GQA_HEADS * LANES, BF16, False), ("kg", GQA_KV_HEADS * LANES, BF16, False),
            ("vgt", GQA_KV_HEADS * LANES, BF16, True), ("qm", MLA_HEADS * LANES, BF16, False),
            ("km", MLA_HEADS * LANES, BF16, False), ("vmt", MLA_HEADS * LANES, BF16, True)]

    def out_spec(w, transposed):
        if transposed:
            return pl.BlockSpec((tmi // TM, w, TM), lambda j: (j, 0, 0))
        return pl.BlockSpec((tmi, w), rowmap)

    def out_shape(w, dt, transposed):
        return jax.ShapeDtypeStruct((r // TM, w, TM) if transposed else (r, w), dt)

    res = pl.pallas_call(
        functools.partial(_inproj_kernel, d_model=d, nbatch=geo["nbatch"], n_lat_steps=n_lat_steps,
                          lat_steps_per_sample=lat_steps_per_sample),
        grid=(r // tmi,),
        in_specs=_row_sources(src, tmi, geo) + [
            full3(mods), full3(p["norm1"]),
            pl.BlockSpec((tmi, LANES), posmap), pl.BlockSpec((tmi, LANES), posmap),
            pl.BlockSpec((tmi, LANES), posmap), pl.BlockSpec((tmi, LANES), posmap),
            full3(p["w_in"]), full3(p["gq"]), full3(p["gk"]), full3(p["mqn"]), full3(p["mkvn"]),
            full3(p["qkq"]), full3(p["qkk"]), full3(p["wuq"]), full3(p["wuk"]),
            full3(p["wgvt"]), full3(p["wuvt"]),
        ],
        out_specs=[out_spec(w, t) for _, w, _, t in outs],
        out_shape=[out_shape(w, dt, t) for _, w, dt, t in outs],
        compiler_params=_cparams(("arbitrary",)),
        name="inproj",
    )(src[0], src[2], mods, p["norm1"], p["cos_g"], p["sin_g"], p["cos_m"], p["sin_m"], p["w_in"],
      p["gq"], p["gk"], p["mqn"], p["mkvn"], p["qkq"], p["qkk"], p["wuq"], p["wuk"],
      p["wgvt"], p["wuvt"])
    return {name: a for (name, _, _, _), a in zip(outs, res)}


def _attn_kernel(q_ref, kl_ref, kc_ref, vtl_ref, vtc_ref, o_ref, q_st, s_buf, p_buf, a_buf, t_buf, m_ref, acc_ref,
                 *, shared_kv, n_lat, n_ctx, tk):
    tq = q_ref.shape[0]
    n_steps = n_lat + n_ctx
    if shared_kv:
        q_st[0] = jnp.concatenate([q_ref[:, 0:LANES], q_ref[:, LANES:2 * LANES]], axis=0)
        cols = [0]
    else:
        q_st[0] = q_ref[:, 0:LANES]
        q_st[1] = q_ref[:, LANES:2 * LANES]
        cols = [0, LANES]
    for si, col in enumerate(cols):
        q_s, m_s, acc_s = q_st.at[si], m_ref.at[si], acc_ref.at[si]
        m_s[...] = jnp.full(m_s.shape, -1e30, F32)
        acc_s[...] = jnp.zeros(acc_s.shape, F32)
        p_buf[1] = jnp.zeros(p_buf.shape[1:], BF16)
        a_buf[1] = jnp.ones(a_buf.shape[1:], F32)

        def pick(i, lat_fn, ctx_fn):
            ctx = ctx_fn(jnp.clip(i - n_lat, 0, n_ctx - 1))
            if n_lat == 0:
                return ctx
            return jnp.where(i >= n_lat, ctx, lat_fn(jnp.clip(i, 0, n_lat - 1)))

        def scores(i, slot, q_s=q_s, col=col):
            k = pick(i, lambda t: kl_ref[pl.ds(pl.multiple_of(t * tk, tk), tk), col:col + LANES],
                     lambda t: kc_ref[pl.ds(pl.multiple_of(t * tk, tk), tk), col:col + LANES])
            s = lax.dot_general(k, q_s[...], (((1,), (1,)), ((), ())),
                                preferred_element_type=F32)
            s_buf[slot] = s
            t_buf[slot] = jnp.broadcast_to(jnp.max(s, axis=0, keepdims=True), t_buf.shape[1:])

        def softmax(slot, m_s=m_s):
            s = s_buf[slot]
            m_old = m_s[...]
            m_new = jnp.maximum(m_old, t_buf[slot])
            a_buf[slot] = jnp.exp2(m_old - m_new)
            p_buf[slot] = jnp.exp2(s - m_new[0:1]).astype(BF16)
            m_s[...] = m_new

        def accumulate(i, slot, acc_s=acc_s, col=col):
            vt = pick(i, lambda t: vtl_ref[t, col:col + V_ROWS, :],
                      lambda t: vtc_ref[t, col:col + V_ROWS, :])
            acc_s[...] = a_buf[slot][0:1] * acc_s[...] + jnp.dot(vt, p_buf[slot],
                                                                 preferred_element_type=F32)

        scores(0, 0)

        def pair(t, carry):
            i = 2 * t
            scores(i + 1, 1)
            softmax(0)
            accumulate(i - 1, 1)
            scores(i + 2, 0)
            softmax(1)
            accumulate(i, 0)
            return carry

        n_pairs = (n_steps - 1) // 2
        lax.fori_loop(0, n_pairs, pair, 0, unroll=ATT_PAIR_UNROLL)
        for i in range(2 * n_pairs, n_steps):
            if i + 1 < n_steps:
                scores(i + 1, (i + 1) % 2)
            softmax(i % 2)
            if i >= 1:
                accumulate(i - 1, (i - 1) % 2)
        accumulate(n_steps - 1, (n_steps - 1) % 2)
    def finish(acc_t):
        o_t = acc_t * (1.0 / acc_t[HEAD_DIM:HEAD_DIM + 1])
        pad = jnp.zeros((LANES - V_ROWS, o_t.shape[1]), F32)
        return jnp.concatenate([o_t, pad], axis=0).T

    if shared_kv:
        o_both = finish(acc_ref[0])
        o0, o1 = o_both[0:tq], o_both[tq:2 * tq]
    else:
        o0, o1 = finish(acc_ref[0]), finish(acc_ref[1])
    lane = lax.broadcasted_iota(I32, (tq, LANES), 1)
    o_ref[...] = jnp.where(lane < HEAD_DIM, o0, pltpu.roll(o1, HEAD_DIM, 1)).astype(BF16)


def _attention(q, k, vt, shared_kv, geo, ctx_queries):
    nb, nc, nl = geo["nbatch"], geo["nc"], geo["nl"]
    kw = LANES if shared_kv else 2 * LANES
    tk = TM
    ctx_blk0 = nb * nl // nc
    ctx_k = pl.BlockSpec((nc, kw), lambda b, g, t: (ctx_blk0 + b, g))
    ctx_vt = pl.BlockSpec((nc // tk, kw, tk), lambda b, g, t: (ctx_blk0 + b, g, 0))
    if ctx_queries:
        tq, q_per, q_blk0, n_lat = nc, 1, ctx_blk0, 0
        lat_k, lat_vt = ctx_k, ctx_vt
    else:
        tq = min(ATT_ROWS, nl) // (2 if shared_kv else 1)
        q_per, q_blk0, n_lat = nl // tq, 0, nl // tk
        lat_k = pl.BlockSpec((nl, kw), lambda b, g, t: (b, g))
        lat_vt = pl.BlockSpec((nl // tk, kw, tk), lambda b, g, t: (b, g, 0))
    n_streams, rows = (1, 2 * tq) if shared_kv else (2, tq)
    return pl.pallas_call(
        functools.partial(_attn_kernel, shared_kv=shared_kv, n_lat=n_lat, n_ctx=nc // tk, tk=tk),
        grid=(nb, 2, q_per),
        in_specs=[
            pl.BlockSpec((tq, 2 * LANES), lambda b, g, t: (q_blk0 + b * q_per + t, g)),
            lat_k, ctx_k, lat_vt, ctx_vt,
        ],
        out_specs=pl.BlockSpec((tq, LANES), lambda b, g, t: (b * q_per + t, g)),
        out_shape=jax.ShapeDtypeStruct((nb * q_per * tq, 2 * LANES), BF16),
        scratch_shapes=[pltpu.VMEM((n_streams, rows, LANES), BF16),
                        pltpu.VMEM((2, tk, rows), F32),
                        pltpu.VMEM((2, tk, rows), BF16),
                        pltpu.VMEM((2, SUBLANES, rows), F32),
                        pltpu.VMEM((2, SUBLANES, rows), F32),
                        pltpu.VMEM((n_streams, SUBLANES, rows), F32),
                        pltpu.VMEM((n_streams, V_ROWS, rows), F32)],
        compiler_params=_cparams(("arbitrary", "arbitrary", "arbitrary")),
        name=("attn_gqa" if shared_kv else "attn_mla") + ("_ctx" if ctx_queries else ""),
    )(q, k, k, vt, vt)


def _top2_sum(a, b, c, d):
    hi_ab, lo_ab = jnp.maximum(a, b), jnp.minimum(a, b)
    hi_cd, lo_cd = jnp.maximum(c, d), jnp.minimum(c, d)
    first = jnp.maximum(hi_ab, hi_cd)
    second = jnp.maximum(jnp.minimum(hi_ab, hi_cd), jnp.maximum(lo_ab, lo_cd))
    return first + second


def _outproj_kernel(xl_ref, xc_ref, cb_ref, cv_ref, cvp_ref, cvn_ref, zp_ref, zpp_ref, zpn_ref,
                    ygl_ref, ygc_ref, yml_ref, ymc_ref,
                    mods_ref, convw_ref, band_ref, wpool_ref, pscale_ref,
                    wout_ref, n2_ref, wr_ref, br_ref, tri_ref,
                    xo_ref, h2_ref, info_ref, cnt_ref,
                    *, d_model, geo):
    tile = pl.program_id(0)
    d = d_model
    tm = xl_ref.shape[0]
    tg = _tile_geometry(tile, geo)
    is_ctx = tg["is_ctx"]
    keep_prev = jnp.where(tg["first"], 0.0, 1.0)
    keep_next = jnp.where(tg["last"], 0.0, 1.0)
    row = tg["mod_row"]
    gate1 = mods_ref[0, pl.ds(row, 1), 2 * d:3 * d]
    shift2 = mods_ref[0, pl.ds(row, 1), 3 * d:4 * d]
    scale2 = mods_ref[0, pl.ds(row, 1), 4 * d:5 * d]

    v = cv_ref[...].astype(F32)
    prev_row = cvp_ref[...].astype(F32)[BF16_ROWS - 1:BF16_ROWS] * keep_prev
    next_row = cvn_ref[...].astype(F32)[0:1] * keep_next
    rid = lax.broadcasted_iota(I32, (tm, CONV_DIM), 0)
    vm1 = jnp.where(rid == 0, prev_row, pltpu.roll(v, 1, 0))
    vp1 = jnp.where(rid == tm - 1, next_row, pltpu.roll(v, tm - 1, 0))
    cw = convw_ref[0]
    y_conv = cb_ref[...].astype(F32) * (vm1 * cw[0:1] + v * cw[1:2] + vp1 * cw[2:3])

    zp = zp_ref[...]
    ext = jnp.concatenate(
        [zpp_ref[...] * keep_prev, zp, zpn_ref[...] * keep_next,
         jnp.zeros((POOL_EXT - tm - 2 * POOL_HALO, POOL_DIM), F32)], axis=0).astype(BF16)
    ext_a, ext_b = ext[:, 0:LANES], ext[:, LANES:2 * LANES]
    lane = lax.broadcasted_iota(I32, (tm, LANES), 1)
    low = lane < POOL_DIM // 4
    sum_a = jnp.where(low, jnp.dot(band_ref[0], ext_a, preferred_element_type=F32),
                      jnp.dot(band_ref[1], ext_a, preferred_element_type=F32))
    sum_b = jnp.where(low, jnp.dot(band_ref[2], ext_b, preferred_element_type=F32),
                      jnp.dot(band_ref[3], ext_b, preferred_element_type=F32))
    sums = jnp.concatenate([sum_a, sum_b], axis=1)
    lane_p = lax.broadcasted_iota(I32, (tm, POOL_DIM), 1)
    half_w = jnp.left_shift(1, jnp.right_shift(lane_p, int(math.log2(POOL_DIM // 4))))
    pos = tg["pos0"] + rid
    cnt = (jnp.minimum(pos + half_w, tg["seg_len"]) - jnp.maximum(pos - half_w, 0)).astype(F32)
    dlt = sums / cnt - zp
    y_pool = jnp.dot(dlt.astype(BF16), wpool_ref[0], preferred_element_type=F32) * pscale_ref[0]

    y_gqa = jnp.where(is_ctx, ygc_ref[...], ygl_ref[...])
    y_mla = jnp.where(is_ctx, ymc_ref[...], yml_ref[...])
    ycat = jnp.concatenate([y_conv.astype(BF16), y_pool.astype(BF16), y_gqa, y_mla], axis=1)
    y = jnp.dot(ycat, wout_ref[0], preferred_element_type=F32)
    xn = jnp.where(is_ctx, xc_ref[...], xl_ref[...]) + gate1 * y
    xo_ref[...] = xn
    ms = jnp.mean(xn * xn, axis=-1, keepdims=True)
    h2 = (xn * lax.rsqrt(ms + NORM_EPS)) * (n2_ref[0] * (1.0 + scale2)) + shift2
    h2_ref[...] = h2

    h_hi = h2.astype(BF16)
    h_lo = (h2 - h_hi.astype(F32)).astype(BF16)
    wr = wr_ref[...]
    part = jnp.dot(h_hi, wr, preferred_element_type=F32)
    small = part[:, LANES:] + jnp.dot(h_lo, wr[:, 0:LANES], preferred_element_type=F32)
    logits = (part[:, 0:LANES] + small).T[0:N_EXPERTS]
    scores = _sigmoid(logits)
    sel = scores + br_ref[...]
    epg = EXPERTS_PER_GROUP
    n_groups = N_EXPERTS // epg
    srow = [sel[e:e + 1] for e in range(N_EXPERTS)]
    crow = [scores[e:e + 1] for e in range(N_EXPERTS)]
    gscore = [_top2_sum(*srow[g * epg:(g + 1) * epg]) for g in range(n_groups)]
    gbest = jnp.zeros_like(gscore[0]).astype(I32)
    best = gscore[0]
    for g in range(1, n_groups):
        upd = gscore[g] > best
        gbest = jnp.where(upd, g, gbest)
        best = jnp.where(upd, gscore[g], best)

    def pick(rows_, j):
        out = rows_[(n_groups - 1) * epg + j]
        for g in range(n_groups - 2, -1, -1):
            out = jnp.where(gbest == g, rows_[g * epg + j], out)
        return out

    sv = [pick(srow, j) for j in range(epg)]
    cv_ = [pick(crow, j) for j in range(epg)]
    i1 = jnp.zeros_like(gbest)
    b1 = sv[0]
    for j in range(1, epg):
        upd = sv[j] > b1
        i1 = jnp.where(upd, j, i1)
        b1 = jnp.where(upd, sv[j], b1)
    i2 = jnp.zeros_like(gbest)
    b2 = jnp.full_like(b1, -jnp.inf)
    for j in range(epg):
        upd = jnp.logical_and(i1 != j, sv[j] > b2)
        i2 = jnp.where(upd, j, i2)
        b2 = jnp.where(upd, sv[j], b2)
    s1 = cv_[epg - 1]
    s2 = cv_[epg - 1]
    for j in range(epg - 2, -1, -1):
        s1 = jnp.where(i1 == j, cv_[j], s1)
        s2 = jnp.where(i2 == j, cv_[j], s2)
    inv = 1.0 / (s1 + s2)
    e1 = gbest * epg + i1
    e2 = gbest * epg + i2

    @pl.when(tile == 0)
    def _():
        cnt_ref[...] = jnp.zeros_like(cnt_ref)

    erow = lax.broadcasted_iota(I32, (N_EXPERTS, tm), 0)
    hit1 = erow == e1
    hit2 = erow == e2
    onehot = jnp.where(hit1, 1.0, 0.0) + jnp.where(hit2, 1.0, 0.0)
    before = jnp.dot(onehot.astype(BF16), tri_ref[...], preferred_element_type=F32)
    tot = cnt_ref[:, 0:1] + before
    rank1 = jnp.sum(jnp.where(hit1, tot, 0.0), axis=0, keepdims=True).astype(I32)
    rank2 = jnp.sum(jnp.where(hit2, tot, 0.0), axis=0, keepdims=True).astype(I32)
    cnt_ref[...] = cnt_ref[...] + jnp.sum(onehot, axis=1, keepdims=True)

    w1 = lax.bitcast_convert_type(s1 * inv, I32)
    w2 = lax.bitcast_convert_type(s2 * inv, I32)
    irow = lax.broadcasted_iota(I32, (SUBLANES, tm), 0)
    info = jnp.where(irow == 0, e1, jnp.where(irow == 1, e2, jnp.where(
        irow == 2, rank1, jnp.where(irow == 3, rank2, jnp.where(
            irow == 4, w1, jnp.where(irow == 5, w2, 0))))))
    info_ref[0] = info


def _outproj(src, mix, yg, ygc, ym, ymc, mods, layer, p, geo, n_tiles):
    d = src[0].shape[1]
    nt_all, n_lat = geo["n_tiles"], geo["n_lat_tiles"]
    r = n_tiles * TM
    nt = n_tiles
    rowmap = lambda j: (j, 0)
    latmap = lambda j: (jnp.minimum(j, n_lat - 1), 0)
    ctxmap = lambda j: (jnp.clip(j - n_lat, 0, ygc.shape[0] // TM - 1), 0)
    lay3 = lambda j: (layer, 0, 0)
    c0 = lambda j: (0, 0)
    c3 = lambda j: (0, 0, 0)
    bf_blocks = TM // BF16_ROWS
    f_blocks = TM // SUBLANES

    def full3(a):
        return pl.BlockSpec((1,) + a.shape[1:], lay3)

    return pl.pallas_call(
        functools.partial(_outproj_kernel, d_model=d, geo=geo),
        grid=(n_tiles,),
        in_specs=_row_sources(src, TM, geo) + [
            pl.BlockSpec((TM, CONV_DIM), rowmap),
            pl.BlockSpec((TM, CONV_DIM), rowmap),
            pl.BlockSpec((BF16_ROWS, CONV_DIM), lambda j: (jnp.maximum(j * bf_blocks - 1, 0), 0)),
            pl.BlockSpec((BF16_ROWS, CONV_DIM),
                         lambda j: (jnp.minimum((j + 1) * bf_blocks, nt_all * bf_blocks - 1), 0)),
            pl.BlockSpec((TM, POOL_DIM), rowmap),
            pl.BlockSpec((SUBLANES, POOL_DIM), lambda j: (jnp.maximum(j * f_blocks - 1, 0), 0)),
            pl.BlockSpec((SUBLANES, POOL_DIM),
                         lambda j: (jnp.minimum((j + 1) * f_blocks, nt_all * f_blocks - 1), 0)),
            pl.BlockSpec((TM, 2 * LANES), latmap),
            pl.BlockSpec((TM, 2 * LANES), ctxmap),
            pl.BlockSpec((TM, 2 * LANES), latmap),
            pl.BlockSpec((TM, 2 * LANES), ctxmap),
            full3(mods), full3(p["conv_w"]),
            pl.BlockSpec(p["band"].shape, c3),
            full3(p["wpool"]), full3(p["pscale"]), full3(p["w_out"]), full3(p["norm2"]),
            pl.BlockSpec(p["wr"].shape, c0), pl.BlockSpec(p["br"].shape, c0),
            pl.BlockSpec(p["tri"].shape, c0),
        ],
        out_specs=[
            pl.BlockSpec((TM, d), rowmap),
            pl.BlockSpec((TM, d), rowmap),
            pl.BlockSpec((1, SUBLANES, TM), lambda j: (j, 0, 0)),
            pl.BlockSpec((N_EXPERTS, LANES), c0),
        ],
        out_shape=[
            jax.ShapeDtypeStruct((r, d), F32),
            jax.ShapeDtypeStruct((r, d), F32),
            jax.ShapeDtypeStruct((nt, SUBLANES, TM), I32),
            jax.ShapeDtypeStruct((N_EXPERTS, LANES), F32),
        ],
        compiler_params=_cparams(("arbitrary",)),
        name="outproj",
    )(src[0], src[2], mix["cb"], mix["cv"], mix["cv"], mix["cv"], mix["zp"], mix["zp"], mix["zp"],
      yg, ygc, ym, ymc, mods, p["conv_w"], p["band"], p["wpool"], p["pscale"], p["w_out"],
      p["norm2"], p["wr"], p["br"], p["tri"])


def _slot_copies(block_ref, idx_ref, sem):
    n = block_ref.shape[2]
    return [pltpu.make_async_copy(block_ref.at[0, k], idx_ref.at[pl.ds(k * n, n)], sem) for k in range(2)]


def _start_all(copies):
    for cp in copies:
        cp.start()


def _wait_all(copies):
    for cp in copies:
        cp.wait()


def _row_copies(idx_ref, n_rows, make):
    def body(t, carry):
        for u in range(SUBLANES):
            r = t * SUBLANES + u
            make(0, t, u, idx_ref[r]).start()
            make(1, t, u, idx_ref[n_rows + r]).start()
        return carry
    lax.fori_loop(0, n_rows // SUBLANES, body, 0)


def _scatter_kernel(ps_ref, zs_ref, has_ref, nu_ref, info_ref, info_next_ref, h2_ref, dest_ref, xs_ref,
                    dsm_ref, zbuf_ref, stage_ref, nxt_ref, sem_ref, *, first_spare, n_blocks, n_tiles):
    j = pl.program_id(0)
    tm = h2_ref.shape[0]

    @pl.when(j == 0)
    def _():
        zbuf_ref[...] = jnp.zeros_like(zbuf_ref)

        def zero_copy(start):
            return pltpu.make_async_copy(
                zbuf_ref, xs_ref.at[pl.ds(pl.multiple_of(start, BM), BM)], sem_ref.at[2])

        for act in ("start", "wait"):
            for e in range(N_EXPERTS):
                @pl.when(has_ref[e] > 0)
                def _():
                    getattr(zero_copy(zs_ref[e]), act)()
            for jb in range(first_spare, n_blocks):
                @pl.when(jb >= nu_ref[0])
                def _():
                    getattr(zero_copy(jb * BM), act)()

    def slots_of(info):
        e1, e2 = info[0:1], info[1:2]
        d1, d2 = info[2:3], info[3:4]
        for e in range(N_EXPERTS):
            d1 = d1 + jnp.where(e1 == e, ps_ref[e], 0)
            d2 = d2 + jnp.where(e2 == e, ps_ref[e], 0)
        irow = lax.broadcasted_iota(I32, (SUBLANES, tm), 0)
        return jnp.where(irow == 0, d1, jnp.where(irow == 1, d2, 0))

    dest_ref[0] = slots_of(info_ref[0])

    @pl.when(j == 0)
    def _():
        first = _slot_copies(dest_ref, dsm_ref, sem_ref.at[3])
        _start_all(first)
        _wait_all(first)

    par = j % 2
    stage_ref[par] = h2_ref[...].reshape(stage_ref.shape[1:])

    def row_copy(_, t, u, slot):
        return pltpu.make_async_copy(stage_ref.at[par, t, pl.ds(u, 1)], xs_ref.at[pl.ds(slot, 1)],
                                     sem_ref.at[par])

    _row_copies(dsm_ref, tm, row_copy)

    nxt_ref[0] = slots_of(info_next_ref[0])
    nxt = _slot_copies(nxt_ref, dsm_ref, sem_ref.at[3])

    @pl.when(j + 1 < n_tiles)
    def _():
        _start_all(nxt)

    def retire(which):
        for _ in range(2):
            pltpu.make_async_copy(h2_ref, xs_ref.at[pl.ds(0, tm)], sem_ref.at[which]).wait()

    @pl.when(j > 0)
    def _():
        retire(1 - par)

    @pl.when(j == n_tiles - 1)
    def _():
        retire(par)

    @pl.when(j + 1 < n_tiles)
    def _():
        _wait_all(nxt)


def _scatter(info, h2, pad_start, zero_start, has_rows, n_used, n_slots):
    r, d = h2.shape
    nt = r // TM
    return pl.pallas_call(
        functools.partial(_scatter_kernel, first_spare=-(-2 * r // BM), n_blocks=n_slots // BM,
                          n_tiles=nt),
        grid_spec=pltpu.PrefetchScalarGridSpec(
            num_scalar_prefetch=4,
            grid=(nt,),
            in_specs=[
                pl.BlockSpec((1, SUBLANES, TM), lambda j, *_: (j, 0, 0)),
                pl.BlockSpec((1, SUBLANES, TM), lambda j, *_: (jnp.minimum(j + 1, nt - 1), 0, 0)),
                pl.BlockSpec((TM, d), lambda j, *_: (j, 0)),
            ],
            out_specs=[
                pl.BlockSpec((1, SUBLANES, TM), lambda j, *_: (j, 0, 0)),
                pl.BlockSpec(memory_space=pl.ANY),
            ],
            scratch_shapes=[
                pltpu.SMEM((2 * TM,), I32),
                pltpu.VMEM((BM, d), F32),
                pltpu.VMEM((2, TM // SUBLANES, SUBLANES, d), F32),
                pltpu.VMEM((1, SUBLANES, TM), I32),
                pltpu.SemaphoreType.DMA((4,)),
            ],
        ),
        out_shape=[
            jax.ShapeDtypeStruct((nt, SUBLANES, TM), I32),
            jax.ShapeDtypeStruct((n_slots, d), F32),
        ],
        compiler_params=_cparams(("arbitrary",)),
        name="moe_scatter",
    )(pad_start, zero_start, has_rows, n_used, info, info, h2)


def _expert_kernel(be_ref, nu_ref, xs_ref, wg_ref, wu_ref, wd_ref, y_ref, wgb, wub, wdb):
    j = pl.program_id(0)

    @pl.when(j < nu_ref[0])
    def _():
        e = be_ref[j]
        prev = be_ref[jnp.maximum(j - 1, 0)]

        @pl.when(jnp.logical_or(j == 0, e != prev))
        def _():
            wgb[...] = wg_ref[0].astype(BF16)
            wub[...] = wu_ref[0].astype(BF16)
            wdb[...] = wd_ref[0].astype(BF16)

        x = xs_ref[...].astype(BF16)
        a = jnp.dot(x, wgb[...], preferred_element_type=F32)
        u = jnp.dot(x, wub[...], preferred_element_type=F32)
        hmid = (a * _sigmoid(a) * u).astype(BF16)
        y_ref[...] = jnp.dot(hmid, wdb[...], preferred_element_type=F32)

    @pl.when(j >= nu_ref[0])
    def _():
        y_ref[...] = jnp.zeros_like(y_ref)


def _experts(xs, block_exp, n_used, w_gate, w_up, w_down, layer):
    n_slots, d = xs.shape
    de = w_gate.shape[-1]
    nbm = n_slots // BM
    n_exp = w_gate.shape[1]

    def blk(j, be, nu):
        return (jnp.minimum(j, nu[0] - 1), 0)

    def wmap(j, be, nu):
        return (layer * n_exp + be[jnp.minimum(j, nu[0] - 1)], 0, 0)

    wg = w_gate.reshape((-1,) + w_gate.shape[2:])
    wu = w_up.reshape((-1,) + w_up.shape[2:])
    wd = w_down.reshape((-1,) + w_down.shape[2:])
    return pl.pallas_call(
        _expert_kernel,
        grid_spec=pltpu.PrefetchScalarGridSpec(
            num_scalar_prefetch=2,
            grid=(nbm,),
            in_specs=[
                pl.BlockSpec((BM, d), blk),
                pl.BlockSpec((1, d, de), wmap),
                pl.BlockSpec((1, d, de), wmap),
                pl.BlockSpec((1, de, d), wmap),
            ],
            out_specs=pl.BlockSpec((BM, d), lambda j, be, nu: (j, 0)),
            scratch_shapes=[
                pltpu.VMEM((d, de), BF16),
                pltpu.VMEM((d, de), BF16),
                pltpu.VMEM((de, d), BF16),
            ],
        ),
        out_shape=jax.ShapeDtypeStruct((n_slots, d), F32),
        compiler_params=_cparams(("arbitrary",)),
        name="moe_experts",
    )(block_exp, n_used, xs, wg, wu, wd)


def _combine_kernel(dest_ref, dest_next_ref, dest_next2_ref, info_ref, x_ref, mods_ref, y_ref, o_ref,
                    dsm_ref, ybuf_ref, sem_ref, *, d_model, geo, n_steps):
    j = pl.program_id(0)
    d = d_model
    tm = x_ref.shape[0]
    row = _tile_geometry(j, geo)["mod_row"]
    gate2 = mods_ref[0, pl.ds(row, 1), 5 * d:6 * d]
    cur = j % 2

    def load_slots(idx_block_ref):
        copies = _slot_copies(idx_block_ref, dsm_ref, sem_ref.at[2])
        _start_all(copies)
        _wait_all(copies)

    def gather(buf):
        def row_copy(k, t, u, slot):
            return pltpu.make_async_copy(y_ref.at[pl.ds(slot, 1)],
                                         ybuf_ref.at[buf, k, t, pl.ds(u, 1)], sem_ref.at[buf])

        _row_copies(dsm_ref, tm, row_copy)

    @pl.when(j == 0)
    def _():
        load_slots(dest_ref)
        gather(0)
        load_slots(dest_next_ref)

    @pl.when(j + 1 < n_steps)
    def _():
        gather(1 - cur)

    nxt2 = _slot_copies(dest_next2_ref, dsm_ref, sem_ref.at[2])

    @pl.when(j + 2 < n_steps)
    def _():
        _start_all(nxt2)

    for k in range(2):
        pltpu.make_async_copy(y_ref.at[pl.ds(0, tm)], o_ref, sem_ref.at[cur]).wait()

    info = info_ref[0]
    wrow = lax.broadcasted_iota(I32, (LANES, tm), 0)
    w_lanes = jnp.where(wrow == 0, lax.bitcast_convert_type(info[4:5], F32),
                        jnp.where(wrow == 1, lax.bitcast_convert_type(info[5:6], F32), 0.0))
    w_rows = w_lanes.T
    y1 = ybuf_ref[cur, 0].reshape(tm, d)
    y2 = ybuf_ref[cur, 1].reshape(tm, d)
    o_ref[...] = x_ref[...] + gate2 * (w_rows[:, 0:1] * y1 + w_rows[:, 1:2] * y2)

    @pl.when(j + 2 < n_steps)
    def _():
        _wait_all(nxt2)


def _combine(dest, info, xn, mods, y, layer, geo):
    r, d = xn.shape
    n_steps = r // TM
    return pl.pallas_call(
        functools.partial(_combine_kernel, d_model=d, geo=geo, n_steps=n_steps),
        grid=(n_steps,),
        in_specs=[
            pl.BlockSpec((1, SUBLANES, TM), lambda j: (j, 0, 0)),
            pl.BlockSpec((1, SUBLANES, TM), lambda j: (jnp.minimum(j + 1, n_steps - 1), 0, 0)),
            pl.BlockSpec((1, SUBLANES, TM), lambda j: (jnp.minimum(j + 2, n_steps - 1), 0, 0)),
            pl.BlockSpec((1, SUBLANES, TM), lambda j: (j, 0, 0)),
            pl.BlockSpec((TM, d), lambda j: (j, 0)),
            pl.BlockSpec((1,) + mods.shape[1:], lambda j: (layer, 0, 0)),
            pl.BlockSpec(memory_space=pl.ANY),
        ],
        out_specs=pl.BlockSpec((TM, d), lambda j: (j, 0)),
        out_shape=jax.ShapeDtypeStruct((n_steps * TM, d), F32),
        scratch_shapes=[
            pltpu.SMEM((2 * TM,), I32),
            pltpu.VMEM((2, 2, TM // SUBLANES, SUBLANES, d), F32),
            pltpu.SemaphoreType.DMA((3,)),
        ],
        compiler_params=_cparams(("arbitrary",)),
        name="moe_combine",
    )(dest, dest, dest, info, xn, mods, y)


def _pad_heads(w, n_heads, width):
    lead = w.shape[:-1]
    w = w.reshape(lead + (n_heads, width))
    w = jnp.pad(w, [(0, 0)] * len(lead) + [(0, 0), (0, LANES - width)])
    return w.reshape(lead + (n_heads * LANES,))


def _rope_tables(id_rows, nl):
    t = jnp.arange(nl)
    row_id = (t // GRID_W).astype(F32)
    col_id = (t % GRID_W).astype(F32)

    def angles(rot_dim):
        n_freq = rot_dim // 4
        inv_freq = jnp.power(ROPE_THETA, -jnp.arange(n_freq, dtype=F32) / n_freq)
        return jnp.concatenate([row_id[:, None] * inv_freq, col_id[:, None] * inv_freq], axis=-1)

    def with_ctx(tab, fill):
        return jnp.concatenate([tab, jnp.full((id_rows, LANES), fill, F32)], axis=0)

    ag = angles(HEAD_DIM)
    one_g = jnp.ones((nl, LANES - HEAD_DIM), F32)
    cos_g = jnp.concatenate([jnp.cos(ag), jnp.cos(ag), one_g], axis=-1)
    sin_g = jnp.concatenate([-jnp.sin(ag), jnp.sin(ag), 0.0 * one_g], axis=-1)
    am = angles(MLA_ROPE_DIM)
    one_n = jnp.ones((nl, MLA_NOPE_DIM), F32)
    one_t = jnp.ones((nl, LANES - MLA_QK_DIM), F32)
    cos_m = jnp.concatenate([one_n, jnp.cos(am), jnp.cos(am), one_t], axis=-1)
    sin_m = jnp.concatenate([0.0 * one_n, -jnp.sin(am), jnp.sin(am), 0.0 * one_t], axis=-1)
    return with_ctx(cos_g, 1.0), with_ctx(sin_g, 0.0), with_ctx(cos_m, 1.0), with_ctx(sin_m, 0.0)


def _pool_band():
    t = np.arange(TM)[:, None]
    src = np.arange(POOL_EXT)[None, :] - POOL_HALO
    live = np.arange(POOL_EXT)[None, :] < TM + 2 * POOL_HALO
    mats = [((src >= t - w // 2) & (src < t + w // 2) & live) for w in POOL_WINDOWS]
    return jnp.asarray(np.stack(mats).astype(np.float32), dtype=BF16)


def _prep_params(w_in, norm1, norm2, conv_w, w_pool, pool_scale, gqa_q_norm, gqa_k_norm,
                 mla_q_norm, mla_kv_norm, mla_w_uq, mla_w_uk, mla_w_uv, mla_qk_q_norm,
                 mla_qk_k_norm, w_out, w_router, b_router, id_rows, nl):
    dep = w_in.shape[0]
    o = 0
    pieces = {}
    for name, n in (("conv", 3 * CONV_DIM), ("pool", POOL_DIM), ("gq", GQA_HEADS * HEAD_DIM),
                    ("gk", GQA_KV_HEADS * HEAD_DIM), ("gv", GQA_KV_HEADS * HEAD_DIM),
                    ("mq", MLA_Q_RANK), ("mkv", MLA_KV_RANK), ("mkr", MLA_ROPE_DIM)):
        pieces[name] = w_in[..., o:o + n]
        o += n
    mkr = jnp.pad(pieces["mkr"], ((0, 0), (0, 0), (MLA_NOPE_DIM, LANES - MLA_QK_DIM)))
    w_in_p = jnp.concatenate([
        pieces["conv"], pieces["pool"], _pad_heads(pieces["gq"], GQA_HEADS, HEAD_DIM),
        _pad_heads(pieces["gk"], GQA_KV_HEADS, HEAD_DIM),
        pieces["mq"], pieces["mkv"], mkr], axis=-1).astype(BF16)
    wgvt = jnp.swapaxes(_pad_heads(pieces["gv"], GQA_KV_HEADS, HEAD_DIM), 1, 2).astype(BF16)
    wuvt = jnp.swapaxes(_pad_heads(mla_w_uv, MLA_HEADS, MLA_V_DIM), 1, 2).astype(BF16)
    eye = jnp.eye(len(POOL_WINDOWS), dtype=F32)
    wpool = jnp.einsum("gh,dgij->dgihj", eye, w_pool).reshape(dep, POOL_DIM, POOL_DIM).astype(BF16)
    cos_g, sin_g, cos_m, sin_m = _rope_tables(id_rows, nl)
    tri = np.triu(np.ones((TM, TM), np.float32), 1)
    wr_pad = jnp.pad(w_router, ((0, 0), (0, LANES - w_router.shape[1])))
    wr_hi = wr_pad.astype(BF16)
    wr_split = jnp.concatenate([wr_hi, (wr_pad - wr_hi.astype(F32)).astype(BF16)], axis=1)

    def row3(a):
        return a.reshape(dep, 1, a.shape[-1])

    return {
        "w_in": w_in_p,
        "norm1": row3(norm1), "norm2": row3(norm2),
        "gq": row3(_pad_heads(gqa_q_norm * (GQA_SCALE * LOG2E), 1, HEAD_DIM)),
        "gk": row3(_pad_heads(gqa_k_norm, 1, HEAD_DIM)),
        "mqn": row3(mla_q_norm), "mkvn": row3(mla_kv_norm),
        "qkq": row3(_pad_heads(mla_qk_q_norm * (MLA_SCALE * LOG2E), 1, MLA_QK_DIM)),
        "qkk": row3(_pad_heads(mla_qk_k_norm, 1, MLA_QK_DIM)),
        "wuq": _pad_heads(mla_w_uq, MLA_HEADS, MLA_QK_DIM).astype(BF16),
        "wuk": _pad_heads(mla_w_uk, MLA_HEADS, MLA_NOPE_DIM).astype(BF16),
        "wgvt": wgvt, "wuvt": wuvt,
        "cos_g": cos_g, "sin_g": sin_g, "cos_m": cos_m, "sin_m": sin_m,
        "conv_w": jnp.pad(conv_w, ((0, 0), (0, SUBLANES - conv_w.shape[1]), (0, 0))),
        "band": _pool_band(),
        "wpool": wpool, "pscale": row3(pool_scale),
        "w_out": w_out.astype(BF16),
        "wr": wr_split, "br": b_router.reshape(-1, 1),
        "tri": jnp.asarray(tri, dtype=BF16),
    }


def _moe_plan(counts):
    counts = counts.astype(I32)
    padded = ((counts + BM - 1) // BM) * BM
    pad_end = jnp.cumsum(padded)
    pad_start = pad_end - padded
    return pad_start, pad_end, padded


def kernel(x, c, ctx, c_ctx, w_mod, b_mod, norm1, norm2, w_in, conv_w, w_pool, pool_scale,
           gqa_q_norm, gqa_k_norm, mla_q_norm, mla_kv_norm, mla_w_uq, mla_w_uk, mla_w_uv,
           mla_qk_q_norm, mla_qk_k_norm, w_out, w_router, b_router, w_gate, w_up, w_down):
    nb, nl, d = x.shape
    nc = ctx.shape[1]
    depth = w_mod.shape[0]
    assert nc % TM == 0 and nl % TM == 0 and nl % GRID_W == 0 and nb < SUBLANES
    lt, nct = nl // TM, nc // TM
    wide = nl % INPROJ_ROWS == 0 and (nb * nc) % INPROJ_ROWS == 0
    geo = {"nbatch": nb, "nc": nc, "nl": nl, "lt": lt, "nct": nct,
           "n_lat_tiles": nb * lt, "n_tiles": nb * (lt + nct),
           "inproj_rows": INPROJ_ROWS if wide else TM}

    p = _prep_params(w_in, norm1, norm2, conv_w, w_pool, pool_scale, gqa_q_norm, gqa_k_norm,
                     mla_q_norm, mla_kv_norm, mla_w_uq, mla_w_uk, mla_w_uv, mla_qk_q_norm,
                     mla_qk_k_norm, w_out, w_router, b_router, geo["inproj_rows"], nl)
    cvec = jnp.concatenate([c, c_ctx[None, :], jnp.zeros((SUBLANES - nb - 1, d), F32)], axis=0)
    mods = _adaln(cvec, w_mod, b_mod)

    src = (x.reshape(nb * nl, d), 0, ctx.reshape(nb * nc, d), 0)
    for i in range(depth):
        last = i == depth - 1
        mix = _inproj(src, mods, i, p, geo)
        yg = _attention(mix["qg"], mix["kg"], mix["vgt"], True, geo, False)
        ym = _attention(mix["qm"], mix["km"], mix["vmt"], False, geo, False)
        if last:
            n_tiles, ygc, ymc = geo["n_lat_tiles"], yg, ym
        else:
            n_tiles = geo["n_tiles"]
            ygc = _attention(mix["qg"], mix["kg"], mix["vgt"], True, geo, True)
            ymc = _attention(mix["qm"], mix["km"], mix["vmt"], False, geo, True)
        xn, h2, info, cnt = _outproj(src, mix, yg, ygc, ym, ymc, mods, i, p, geo, n_tiles)
        n_blocks = -(-2 * n_tiles * TM // BM) + N_EXPERTS
        n_slots = n_blocks * BM
        pad_start, pad_end, padded = _moe_plan(cnt[:, 0])
        n_used = (pad_end[-1:] // BM).astype(I32)
        block_row0 = jnp.arange(n_blocks, dtype=I32) * BM
        block_exp = jnp.minimum(jnp.sum((pad_end[None, :] <= block_row0[:, None]).astype(I32), axis=1),
                                N_EXPERTS - 1)
        dest, xs = _scatter(info, h2, pad_start, jnp.maximum(pad_end - BM, 0),
                            (padded > 0).astype(I32), n_used, n_slots)
        y = _experts(xs, block_exp, n_used, w_gate, w_up, w_down, i)
        xa = _combine(dest, info, xn, mods, y, i, geo)
        src = (xa, 0, xa, nb * nl)
    return xa.reshape(nb, nl, d)
```

```python
import functools
import math

import numpy as np
import jax
import jax.numpy as jnp
from jax import lax
from jax.experimental import pallas as pl
from jax.experimental.pallas import tpu as pltpu

F32 = jnp.float32
BF16 = jnp.bfloat16
I32 = jnp.int32

GRID_W = 64
CONV_DIM = 256
POOL_DIM = 256
POOL_WINDOWS = (2, 4, 8, 16)
HEAD_DIM = 64
GQA_HEADS = 4
GQA_KV_HEADS = 2
MLA_HEADS = 4
MLA_NOPE_DIM = 64
MLA_ROPE_DIM = 32
MLA_QK_DIM = MLA_NOPE_DIM + MLA_ROPE_DIM
MLA_V_DIM = 64
MLA_Q_RANK = 256
MLA_KV_RANK = 128
N_EXPERTS = 16
EXPERTS_PER_GROUP = 4
ROPE_THETA = 10000.0
NORM_EPS = 1e-6
LOG2E = 1.4426950408889634
GQA_SCALE = HEAD_DIM ** -0.5
MLA_SCALE = MLA_QK_DIM ** -0.5

LANES = 128
SUBLANES = 8
BF16_ROWS = 16
VMEM_LIMIT = 56 * 1024 * 1024

TM = 256
ATT_ROWS = 2048
INPROJ_ROWS = 512
BM = 512
V_ROWS = HEAD_DIM + BF16_ROWS
ATT_PAIR_UNROLL = 4
POOL_EXT = 512
POOL_HALO = 8

ZC_B, ZC_C, ZC_U, ZC_P = 0, 256, 512, 768
ZC_GQ = 1024
ZC_GK = ZC_GQ + GQA_HEADS * LANES
ZC_MQ = ZC_GK + GQA_KV_HEADS * LANES
ZC_MKV = ZC_MQ + MLA_Q_RANK
ZC_MKR = ZC_MKV + MLA_KV_RANK

ADALN_COLS = 1536
SOFTMAX_FLOOR = -1e30
HIGHEST = lax.Precision.HIGHEST


def _cparams(sem, vmem=VMEM_LIMIT):
    return pltpu.CompilerParams(dimension_semantics=sem, vmem_limit_bytes=vmem)


def _sigmoid(v):
    return 1.0 / (1.0 + jnp.exp(-v))


def _adaln_kernel(c_ref, w_ref, b_ref, o_ref):
    c = c_ref[...]
    s = c * _sigmoid(c)
    o_ref[0] = jnp.dot(s, w_ref[0], preferred_element_type=F32, precision=HIGHEST) + b_ref[0]


def _adaln(cvec, w_mod, b_mod):
    depth, d, n6 = w_mod.shape
    tn = ADALN_COLS if n6 % ADALN_COLS == 0 else n6
    rows = cvec.shape[0]
    return pl.pallas_call(
        _adaln_kernel,
        grid=(depth, n6 // tn),
        in_specs=[
            pl.BlockSpec((rows, d), lambda i, n: (0, 0)),
            pl.BlockSpec((1, d, tn), lambda i, n: (i, 0, n)),
            pl.BlockSpec((1, 1, tn), lambda i, n: (i, 0, n)),
        ],
        out_specs=pl.BlockSpec((1, rows, tn), lambda i, n: (i, 0, n)),
        out_shape=jax.ShapeDtypeStruct((depth, rows, n6), F32),
        compiler_params=_cparams(("arbitrary", "arbitrary")),
        name="adaln",
    )(cvec, w_mod, b_mod.reshape(depth, 1, n6))


def _tile_geometry(j, geo):
    lt, nct, n_lat = geo["lt"], geo["nct"], geo["n_lat_tiles"]
    is_ctx = j >= n_lat
    jc = j - n_lat
    sample = jnp.where(is_ctx, jc // nct, j // lt)
    jt = jnp.where(is_ctx, jc % nct, j % lt)
    return {
        "is_ctx": is_ctx,
        "mod_row": jnp.where(is_ctx, geo["nbatch"], sample),
        "first": jt == 0,
        "last": jt == jnp.where(is_ctx, nct, lt) - 1,
        "pos0": jt * TM,
        "seg_len": jnp.where(is_ctx, geo["nc"], geo["nl"]),
    }


def _norm_rope(slab, gain, cos, sin, n_valid, first_half, half):
    ssq = jnp.dot((slab * slab).astype(BF16), jnp.ones((LANES, LANES), BF16), preferred_element_type=F32)
    y = slab * lax.rsqrt(ssq * (1.0 / n_valid) + NORM_EPS) * gain
    partner = jnp.where(first_half, pltpu.roll(y, LANES - half, 1), pltpu.roll(y, half, 1))
    return y * cos + partner * sin


def _inproj_kernel(xl_ref, xc_ref, mods_ref, n1_ref, cg_ref, sg_ref, cm_ref, sm_ref, win_ref,
                   gq_ref, gk_ref, mqn_ref, mkvn_ref, qkq_ref, qkk_ref, wuq_ref, wuk_ref,
                   wgvt_ref, wuvt_ref,
                   cb_ref, cv_ref, zp_ref, qg_ref, kg_ref, vgt_ref, qm_ref, km_ref, vmt_ref,
                   *, d_model, nbatch, n_lat_steps, lat_steps_per_sample):
    d = d_model
    step = pl.program_id(0)
    row = jnp.where(step >= n_lat_steps, nbatch, step // lat_steps_per_sample)
    shift = mods_ref[0, pl.ds(row, 1), 0:d]
    scale = mods_ref[0, pl.ds(row, 1), d:2 * d]
    x = jnp.where(step >= n_lat_steps, xc_ref[...], xl_ref[...])
    ms = jnp.mean(x * x, axis=-1, keepdims=True)
    h = ((x * lax.rsqrt(ms + NORM_EPS)) * (n1_ref[0] * (1.0 + scale)) + shift).astype(BF16)
    nt_dims = (((1,), (1,)), ((), ()))

    def proj(c0, width):
        return jnp.dot(h, win_ref[0, :, c0:c0 + width], preferred_element_type=F32)

    def store_transposed(vt_ref, vt):
        srow = lax.broadcasted_iota(I32, vt.shape, 0)
        vt = jnp.where(jnp.bitwise_and(srow, LANES - 1) == HEAD_DIM, 1.0, vt).astype(BF16)
        for c in range(vt_ref.shape[0]):
            vt_ref[c] = vt[:, c * TM:(c + 1) * TM]

    tm = x.shape[0]
    lane = lax.broadcasted_iota(I32, (tm, LANES), 1)

    cm, sm = cm_ref[...], sm_ref[...]
    m_first = lane < MLA_NOPE_DIM + MLA_ROPE_DIM // 2
    zq = proj(ZC_MQ, MLA_Q_RANK)
    cq = zq * lax.rsqrt(jnp.mean(zq * zq, axis=-1, keepdims=True) + NORM_EPS) * mqn_ref[0]
    qpre = jnp.dot(cq.astype(BF16), wuq_ref[0], preferred_element_type=F32)
    zk = proj(ZC_MKV, MLA_KV_RANK + LANES)
    zkv, zkr = zk[:, 0:MLA_KV_RANK], zk[:, MLA_KV_RANK:]
    ckv = zkv * lax.rsqrt(jnp.mean(zkv * zkv, axis=-1, keepdims=True) + NORM_EPS) * mkvn_ref[0]
    ckv = ckv.astype(BF16)
    kvp = jnp.dot(ckv, wuk_ref[0], preferred_element_type=F32)
    store_transposed(vmt_ref, lax.dot_general(wuvt_ref[0], ckv, nt_dims, preferred_element_type=F32))
    for hd in range(MLA_HEADS):
        sl = slice(hd * LANES, (hd + 1) * LANES)
        qm_ref[:, sl] = _norm_rope(qpre[:, sl], qkq_ref[0], cm, sm, MLA_QK_DIM,
                                   m_first, MLA_ROPE_DIM // 2).astype(BF16)
        km_ref[:, sl] = _norm_rope(kvp[:, sl] + zkr, qkk_ref[0], cm, sm, MLA_QK_DIM,
                                   m_first, MLA_ROPE_DIM // 2).astype(BF16)

    cg, sg = cg_ref[...], sg_ref[...]
    g_first = lane < HEAD_DIM // 2
    zg = proj(ZC_GQ, (GQA_HEADS + GQA_KV_HEADS) * LANES)
    for hd in range(GQA_HEADS):
        qg_ref[:, hd * LANES:(hd + 1) * LANES] = _norm_rope(
            zg[:, hd * LANES:(hd + 1) * LANES], gq_ref[0], cg, sg, HEAD_DIM, g_first,
            HEAD_DIM // 2).astype(BF16)
    for hd in range(GQA_KV_HEADS):
        slab = zg[:, (GQA_HEADS + hd) * LANES:(GQA_HEADS + hd + 1) * LANES]
        kg_ref[:, hd * LANES:(hd + 1) * LANES] = _norm_rope(
            slab, gk_ref[0], cg, sg, HEAD_DIM, g_first, HEAD_DIM // 2).astype(BF16)
    store_transposed(vgt_ref, lax.dot_general(wgvt_ref[0], h, nt_dims, preferred_element_type=F32))

    zc = proj(ZC_B, 3 * CONV_DIM + POOL_DIM)
    cb_ref[...] = zc[:, ZC_B:ZC_B + CONV_DIM].astype(BF16)
    cv_ref[...] = (zc[:, ZC_C:ZC_C + CONV_DIM] * zc[:, ZC_U:ZC_U + CONV_DIM]).astype(BF16)
    zp_ref[...] = zc[:, ZC_P:ZC_P + POOL_DIM]


def _row_sources(src, rows_per_step, geo):
    lat, lat_row0, cx, cx_row0 = src
    n_lat = geo["nbatch"] * geo["nl"] // rows_per_step
    n_ctx = geo["nbatch"] * geo["nc"] // rows_per_step
    lat0, cx0 = lat_row0 // rows_per_step, cx_row0 // rows_per_step
    d = lat.shape[1]
    return [pl.BlockSpec((rows_per_step, d), lambda j: (lat0 + jnp.minimum(j, n_lat - 1), 0)),
            pl.BlockSpec((rows_per_step, d), lambda j: (cx0 + jnp.clip(j - n_lat, 0, n_ctx - 1), 0))]


def _inproj(src, mods, layer, p, geo):
    d = src[0].shape[1]
    r = geo["n_tiles"] * TM
    tmi = geo["inproj_rows"]
    n_lat_steps = geo["nbatch"] * geo["nl"] // tmi
    lat_steps_per_sample = geo["nl"] // tmi
    rowmap = lambda j: (j, 0)
    posmap = lambda j: (jnp.where(j >= n_lat_steps, lat_steps_per_sample, j % lat_steps_per_sample), 0)
    lay3 = lambda j: (layer, 0, 0)

    def full3(a):
        return pl.BlockSpec((1,) + a.shape[1:], lay3)

    outs = [("cb", CONV_DIM, BF16, False), ("cv", CONV_DIM, BF16, False), ("zp", POOL_DIM, F32, False),
            ("qg", GQA_HEADS * LANES, BF16, False), ("kg", GQA_KV_HEADS * LANES, BF16, False),
            ("vgt", GQA_KV_HEADS * LANES, BF16, True), ("qm", MLA_HEADS * LANES, BF16, False),
            ("km", MLA_HEADS * LANES, BF16, False), ("vmt", MLA_HEADS * LANES, BF16, True)]

    def out_spec(w, transposed):
        if transposed:
            return pl.BlockSpec((tmi // TM, w, TM), lambda j: (j, 0, 0))
        return pl.BlockSpec((tmi, w), rowmap)

    def out_shape(w, dt, transposed):
        return jax.ShapeDtypeStruct((r // TM, w, TM) if transposed else (r, w), dt)

    res = pl.pallas_call(
        functools.partial(_inproj_kernel, d_model=d, nbatch=geo["nbatch"], n_lat_steps=n_lat_steps,
                          lat_steps_per_sample=lat_steps_per_sample),
        grid=(r // tmi,),
        in_specs=_row_sources(src, tmi, geo) + [
            full3(mods), full3(p["norm1"]),
            pl.BlockSpec((tmi, LANES), posmap), pl.BlockSpec((tmi, LANES), posmap),
            pl.BlockSpec((tmi, LANES), posmap), pl.BlockSpec((tmi, LANES), posmap),
            full3(p["w_in"]), full3(p["gq"]), full3(p["gk"]), full3(p["mqn"]), full3(p["mkvn"]),
            full3(p["qkq"]), full3(p["qkk"]), full3(p["wuq"]), full3(p["wuk"]),
            full3(p["wgvt"]), full3(p["wuvt"]),
        ],
        out_specs=[out_spec(w, t) for _, w, _, t in outs],
        out_shape=[out_shape(w, dt, t) for _, w, dt, t in outs],
        compiler_params=_cparams(("arbitrary",)),
        name="inproj",
    )(src[0], src[2], mods, p["norm1"], p["cos_g"], p["sin_g"], p["cos_m"], p["sin_m"], p["w_in"],
      p["gq"], p["gk"], p["mqn"], p["mkvn"], p["qkq"], p["qkk"], p["wuq"], p["wuk"],
      p["wgvt"], p["wuvt"])
    return {name: a for (name, _, _, _), a in zip(outs, res)}


def _attn_kernel(q_ref, kl_ref, kc_ref, vtl_ref, vtc_ref, o_ref, q_st, s_buf, p_buf, a_buf, t_buf, m_ref, acc_ref,
                 *, shared_kv, n_lat, n_ctx, tk):
    tq = q_ref.shape[0]
    n_steps = n_lat + n_ctx
    if shared_kv:
        q_st[0] = jnp.concatenate([q_ref[:, 0:LANES], q_ref[:, LANES:2 * LANES]], axis=0)
        cols = [0]
    else:
        q_st[0] = q_ref[:, 0:LANES]
        q_st[1] = q_ref[:, LANES:2 * LANES]
        cols = [0, LANES]
    for si, col in enumerate(cols):
        q_s, m_s, acc_s = q_st.at[si], m_ref.at[si], acc_ref.at[si]
        m_s[...] = jnp.full(m_s.shape, SOFTMAX_FLOOR, F32)
        acc_s[...] = jnp.zeros(acc_s.shape, F32)
        p_buf[1] = jnp.zeros(p_buf.shape[1:], BF16)
        a_buf[1] = jnp.ones(a_buf.shape[1:], F32)

        def pick(i, lat_fn, ctx_fn):
            ctx = ctx_fn(jnp.clip(i - n_lat, 0, n_ctx - 1))
            if n_lat == 0:
                return ctx
            return jnp.where(i >= n_lat, ctx, lat_fn(jnp.clip(i, 0, n_lat - 1)))

        def scores(i, slot, q_s=q_s, col=col):
            k = pick(i, lambda t: kl_ref[pl.ds(pl.multiple_of(t * tk, tk), tk), col:col + LANES],
                     lambda t: kc_ref[pl.ds(pl.multiple_of(t * tk, tk), tk), col:col + LANES])
            s = lax.dot_general(k, q_s[...], (((1,), (1,)), ((), ())),
                                preferred_element_type=F32)
            s_buf[slot] = s
            t_buf[slot] = jnp.broadcast_to(jnp.max(s, axis=0, keepdims=True), t_buf.shape[1:])

        def softmax(slot, m_s=m_s):
            s = s_buf[slot]
            m_old = m_s[...]
            m_new = jnp.maximum(m_old, t_buf[slot])
            a_buf[slot] = jnp.exp2(m_old - m_new)
            p_buf[slot] = jnp.exp2(s - m_new[0:1]).astype(BF16)
            m_s[...] = m_new

        def accumulate(i, slot, acc_s=acc_s, col=col):
            vt = pick(i, lambda t: vtl_ref[t, col:col + V_ROWS, :],
                      lambda t: vtc_ref[t, col:col + V_ROWS, :])
            acc_s[...] = a_buf[slot][0:1] * acc_s[...] + jnp.dot(vt, p_buf[slot],
                                                                 preferred_element_type=F32)

        scores(0, 0)

        def pair(t, carry):
            i = 2 * t
            scores(i + 1, 1)
            softmax(0)
            accumulate(i - 1, 1)
            scores(i + 2, 0)
            softmax(1)
            accumulate(i, 0)
            return carry

        n_pairs = (n_steps - 1) // 2
        lax.fori_loop(0, n_pairs, pair, 0, unroll=ATT_PAIR_UNROLL)
        for i in range(2 * n_pairs, n_steps):
            if i + 1 < n_steps:
                scores(i + 1, (i + 1) % 2)
            softmax(i % 2)
            if i >= 1:
                accumulate(i - 1, (i - 1) % 2)
        accumulate(n_steps - 1, (n_steps - 1) % 2)

    def finish(acc_t):
        o_t = acc_t * (1.0 / acc_t[HEAD_DIM:HEAD_DIM + 1])
        pad = jnp.zeros((LANES - V_ROWS, o_t.shape[1]), F32)
        return jnp.concatenate([o_t, pad], axis=0).T

    if shared_kv:
        o_both = finish(acc_ref[0])
        o0, o1 = o_both[0:tq], o_both[tq:2 * tq]
    else:
        o0, o1 = finish(acc_ref[0]), finish(acc_ref[1])
    lane = lax.broadcasted_iota(I32, (tq, LANES), 1)
    o_ref[...] = jnp.where(lane < HEAD_DIM, o0, pltpu.roll(o1, HEAD_DIM, 1)).astype(BF16)


def _attention(q, k, vt, shared_kv, geo, ctx_queries):
    nb, nc, nl = geo["nbatch"], geo["nc"], geo["nl"]
    kw = LANES if shared_kv else 2 * LANES
    tk = TM
    ctx_blk0 = nb * nl // nc
    ctx_k = pl.BlockSpec((nc, kw), lambda b, g, t: (ctx_blk0 + b, g))
    ctx_vt = pl.BlockSpec((nc // tk, kw, tk), lambda b, g, t: (ctx_blk0 + b, g, 0))
    if ctx_queries:
        tq, q_per, q_blk0, n_lat = nc, 1, ctx_blk0, 0
        lat_k, lat_vt = ctx_k, ctx_vt
    else:
        tq = min(ATT_ROWS, nl) // (2 if shared_kv else 1)
        q_per, q_blk0, n_lat = nl // tq, 0, nl // tk
        lat_k = pl.BlockSpec((nl, kw), lambda b, g, t: (b, g))
        lat_vt = pl.BlockSpec((nl // tk, kw, tk), lambda b, g, t: (b, g, 0))
    n_streams, rows = (1, 2 * tq) if shared_kv else (2, tq)
    return pl.pallas_call(
        functools.partial(_attn_kernel, shared_kv=shared_kv, n_lat=n_lat, n_ctx=nc // tk, tk=tk),
        grid=(nb, 2, q_per),
        in_specs=[
            pl.BlockSpec((tq, 2 * LANES), lambda b, g, t: (q_blk0 + b * q_per + t, g)),
            lat_k, ctx_k, lat_vt, ctx_vt,
        ],
        out_specs=pl.BlockSpec((tq, LANES), lambda b, g, t: (b * q_per + t, g)),
        out_shape=jax.ShapeDtypeStruct((nb * q_per * tq, 2 * LANES), BF16),
        scratch_shapes=[pltpu.VMEM((n_streams, rows, LANES), BF16),
                        pltpu.VMEM((2, tk, rows), F32),
                        pltpu.VMEM((2, tk, rows), BF16),
                        pltpu.VMEM((2, SUBLANES, rows), F32),
                        pltpu.VMEM((2, SUBLANES, rows), F32),
                        pltpu.VMEM((n_streams, SUBLANES, rows), F32),
                        pltpu.VMEM((n_streams, V_ROWS, rows), F32)],
        compiler_params=_cparams(("arbitrary", "arbitrary", "arbitrary")),
        name=("attn_gqa" if shared_kv else "attn_mla") + ("_ctx" if ctx_queries else ""),
    )(q, k, k, vt, vt)


def _top2_sum(a, b, c, d):
    hi_ab, lo_ab = jnp.maximum(a, b), jnp.minimum(a, b)
    hi_cd, lo_cd = jnp.maximum(c, d), jnp.minimum(c, d)
    first = jnp.maximum(hi_ab, hi_cd)
    second = jnp.maximum(jnp.minimum(hi_ab, hi_cd), jnp.maximum(lo_ab, lo_cd))
    return first + second


def _outproj_kernel(xl_ref, xc_ref, cb_ref, cv_ref, cvp_ref, cvn_ref, zp_ref, zpp_ref, zpn_ref,
                    ygl_ref, ygc_ref, yml_ref, ymc_ref,
                    mods_ref, convw_ref, band_ref, wpool_ref, pscale_ref,
                    wout_ref, n2_ref, wr_ref, br_ref, tri_ref,
                    xo_ref, h2_ref, info_ref, cnt_ref,
                    *, d_model, geo):
    tile = pl.program_id(0)
    d = d_model
    tm = xl_ref.shape[0]
    tg = _tile_geometry(tile, geo)
    is_ctx = tg["is_ctx"]
    keep_prev = jnp.where(tg["first"], 0.0, 1.0)
    keep_next = jnp.where(tg["last"], 0.0, 1.0)
    row = tg["mod_row"]
    gate1 = mods_ref[0, pl.ds(row, 1), 2 * d:3 * d]
    shift2 = mods_ref[0, pl.ds(row, 1), 3 * d:4 * d]
    scale2 = mods_ref[0, pl.ds(row, 1), 4 * d:5 * d]

    v = cv_ref[...].astype(F32)
    prev_row = cvp_ref[...].astype(F32)[BF16_ROWS - 1:BF16_ROWS] * keep_prev
    next_row = cvn_ref[...].astype(F32)[0:1] * keep_next
    rid = lax.broadcasted_iota(I32, (tm, CONV_DIM), 0)
    vm1 = jnp.where(rid == 0, prev_row, pltpu.roll(v, 1, 0))
    vp1 = jnp.where(rid == tm - 1, next_row, pltpu.roll(v, tm - 1, 0))
    cw = convw_ref[0]
    y_conv = cb_ref[...].astype(F32) * (vm1 * cw[0:1] + v * cw[1:2] + vp1 * cw[2:3])

    zp = zp_ref[...]
    ext = jnp.concatenate(
        [zpp_ref[...] * keep_prev, zp, zpn_ref[...] * keep_next,
         jnp.zeros((POOL_EXT - tm - 2 * POOL_HALO, POOL_DIM), F32)], axis=0).astype(BF16)
    ext_a, ext_b = ext[:, 0:LANES], ext[:, LANES:2 * LANES]
    lane = lax.broadcasted_iota(I32, (tm, LANES), 1)
    low = lane < POOL_DIM // 4
    sum_a = jnp.where(low, jnp.dot(band_ref[0], ext_a, preferred_element_type=F32),
                      jnp.dot(band_ref[1], ext_a, preferred_element_type=F32))
    sum_b = jnp.where(low, jnp.dot(band_ref[2], ext_b, preferred_element_type=F32),
                      jnp.dot(band_ref[3], ext_b, preferred_element_type=F32))
    sums = jnp.concatenate([sum_a, sum_b], axis=1)
    lane_p = lax.broadcasted_iota(I32, (tm, POOL_DIM), 1)
    half_w = jnp.left_shift(1, jnp.right_shift(lane_p, int(math.log2(POOL_DIM // 4))))
    pos = tg["pos0"] + rid
    cnt = (jnp.minimum(pos + half_w, tg["seg_len"]) - jnp.maximum(pos - half_w, 0)).astype(F32)
    dlt = sums / cnt - zp
    y_pool = jnp.dot(dlt.astype(BF16), wpool_ref[0], preferred_element_type=F32) * pscale_ref[0]

    y_gqa = jnp.where(is_ctx, ygc_ref[...], ygl_ref[...])
    y_mla = jnp.where(is_ctx, ymc_ref[...], yml_ref[...])
    ycat = jnp.concatenate([y_conv.astype(BF16), y_pool.astype(BF16), y_gqa, y_mla], axis=1)
    y = jnp.dot(ycat, wout_ref[0], preferred_element_type=F32)
    xn = jnp.where(is_ctx, xc_ref[...], xl_ref[...]) + gate1 * y
    xo_ref[...] = xn
    ms = jnp.mean(xn * xn, axis=-1, keepdims=True)
    h2 = (xn * lax.rsqrt(ms + NORM_EPS)) * (n2_ref[0] * (1.0 + scale2)) + shift2
    h2_ref[...] = h2

    h_hi = h2.astype(BF16)
    h_lo = (h2 - h_hi.astype(F32)).astype(BF16)
    wr = wr_ref[...]
    part = jnp.dot(h_hi, wr, preferred_element_type=F32)
    small = part[:, LANES:] + jnp.dot(h_lo, wr[:, 0:LANES], preferred_element_type=F32)
    logits = (part[:, 0:LANES] + small).T[0:N_EXPERTS]
    scores = _sigmoid(logits)
    sel = scores + br_ref[...]
    epg = EXPERTS_PER_GROUP
    n_groups = N_EXPERTS // epg
    srow = [sel[e:e + 1] for e in range(N_EXPERTS)]
    crow = [scores[e:e + 1] for e in range(N_EXPERTS)]
    gscore = [_top2_sum(*srow[g * epg:(g + 1) * epg]) for g in range(n_groups)]
    gbest = jnp.zeros_like(gscore[0]).astype(I32)
    best = gscore[0]
    for g in range(1, n_groups):
        upd = gscore[g] > best
        gbest = jnp.where(upd, g, gbest)
        best = jnp.where(upd, gscore[g], best)

    def pick(rows_, j):
        out = rows_[(n_groups - 1) * epg + j]
        for g in range(n_groups - 2, -1, -1):
            out = jnp.where(gbest == g, rows_[g * epg + j], out)
        return out

    sv = [pick(srow, j) for j in range(epg)]
    cv_ = [pick(crow, j) for j in range(epg)]
    i1 = jnp.zeros_like(gbest)
    b1 = sv[0]
    for j in range(1, epg):
        upd = sv[j] > b1
        i1 = jnp.where(upd, j, i1)
        b1 = jnp.where(upd, sv[j], b1)
    i2 = jnp.zeros_like(gbest)
    b2 = jnp.full_like(b1, -jnp.inf)
    for j in range(epg):
        upd = jnp.logical_and(i1 != j, sv[j] > b2)
        i2 = jnp.where(upd, j, i2)
        b2 = jnp.where(upd, sv[j], b2)
    s1 = cv_[epg - 1]
    s2 = cv_[epg - 1]
    for j in range(epg - 2, -1, -1):
        s1 = jnp.where(i1 == j, cv_[j], s1)
        s2 = jnp.where(i2 == j, cv_[j], s2)
    inv = 1.0 / (s1 + s2)
    e1 = gbest * epg + i1
    e2 = gbest * epg + i2

    @pl.when(tile == 0)
    def _():
        cnt_ref[...] = jnp.zeros_like(cnt_ref)

    erow = lax.broadcasted_iota(I32, (N_EXPERTS, tm), 0)
    hit1 = erow == e1
    hit2 = erow == e2
    onehot = jnp.where(hit1, 1.0, 0.0) + jnp.where(hit2, 1.0, 0.0)
    before = jnp.dot(onehot.astype(BF16), tri_ref[...], preferred_element_type=F32)
    tot = cnt_ref[:, 0:1] + before
    rank1 = jnp.sum(jnp.where(hit1, tot, 0.0), axis=0, keepdims=True).astype(I32)
    rank2 = jnp.sum(jnp.where(hit2, tot, 0.0), axis=0, keepdims=True).astype(I32)
    cnt_ref[...] = cnt_ref[...] + jnp.sum(onehot, axis=1, keepdims=True)

    w1 = lax.bitcast_convert_type(s1 * inv, I32)
    w2 = lax.bitcast_convert_type(s2 * inv, I32)
    irow = lax.broadcasted_iota(I32, (SUBLANES, tm), 0)
    info = jnp.where(irow == 0, e1, jnp.where(irow == 1, e2, jnp.where(
        irow == 2, rank1, jnp.where(irow == 3, rank2, jnp.where(
            irow == 4, w1, jnp.where(irow == 5, w2, 0))))))
    info_ref[0] = info


def _outproj(src, mix, yg, ygc, ym, ymc, mods, layer, p, geo, n_tiles):
    d = src[0].shape[1]
    nt_all, n_lat = geo["n_tiles"], geo["n_lat_tiles"]
    r = n_tiles * TM
    nt = n_tiles
    rowmap = lambda j: (j, 0)
    latmap = lambda j: (jnp.minimum(j, n_lat - 1), 0)
    ctxmap = lambda j: (jnp.clip(j - n_lat, 0, ygc.shape[0] // TM - 1), 0)
    lay3 = lambda j: (layer, 0, 0)
    c0 = lambda j: (0, 0)
    c3 = lambda j: (0, 0, 0)
    bf_blocks = TM // BF16_ROWS
    f_blocks = TM // SUBLANES

    def full3(a):
        return pl.BlockSpec((1,) + a.shape[1:], lay3)

    return pl.pallas_call(
        functools.partial(_outproj_kernel, d_model=d, geo=geo),
        grid=(n_tiles,),
        in_specs=_row_sources(src, TM, geo) + [
            pl.BlockSpec((TM, CONV_DIM), rowmap),
            pl.BlockSpec((TM, CONV_DIM), rowmap),
            pl.BlockSpec((BF16_ROWS, CONV_DIM), lambda j: (jnp.maximum(j * bf_blocks - 1, 0), 0)),
            pl.BlockSpec((BF16_ROWS, CONV_DIM),
                         lambda j: (jnp.minimum((j + 1) * bf_blocks, nt_all * bf_blocks - 1), 0)),
            pl.BlockSpec((TM, POOL_DIM), rowmap),
            pl.BlockSpec((SUBLANES, POOL_DIM), lambda j: (jnp.maximum(j * f_blocks - 1, 0), 0)),
            pl.BlockSpec((SUBLANES, POOL_DIM),
                         lambda j: (jnp.minimum((j + 1) * f_blocks, nt_all * f_blocks - 1), 0)),
            pl.BlockSpec((TM, 2 * LANES), latmap),
            pl.BlockSpec((TM, 2 * LANES), ctxmap),
            pl.BlockSpec((TM, 2 * LANES), latmap),
            pl.BlockSpec((TM, 2 * LANES), ctxmap),
            full3(mods), full3(p["conv_w"]),
            pl.BlockSpec(p["band"].shape, c3),
            full3(p["wpool"]), full3(p["pscale"]), full3(p["w_out"]), full3(p["norm2"]),
            pl.BlockSpec(p["wr"].shape, c0), pl.BlockSpec(p["br"].shape, c0),
            pl.BlockSpec(p["tri"].shape, c0),
        ],
        out_specs=[
            pl.BlockSpec((TM, d), rowmap),
            pl.BlockSpec((TM, d), rowmap),
            pl.BlockSpec((1, SUBLANES, TM), lambda j: (j, 0, 0)),
            pl.BlockSpec((N_EXPERTS, LANES), c0),
        ],
        out_shape=[
            jax.ShapeDtypeStruct((r, d), F32),
            jax.ShapeDtypeStruct((r, d), F32),
            jax.ShapeDtypeStruct((nt, SUBLANES, TM), I32),
            jax.ShapeDtypeStruct((N_EXPERTS, LANES), F32),
        ],
        compiler_params=_cparams(("arbitrary",)),
        name="outproj",
    )(src[0], src[2], mix["cb"], mix["cv"], mix["cv"], mix["cv"], mix["zp"], mix["zp"], mix["zp"],
      yg, ygc, ym, ymc, mods, p["conv_w"], p["band"], p["wpool"], p["pscale"], p["w_out"],
      p["norm2"], p["wr"], p["br"], p["tri"])


def _slot_copies(block_ref, idx_ref, sem):
    n = block_ref.shape[2]
    return [pltpu.make_async_copy(block_ref.at[0, k], idx_ref.at[pl.ds(k * n, n)], sem) for k in range(2)]


def _start_all(copies):
    for cp in copies:
        cp.start()


def _wait_all(copies):
    for cp in copies:
        cp.wait()


def _row_copies(idx_ref, n_rows, make):
    def body(t, carry):
        for u in range(SUBLANES):
            r = t * SUBLANES + u
            make(0, t, u, idx_ref[r]).start()
            make(1, t, u, idx_ref[n_rows + r]).start()
        return carry
    lax.fori_loop(0, n_rows // SUBLANES, body, 0)


def _scatter_kernel(ps_ref, zs_ref, has_ref, nu_ref, info_ref, info_next_ref, h2_ref, dest_ref, xs_ref,
                    dsm_ref, zbuf_ref, stage_ref, nxt_ref, sem_ref, *, first_spare, n_blocks, n_tiles):
    j = pl.program_id(0)
    tm = h2_ref.shape[0]

    @pl.when(j == 0)
    def _():
        zbuf_ref[...] = jnp.zeros_like(zbuf_ref)

        def zero_copy(start):
            return pltpu.make_async_copy(
                zbuf_ref, xs_ref.at[pl.ds(pl.multiple_of(start, BM), BM)], sem_ref.at[2])

        for act in ("start", "wait"):
            for e in range(N_EXPERTS):
                @pl.when(has_ref[e] > 0)
                def _():
                    getattr(zero_copy(zs_ref[e]), act)()
            for jb in range(first_spare, n_blocks):
                @pl.when(jb >= nu_ref[0])
                def _():
                    getattr(zero_copy(jb * BM), act)()

    def slots_of(info):
        e1, e2 = info[0:1], info[1:2]
        d1, d2 = info[2:3], info[3:4]
        for e in range(N_EXPERTS):
            d1 = d1 + jnp.where(e1 == e, ps_ref[e], 0)
            d2 = d2 + jnp.where(e2 == e, ps_ref[e], 0)
        irow = lax.broadcasted_iota(I32, (SUBLANES, tm), 0)
        return jnp.where(irow == 0, d1, jnp.where(irow == 1, d2, 0))

    dest_ref[0] = slots_of(info_ref[0])

    @pl.when(j == 0)
    def _():
        first = _slot_copies(dest_ref, dsm_ref, sem_ref.at[3])
        _start_all(first)
        _wait_all(first)

    par = j % 2
    stage_ref[par] = h2_ref[...].reshape(stage_ref.shape[1:])

    def row_copy(_, t, u, slot):
        return pltpu.make_async_copy(stage_ref.at[par, t, pl.ds(u, 1)], xs_ref.at[pl.ds(slot, 1)],
                                     sem_ref.at[par])

    _row_copies(dsm_ref, tm, row_copy)

    nxt_ref[0] = slots_of(info_next_ref[0])
    nxt = _slot_copies(nxt_ref, dsm_ref, sem_ref.at[3])

    @pl.when(j + 1 < n_tiles)
    def _():
        _start_all(nxt)

    def retire(which):
        for _ in range(2):
            pltpu.make_async_copy(h2_ref, xs_ref.at[pl.ds(0, tm)], sem_ref.at[which]).wait()

    @pl.when(j > 0)
    def _():
        retire(1 - par)

    @pl.when(j == n_tiles - 1)
    def _():
        retire(par)

    @pl.when(j + 1 < n_tiles)
    def _():
        _wait_all(nxt)


def _scatter(info, h2, pad_start, zero_start, has_rows, n_used, n_slots):
    r, d = h2.shape
    nt = r // TM
    return pl.pallas_call(
        functools.partial(_scatter_kernel, first_spare=-(-2 * r // BM), n_blocks=n_slots // BM,
                          n_tiles=nt),
        grid_spec=pltpu.PrefetchScalarGridSpec(
            num_scalar_prefetch=4,
            grid=(nt,),
            in_specs=[
                pl.BlockSpec((1, SUBLANES, TM), lambda j, *_: (j, 0, 0)),
                pl.BlockSpec((1, SUBLANES, TM), lambda j, *_: (jnp.minimum(j + 1, nt - 1), 0, 0)),
                pl.BlockSpec((TM, d), lambda j, *_: (j, 0)),
            ],
            out_specs=[
                pl.BlockSpec((1, SUBLANES, TM), lambda j, *_: (j, 0, 0)),
                pl.BlockSpec(memory_space=pl.ANY),
            ],
            scratch_shapes=[
                pltpu.SMEM((2 * TM,), I32),
                pltpu.VMEM((BM, d), F32),
                pltpu.VMEM((2, TM // SUBLANES, SUBLANES, d), F32),
                pltpu.VMEM((1, SUBLANES, TM), I32),
                pltpu.SemaphoreType.DMA((4,)),
            ],
        ),
        out_shape=[
            jax.ShapeDtypeStruct((nt, SUBLANES, TM), I32),
            jax.ShapeDtypeStruct((n_slots, d), F32),
        ],
        compiler_params=_cparams(("arbitrary",)),
        name="moe_scatter",
    )(pad_start, zero_start, has_rows, n_used, info, info, h2)


def _expert_kernel(be_ref, nu_ref, xs_ref, wg_ref, wu_ref, wd_ref, y_ref, wgb, wub, wdb):
    j = pl.program_id(0)

    @pl.when(j < nu_ref[0])
    def _():
        e = be_ref[j]
        prev = be_ref[jnp.maximum(j - 1, 0)]

        @pl.when(jnp.logical_or(j == 0, e != prev))
        def _():
            wgb[...] = wg_ref[0].astype(BF16)
            wub[...] = wu_ref[0].astype(BF16)
            wdb[...] = wd_ref[0].astype(BF16)

        x = xs_ref[...].astype(BF16)
        a = jnp.dot(x, wgb[...], preferred_element_type=F32)
        u = jnp.dot(x, wub[...], preferred_element_type=F32)
        hmid = (a * _sigmoid(a) * u).astype(BF16)
        y_ref[...] = jnp.dot(hmid, wdb[...], preferred_element_type=F32)

    @pl.when(j >= nu_ref[0])
    def _():
        y_ref[...] = jnp.zeros_like(y_ref)


def _experts(xs, block_exp, n_used, w_gate, w_up, w_down, layer):
    n_slots, d = xs.shape
    de = w_gate.shape[-1]
    nbm = n_slots // BM
    n_exp = w_gate.shape[1]

    def blk(j, be, nu):
        return (jnp.minimum(j, nu[0] - 1), 0)

    def wmap(j, be, nu):
        return (layer * n_exp + be[jnp.minimum(j, nu[0] - 1)], 0, 0)

    wg = w_gate.reshape((-1,) + w_gate.shape[2:])
    wu = w_up.reshape((-1,) + w_up.shape[2:])
    wd = w_down.reshape((-1,) + w_down.shape[2:])
    return pl.pallas_call(
        _expert_kernel,
        grid_spec=pltpu.PrefetchScalarGridSpec(
            num_scalar_prefetch=2,
            grid=(nbm,),
            in_specs=[
                pl.BlockSpec((BM, d), blk),
                pl.BlockSpec((1, d, de), wmap),
                pl.BlockSpec((1, d, de), wmap),
                pl.BlockSpec((1, de, d), wmap),
            ],
            out_specs=pl.BlockSpec((BM, d), lambda j, be, nu: (j, 0)),
            scratch_shapes=[
                pltpu.VMEM((d, de), BF16),
                pltpu.VMEM((d, de), BF16),
                pltpu.VMEM((de, d), BF16),
            ],
        ),
        out_shape=jax.ShapeDtypeStruct((n_slots, d), F32),
        compiler_params=_cparams(("arbitrary",)),
        name="moe_experts",
    )(block_exp, n_used, xs, wg, wu, wd)


def _combine_kernel(dest_ref, dest_next_ref, dest_next2_ref, info_ref, x_ref, mods_ref, y_ref, o_ref,
                    dsm_ref, ybuf_ref, sem_ref, *, d_model, geo, n_steps):
    j = pl.program_id(0)
    d = d_model
    tm = x_ref.shape[0]
    row = _tile_geometry(j, geo)["mod_row"]
    gate2 = mods_ref[0, pl.ds(row, 1), 5 * d:6 * d]
    cur = j % 2

    def load_slots(idx_block_ref):
        copies = _slot_copies(idx_block_ref, dsm_ref, sem_ref.at[2])
        _start_all(copies)
        _wait_all(copies)

    def gather(buf):
        def row_copy(k, t, u, slot):
            return pltpu.make_async_copy(y_ref.at[pl.ds(slot, 1)],
                                         ybuf_ref.at[buf, k, t, pl.ds(u, 1)], sem_ref.at[buf])

        _row_copies(dsm_ref, tm, row_copy)

    @pl.when(j == 0)
    def _():
        load_slots(dest_ref)
        gather(0)
        load_slots(dest_next_ref)

    @pl.when(j + 1 < n_steps)
    def _():
        gather(1 - cur)

    nxt2 = _slot_copies(dest_next2_ref, dsm_ref, sem_ref.at[2])

    @pl.when(j + 2 < n_steps)
    def _():
        _start_all(nxt2)

    for k in range(2):
        pltpu.make_async_copy(y_ref.at[pl.ds(0, tm)], o_ref, sem_ref.at[cur]).wait()

    info = info_ref[0]
    wrow = lax.broadcasted_iota(I32, (LANES, tm), 0)
    w_lanes = jnp.where(wrow == 0, lax.bitcast_convert_type(info[4:5], F32),
                        jnp.where(wrow == 1, lax.bitcast_convert_type(info[5:6], F32), 0.0))
    w_rows = w_lanes.T
    y1 = ybuf_ref[cur, 0].reshape(tm, d)
    y2 = ybuf_ref[cur, 1].reshape(tm, d)
    o_ref[...] = x_ref[...] + gate2 * (w_rows[:, 0:1] * y1 + w_rows[:, 1:2] * y2)

    @pl.when(j + 2 < n_steps)
    def _():
        _wait_all(nxt2)


def _combine(dest, info, xn, mods, y, layer, geo):
    r, d = xn.shape
    n_steps = r // TM
    return pl.pallas_call(
        functools.partial(_combine_kernel, d_model=d, geo=geo, n_steps=n_steps),
        grid=(n_steps,),
        in_specs=[
            pl.BlockSpec((1, SUBLANES, TM), lambda j: (j, 0, 0)),
            pl.BlockSpec((1, SUBLANES, TM), lambda j: (jnp.minimum(j + 1, n_steps - 1), 0, 0)),
            pl.BlockSpec((1, SUBLANES, TM), lambda j: (jnp.minimum(j + 2, n_steps - 1), 0, 0)),
            pl.BlockSpec((1, SUBLANES, TM), lambda j: (j, 0, 0)),
            pl.BlockSpec((TM, d), lambda j: (j, 0)),
            pl.BlockSpec((1,) + mods.shape[1:], lambda j: (layer, 0, 0)),
            pl.BlockSpec(memory_space=pl.ANY),
        ],
        out_specs=pl.BlockSpec((TM, d), lambda j: (j, 0)),
        out_shape=jax.ShapeDtypeStruct((n_steps * TM, d), F32),
        scratch_shapes=[
            pltpu.SMEM((2 * TM,), I32),
            pltpu.VMEM((2, 2, TM // SUBLANES, SUBLANES, d), F32),
            pltpu.SemaphoreType.DMA((3,)),
        ],
        compiler_params=_cparams(("arbitrary",)),
        name="moe_combine",
    )(dest, dest, dest, info, xn, mods, y)


def _pad_heads(w, n_heads, width):
    lead = w.shape[:-1]
    w = w.reshape(lead + (n_heads, width))
    w = jnp.pad(w, [(0, 0)] * len(lead) + [(0, 0), (0, LANES - width)])
    return w.reshape(lead + (n_heads * LANES,))


def _rope_tables(id_rows, nl):
    t = jnp.arange(nl)
    row_id = (t // GRID_W).astype(F32)
    col_id = (t % GRID_W).astype(F32)

    def angles(rot_dim):
        n_freq = rot_dim // 4
        inv_freq = jnp.power(ROPE_THETA, -jnp.arange(n_freq, dtype=F32) / n_freq)
        return jnp.concatenate([row_id[:, None] * inv_freq, col_id[:, None] * inv_freq], axis=-1)

    def with_ctx(tab, fill):
        return jnp.concatenate([tab, jnp.full((id_rows, LANES), fill, F32)], axis=0)

    ag = angles(HEAD_DIM)
    one_g = jnp.ones((nl, LANES - HEAD_DIM), F32)
    cos_g = jnp.concatenate([jnp.cos(ag), jnp.cos(ag), one_g], axis=-1)
    sin_g = jnp.concatenate([-jnp.sin(ag), jnp.sin(ag), 0.0 * one_g], axis=-1)
    am = angles(MLA_ROPE_DIM)
    one_n = jnp.ones((nl, MLA_NOPE_DIM), F32)
    one_t = jnp.ones((nl, LANES - MLA_QK_DIM), F32)
    cos_m = jnp.concatenate([one_n, jnp.cos(am), jnp.cos(am), one_t], axis=-1)
    sin_m = jnp.concatenate([0.0 * one_n, -jnp.sin(am), jnp.sin(am), 0.0 * one_t], axis=-1)
    return with_ctx(cos_g, 1.0), with_ctx(sin_g, 0.0), with_ctx(cos_m, 1.0), with_ctx(sin_m, 0.0)


def _pool_band():
    t = np.arange(TM)[:, None]
    src = np.arange(POOL_EXT)[None, :] - POOL_HALO
    live = np.arange(POOL_EXT)[None, :] < TM + 2 * POOL_HALO
    mats = [((src >= t - w // 2) & (src < t + w // 2) & live) for w in POOL_WINDOWS]
    return jnp.asarray(np.stack(mats).astype(np.float32), dtype=BF16)


def _prep_params(w_in, norm1, norm2, conv_w, w_pool, pool_scale, gqa_q_norm, gqa_k_norm,
                 mla_q_norm, mla_kv_norm, mla_w_uq, mla_w_uk, mla_w_uv, mla_qk_q_norm,
                 mla_qk_k_norm, w_out, w_router, b_router, id_rows, nl):
    dep = w_in.shape[0]
    o = 0
    pieces = {}
    for name, n in (("conv", 3 * CONV_DIM), ("pool", POOL_DIM), ("gq", GQA_HEADS * HEAD_DIM),
                    ("gk", GQA_KV_HEADS * HEAD_DIM), ("gv", GQA_KV_HEADS * HEAD_DIM),
                    ("mq", MLA_Q_RANK), ("mkv", MLA_KV_RANK), ("mkr", MLA_ROPE_DIM)):
        pieces[name] = w_in[..., o:o + n]
        o += n
    mkr = jnp.pad(pieces["mkr"], ((0, 0), (0, 0), (MLA_NOPE_DIM, LANES - MLA_QK_DIM)))
    w_in_p = jnp.concatenate([
        pieces["conv"], pieces["pool"], _pad_heads(pieces["gq"], GQA_HEADS, HEAD_DIM),
        _pad_heads(pieces["gk"], GQA_KV_HEADS, HEAD_DIM),
        pieces["mq"], pieces["mkv"], mkr], axis=-1).astype(BF16)
    wgvt = jnp.swapaxes(_pad_heads(pieces["gv"], GQA_KV_HEADS, HEAD_DIM), 1, 2).astype(BF16)
    wuvt = jnp.swapaxes(_pad_heads(mla_w_uv, MLA_HEADS, MLA_V_DIM), 1, 2).astype(BF16)
    eye = jnp.eye(len(POOL_WINDOWS), dtype=F32)
    wpool = jnp.einsum("gh,dgij->dgihj", eye, w_pool).reshape(dep, POOL_DIM, POOL_DIM).astype(BF16)
    cos_g, sin_g, cos_m, sin_m = _rope_tables(id_rows, nl)
    tri = np.triu(np.ones((TM, TM), np.float32), 1)
    wr_pad = jnp.pad(w_router, ((0, 0), (0, LANES - w_router.shape[1])))
    wr_hi = wr_pad.astype(BF16)
    wr_split = jnp.concatenate([wr_hi, (wr_pad - wr_hi.astype(F32)).astype(BF16)], axis=1)

    def row3(a):
        return a.reshape(dep, 1, a.shape[-1])

    return {
        "w_in": w_in_p,
        "norm1": row3(norm1), "norm2": row3(norm2),
        "gq": row3(_pad_heads(gqa_q_norm * (GQA_SCALE * LOG2E), 1, HEAD_DIM)),
        "gk": row3(_pad_heads(gqa_k_norm, 1, HEAD_DIM)),
        "mqn": row3(mla_q_norm), "mkvn": row3(mla_kv_norm),
        "qkq": row3(_pad_heads(mla_qk_q_norm * (MLA_SCALE * LOG2E), 1, MLA_QK_DIM)),
        "qkk": row3(_pad_heads(mla_qk_k_norm, 1, MLA_QK_DIM)),
        "wuq": _pad_heads(mla_w_uq, MLA_HEADS, MLA_QK_DIM).astype(BF16),
        "wuk": _pad_heads(mla_w_uk, MLA_HEADS, MLA_NOPE_DIM).astype(BF16),
        "wgvt": wgvt, "wuvt": wuvt,
        "cos_g": cos_g, "sin_g": sin_g, "cos_m": cos_m, "sin_m": sin_m,
        "conv_w": jnp.pad(conv_w, ((0, 0), (0, SUBLANES - conv_w.shape[1]), (0, 0))),
        "band": _pool_band(),
        "wpool": wpool, "pscale": row3(pool_scale),
        "w_out": w_out.astype(BF16),
        "wr": wr_split, "br": b_router.reshape(-1, 1),
        "tri": jnp.asarray(tri, dtype=BF16),
    }


def _moe_plan(counts):
    counts = counts.astype(I32)
    padded = ((counts + BM - 1) // BM) * BM
    pad_end = jnp.cumsum(padded)
    pad_start = pad_end - padded
    return pad_start, pad_end, padded


def kernel(x, c, ctx, c_ctx, w_mod, b_mod, norm1, norm2, w_in, conv_w, w_pool, pool_scale,
           gqa_q_norm, gqa_k_norm, mla_q_norm, mla_kv_norm, mla_w_uq, mla_w_uk, mla_w_uv,
           mla_qk_q_norm, mla_qk_k_norm, w_out, w_router, b_router, w_gate, w_up, w_down):
    nb, nl, d = x.shape
    nc = ctx.shape[1]
    depth = w_mod.shape[0]
    assert nc % TM == 0 and nl % TM == 0 and nl % GRID_W == 0 and nb < SUBLANES
    lt, nct = nl // TM, nc // TM
    wide = nl % INPROJ_ROWS == 0 and (nb * nc) % INPROJ_ROWS == 0
    geo = {"nbatch": nb, "nc": nc, "nl": nl, "lt": lt, "nct": nct,
           "n_lat_tiles": nb * lt, "n_tiles": nb * (lt + nct),
           "inproj_rows": INPROJ_ROWS if wide else TM}

    p = _prep_params(w_in, norm1, norm2, conv_w, w_pool, pool_scale, gqa_q_norm, gqa_k_norm,
                     mla_q_norm, mla_kv_norm, mla_w_uq, mla_w_uk, mla_w_uv, mla_qk_q_norm,
                     mla_qk_k_norm, w_out, w_router, b_router, geo["inproj_rows"], nl)
    cvec = jnp.concatenate([c, c_ctx[None, :], jnp.zeros((SUBLANES - nb - 1, d), F32)], axis=0)
    mods = _adaln(cvec, w_mod, b_mod)

    src = (x.reshape(nb * nl, d), 0, ctx.reshape(nb * nc, d), 0)
    for i in range(depth):
        last = i == depth - 1
        mix = _inproj(src, mods, i, p, geo)
        yg = _attention(mix["qg"], mix["kg"], mix["vgt"], True, geo, False)
        ym = _attention(mix["qm"], mix["km"], mix["vmt"], False, geo, False)
        if last:
            n_tiles, ygc, ymc = geo["n_lat_tiles"], yg, ym
        else:
            n_tiles = geo["n_tiles"]
            ygc = _attention(mix["qg"], mix["kg"], mix["vgt"], True, geo, True)
            ymc = _attention(mix["qm"], mix["km"], mix["vmt"], False, geo, True)
        xn, h2, info, cnt = _outproj(src, mix, yg, ygc, ym, ymc, mods, i, p, geo, n_tiles)
        n_blocks = -(-2 * n_tiles * TM // BM) + N_EXPERTS
        n_slots = n_blocks * BM
        pad_start, pad_end, padded = _moe_plan(cnt[:, 0])
        n_used = (pad_end[-1:] // BM).astype(I32)
        block_row0 = jnp.arange(n_blocks, dtype=I32) * BM
        block_exp = jnp.minimum(jnp.sum((pad_end[None, :] <= block_row0[:, None]).astype(I32), axis=1),
                                N_EXPERTS - 1)
        dest, xs = _scatter(info, h2, pad_start, jnp.maximum(pad_end - BM, 0),
                            (padded > 0).astype(I32), n_used, n_slots)
        y = _experts(xs, block_exp, n_used, w_gate, w_up, w_down, i)
        xa = _combine(dest, info, xn, mods, y, i, geo)
        src = (xa, 0, xa, nb * nl)
    return xa.reshape(nb, nl, d)
```

```python
import functools
import math

import numpy as np
import jax
import jax.numpy as jnp
from jax import lax
from jax.experimental import pallas as pl
from jax.experimental.pallas import tpu as pltpu

F32 = jnp.float32
BF16 = jnp.bfloat16
I32 = jnp.int32

GRID_W = 64
CONV_DIM = 256
POOL_DIM = 256
POOL_WINDOWS = (2, 4, 8, 16)
HEAD_DIM = 64
GQA_HEADS = 4
GQA_KV_HEADS = 2
MLA_HEADS = 4
MLA_NOPE_DIM = 64
MLA_ROPE_DIM = 32
MLA_QK_DIM = MLA_NOPE_DIM + MLA_ROPE_DIM
MLA_V_DIM = 64
MLA_Q_RANK = 256
MLA_KV_RANK = 128
N_EXPERTS = 16
EXPERTS_PER_GROUP = 4
ROPE_THETA = 10000.0
NORM_EPS = 1e-6
LOG2E = 1.4426950408889634
GQA_SCALE = HEAD_DIM ** -0.5
MLA_SCALE = MLA_QK_DIM ** -0.5

LANES = 128
SUBLANES = 8
BF16_ROWS = 16
VMEM_LIMIT = 56 * 1024 * 1024

TM = 256
ATT_ROWS = 2048
INPROJ_ROWS = 512
BM = 512
V_ROWS = HEAD_DIM + BF16_ROWS
ATT_PAIR_UNROLL = 5
POOL_EXT = 512
POOL_HALO = 8

ZC_B, ZC_C, ZC_U, ZC_P = 0, 256, 512, 768
ZC_GQ = 1024
ZC_GK = ZC_GQ + GQA_HEADS * LANES
ZC_MQ = ZC_GK + GQA_KV_HEADS * LANES
ZC_MKV = ZC_MQ + MLA_Q_RANK
ZC_MKR = ZC_MKV + MLA_KV_RANK

ADALN_COLS = 1536
SOFTMAX_FLOOR = -1e30
HIGHEST = lax.Precision.HIGHEST


def _cparams(sem, vmem=VMEM_LIMIT):
    return pltpu.CompilerParams(dimension_semantics=sem, vmem_limit_bytes=vmem)


def _sigmoid(v):
    return 1.0 / (1.0 + jnp.exp(-v))


def _adaln_kernel(c_ref, w_ref, b_ref, o_ref):
    c = c_ref[...]
    s = c * _sigmoid(c)
    o_ref[0] = jnp.dot(s, w_ref[0], preferred_element_type=F32, precision=HIGHEST) + b_ref[0]


def _adaln(cvec, w_mod, b_mod):
    depth, d, n6 = w_mod.shape
    tn = ADALN_COLS if n6 % ADALN_COLS == 0 else n6
    rows = cvec.shape[0]
    return pl.pallas_call(
        _adaln_kernel,
        grid=(depth, n6 // tn),
        in_specs=[
            pl.BlockSpec((rows, d), lambda i, n: (0, 0)),
            pl.BlockSpec((1, d, tn), lambda i, n: (i, 0, n)),
            pl.BlockSpec((1, 1, tn), lambda i, n: (i, 0, n)),
        ],
        out_specs=pl.BlockSpec((1, rows, tn), lambda i, n: (i, 0, n)),
        out_shape=jax.ShapeDtypeStruct((depth, rows, n6), F32),
        compiler_params=_cparams(("arbitrary", "arbitrary")),
        name="adaln",
    )(cvec, w_mod, b_mod.reshape(depth, 1, n6))


def _tile_geometry(j, geo):
    lt, nct, n_lat = geo["lt"], geo["nct"], geo["n_lat_tiles"]
    is_ctx = j >= n_lat
    jc = j - n_lat
    sample = jnp.where(is_ctx, jc // nct, j // lt)
    jt = jnp.where(is_ctx, jc % nct, j % lt)
    return {
        "is_ctx": is_ctx,
        "mod_row": jnp.where(is_ctx, geo["nbatch"], sample),
        "first": jt == 0,
        "last": jt == jnp.where(is_ctx, nct, lt) - 1,
        "pos0": jt * TM,
        "seg_len": jnp.where(is_ctx, geo["nc"], geo["nl"]),
    }


def _norm_rope(slab, gain, cos, sin, n_valid, first_half, half):
    ssq = jnp.dot((slab * slab).astype(BF16), jnp.ones((LANES, LANES), BF16), preferred_element_type=F32)
    y = slab * lax.rsqrt(ssq * (1.0 / n_valid) + NORM_EPS) * gain
    partner = jnp.where(first_half, pltpu.roll(y, LANES - half, 1), pltpu.roll(y, half, 1))
    return y * cos + partner * sin


def _inproj_kernel(xl_ref, xc_ref, mods_ref, n1_ref, cg_ref, sg_ref, cm_ref, sm_ref, win_ref,
                   gq_ref, gk_ref, mqn_ref, mkvn_ref, qkq_ref, qkk_ref, wuq_ref, wuk_ref,
                   wgvt_ref, wuvt_ref,
                   cb_ref, cv_ref, zp_ref, qg_ref, kg_ref, vgt_ref, qm_ref, km_ref, vmt_ref,
                   *, d_model, nbatch, n_lat_steps, lat_steps_per_sample):
    d = d_model
    step = pl.program_id(0)
    row = jnp.where(step >= n_lat_steps, nbatch, step // lat_steps_per_sample)
    shift = mods_ref[0, pl.ds(row, 1), 0:d]
    scale = mods_ref[0, pl.ds(row, 1), d:2 * d]
    x = jnp.where(step >= n_lat_steps, xc_ref[...], xl_ref[...])
    ms = jnp.mean(x * x, axis=-1, keepdims=True)
    h = ((x * lax.rsqrt(ms + NORM_EPS)) * (n1_ref[0] * (1.0 + scale)) + shift).astype(BF16)
    nt_dims = (((1,), (1,)), ((), ()))

    def proj(c0, width):
        return jnp.dot(h, win_ref[0, :, c0:c0 + width], preferred_element_type=F32)

    def store_transposed(vt_ref, vt):
        srow = lax.broadcasted_iota(I32, vt.shape, 0)
        vt = jnp.where(jnp.bitwise_and(srow, LANES - 1) == HEAD_DIM, 1.0, vt).astype(BF16)
        for c in range(vt_ref.shape[0]):
            vt_ref[c] = vt[:, c * TM:(c + 1) * TM]

    tm = x.shape[0]
    lane = lax.broadcasted_iota(I32, (tm, LANES), 1)

    cm, sm = cm_ref[...], sm_ref[...]
    m_first = lane < MLA_NOPE_DIM + MLA_ROPE_DIM // 2
    zq = proj(ZC_MQ, MLA_Q_RANK)
    cq = zq * lax.rsqrt(jnp.mean(zq * zq, axis=-1, keepdims=True) + NORM_EPS) * mqn_ref[0]
    qpre = jnp.dot(cq.astype(BF16), wuq_ref[0], preferred_element_type=F32)
    zk = proj(ZC_MKV, MLA_KV_RANK + LANES)
    zkv, zkr = zk[:, 0:MLA_KV_RANK], zk[:, MLA_KV_RANK:]
    ckv = zkv * lax.rsqrt(jnp.mean(zkv * zkv, axis=-1, keepdims=True) + NORM_EPS) * mkvn_ref[0]
    ckv = ckv.astype(BF16)
    kvp = jnp.dot(ckv, wuk_ref[0], preferred_element_type=F32)
    store_transposed(vmt_ref, lax.dot_general(wuvt_ref[0], ckv, nt_dims, preferred_element_type=F32))
    for hd in range(MLA_HEADS):
        sl = slice(hd * LANES, (hd + 1) * LANES)
        qm_ref[:, sl] = _norm_rope(qpre[:, sl], qkq_ref[0], cm, sm, MLA_QK_DIM,
                                   m_first, MLA_ROPE_DIM // 2).astype(BF16)
        km_ref[:, sl] = _norm_rope(kvp[:, sl] + zkr, qkk_ref[0], cm, sm, MLA_QK_DIM,
                                   m_first, MLA_ROPE_DIM // 2).astype(BF16)

    cg, sg = cg_ref[...], sg_ref[...]
    g_first = lane < HEAD_DIM // 2
    zg = proj(ZC_GQ, (GQA_HEADS + GQA_KV_HEADS) * LANES)
    for hd in range(GQA_HEADS):
        qg_ref[:, hd * LANES:(hd + 1) * LANES] = _norm_rope(
            zg[:, hd * LANES:(hd + 1) * LANES], gq_ref[0], cg, sg, HEAD_DIM, g_first,
            HEAD_DIM // 2).astype(BF16)
    for hd in range(GQA_KV_HEADS):
        slab = zg[:, (GQA_HEADS + hd) * LANES:(GQA_HEADS + hd + 1) * LANES]
        kg_ref[:, hd * LANES:(hd + 1) * LANES] = _norm_rope(
            slab, gk_ref[0], cg, sg, HEAD_DIM, g_first, HEAD_DIM // 2).astype(BF16)
    store_transposed(vgt_ref, lax.dot_general(wgvt_ref[0], h, nt_dims, preferred_element_type=F32))

    zc = proj(ZC_B, 3 * CONV_DIM + POOL_DIM)
    cb_ref[...] = zc[:, ZC_B:ZC_B + CONV_DIM].astype(BF16)
    cv_ref[...] = (zc[:, ZC_C:ZC_C + CONV_DIM] * zc[:, ZC_U:ZC_U + CONV_DIM]).astype(BF16)
    zp_ref[...] = zc[:, ZC_P:ZC_P + POOL_DIM]


def _row_sources(src, rows_per_step, geo):
    lat, lat_row0, cx, cx_row0 = src
    n_lat = geo["nbatch"] * geo["nl"] // rows_per_step
    n_ctx = geo["nbatch"] * geo["nc"] // rows_per_step
    lat0, cx0 = lat_row0 // rows_per_step, cx_row0 // rows_per_step
    d = lat.shape[1]
    return [pl.BlockSpec((rows_per_step, d), lambda j: (lat0 + jnp.minimum(j, n_lat - 1), 0)),
            pl.BlockSpec((rows_per_step, d), lambda j: (cx0 + jnp.clip(j - n_lat, 0, n_ctx - 1), 0))]


def _inproj(src, mods, layer, p, geo):
    d = src[0].shape[1]
    r = geo["n_tiles"] * TM
    tmi = geo["inproj_rows"]
    n_lat_steps = geo["nbatch"] * geo["nl"] // tmi
    lat_steps_per_sample = geo["nl"] // tmi
    rowmap = lambda j: (j, 0)
    posmap = lambda j: (jnp.where(j >= n_lat_steps, lat_steps_per_sample, j % lat_steps_per_sample), 0)
    lay3 = lambda j: (layer, 0, 0)

    def full3(a):
        return pl.BlockSpec((1,) + a.shape[1:], lay3)

    outs = [("cb", CONV_DIM, BF16, False), ("cv", CONV_DIM, BF16, False), ("zp", POOL_DIM, F32, False),
            ("qg", GQA_HEADS * LANES, BF16, False), ("kg", GQA_KV_HEADS * LANES, BF16, False),
            ("vgt", GQA_KV_HEADS * LANES, BF16, True), ("qm", MLA_HEADS * LANES, BF16, False),
            ("km", MLA_HEADS * LANES, BF16, False), ("vmt", MLA_HEADS * LANES, BF16, True)]

    def out_spec(w, transposed):
        if transposed:
            return pl.BlockSpec((tmi // TM, w, TM), lambda j: (j, 0, 0))
        return pl.BlockSpec((tmi, w), rowmap)

    def out_shape(w, dt, transposed):
        return jax.ShapeDtypeStruct((r // TM, w, TM) if transposed else (r, w), dt)

    res = pl.pallas_call(
        functools.partial(_inproj_kernel, d_model=d, nbatch=geo["nbatch"], n_lat_steps=n_lat_steps,
                          lat_steps_per_sample=lat_steps_per_sample),
        grid=(r // tmi,),
        in_specs=_row_sources(src, tmi, geo) + [
            full3(mods), full3(p["norm1"]),
            pl.BlockSpec((tmi, LANES), posmap), pl.BlockSpec((tmi, LANES), posmap),
            pl.BlockSpec((tmi, LANES), posmap), pl.BlockSpec((tmi, LANES), posmap),
            full3(p["w_in"]), full3(p["gq"]), full3(p["gk"]), full3(p["mqn"]), full3(p["mkvn"]),
            full3(p["qkq"]), full3(p["qkk"]), full3(p["wuq"]), full3(p["wuk"]),
            full3(p["wgvt"]), full3(p["wuvt"]),
        ],
        out_specs=[out_spec(w, t) for _, w, _, t in outs],
        out_shape=[out_shape(w, dt, t) for _, w, dt, t in outs],
        compiler_params=_cparams(("arbitrary",)),
        name="inproj",
    )(src[0], src[2], mods, p["norm1"], p["cos_g"], p["sin_g"], p["cos_m"], p["sin_m"], p["w_in"],
      p["gq"], p["gk"], p["mqn"], p["mkvn"], p["qkq"], p["qkk"], p["wuq"], p["wuk"],
      p["wgvt"], p["wuvt"])
    return {name: a for (name, _, _, _), a in zip(outs, res)}


def _attn_kernel(q_ref, kl_ref, kc_ref, vtl_ref, vtc_ref, o_ref, q_st, s_buf, p_buf, a_buf, t_buf, m_ref, acc_ref,
                 *, shared_kv, n_lat, n_ctx, tk):
    tq = q_ref.shape[0]
    n_steps = n_lat + n_ctx
    if shared_kv:
        q_st[0] = jnp.concatenate([q_ref[:, 0:LANES], q_ref[:, LANES:2 * LANES]], axis=0)
        cols = [0]
    else:
        q_st[0] = q_ref[:, 0:LANES]
        q_st[1] = q_ref[:, LANES:2 * LANES]
        cols = [0, LANES]
    for si, col in enumerate(cols):
        q_s, m_s, acc_s = q_st.at[si], m_ref.at[si], acc_ref.at[si]
        m_s[...] = jnp.full(m_s.shape, SOFTMAX_FLOOR, F32)
        acc_s[...] = jnp.zeros(acc_s.shape, F32)

        def pick(i, lat_fn, ctx_fn):
            if isinstance(i, int):
                return ctx_fn(i - n_lat) if i >= n_lat else lat_fn(i)
            return lat_fn(i)

        def key_rows(t):
            start = t * tk if isinstance(t, int) else pl.multiple_of(t * tk, tk)
            return pl.ds(start, tk)

        def scores(i, slot, q_s=q_s, col=col):
            k = pick(i, lambda t: kl_ref[key_rows(t), col:col + LANES],
                     lambda t: kc_ref[key_rows(t), col:col + LANES])
            s = lax.dot_general(k, q_s[...], (((1,), (1,)), ((), ())),
                                preferred_element_type=F32)
            s_buf[slot] = s
            t_buf[slot] = jnp.broadcast_to(jnp.max(s, axis=0, keepdims=True), t_buf.shape[1:])

        def softmax(slot, m_s=m_s):
            s = s_buf[slot]
            m_old = m_s[...]
            m_new = jnp.maximum(m_old, t_buf[slot])
            a_buf[slot] = jnp.exp2(m_old - m_new)
            p_buf[slot] = jnp.exp2(s - m_new[0:1]).astype(BF16)
            m_s[...] = m_new

        def accumulate(i, slot, acc_s=acc_s, col=col):
            vt = pick(i, lambda t: vtl_ref[t, col:col + V_ROWS, :],
                      lambda t: vtc_ref[t, col:col + V_ROWS, :])
            acc_s[...] = a_buf[slot][0:1] * acc_s[...] + jnp.dot(vt, p_buf[slot],
                                                                 preferred_element_type=F32)

        scores(0, 0)

        def pair(t, carry):
            i = 2 * t
            scores(i + 1, 1)
            softmax(0)
            accumulate(i, 0)
            scores(i + 2, 0)
            softmax(1)
            accumulate(i + 1, 1)
            return carry

        n_pairs = max(n_lat - 1, 0) // 2
        lax.fori_loop(0, n_pairs, pair, 0, unroll=ATT_PAIR_UNROLL)
        for i in range(2 * n_pairs, n_steps):
            if i + 1 < n_steps:
                scores(i + 1, (i + 1) % 2)
            softmax(i % 2)
            accumulate(i, i % 2)

    def finish(acc_t):
        o_t = acc_t * (1.0 / acc_t[HEAD_DIM:HEAD_DIM + 1])
        pad = jnp.zeros((LANES - V_ROWS, o_t.shape[1]), F32)
        return jnp.concatenate([o_t, pad], axis=0).T

    if shared_kv:
        o_both = finish(acc_ref[0])
        o0, o1 = o_both[0:tq], o_both[tq:2 * tq]
    else:
        o0, o1 = finish(acc_ref[0]), finish(acc_ref[1])
    lane = lax.broadcasted_iota(I32, (tq, LANES), 1)
    o_ref[...] = jnp.where(lane < HEAD_DIM, o0, pltpu.roll(o1, HEAD_DIM, 1)).astype(BF16)


def _attention(q, k, vt, shared_kv, geo, ctx_queries):
    nb, nc, nl = geo["nbatch"], geo["nc"], geo["nl"]
    kw = LANES if shared_kv else 2 * LANES
    tk = TM
    ctx_blk0 = nb * nl // nc
    ctx_k = pl.BlockSpec((nc, kw), lambda b, g, t: (ctx_blk0 + b, g))
    ctx_vt = pl.BlockSpec((nc // tk, kw, tk), lambda b, g, t: (ctx_blk0 + b, g, 0))
    if ctx_queries:
        tq, q_per, q_blk0, n_lat = nc, 1, ctx_blk0, 0
        lat_k, lat_vt = ctx_k, ctx_vt
    else:
        tq = min(ATT_ROWS, nl) // (2 if shared_kv else 1)
        q_per, q_blk0, n_lat = nl // tq, 0, nl // tk
        lat_k = pl.BlockSpec((nl, kw), lambda b, g, t: (b, g))
        lat_vt = pl.BlockSpec((nl // tk, kw, tk), lambda b, g, t: (b, g, 0))
    n_streams, rows = (1, 2 * tq) if shared_kv else (2, tq)
    return pl.pallas_call(
        functools.partial(_attn_kernel, shared_kv=shared_kv, n_lat=n_lat, n_ctx=nc // tk, tk=tk),
        grid=(nb, 2, q_per),
        in_specs=[
            pl.BlockSpec((tq, 2 * LANES), lambda b, g, t: (q_blk0 + b * q_per + t, g)),
            lat_k, ctx_k, lat_vt, ctx_vt,
        ],
        out_specs=pl.BlockSpec((tq, LANES), lambda b, g, t: (b * q_per + t, g)),
        out_shape=jax.ShapeDtypeStruct((nb * q_per * tq, 2 * LANES), BF16),
        scratch_shapes=[pltpu.VMEM((n_streams, rows, LANES), BF16),
                        pltpu.VMEM((2, tk, rows), F32),
                        pltpu.VMEM((2, tk, rows), BF16),
                        pltpu.VMEM((2, SUBLANES, rows), F32),
                        pltpu.VMEM((2, SUBLANES, rows), F32),
                        pltpu.VMEM((n_streams, SUBLANES, rows), F32),
                        pltpu.VMEM((n_streams, V_ROWS, rows), F32)],
        compiler_params=_cparams(("arbitrary", "arbitrary", "arbitrary")),
        name=("attn_gqa" if shared_kv else "attn_mla") + ("_ctx" if ctx_queries else ""),
    )(q, k, k, vt, vt)


def _top2_sum(a, b, c, d):
    hi_ab, lo_ab = jnp.maximum(a, b), jnp.minimum(a, b)
    hi_cd, lo_cd = jnp.maximum(c, d), jnp.minimum(c, d)
    first = jnp.maximum(hi_ab, hi_cd)
    second = jnp.maximum(jnp.minimum(hi_ab, hi_cd), jnp.maximum(lo_ab, lo_cd))
    return first + second


def _outproj_kernel(xl_ref, xc_ref, cb_ref, cv_ref, cvp_ref, cvn_ref, zp_ref, zpp_ref, zpn_ref,
                    ygl_ref, ygc_ref, yml_ref, ymc_ref,
                    mods_ref, convw_ref, band_ref, wpool_ref, pscale_ref,
                    wout_ref, n2_ref, wr_ref, br_ref, tri_ref,
                    xo_ref, h2_ref, info_ref, cnt_ref,
                    *, d_model, geo):
    tile = pl.program_id(0)
    d = d_model
    tm = xl_ref.shape[0]
    tg = _tile_geometry(tile, geo)
    is_ctx = tg["is_ctx"]
    keep_prev = jnp.where(tg["first"], 0.0, 1.0)
    keep_next = jnp.where(tg["last"], 0.0, 1.0)
    row = tg["mod_row"]
    gate1 = mods_ref[0, pl.ds(row, 1), 2 * d:3 * d]
    shift2 = mods_ref[0, pl.ds(row, 1), 3 * d:4 * d]
    scale2 = mods_ref[0, pl.ds(row, 1), 4 * d:5 * d]

    v = cv_ref[...].astype(F32)
    prev_row = cvp_ref[...].astype(F32)[BF16_ROWS - 1:BF16_ROWS] * keep_prev
    next_row = cvn_ref[...].astype(F32)[0:1] * keep_next
    rid = lax.broadcasted_iota(I32, (tm, CONV_DIM), 0)
    vm1 = jnp.where(rid == 0, prev_row, pltpu.roll(v, 1, 0))
    vp1 = jnp.where(rid == tm - 1, next_row, pltpu.roll(v, tm - 1, 0))
    cw = convw_ref[0]
    y_conv = cb_ref[...].astype(F32) * (vm1 * cw[0:1] + v * cw[1:2] + vp1 * cw[2:3])

    zp = zp_ref[...]
    ext = jnp.concatenate(
        [zpp_ref[...] * keep_prev, zp, zpn_ref[...] * keep_next,
         jnp.zeros((POOL_EXT - tm - 2 * POOL_HALO, POOL_DIM), F32)], axis=0).astype(BF16)
    ext_a, ext_b = ext[:, 0:LANES], ext[:, LANES:2 * LANES]
    lane = lax.broadcasted_iota(I32, (tm, LANES), 1)
    low = lane < POOL_DIM // 4
    sum_a = jnp.where(low, jnp.dot(band_ref[0], ext_a, preferred_element_type=F32),
                      jnp.dot(band_ref[1], ext_a, preferred_element_type=F32))
    sum_b = jnp.where(low, jnp.dot(band_ref[2], ext_b, preferred_element_type=F32),
                      jnp.dot(band_ref[3], ext_b, preferred_element_type=F32))
    sums = jnp.concatenate([sum_a, sum_b], axis=1)
    lane_p = lax.broadcasted_iota(I32, (tm, POOL_DIM), 1)
    half_w = jnp.left_shift(1, jnp.right_shift(lane_p, int(math.log2(POOL_DIM // 4))))
    pos = tg["pos0"] + rid
    cnt = (jnp.minimum(pos + half_w, tg["seg_len"]) - jnp.maximum(pos - half_w, 0)).astype(F32)
    dlt = sums / cnt - zp
    y_pool = jnp.dot(dlt.astype(BF16), wpool_ref[0], preferred_element_type=F32) * pscale_ref[0]

    y_gqa = jnp.where(is_ctx, ygc_ref[...], ygl_ref[...])
    y_mla = jnp.where(is_ctx, ymc_ref[...], yml_ref[...])
    ycat = jnp.concatenate([y_conv.astype(BF16), y_pool.astype(BF16), y_gqa, y_mla], axis=1)
    y = jnp.dot(ycat, wout_ref[0], preferred_element_type=F32)
    xn = jnp.where(is_ctx, xc_ref[...], xl_ref[...]) + gate1 * y
    xo_ref[...] = xn
    ms = jnp.mean(xn * xn, axis=-1, keepdims=True)
    h2 = (xn * lax.rsqrt(ms + NORM_EPS)) * (n2_ref[0] * (1.0 + scale2)) + shift2
    h2_ref[...] = h2

    h_hi = h2.astype(BF16)
    h_lo = (h2 - h_hi.astype(F32)).astype(BF16)
    wr = wr_ref[...]
    part = jnp.dot(h_hi, wr, preferred_element_type=F32)
    small = part[:, LANES:] + jnp.dot(h_lo, wr[:, 0:LANES], preferred_element_type=F32)
    logits = (part[:, 0:LANES] + small).T[0:N_EXPERTS]
    scores = _sigmoid(logits)
    sel = scores + br_ref[...]
    epg = EXPERTS_PER_GROUP
    n_groups = N_EXPERTS // epg
    srow = [sel[e:e + 1] for e in range(N_EXPERTS)]
    crow = [scores[e:e + 1] for e in range(N_EXPERTS)]
    gscore = [_top2_sum(*srow[g * epg:(g + 1) * epg]) for g in range(n_groups)]
    gbest = jnp.zeros_like(gscore[0]).astype(I32)
    best = gscore[0]
    for g in range(1, n_groups):
        upd = gscore[g] > best
        gbest = jnp.where(upd, g, gbest)
        best = jnp.where(upd, gscore[g], best)

    def pick(rows_, j):
        out = rows_[(n_groups - 1) * epg + j]
        for g in range(n_groups - 2, -1, -1):
            out = jnp.where(gbest == g, rows_[g * epg + j], out)
        return out

    sv = [pick(srow, j) for j in range(epg)]
    cv_ = [pick(crow, j) for j in range(epg)]
    i1 = jnp.zeros_like(gbest)
    b1 = sv[0]
    for j in range(1, epg):
        upd = sv[j] > b1
        i1 = jnp.where(upd, j, i1)
        b1 = jnp.where(upd, sv[j], b1)
    i2 = jnp.zeros_like(gbest)
    b2 = jnp.full_like(b1, -jnp.inf)
    for j in range(epg):
        upd = jnp.logical_and(i1 != j, sv[j] > b2)
        i2 = jnp.where(upd, j, i2)
        b2 = jnp.where(upd, sv[j], b2)
    s1 = cv_[epg - 1]
    s2 = cv_[epg - 1]
    for j in range(epg - 2, -1, -1):
        s1 = jnp.where(i1 == j, cv_[j], s1)
        s2 = jnp.where(i2 == j, cv_[j], s2)
    inv = 1.0 / (s1 + s2)
    e1 = gbest * epg + i1
    e2 = gbest * epg + i2

    @pl.when(tile == 0)
    def _():
        cnt_ref[...] = jnp.zeros_like(cnt_ref)

    erow = lax.broadcasted_iota(I32, (N_EXPERTS, tm), 0)
    hit1 = erow == e1
    hit2 = erow == e2
    onehot = jnp.where(hit1, 1.0, 0.0) + jnp.where(hit2, 1.0, 0.0)
    before = jnp.dot(onehot.astype(BF16), tri_ref[...], preferred_element_type=F32)
    tot = cnt_ref[:, 0:1] + before
    rank1 = jnp.sum(jnp.where(hit1, tot, 0.0), axis=0, keepdims=True).astype(I32)
    rank2 = jnp.sum(jnp.where(hit2, tot, 0.0), axis=0, keepdims=True).astype(I32)
    cnt_ref[...] = cnt_ref[...] + jnp.sum(onehot, axis=1, keepdims=True)

    w1 = lax.bitcast_convert_type(s1 * inv, I32)
    w2 = lax.bitcast_convert_type(s2 * inv, I32)
    irow = lax.broadcasted_iota(I32, (SUBLANES, tm), 0)
    info = jnp.where(irow == 0, e1, jnp.where(irow == 1, e2, jnp.where(
        irow == 2, rank1, jnp.where(irow == 3, rank2, jnp.where(
            irow == 4, w1, jnp.where(irow == 5, w2, 0))))))
    info_ref[0] = info


def _outproj(src, mix, yg, ygc, ym, ymc, mods, layer, p, geo, n_tiles):
    d = src[0].shape[1]
    nt_all, n_lat = geo["n_tiles"], geo["n_lat_tiles"]
    r = n_tiles * TM
    nt = n_tiles
    rowmap = lambda j: (j, 0)
    latmap = lambda j: (jnp.minimum(j, n_lat - 1), 0)
    ctxmap = lambda j: (jnp.clip(j - n_lat, 0, ygc.shape[0] // TM - 1), 0)
    lay3 = lambda j: (layer, 0, 0)
    c0 = lambda j: (0, 0)
    c3 = lambda j: (0, 0, 0)
    bf_blocks = TM // BF16_ROWS
    f_blocks = TM // SUBLANES

    def full3(a):
        return pl.BlockSpec((1,) + a.shape[1:], lay3)

    return pl.pallas_call(
        functools.partial(_outproj_kernel, d_model=d, geo=geo),
        grid=(n_tiles,),
        in_specs=_row_sources(src, TM, geo) + [
            pl.BlockSpec((TM, CONV_DIM), rowmap),
            pl.BlockSpec((TM, CONV_DIM), rowmap),
            pl.BlockSpec((BF16_ROWS, CONV_DIM), lambda j: (jnp.maximum(j * bf_blocks - 1, 0), 0)),
            pl.BlockSpec((BF16_ROWS, CONV_DIM),
                         lambda j: (jnp.minimum((j + 1) * bf_blocks, nt_all * bf_blocks - 1), 0)),
            pl.BlockSpec((TM, POOL_DIM), rowmap),
            pl.BlockSpec((SUBLANES, POOL_DIM), lambda j: (jnp.maximum(j * f_blocks - 1, 0), 0)),
            pl.BlockSpec((SUBLANES, POOL_DIM),
                         lambda j: (jnp.minimum((j + 1) * f_blocks, nt_all * f_blocks - 1), 0)),
            pl.BlockSpec((TM, 2 * LANES), latmap),
            pl.BlockSpec((TM, 2 * LANES), ctxmap),
            pl.BlockSpec((TM, 2 * LANES), latmap),
            pl.BlockSpec((TM, 2 * LANES), ctxmap),
            full3(mods), full3(p["conv_w"]),
            pl.BlockSpec(p["band"].shape, c3),
            full3(p["wpool"]), full3(p["pscale"]), full3(p["w_out"]), full3(p["norm2"]),
            pl.BlockSpec(p["wr"].shape, c0), pl.BlockSpec(p["br"].shape, c0),
            pl.BlockSpec(p["tri"].shape, c0),
        ],
        out_specs=[
            pl.BlockSpec((TM, d), rowmap),
            pl.BlockSpec((TM, d), rowmap),
            pl.BlockSpec((1, SUBLANES, TM), lambda j: (j, 0, 0)),
            pl.BlockSpec((N_EXPERTS, LANES), c0),
        ],
        out_shape=[
            jax.ShapeDtypeStruct((r, d), F32),
            jax.ShapeDtypeStruct((r, d), F32),
            jax.ShapeDtypeStruct((nt, SUBLANES, TM), I32),
            jax.ShapeDtypeStruct((N_EXPERTS, LANES), F32),
        ],
        compiler_params=_cparams(("arbitrary",)),
        name="outproj",
    )(src[0], src[2], mix["cb"], mix["cv"], mix["cv"], mix["cv"], mix["zp"], mix["zp"], mix["zp"],
      yg, ygc, ym, ymc, mods, p["conv_w"], p["band"], p["wpool"], p["pscale"], p["w_out"],
      p["norm2"], p["wr"], p["br"], p["tri"])


def _slot_copies(block_ref, idx_ref, sem):
    n = block_ref.shape[2]
    return [pltpu.make_async_copy(block_ref.at[0, k], idx_ref.at[pl.ds(k * n, n)], sem) for k in range(2)]


def _start_all(copies):
    for cp in copies:
        cp.start()


def _wait_all(copies):
    for cp in copies:
        cp.wait()


def _row_copies(idx_ref, n_rows, make):
    def body(t, carry):
        for u in range(SUBLANES):
            r = t * SUBLANES + u
            make(0, t, u, idx_ref[r]).start()
            make(1, t, u, idx_ref[n_rows + r]).start()
        return carry
    lax.fori_loop(0, n_rows // SUBLANES, body, 0)


def _scatter_kernel(ps_ref, zs_ref, has_ref, nu_ref, info_ref, info_next_ref, h2_ref, dest_ref, xs_ref,
                    dsm_ref, zbuf_ref, stage_ref, nxt_ref, sem_ref, *, first_spare, n_blocks, n_tiles):
    j = pl.program_id(0)
    tm = h2_ref.shape[0]

    @pl.when(j == 0)
    def _():
        zbuf_ref[...] = jnp.zeros_like(zbuf_ref)

        def zero_copy(start):
            return pltpu.make_async_copy(
                zbuf_ref, xs_ref.at[pl.ds(pl.multiple_of(start, BM), BM)], sem_ref.at[2])

        for act in ("start", "wait"):
            for e in range(N_EXPERTS):
                @pl.when(has_ref[e] > 0)
                def _():
                    getattr(zero_copy(zs_ref[e]), act)()
            for jb in range(first_spare, n_blocks):
                @pl.when(jb >= nu_ref[0])
                def _():
                    getattr(zero_copy(jb * BM), act)()

    def slots_of(info):
        e1, e2 = info[0:1], info[1:2]
        d1, d2 = info[2:3], info[3:4]
        for e in range(N_EXPERTS):
            d1 = d1 + jnp.where(e1 == e, ps_ref[e], 0)
            d2 = d2 + jnp.where(e2 == e, ps_ref[e], 0)
        irow = lax.broadcasted_iota(I32, (SUBLANES, tm), 0)
        return jnp.where(irow == 0, d1, jnp.where(irow == 1, d2, 0))

    dest_ref[0] = slots_of(info_ref[0])

    @pl.when(j == 0)
    def _():
        first = _slot_copies(dest_ref, dsm_ref, sem_ref.at[3])
        _start_all(first)
        _wait_all(first)

    par = j % 2
    stage_ref[par] = h2_ref[...].reshape(stage_ref.shape[1:])

    def row_copy(_, t, u, slot):
        return pltpu.make_async_copy(stage_ref.at[par, t, pl.ds(u, 1)], xs_ref.at[pl.ds(slot, 1)],
                                     sem_ref.at[par])

    _row_copies(dsm_ref, tm, row_copy)

    nxt_ref[0] = slots_of(info_next_ref[0])
    nxt = _slot_copies(nxt_ref, dsm_ref, sem_ref.at[3])

    @pl.when(j + 1 < n_tiles)
    def _():
        _start_all(nxt)

    def retire(which):
        for _ in range(2):
            pltpu.make_async_copy(h2_ref, xs_ref.at[pl.ds(0, tm)], sem_ref.at[which]).wait()

    @pl.when(j > 0)
    def _():
        retire(1 - par)

    @pl.when(j == n_tiles - 1)
    def _():
        retire(par)

    @pl.when(j + 1 < n_tiles)
    def _():
        _wait_all(nxt)


def _scatter(info, h2, pad_start, zero_start, has_rows, n_used, n_slots):
    r, d = h2.shape
    nt = r // TM
    return pl.pallas_call(
        functools.partial(_scatter_kernel, first_spare=-(-2 * r // BM), n_blocks=n_slots // BM,
                          n_tiles=nt),
        grid_spec=pltpu.PrefetchScalarGridSpec(
            num_scalar_prefetch=4,
            grid=(nt,),
            in_specs=[
                pl.BlockSpec((1, SUBLANES, TM), lambda j, *_: (j, 0, 0)),
                pl.BlockSpec((1, SUBLANES, TM), lambda j, *_: (jnp.minimum(j + 1, nt - 1), 0, 0)),
                pl.BlockSpec((TM, d), lambda j, *_: (j, 0)),
            ],
            out_specs=[
                pl.BlockSpec((1, SUBLANES, TM), lambda j, *_: (j, 0, 0)),
                pl.BlockSpec(memory_space=pl.ANY),
            ],
            scratch_shapes=[
                pltpu.SMEM((2 * TM,), I32),
                pltpu.VMEM((BM, d), F32),
                pltpu.VMEM((2, TM // SUBLANES, SUBLANES, d), F32),
                pltpu.VMEM((1, SUBLANES, TM), I32),
                pltpu.SemaphoreType.DMA((4,)),
            ],
        ),
        out_shape=[
            jax.ShapeDtypeStruct((nt, SUBLANES, TM), I32),
            jax.ShapeDtypeStruct((n_slots, d), F32),
        ],
        compiler_params=_cparams(("arbitrary",)),
        name="moe_scatter",
    )(pad_start, zero_start, has_rows, n_used, info, info, h2)


def _expert_kernel(be_ref, nu_ref, xs_ref, wg_ref, wu_ref, wd_ref, y_ref, wgb, wub, wdb):
    j = pl.program_id(0)

    @pl.when(j < nu_ref[0])
    def _():
        e = be_ref[j]
        prev = be_ref[jnp.maximum(j - 1, 0)]

        @pl.when(jnp.logical_or(j == 0, e != prev))
        def _():
            wgb[...] = wg_ref[0].astype(BF16)
            wub[...] = wu_ref[0].astype(BF16)
            wdb[...] = wd_ref[0].astype(BF16)

        x = xs_ref[...].astype(BF16)
        a = jnp.dot(x, wgb[...], preferred_element_type=F32)
        u = jnp.dot(x, wub[...], preferred_element_type=F32)
        hmid = (a * _sigmoid(a) * u).astype(BF16)
        y_ref[...] = jnp.dot(hmid, wdb[...], preferred_element_type=F32)

    @pl.when(j >= nu_ref[0])
    def _():
        y_ref[...] = jnp.zeros_like(y_ref)


def _experts(xs, block_exp, n_used, w_gate, w_up, w_down, layer):
    n_slots, d = xs.shape
    de = w_gate.shape[-1]
    nbm = n_slots // BM
    n_exp = w_gate.shape[1]

    def blk(j, be, nu):
        return (jnp.minimum(j, nu[0] - 1), 0)

    def wmap(j, be, nu):
        return (layer * n_exp + be[jnp.minimum(j, nu[0] - 1)], 0, 0)

    wg = w_gate.reshape((-1,) + w_gate.shape[2:])
    wu = w_up.reshape((-1,) + w_up.shape[2:])
    wd = w_down.reshape((-1,) + w_down.shape[2:])
    return pl.pallas_call(
        _expert_kernel,
        grid_spec=pltpu.PrefetchScalarGridSpec(
            num_scalar_prefetch=2,
            grid=(nbm,),
            in_specs=[
                pl.BlockSpec((BM, d), blk),
                pl.BlockSpec((1, d, de), wmap),
                pl.BlockSpec((1, d, de), wmap),
                pl.BlockSpec((1, de, d), wmap),
            ],
            out_specs=pl.BlockSpec((BM, d), lambda j, be, nu: (j, 0)),
            scratch_shapes=[
                pltpu.VMEM((d, de), BF16),
                pltpu.VMEM((d, de), BF16),
                pltpu.VMEM((de, d), BF16),
            ],
        ),
        out_shape=jax.ShapeDtypeStruct((n_slots, d), F32),
        compiler_params=_cparams(("arbitrary",)),
        name="moe_experts",
    )(block_exp, n_used, xs, wg, wu, wd)


def _combine_kernel(dest_ref, dest_next_ref, dest_next2_ref, info_ref, x_ref, mods_ref, y_ref, o_ref,
                    dsm_ref, ybuf_ref, sem_ref, *, d_model, geo, n_steps):
    j = pl.program_id(0)
    d = d_model
    tm = x_ref.shape[0]
    row = _tile_geometry(j, geo)["mod_row"]
    gate2 = mods_ref[0, pl.ds(row, 1), 5 * d:6 * d]
    cur = j % 2

    def load_slots(idx_block_ref):
        copies = _slot_copies(idx_block_ref, dsm_ref, sem_ref.at[2])
        _start_all(copies)
        _wait_all(copies)

    def gather(buf):
        def row_copy(k, t, u, slot):
            return pltpu.make_async_copy(y_ref.at[pl.ds(slot, 1)],
                                         ybuf_ref.at[buf, k, t, pl.ds(u, 1)], sem_ref.at[buf])

        _row_copies(dsm_ref, tm, row_copy)

    @pl.when(j == 0)
    def _():
        load_slots(dest_ref)
        gather(0)
        load_slots(dest_next_ref)

    @pl.when(j + 1 < n_steps)
    def _():
        gather(1 - cur)

    nxt2 = _slot_copies(dest_next2_ref, dsm_ref, sem_ref.at[2])

    @pl.when(j + 2 < n_steps)
    def _():
        _start_all(nxt2)

    for k in range(2):
        pltpu.make_async_copy(y_ref.at[pl.ds(0, tm)], o_ref, sem_ref.at[cur]).wait()

    info = info_ref[0]
    wrow = lax.broadcasted_iota(I32, (LANES, tm), 0)
    w_lanes = jnp.where(wrow == 0, lax.bitcast_convert_type(info[4:5], F32),
                        jnp.where(wrow == 1, lax.bitcast_convert_type(info[5:6], F32), 0.0))
    w_rows = w_lanes.T
    y1 = ybuf_ref[cur, 0].reshape(tm, d)
    y2 = ybuf_ref[cur, 1].reshape(tm, d)
    o_ref[...] = x_ref[...] + gate2 * (w_rows[:, 0:1] * y1 + w_rows[:, 1:2] * y2)

    @pl.when(j + 2 < n_steps)
    def _():
        _wait_all(nxt2)


def _combine(dest, info, xn, mods, y, layer, geo):
    r, d = xn.shape
    n_steps = r // TM
    return pl.pallas_call(
        functools.partial(_combine_kernel, d_model=d, geo=geo, n_steps=n_steps),
        grid=(n_steps,),
        in_specs=[
            pl.BlockSpec((1, SUBLANES, TM), lambda j: (j, 0, 0)),
            pl.BlockSpec((1, SUBLANES, TM), lambda j: (jnp.minimum(j + 1, n_steps - 1), 0, 0)),
            pl.BlockSpec((1, SUBLANES, TM), lambda j: (jnp.minimum(j + 2, n_steps - 1), 0, 0)),
            pl.BlockSpec((1, SUBLANES, TM), lambda j: (j, 0, 0)),
            pl.BlockSpec((TM, d), lambda j: (j, 0)),
            pl.BlockSpec((1,) + mods.shape[1:], lambda j: (layer, 0, 0)),
            pl.BlockSpec(memory_space=pl.ANY),
        ],
        out_specs=pl.BlockSpec((TM, d), lambda j: (j, 0)),
        out_shape=jax.ShapeDtypeStruct((n_steps * TM, d), F32),
        scratch_shapes=[
            pltpu.SMEM((2 * TM,), I32),
            pltpu.VMEM((2, 2, TM // SUBLANES, SUBLANES, d), F32),
            pltpu.SemaphoreType.DMA((3,)),
        ],
        compiler_params=_cparams(("arbitrary",)),
        name="moe_combine",
    )(dest, dest, dest, info, xn, mods, y)


def _pad_heads(w, n_heads, width):
    lead = w.shape[:-1]
    w = w.reshape(lead + (n_heads, width))
    w = jnp.pad(w, [(0, 0)] * len(lead) + [(0, 0), (0, LANES - width)])
    return w.reshape(lead + (n_heads * LANES,))


def _rope_tables(id_rows, nl):
    t = jnp.arange(nl)
    row_id = (t // GRID_W).astype(F32)
    col_id = (t % GRID_W).astype(F32)

    def angles(rot_dim):
        n_freq = rot_dim // 4
        inv_freq = jnp.power(ROPE_THETA, -jnp.arange(n_freq, dtype=F32) / n_freq)
        return jnp.concatenate([row_id[:, None] * inv_freq, col_id[:, None] * inv_freq], axis=-1)

    def with_ctx(tab, fill):
        return jnp.concatenate([tab, jnp.full((id_rows, LANES), fill, F32)], axis=0)

    ag = angles(HEAD_DIM)
    one_g = jnp.ones((nl, LANES - HEAD_DIM), F32)
    cos_g = jnp.concatenate([jnp.cos(ag), jnp.cos(ag), one_g], axis=-1)
    sin_g = jnp.concatenate([-jnp.sin(ag), jnp.sin(ag), 0.0 * one_g], axis=-1)
    am = angles(MLA_ROPE_DIM)
    one_n = jnp.ones((nl, MLA_NOPE_DIM), F32)
    one_t = jnp.ones((nl, LANES - MLA_QK_DIM), F32)
    cos_m = jnp.concatenate([one_n, jnp.cos(am), jnp.cos(am), one_t], axis=-1)
    sin_m = jnp.concatenate([0.0 * one_n, -jnp.sin(am), jnp.sin(am), 0.0 * one_t], axis=-1)
    return with_ctx(cos_g, 1.0), with_ctx(sin_g, 0.0), with_ctx(cos_m, 1.0), with_ctx(sin_m, 0.0)


def _pool_band():
    t = np.arange(TM)[:, None]
    src = np.arange(POOL_EXT)[None, :] - POOL_HALO
    live = np.arange(POOL_EXT)[None, :] < TM + 2 * POOL_HALO
    mats = [((src >= t - w // 2) & (src < t + w // 2) & live) for w in POOL_WINDOWS]
    return jnp.asarray(np.stack(mats).astype(np.float32), dtype=BF16)


def _prep_params(w_in, norm1, norm2, conv_w, w_pool, pool_scale, gqa_q_norm, gqa_k_norm,
                 mla_q_norm, mla_kv_norm, mla_w_uq, mla_w_uk, mla_w_uv, mla_qk_q_norm,
                 mla_qk_k_norm, w_out, w_router, b_router, id_rows, nl):
    dep = w_in.shape[0]
    o = 0
    pieces = {}
    for name, n in (("conv", 3 * CONV_DIM), ("pool", POOL_DIM), ("gq", GQA_HEADS * HEAD_DIM),
                    ("gk", GQA_KV_HEADS * HEAD_DIM), ("gv", GQA_KV_HEADS * HEAD_DIM),
                    ("mq", MLA_Q_RANK), ("mkv", MLA_KV_RANK), ("mkr", MLA_ROPE_DIM)):
        pieces[name] = w_in[..., o:o + n]
        o += n
    mkr = jnp.pad(pieces["mkr"], ((0, 0), (0, 0), (MLA_NOPE_DIM, LANES - MLA_QK_DIM)))
    w_in_p = jnp.concatenate([
        pieces["conv"], pieces["pool"], _pad_heads(pieces["gq"], GQA_HEADS, HEAD_DIM),
        _pad_heads(pieces["gk"], GQA_KV_HEADS, HEAD_DIM),
        pieces["mq"], pieces["mkv"], mkr], axis=-1).astype(BF16)
    wgvt = jnp.swapaxes(_pad_heads(pieces["gv"], GQA_KV_HEADS, HEAD_DIM), 1, 2).astype(BF16)
    wuvt = jnp.swapaxes(_pad_heads(mla_w_uv, MLA_HEADS, MLA_V_DIM), 1, 2).astype(BF16)
    eye = jnp.eye(len(POOL_WINDOWS), dtype=F32)
    wpool = jnp.einsum("gh,dgij->dgihj", eye, w_pool).reshape(dep, POOL_DIM, POOL_DIM).astype(BF16)
    cos_g, sin_g, cos_m, sin_m = _rope_tables(id_rows, nl)
    tri = np.triu(np.ones((TM, TM), np.float32), 1)
    wr_pad = jnp.pad(w_router, ((0, 0), (0, LANES - w_router.shape[1])))
    wr_hi = wr_pad.astype(BF16)
    wr_split = jnp.concatenate([wr_hi, (wr_pad - wr_hi.astype(F32)).astype(BF16)], axis=1)

    def row3(a):
        return a.reshape(dep, 1, a.shape[-1])

    return {
        "w_in": w_in_p,
        "norm1": row3(norm1), "norm2": row3(norm2),
        "gq": row3(_pad_heads(gqa_q_norm * (GQA_SCALE * LOG2E), 1, HEAD_DIM)),
        "gk": row3(_pad_heads(gqa_k_norm, 1, HEAD_DIM)),
        "mqn": row3(mla_q_norm), "mkvn": row3(mla_kv_norm),
        "qkq": row3(_pad_heads(mla_qk_q_norm * (MLA_SCALE * LOG2E), 1, MLA_QK_DIM)),
        "qkk": row3(_pad_heads(mla_qk_k_norm, 1, MLA_QK_DIM)),
        "wuq": _pad_heads(mla_w_uq, MLA_HEADS, MLA_QK_DIM).astype(BF16),
        "wuk": _pad_heads(mla_w_uk, MLA_HEADS, MLA_NOPE_DIM).astype(BF16),
        "wgvt": wgvt, "wuvt": wuvt,
        "cos_g": cos_g, "sin_g": sin_g, "cos_m": cos_m, "sin_m": sin_m,
        "conv_w": jnp.pad(conv_w, ((0, 0), (0, SUBLANES - conv_w.shape[1]), (0, 0))),
        "band": _pool_band(),
        "wpool": wpool, "pscale": row3(pool_scale),
        "w_out": w_out.astype(BF16),
        "wr": wr_split, "br": b_router.reshape(-1, 1),
        "tri": jnp.asarray(tri, dtype=BF16),
    }


def _moe_plan(counts):
    counts = counts.astype(I32)
    padded = ((counts + BM - 1) // BM) * BM
    pad_end = jnp.cumsum(padded)
    pad_start = pad_end - padded
    return pad_start, pad_end, padded


def kernel(x, c, ctx, c_ctx, w_mod, b_mod, norm1, norm2, w_in, conv_w, w_pool, pool_scale,
           gqa_q_norm, gqa_k_norm, mla_q_norm, mla_kv_norm, mla_w_uq, mla_w_uk, mla_w_uv,
           mla_qk_q_norm, mla_qk_k_norm, w_out, w_router, b_router, w_gate, w_up, w_down):
    nb, nl, d = x.shape
    nc = ctx.shape[1]
    depth = w_mod.shape[0]
    assert nc % TM == 0 and nl % TM == 0 and nl % GRID_W == 0 and nb < SUBLANES
    lt, nct = nl // TM, nc // TM
    wide = nl % INPROJ_ROWS == 0 and (nb * nc) % INPROJ_ROWS == 0
    geo = {"nbatch": nb, "nc": nc, "nl": nl, "lt": lt, "nct": nct,
           "n_lat_tiles": nb * lt, "n_tiles": nb * (lt + nct),
           "inproj_rows": INPROJ_ROWS if wide else TM}

    p = _prep_params(w_in, norm1, norm2, conv_w, w_pool, pool_scale, gqa_q_norm, gqa_k_norm,
                     mla_q_norm, mla_kv_norm, mla_w_uq, mla_w_uk, mla_w_uv, mla_qk_q_norm,
                     mla_qk_k_norm, w_out, w_router, b_router, geo["inproj_rows"], nl)
    cvec = jnp.concatenate([c, c_ctx[None, :], jnp.zeros((SUBLANES - nb - 1, d), F32)], axis=0)
    mods = _adaln(cvec, w_mod, b_mod)

    src = (x.reshape(nb * nl, d), 0, ctx.reshape(nb * nc, d), 0)
    for i in range(depth):
        last = i == depth - 1
        mix = _inproj(src, mods, i, p, geo)
        yg = _attention(mix["qg"], mix["kg"], mix["vgt"], True, geo, False)
        ym = _attention(mix["qm"], mix["km"], mix["vmt"], False, geo, False)
        if last:
            n_tiles, ygc, ymc = geo["n_lat_tiles"], yg, ym
        else:
            n_tiles = geo["n_tiles"]
            ygc = _attention(mix["qg"], mix["kg"], mix["vgt"], True, geo, True)
            ymc = _attention(mix["qm"], mix["km"], mix["vmt"], False, geo, True)
        xn, h2, info, cnt = _outproj(src, mix, yg, ygc, ym, ymc, mods, i, p, geo, n_tiles)
        n_blocks = -(-2 * n_tiles * TM // BM) + N_EXPERTS
        n_slots = n_blocks * BM
        pad_start, pad_end, padded = _moe_plan(cnt[:, 0])
        n_used = (pad_end[-1:] // BM).astype(I32)
        block_row0 = jnp.arange(n_blocks, dtype=I32) * BM
        block_exp = jnp.minimum(jnp.sum((pad_end[None, :] <= block_row0[:, None]).astype(I32), axis=1),
                                N_EXPERTS - 1)
        dest, xs = _scatter(info, h2, pad_start, jnp.maximum(pad_end - BM, 0),
                            (padded > 0).astype(I32), n_used, n_slots)
        y = _experts(xs, block_exp, n_used, w_gate, w_up, w_down, i)
        xa = _combine(dest, info, xn, mods, y, i, geo)
        src = (xa, 0, xa, nb * nl)
    return xa.reshape(nb, nl, d)
```
